```python
import math
import jax, jax.numpy as jnp
from jax import lax
import numpy as np

D_MODEL = 1024
BATCH = 8
SEQ = 4096
DEPTH = 2

RET_HEADS = 4
RET_DK = 128
RET_DV = 128
GDN_HEADS = 4
GDN_DK = 128
GDN_DV = 128
CONV_K = 4
CHUNK = 64
ROPE_BASE = 10000.0
N_BRANCH = 2
BRANCH_W = 512
D_FF = 2816
N_EXPERTS = 8
TOP_K = 2
MOE_BLOCK = 256
EPS = 1e-6
N_DENSE = (DEPTH + 1) // 2
N_MOE = DEPTH // 2
RET_QK_W = RET_HEADS * RET_DK
RET_V_W = RET_HEADS * RET_DV
GDN_QK_W = GDN_HEADS * GDN_DK
GDN_V_W = GDN_HEADS * GDN_DV
CONV_CH = 2 * GDN_QK_W + GDN_V_W
IN_SPLITS = (RET_QK_W, RET_QK_W, RET_V_W, RET_V_W, CONV_CH, GDN_V_W, GDN_HEADS, GDN_HEADS, D_MODEL, D_MODEL)
IN_COLS = sum(IN_SPLITS)

kernel_name = 'hybrid_retention_gdn_moe_block'


def rms_norm(x, w):
    xf = x.astype(jnp.float32)
    y = xf * lax.rsqrt(jnp.mean(xf * xf, axis=-1, keepdims=True) + EPS)
    return (y * w.astype(jnp.float32)).astype(x.dtype)


def head_rms(o):
    return o * lax.rsqrt(jnp.mean(o * o, axis=-1, keepdims=True) + EPS)


def l2_normalize(t):
    return t * lax.rsqrt(jnp.sum(t * t, axis=-1, keepdims=True) + EPS)


def swiglu(x, w_gate, w_up, w_down):
    return (jax.nn.silu(x @ w_gate) * (x @ w_up)) @ w_down


def split_columns(t, sizes):
    offsets = np.cumsum(sizes)[:-1].tolist()
    return jnp.split(t, offsets, axis=-1)


def rotary_tables(seq_len):
    inv_freq = ROPE_BASE ** (-jnp.arange(0, RET_DK, 2, dtype=jnp.float32) / RET_DK)
    ang = jnp.arange(seq_len, dtype=jnp.float32)[:, None] * inv_freq[None, :]
    return jnp.cos(ang), jnp.sin(ang)


def apply_rotary(t, cos, sin):
    t1, t2 = jnp.split(t, 2, axis=-1)
    c = cos[None, :, None, :]
    s = sin[None, :, None, :]
    return jnp.concatenate([t1 * c - t2 * s, t2 * c + t1 * s], axis=-1)


def to_chunks(t, n_chunks):
    b, h = t.shape[0], t.shape[2]
    t = t.reshape((b, n_chunks, CHUNK, h) + t.shape[3:])
    return jnp.moveaxis(t, 3, 1)


def from_chunks(o):
    b, h, n, c, d = o.shape
    return jnp.transpose(o, (0, 2, 3, 1, 4)).reshape(b, n * c, h, d)


def causal_depthwise_conv(x, w):
    k, c = w.shape
    return lax.conv_general_dilated(x, w[:, None, :], window_strides=(1,), padding=[(k - 1, 0)],
                                    dimension_numbers=('NWC', 'WIO', 'NWC'), feature_group_count=c)


def retention_chunkwise(q, k, v):
    b, s, h, dk = q.shape
    dv = v.shape[-1]
    n = s // CHUNK
    log_gamma = jnp.log(1.0 - jnp.exp2(-5.0 - jnp.arange(RET_HEADS, dtype=jnp.float32)))
    q, k, v = to_chunks(q, n), to_chunks(k, n), to_chunks(v, n)
    pos = jnp.arange(CHUNK, dtype=jnp.float32)
    diff = pos[:, None] - pos[None, :]
    decay = jnp.where(diff >= 0, jnp.exp(jnp.maximum(diff, 0.0)[None] * log_gamma[:, None, None]), 0.0)
    scores = jnp.einsum('bhnid,bhnjd->bhnij', q, k) * decay[None, :, None]
    o_intra = jnp.einsum('bhnij,bhnjv->bhniv', scores, v)
    k_dec = k * jnp.exp((CHUNK - 1 - pos)[None, :] * log_gamma[:, None])[None, :, None, :, None]
    kv = jnp.einsum('bhncd,bhncv->bhndv', k_dec, v)
    chunk_decay = jnp.exp(CHUNK * log_gamma)[None, :, None, None]

    def step(state, kv_n):
        return state * chunk_decay + kv_n, state

    _, prev = lax.scan(step, jnp.zeros((b, h, dk, dv), q.dtype), jnp.moveaxis(kv, 2, 0))
    q_dec = q * jnp.exp((pos + 1.0)[None, :] * log_gamma[:, None])[None, :, None, :, None]
    o_inter = jnp.einsum('bhncd,nbhdv->bhncv', q_dec, prev)
    return from_chunks(o_intra + o_inter)


def gated_delta_chunkwise(q, k, v, g, beta):
    b, s, h, dk = q.shape
    dv = v.shape[-1]
    n = s // CHUNK
    q, k, v = to_chunks(q, n), to_chunks(k, n), to_chunks(v, n)
    g, beta = to_chunks(g, n), to_chunks(beta, n)
    g_cum = jnp.cumsum(g, axis=-1)
    causal = jnp.tril(jnp.ones((CHUNK, CHUNK), dtype=bool))
    strict = jnp.tril(jnp.ones((CHUNK, CHUNK), dtype=bool), -1)
    diff = g_cum[..., :, None] - g_cum[..., None, :]
    decay = jnp.where(causal, jnp.exp(jnp.where(causal, diff, 0.0)), 0.0)
    k_beta = k * beta[..., None]
    a_mat = jnp.where(strict, jnp.einsum('bhnid,bhnjd->bhnij', k_beta, k) * decay, 0.0)
    t_mat = a_mat + jnp.eye(CHUNK, dtype=a_mat.dtype)
    u = lax.linalg.triangular_solve(t_mat, v * beta[..., None], left_side=True, lower=True, unit_diagonal=True)
    w = lax.linalg.triangular_solve(t_mat, k_beta * jnp.exp(g_cum)[..., None], left_side=True, lower=True,
                                    unit_diagonal=True)
    attn = jnp.where(causal, jnp.einsum('bhnid,bhnjd->bhnij', q, k) * decay, 0.0)
    q_dec = q * jnp.exp(g_cum)[..., None]
    g_last = g_cum[..., -1]
    k_dec = k * jnp.exp(g_last[..., None] - g_cum)[..., None]

    def step(state, xs):
        w_n, u_n, q_n, k_n, attn_n, gl_n = xs
        v_new = u_n - jnp.einsum('bhcd,bhdv->bhcv', w_n, state)
        o_n = jnp.einsum('bhcd,bhdv->bhcv', q_n, state) + jnp.einsum('bhij,bhjv->bhiv', attn_n, v_new)
        state = state * jnp.exp(gl_n)[..., None, None] + jnp.einsum('bhcd,bhcv->bhdv', k_n, v_new)
        return state, o_n

    xs = (jnp.moveaxis(w, 2, 0), jnp.moveaxis(u, 2, 0), jnp.moveaxis(q_dec, 2, 0), jnp.moveaxis(k_dec, 2, 0),
          jnp.moveaxis(attn, 2, 0), jnp.moveaxis(g_last, 2, 0))
    _, o = lax.scan(step, jnp.zeros((b, h, dk, dv), q.dtype), xs)
    return from_chunks(jnp.moveaxis(o, 0, 2))


def hybrid_mixer(hn, w_in, conv_w, a_log, dt_bias, gdn_norm, w_branch, w_out, cos, sin):
    b, s, _ = hn.shape
    f32 = jnp.float32
    proj = hn @ w_in
    rq, rk, rv, rg, gqkv, gz, ga, gb, gate_a, gate_b = split_columns(proj, IN_SPLITS)

    rq = apply_rotary(rq.astype(f32).reshape(b, s, RET_HEADS, RET_DK), cos, sin)
    rk = apply_rotary(rk.astype(f32).reshape(b, s, RET_HEADS, RET_DK), cos, sin) * (RET_DK ** -0.5)
    rv = rv.astype(f32).reshape(b, s, RET_HEADS, RET_DV)
    ret = head_rms(retention_chunkwise(rq, rk, rv)).reshape(b, s, RET_V_W)
    ret = (jax.nn.silu(rg.astype(f32)) * ret).astype(hn.dtype)

    gqkv = jax.nn.silu(causal_depthwise_conv(gqkv, conv_w))
    dq, dk_, dv_ = split_columns(gqkv, (GDN_QK_W, GDN_QK_W, GDN_V_W))
    dq = l2_normalize(dq.astype(f32).reshape(b, s, GDN_HEADS, GDN_DK)) * (GDN_DK ** -0.5)
    dk_ = l2_normalize(dk_.astype(f32).reshape(b, s, GDN_HEADS, GDN_DK))
    dv_ = dv_.astype(f32).reshape(b, s, GDN_HEADS, GDN_DV)
    beta = jax.nn.sigmoid(gb.astype(f32))
    g = -jnp.exp(a_log.astype(f32)) * jax.nn.softplus(ga.astype(f32) + dt_bias.astype(f32))
    gdn = gated_delta_chunkwise(dq, dk_, dv_, g, beta)
    gdn = head_rms(gdn) * gdn_norm.astype(f32) * jax.nn.silu(gz.astype(f32).reshape(b, s, GDN_HEADS, GDN_DV))
    gdn = gdn.reshape(b, s, GDN_V_W).astype(hn.dtype)

    y_a = ret @ w_branch[0]
    y_b = gdn @ w_branch[1]
    merged = jax.nn.sigmoid(gate_a) * y_a + jax.nn.sigmoid(gate_b) * y_b
    return merged @ w_out


def moe_swiglu(hn, router, w_gate, w_up, w_down):
    b, s, d = hn.shape
    t = b * s
    a = t * TOP_K
    n_rows = a + N_EXPERTS * MOE_BLOCK
    n_blocks = n_rows // MOE_BLOCK
    xt = hn.reshape(t, d)
    logits = (xt @ router).astype(jnp.float32)
    top_vals, top_idx = lax.top_k(logits, TOP_K)
    gate = jax.nn.softmax(top_vals, axis=-1)
    flat_e = top_idx.reshape(a)
    order = jnp.argsort(flat_e)
    sorted_e = flat_e[order]
    sorted_tok = (order // TOP_K).astype(jnp.int32)
    sorted_w = gate.reshape(a)[order]
    counts = jnp.zeros((N_EXPERTS,), jnp.int32).at[flat_e].add(1)
    padded = ((counts + MOE_BLOCK - 1) // MOE_BLOCK) * MOE_BLOCK
    start = jnp.cumsum(counts) - counts
    pend = jnp.cumsum(padded)
    pstart = pend - padded
    dest = pstart[sorted_e] + (jnp.arange(a, dtype=jnp.int32) - start[sorted_e])
    row_tok = jnp.full((n_rows,), t, jnp.int32).at[dest].set(sorted_tok)
    row_w = jnp.zeros((n_rows,), jnp.float32).at[dest].set(sorted_w)
    block_start = jnp.arange(n_blocks, dtype=jnp.int32) * MOE_BLOCK
    block_e = jnp.clip(jnp.searchsorted(pend, block_start, side='right'), 0, N_EXPERTS - 1)
    x_pad = jnp.concatenate([xt, jnp.zeros((1, d), xt.dtype)], axis=0)
    xg = x_pad[row_tok].reshape(n_blocks, MOE_BLOCK, d)

    def expert_block(args):
        xb, e = args
        return swiglu(xb, w_gate[e], w_up[e], w_down[e])

    yg = lax.map(expert_block, (xg, block_e)).reshape(n_rows, d)
    out = jax.ops.segment_sum(yg.astype(jnp.float32) * row_w[:, None], row_tok, num_segments=t + 1)[:t]
    return out.astype(hn.dtype).reshape(b, s, d)


def setup_inputs(seed: int = 0) -> dict:
    key = jax.random.key(seed)
    ks = jax.random.split(key, 20)
    f32 = jnp.float32

    def nrm(k, shape, fan_in):
        return jax.random.normal(k, shape, f32) * (fan_in ** -0.5)

    def gain(k, shape):
        return 1.0 + 0.02 * jax.random.normal(k, shape, f32)

    x = jax.random.normal(ks[0], (BATCH, SEQ, D_MODEL), f32)
    norm_mix = gain(ks[1], (DEPTH, D_MODEL))
    w_in = nrm(ks[2], (DEPTH, D_MODEL, IN_COLS), D_MODEL)
    conv_w = nrm(ks[3], (DEPTH, CONV_K, CONV_CH), CONV_K)
    a_log = jnp.log(jax.random.uniform(ks[4], (DEPTH, GDN_HEADS), f32, minval=1.0, maxval=16.0))
    dt = jnp.exp(jax.random.uniform(ks[5], (DEPTH, GDN_HEADS), f32, minval=math.log(1e-3), maxval=math.log(1e-1)))
    dt_bias = dt + jnp.log(-jnp.expm1(-dt))
    gdn_norm = gain(ks[6], (DEPTH, GDN_DV))
    w_branch = nrm(ks[7], (DEPTH, N_BRANCH, BRANCH_W, D_MODEL), BRANCH_W)
    w_out = nrm(ks[8], (DEPTH, D_MODEL, D_MODEL), D_MODEL)
    norm_ffn = gain(ks[9], (DEPTH, D_MODEL))
    dense_w_gate = nrm(ks[10], (N_DENSE, D_MODEL, D_FF), D_MODEL)
    dense_w_up = nrm(ks[11], (N_DENSE, D_MODEL, D_FF), D_MODEL)
    dense_w_down = nrm(ks[12], (N_DENSE, D_FF, D_MODEL), D_FF)
    router = nrm(ks[13], (N_MOE, D_MODEL, N_EXPERTS), D_MODEL)
    moe_w_gate = nrm(ks[14], (N_MOE, N_EXPERTS, D_MODEL, D_FF), D_MODEL)
    moe_w_up = nrm(ks[15], (N_MOE, N_EXPERTS, D_MODEL, D_FF), D_MODEL)
    moe_w_down = nrm(ks[16], (N_MOE, N_EXPERTS, D_FF, D_MODEL), D_FF)
    final_norm = gain(ks[17], (D_MODEL,))
    return {'x': x, 'norm_mix': norm_mix, 'w_in': w_in, 'conv_w': conv_w, 'a_log': a_log, 'dt_bias': dt_bias,
            'gdn_norm': gdn_norm, 'w_branch': w_branch, 'w_out': w_out, 'norm_ffn': norm_ffn,
            'dense_w_gate': dense_w_gate, 'dense_w_up': dense_w_up, 'dense_w_down': dense_w_down,
            'router': router, 'moe_w_gate': moe_w_gate, 'moe_w_up': moe_w_up, 'moe_w_down': moe_w_down,
            'final_norm': final_norm}


def reference(x, norm_mix, w_in, conv_w, a_log, dt_bias, gdn_norm, w_branch, w_out, norm_ffn,
              dense_w_gate, dense_w_up, dense_w_down, router, moe_w_gate, moe_w_up, moe_w_down, final_norm):
    cos, sin = rotary_tables(x.shape[1])
    h = x
    for layer in range(DEPTH):
        hn = rms_norm(h, norm_mix[layer])
        h = h + hybrid_mixer(hn, w_in[layer], conv_w[layer], a_log[layer], dt_bias[layer], gdn_norm[layer],
                             w_branch[layer], w_out[layer], cos, sin)
        hn = rms_norm(h, norm_ffn[layer])
        i = layer // 2
        if layer % 2 == 0:
            h = h + swiglu(hn, dense_w_gate[i], dense_w_up[i], dense_w_down[i])
        else:
            h = h + moe_swiglu(hn, router[i], moe_w_gate[i], moe_w_up[i], moe_w_down[i])
    return rms_norm(h, final_norm)
```

```python
import functools
import math

import jax
import jax.numpy as jnp
import numpy as np
from jax import lax
from jax.experimental import pallas as pl
from jax.experimental.pallas import tpu as pltpu

F32 = jnp.float32
BF16 = jnp.bfloat16

D_MODEL = 1024
RET_HEADS = 4
GDN_HEADS = 4
HEAD_DIM = 128
HEADS_W = 512
CONV_K = 4
CONV_CH = 3 * HEADS_W
GDN_CHUNK = 64
ROPE_BASE = 10000.0
D_FF = 2816
N_EXPERTS = 8
TOP_K = 2
EPS = 1e-6
LANES = 128
SMALL_ROWS = 16

MAIN_COLS = CONV_CH + 5 * HEADS_W + 2 * D_MODEL
VMEM_LIMIT = 56 * 1024 * 1024

HIGHEST = lax.Precision.HIGHEST


def _bf(x):
    return x.astype(BF16)


def _dot(a, b):
    return jnp.dot(a, b, preferred_element_type=F32)


def _dot_nt(a, b):
    return lax.dot_general(a, b, (((1,), (1,)), ((), ())), preferred_element_type=F32)


def _dot_tn(a, b):
    return lax.dot_general(a, b, (((0,), (0,)), ((), ())), preferred_element_type=F32)


def _dot_f32(a, b):
    return jnp.dot(a, b, preferred_element_type=F32, precision=HIGHEST)


def _sigmoid(x):
    return 1.0 / (1.0 + jnp.exp(-x))


def _silu(x):
    return x * _sigmoid(x)


def _softplus(x):
    return jnp.maximum(x, 0.0) + jnp.log(1.0 + jnp.exp(-jnp.abs(x)))


def _rms_rows(x, w):
    ms = jnp.mean(x * x, axis=-1, keepdims=True)
    return x * lax.rsqrt(ms + EPS) * w


def _params(*sem):
    return pltpu.CompilerParams(dimension_semantics=sem, vmem_limit_bytes=VMEM_LIMIT)


def _inproj_kernel(x_ref, nw_ref, w_ref, ws_ref, wst_ref, o_ref, os_ref, ost_ref, hn_ref):
    @pl.when(pl.program_id(1) == 0)
    def _():
        hn = _bf(_rms_rows(x_ref[...], nw_ref[...]))
        hn_ref[...] = hn
        os_ref[...] = _dot(hn, ws_ref[...])
        ost_ref[...] = _dot_nt(wst_ref[...], hn)

    o_ref[...] = _dot(hn_ref[...], w_ref[...])


def _in_projection(h, norm_w, w_main, w_small, w_small_t, tm, tn):
    t = h.shape[0]
    return pl.pallas_call(
        _inproj_kernel,
        grid=(t // tm, MAIN_COLS // tn),
        in_specs=[
            pl.BlockSpec((tm, D_MODEL), lambda i, j: (i, 0)),
            pl.BlockSpec((1, D_MODEL), lambda i, j: (0, 0)),
            pl.BlockSpec((D_MODEL, tn), lambda i, j: (0, j)),
            pl.BlockSpec((D_MODEL, LANES), lambda i, j: (0, 0)),
            pl.BlockSpec((SMALL_ROWS, D_MODEL), lambda i, j: (0, 0)),
        ],
        out_specs=[
            pl.BlockSpec((tm, tn), lambda i, j: (i, j)),
            pl.BlockSpec((tm, LANES), lambda i, j: (i, 0)),
            pl.BlockSpec((SMALL_ROWS, tm), lambda i, j: (0, i)),
        ],
        out_shape=[
            jax.ShapeDtypeStruct((t, MAIN_COLS), F32),
            jax.ShapeDtypeStruct((t, LANES), F32),
            jax.ShapeDtypeStruct((SMALL_ROWS, t), F32),
        ],
        scratch_shapes=[pltpu.VMEM((tm, D_MODEL), BF16)],
        compiler_params=_params("parallel", "arbitrary"),
        name="in_projection",
    )(h, norm_w, w_main, w_small, w_small_t)


def _unit_lower_inverse(a, level_masks, eye):
    x = eye
    for m in level_masks:
        lb = jnp.where(m, a, 0.0)
        x = x - _dot_f32(_dot_f32(x, lb), x)
    return x


def _mixer_kernel(gqkv_ref, rq_ref, rk_ref, rv_ref, rg_ref, gz_ref, gab_ref, gabt_ref,
                  cos_ref, sin_ref, dmat_ref, qdec_ref, kdec_ref, cdec_ref,
                  convw_ref, prow_ref, pcol_ref, gnorm_ref, tril_ref, triu_ref,
                  ret_ref, gdn_ref,
                  rstate_ref, gstate_ref, xs_ref, *, blk):
    c = GDN_CHUNK
    n_chunks = blk // c

    @pl.when(pl.program_id(1) == 0)
    def _():
        rstate_ref[...] = jnp.zeros_like(rstate_ref)
        gstate_ref[...] = jnp.zeros_like(gstate_ref)
        xs_ref[0:8, :] = jnp.zeros((8, CONV_CH), F32)

    cos2 = cos_ref[...]
    sin2 = sin_ref[...]
    for h in range(RET_HEADS):
        sl = slice(h * HEAD_DIM, (h + 1) * HEAD_DIM)
        q = rq_ref[0, :, sl]
        k = rk_ref[0, :, sl]
        v = _bf(rv_ref[0, :, sl])
        q = q * cos2 + pltpu.roll(q, HEAD_DIM // 2, 1) * sin2
        k = (k * cos2 + pltpu.roll(k, HEAD_DIM // 2, 1) * sin2) * (HEAD_DIM ** -0.5)
        state = rstate_ref[h]
        scores = _dot_nt(_bf(q), _bf(k)) * dmat_ref[h]
        o = _dot(_bf(scores), v) + _dot(_bf(q * qdec_ref[h]), _bf(state))
        rstate_ref[h] = state * cdec_ref[h] + _dot_tn(_bf(k * kdec_ref[h]), v)
        o = o * lax.rsqrt(jnp.mean(o * o, axis=-1, keepdims=True) + EPS)
        ret_ref[0, :, sl] = _bf(_silu(rg_ref[0, :, sl]) * o)

    xs_ref[8:8 + blk, :] = gqkv_ref[0]
    conv = convw_ref[3:4, :] * xs_ref[pl.ds(8, blk), :]
    for s in range(1, CONV_K):
        conv = conv + convw_ref[3 - s:4 - s, :] * xs_ref[pl.ds(8 - s, blk), :]
    xs_ref[0:8, :] = xs_ref[blk:blk + 8, :]
    conv = _silu(conv)

    gab = gab_ref[0]
    g_cols = -jnp.exp(prow_ref[0:1, :]) * _softplus(gab + prow_ref[1:2, :])
    beta_cols = _sigmoid(gab)
    gc_cols = _dot_f32(tril_ref[...], g_cols)
    egc_cols = jnp.exp(gc_cols)
    g_rows = -jnp.exp(pcol_ref[:, 0:1]) * _softplus(gabt_ref[...] + pcol_ref[:, 1:2])
    gc_rows = _dot_f32(g_rows, triu_ref[...])

    ri = lax.broadcasted_iota(jnp.int32, (c, c), 0)
    ci = lax.broadcasted_iota(jnp.int32, (c, c), 1)
    ge = ri >= ci
    gt = ri > ci
    eye = jnp.where(ri == ci, 1.0, 0.0).astype(F32)
    level_masks = []
    for lg in range(int(math.log2(c))):
        b = 1 << lg
        same_block = (ri >> (lg + 1)) == (ci >> (lg + 1))
        level_masks.append(same_block & ((ri & (2 * b - 1)) >= b) & ((ci & (2 * b - 1)) < b))

    gnorm = gnorm_ref[...]
    for h in range(GDN_HEADS):
        qsl = slice(h * HEAD_DIM, (h + 1) * HEAD_DIM)
        ksl = slice(HEADS_W + h * HEAD_DIM, HEADS_W + (h + 1) * HEAD_DIM)
        vsl = slice(2 * HEADS_W + h * HEAD_DIM, 2 * HEADS_W + (h + 1) * HEAD_DIM)
        qh = conv[:, qsl]
        kh = conv[:, ksl]
        vh = conv[:, vsl]
        qh = qh * lax.rsqrt(jnp.sum(qh * qh, axis=-1, keepdims=True) + EPS) * (HEAD_DIM ** -0.5)
        kh = kh * lax.rsqrt(jnp.sum(kh * kh, axis=-1, keepdims=True) + EPS)
        state = gstate_ref[h]
        outs = []
        for n in range(n_chunks):
            rows = slice(n * c, (n + 1) * c)
            qc, kc, vc = qh[rows], kh[rows], vh[rows]
            bcol = beta_cols[rows, 4 + h:5 + h]
            gcol = gc_cols[rows, h:h + 1]
            egcol = egc_cols[rows, h:h + 1]
            grow = gc_rows[h:h + 1, rows]
            glast = gc_cols[n * c + c - 1:(n + 1) * c, h:h + 1]
            decay = jnp.where(ge, jnp.exp(jnp.where(ge, gcol - grow, 0.0)), 0.0)
            kb = kc * bcol
            kcb = _bf(kc)
            a_mat = jnp.where(gt, _dot_nt(_bf(kb), kcb) * decay, 0.0)
            t_inv = _unit_lower_inverse(a_mat, level_masks, eye)
            u = _dot_f32(t_inv, vc * bcol)
            w = _dot_f32(t_inv, kb * egcol)
            attn = jnp.where(ge, _dot_nt(_bf(qc), kcb) * decay, 0.0)
            qd = qc * egcol
            kd = kc * jnp.exp(glast - gcol)
            sb = _bf(state)
            v_new = u - _dot(_bf(w), sb)
            vnb = _bf(v_new)
            outs.append(_dot(_bf(qd), sb) + _dot(_bf(attn), vnb))
            state = state * jnp.exp(glast) + _dot_tn(_bf(kd), vnb)
        gstate_ref[h] = state
        o = jnp.concatenate(outs, axis=0)
        o = o * lax.rsqrt(jnp.mean(o * o, axis=-1, keepdims=True) + EPS)
        gdn_ref[0, :, qsl] = _bf(o * gnorm * _silu(gz_ref[0, :, qsl]))


def _token_mixers(proj, gab, gabt, tabs, conv_w, prow, pcol, gnorm, batch, seq, blk):
    proj3 = proj.reshape(batch, seq, MAIN_COLS)
    gab3 = gab.reshape(batch, seq, LANES)
    n_blk = seq // blk
    cos2, sin2, dmat, qdec, kdec, cdec, tril, triu = tabs
    w512 = lambda col: pl.BlockSpec((1, blk, HEADS_W), lambda b, n, col=col: (b, n, col))
    full = lambda shape: pl.BlockSpec(shape, lambda b, n: (0,) * len(shape))
    out = pl.pallas_call(
        functools.partial(_mixer_kernel, blk=blk),
        grid=(batch, seq // blk),
        in_specs=[
            pl.BlockSpec((1, blk, CONV_CH), lambda b, n: (b, n, 0)),
            w512(3), w512(4), w512(5), w512(6), w512(7),
            pl.BlockSpec((1, blk, LANES), lambda b, n: (b, n, 0)),
            pl.BlockSpec((SMALL_ROWS, blk), lambda b, n: (0, b * n_blk + n)),
            pl.BlockSpec((blk, HEAD_DIM), lambda b, n: (n, 0)),
            pl.BlockSpec((blk, HEAD_DIM), lambda b, n: (n, 0)),
            full((RET_HEADS, blk, blk)),
            full((RET_HEADS, blk, HEAD_DIM)),
            full((RET_HEADS, blk, HEAD_DIM)),
            full((RET_HEADS, 1, HEAD_DIM)),
            full((8, CONV_CH)),
            full((8, LANES)),
            full((SMALL_ROWS, LANES)),
            full((1, HEAD_DIM)),
            full((blk, blk)),
            full((blk, blk)),
        ],
        out_specs=[
            pl.BlockSpec((1, blk, HEADS_W), lambda b, n: (b, n, 0)),
            pl.BlockSpec((1, blk, HEADS_W), lambda b, n: (b, n, 0)),
        ],
        out_shape=[
            jax.ShapeDtypeStruct((batch, seq, HEADS_W), BF16),
            jax.ShapeDtypeStruct((batch, seq, HEADS_W), BF16),
        ],
        scratch_shapes=[
            pltpu.VMEM((RET_HEADS, HEAD_DIM, HEAD_DIM), F32),
            pltpu.VMEM((GDN_HEADS, HEAD_DIM, HEAD_DIM), F32),
            pltpu.VMEM((blk + 8, CONV_CH), F32),
        ],
        compiler_params=_params("parallel", "arbitrary"),
        name="token_mixers",
    )(proj3, proj3, proj3, proj3, proj3, proj3, gab3, gabt,
      cos2, sin2, dmat, qdec, kdec, cdec, conv_w, prow, pcol, gnorm, tril, triu)
    ret, gdn = out
    return ret.reshape(batch * seq, HEADS_W), gdn.reshape(batch * seq, HEADS_W)


def _mixer_tables(seq, blk):
    inv_freq = ROPE_BASE ** (-jnp.arange(0, HEAD_DIM, 2, dtype=F32) / HEAD_DIM)
    ang = jnp.arange(seq, dtype=F32)[:, None] * inv_freq[None, :]
    cos, sin = jnp.cos(ang), jnp.sin(ang)
    cos2 = jnp.concatenate([cos, cos], axis=-1)
    sin2 = jnp.concatenate([-sin, sin], axis=-1)
    log_gamma = jnp.log(1.0 - jnp.exp2(-5.0 - jnp.arange(RET_HEADS, dtype=F32)))
    pos = jnp.arange(blk, dtype=F32)
    diff = pos[:, None] - pos[None, :]
    dmat = jnp.where(diff >= 0, jnp.exp(jnp.maximum(diff, 0.0)[None] * log_gamma[:, None, None]), 0.0)
    ones = jnp.ones((1, 1, HEAD_DIM), F32)
    qdec = jnp.exp((pos + 1.0)[None, :] * log_gamma[:, None])[:, :, None] * ones
    kdec = jnp.exp((blk - 1 - pos)[None, :] * log_gamma[:, None])[:, :, None] * ones
    cdec = jnp.exp(blk * log_gamma)[:, None, None] * ones
    idx = np.arange(blk)
    same = (idx[:, None] // GDN_CHUNK) == (idx[None, :] // GDN_CHUNK)
    tril = jnp.asarray((same & (idx[:, None] >= idx[None, :])).astype(np.float32))
    triu = jnp.asarray((same & (idx[:, None] <= idx[None, :])).astype(np.float32))
    return cos2, sin2, dmat, qdec, kdec, cdec, tril, triu


def _merge_kernel(ret_ref, gdn_ref, ga_ref, gb_ref, h_ref, wa_ref, wb_ref, wo_ref, nw_ref, *rest,
                  with_router):
    if with_router:
        wr_ref, ho_ref, logit_ref = rest
    else:
        ho_ref, hn_ref = rest
    ya = _dot(ret_ref[...], wa_ref[...])
    yb = _dot(gdn_ref[...], wb_ref[...])
    merged = _sigmoid(ga_ref[...]) * ya + _sigmoid(gb_ref[...]) * yb
    h_new = h_ref[...] + _dot(_bf(merged), wo_ref[...])
    ho_ref[...] = h_new
    hn = _bf(_rms_rows(h_new, nw_ref[...]))
    if with_router:
        logit_ref[...] = _dot(hn, wr_ref[...])
    else:
        hn_ref[...] = hn


def _merge_project(ret, gdn, proj, h, wa, wb, wo, norm_w, router_w, tm):
    t = h.shape[0]
    with_router = router_w is not None
    in_specs = [
        pl.BlockSpec((tm, HEADS_W), lambda i: (i, 0)),
        pl.BlockSpec((tm, HEADS_W), lambda i: (i, 0)),
        pl.BlockSpec((tm, D_MODEL), lambda i: (i, 4)),
        pl.BlockSpec((tm, D_MODEL), lambda i: (i, 5)),
        pl.BlockSpec((tm, D_MODEL), lambda i: (i, 0)),
        pl.BlockSpec((HEADS_W, D_MODEL), lambda i: (0, 0)),
        pl.BlockSpec((HEADS_W, D_MODEL), lambda i: (0, 0)),
        pl.BlockSpec((D_MODEL, D_MODEL), lambda i: (0, 0)),
        pl.BlockSpec((1, D_MODEL), lambda i: (0, 0)),
    ]
    args = [ret, gdn, proj, proj, h, wa, wb, wo, norm_w]
    if with_router:
        in_specs.append(pl.BlockSpec((D_MODEL, LANES), lambda i: (0, 0)))
        args.append(router_w)
        second = (pl.BlockSpec((tm, LANES), lambda i: (i, 0)), jax.ShapeDtypeStruct((t, LANES), F32))
    else:
        second = (pl.BlockSpec((tm, D_MODEL), lambda i: (i, 0)), jax.ShapeDtypeStruct((t, D_MODEL), BF16))
    return pl.pallas_call(
        functools.partial(_merge_kernel, with_router=with_router),
        grid=(t // tm,),
        in_specs=in_specs,
        out_specs=[pl.BlockSpec((tm, D_MODEL), lambda i: (i, 0)), second[0]],
        out_shape=[jax.ShapeDtypeStruct((t, D_MODEL), F32), second[1]],
        compiler_params=_params("parallel"),
        name="merge_project",
    )(*args)


def _ffn_kernel(hn_ref, h_ref, wg_ref, wu_ref, wd_ref, o_ref, acc_ref):
    f = pl.program_id(1)

    @pl.when(f == 0)
    def _():
        acc_ref[...] = h_ref[...]

    x = hn_ref[...]
    act = _silu(_dot(x, wg_ref[0])) * _dot(x, wu_ref[0])
    acc_ref[...] += _dot(_bf(act), wd_ref[...])

    @pl.when(f == pl.num_programs(1) - 1)
    def _():
        o_ref[...] = acc_ref[...]


def _dense_ffn(hn, h, wg, wu, wd, tm, tf):
    t = h.shape[0]
    n_f = D_FF // tf
    return pl.pallas_call(
        _ffn_kernel,
        grid=(t // tm, n_f),
        in_specs=[
            pl.BlockSpec((tm, D_MODEL), lambda i, f: (i, 0)),
            pl.BlockSpec((tm, D_MODEL), lambda i, f: (i, 0)),
            pl.BlockSpec((1, D_MODEL, tf), lambda i, f: (f, 0, 0)),
            pl.BlockSpec((1, D_MODEL, tf), lambda i, f: (f, 0, 0)),
            pl.BlockSpec((tf, D_MODEL), lambda i, f: (f, 0)),
        ],
        out_specs=pl.BlockSpec((tm, D_MODEL), lambda i, f: (i, 0)),
        out_shape=jax.ShapeDtypeStruct((t, D_MODEL), F32),
        scratch_shapes=[pltpu.VMEM((tm, D_MODEL), F32)],
        compiler_params=_params("parallel", "arbitrary"),
        name="dense_ffn",
    )(hn, h, wg, wu, wd)


def _column_slabs(w, tf):
    d, n = w.shape
    return _bf(w).reshape(d, n // tf, tf).transpose(1, 0, 2)


def _route_kernel(logit_ref, tri_ref, meta_ref, count_ref, run_ref):
    @pl.when(pl.program_id(0) == 0)
    def _():
        run_ref[...] = jnp.zeros_like(run_ref)

    tm = logit_ref.shape[0]
    lane = lax.broadcasted_iota(jnp.int32, (tm, LANES), 1)
    neg = jnp.float32(-jnp.inf)
    l1 = jnp.where(lane < N_EXPERTS, logit_ref[...], neg)
    m1 = jnp.max(l1, axis=-1, keepdims=True)
    i1 = jnp.min(jnp.where(l1 == m1, lane, LANES), axis=-1, keepdims=True)
    l2 = jnp.where(lane == i1, neg, l1)
    m2 = jnp.max(l2, axis=-1, keepdims=True)
    i2 = jnp.min(jnp.where(l2 == m2, lane, LANES), axis=-1, keepdims=True)
    e2 = jnp.exp(m2 - m1)
    g1 = 1.0 / (1.0 + e2)
    g2 = e2 / (1.0 + e2)
    sel1 = lane == i1
    sel2 = lane == i2
    onehot = jnp.where(sel1 | sel2, 1.0, 0.0).astype(F32)
    before = _dot(tri_ref[...], _bf(onehot)) + run_ref[0:1, :]
    r1 = jnp.sum(jnp.where(sel1, before, 0.0), axis=-1, keepdims=True)
    r2 = jnp.sum(jnp.where(sel2, before, 0.0), axis=-1, keepdims=True)
    run_new = run_ref[0:1, :] + jnp.sum(onehot, axis=0, keepdims=True)
    run_ref[0:1, :] = run_new
    count_ref[...] = jnp.broadcast_to(run_new, count_ref.shape)
    meta = jnp.where(lane == 0, i1.astype(F32), 0.0)
    meta = jnp.where(lane == 1, i2.astype(F32), meta)
    meta = jnp.where(lane == 2, r1, meta)
    meta = jnp.where(lane == 3, r2, meta)
    meta = jnp.where(lane == 4, g1, meta)
    meta = jnp.where(lane == 5, g2, meta)
    meta_ref[...] = meta


def _route(logits, tm):
    t = logits.shape[0]
    idx = np.arange(tm)
    tri = jnp.asarray((idx[:, None] > idx[None, :]).astype(np.float32), dtype=BF16)
    return pl.pallas_call(
        _route_kernel,
        grid=(t // tm,),
        in_specs=[pl.BlockSpec((tm, LANES), lambda i: (i, 0)),
                  pl.BlockSpec((tm, tm), lambda i: (0, 0))],
        out_specs=[pl.BlockSpec((tm, LANES), lambda i: (i, 0)),
                   pl.BlockSpec((8, LANES), lambda i: (0, 0))],
        out_shape=[jax.ShapeDtypeStruct((t, LANES), F32),
                   jax.ShapeDtypeStruct((8, LANES), F32)],
        scratch_shapes=[pltpu.VMEM((8, LANES), F32)],
        compiler_params=_params("arbitrary"),
        name="route",
    )(logits, tri)


def _row_wait(src_hbm, dst_vmem, sem, rows):
    pltpu.make_async_copy(src_hbm.at[pl.ds(0, rows)], dst_vmem, sem).wait()


def _dispatch_kernel(dest_ref, h_ref, nw_ref, xg_in_ref, xg_ref, hn_ref, sem):
    del xg_in_ref
    tm = h_ref.shape[0]
    hn_ref[...] = _rms_rows(h_ref[...], nw_ref[...])

    def body(r, carry):
        for j in range(TOP_K):
            d = dest_ref[TOP_K * r + j]
            pltpu.make_async_copy(hn_ref.at[pl.ds(r, 1)], xg_ref.at[pl.ds(d, 1)], sem).start()
        return carry

    lax.fori_loop(0, tm, body, 0)
    for _ in range(TOP_K):
        pltpu.make_async_copy(hn_ref, xg_ref.at[pl.ds(0, tm)], sem).wait()


def _dispatch(dest_flat, h, norm_w, n_rows, tm):
    t = h.shape[0]
    xg0 = jnp.zeros((n_rows, D_MODEL), F32)
    return pl.pallas_call(
        _dispatch_kernel,
        grid=(t // tm,),
        in_specs=[
            pl.BlockSpec((TOP_K * tm,), lambda i: (i,), memory_space=pltpu.SMEM),
            pl.BlockSpec((tm, D_MODEL), lambda i: (i, 0)),
            pl.BlockSpec((1, D_MODEL), lambda i: (0, 0)),
            pl.BlockSpec(memory_space=pl.ANY),
        ],
        out_specs=pl.BlockSpec(memory_space=pl.ANY),
        out_shape=jax.ShapeDtypeStruct((n_rows, D_MODEL), F32),
        scratch_shapes=[pltpu.VMEM((tm, D_MODEL), F32), pltpu.SemaphoreType.DMA(())],
        input_output_aliases={3: 0},
        compiler_params=_params("arbitrary"),
        name="dispatch",
    )(dest_flat, h, norm_w, xg0)


def _expert_kernel(be_ref, x_ref, wg_ref, wu_ref, wd_ref, y_ref, *, tf):
    del be_ref
    x = _bf(x_ref[...])
    acc = jnp.zeros(y_ref.shape, F32)
    for f in range(D_FF // tf):
        cols = slice(f * tf, (f + 1) * tf)
        act = _silu(_dot(x, wg_ref[0, :, cols])) * _dot(x, wu_ref[0, :, cols])
        acc = acc + _dot(_bf(act), wd_ref[0, cols, :])
    y_ref[...] = acc


def _experts(block_e, xg, wg, wu, wd, rows, tf):
    n_rows = xg.shape[0]
    grid_spec = pltpu.PrefetchScalarGridSpec(
        num_scalar_prefetch=1,
        grid=(n_rows // rows,),
        in_specs=[
            pl.BlockSpec((rows, D_MODEL), lambda i, be: (i, 0)),
            pl.BlockSpec((1, D_MODEL, D_FF), lambda i, be: (be[i], 0, 0)),
            pl.BlockSpec((1, D_MODEL, D_FF), lambda i, be: (be[i], 0, 0)),
            pl.BlockSpec((1, D_FF, D_MODEL), lambda i, be: (be[i], 0, 0)),
        ],
        out_specs=pl.BlockSpec((rows, D_MODEL), lambda i, be: (i, 0)),
    )
    return pl.pallas_call(
        functools.partial(_expert_kernel, tf=tf),
        grid_spec=grid_spec,
        out_shape=jax.ShapeDtypeStruct((n_rows, D_MODEL), F32),
        compiler_params=_params("arbitrary"),
        name="experts",
    )(block_e, xg, wg, wu, wd)


def _combine_kernel(dest_ref, meta_ref, h_ref, nw_ref, y_ref, o_ref, buf_ref, sem):
    tm = h_ref.shape[0]

    def body(r, carry):
        for j in range(TOP_K):
            d = dest_ref[TOP_K * r + j]
            pltpu.make_async_copy(y_ref.at[pl.ds(d, 1)], buf_ref.at[j, pl.ds(r, 1)], sem).start()
        return carry

    lax.fori_loop(0, tm, body, 0)
    for j in range(TOP_K):
        _row_wait(y_ref, buf_ref.at[j], sem, tm)
    meta = meta_ref[...]
    moe = meta[:, 4:5] * buf_ref[0] + meta[:, 5:6] * buf_ref[1]
    o_ref[...] = _rms_rows(h_ref[...] + moe, nw_ref[...])


def _combine(dest_flat, meta, h, norm_w, y, tm):
    t = h.shape[0]
    return pl.pallas_call(
        _combine_kernel,
        grid=(t // tm,),
        in_specs=[
            pl.BlockSpec((TOP_K * tm,), lambda i: (i,), memory_space=pltpu.SMEM),
            pl.BlockSpec((tm, LANES), lambda i: (i, 0)),
            pl.BlockSpec((tm, D_MODEL), lambda i: (i, 0)),
            pl.BlockSpec((1, D_MODEL), lambda i: (0, 0)),
            pl.BlockSpec(memory_space=pl.ANY),
        ],
        out_specs=pl.BlockSpec((tm, D_MODEL), lambda i: (i, 0)),
        out_shape=jax.ShapeDtypeStruct((t, D_MODEL), F32),
        scratch_shapes=[pltpu.VMEM((TOP_K, tm, D_MODEL), F32), pltpu.SemaphoreType.DMA(())],
        compiler_params=_params("arbitrary"),
        name="combine",
    )(dest_flat, meta, h, norm_w, y)


def _moe_layer(logits, h, ffn_norm_w, final_norm_w, wg, wu, wd, rows, tm_route, tm_rows, tf):
    t = h.shape[0]
    meta, counts = _route(logits, tm_route)
    counts = counts[0, :N_EXPERTS].astype(jnp.int32)
    padded = ((counts + rows - 1) // rows) * rows
    pend = jnp.cumsum(padded)
    pstart = pend - padded
    n_rows = t * TOP_K + N_EXPERTS * rows
    expert = meta[:, 0:TOP_K].astype(jnp.int32)
    rank = meta[:, TOP_K:2 * TOP_K].astype(jnp.int32)
    dest = (pstart[expert] + rank).reshape(t * TOP_K)
    block_start = jnp.arange(n_rows // rows, dtype=jnp.int32) * rows
    block_e = jnp.clip(jnp.searchsorted(pend, block_start, side="right"), 0, N_EXPERTS - 1).astype(jnp.int32)
    xg = _dispatch(dest, h, ffn_norm_w, n_rows, tm_rows)
    y = _experts(block_e, xg, wg, wu, wd, rows, tf)
    return _combine(dest, meta, h, final_norm_w, y, tm_rows)


def _pack_in_weights(w_in):
    o = np.cumsum((0, HEADS_W, HEADS_W, HEADS_W, HEADS_W, CONV_CH, HEADS_W, GDN_HEADS, GDN_HEADS, D_MODEL, D_MODEL))
    rq, rk, rv, rg, gqkv, gz, ga, gb, ma, mb = (w_in[:, o[i]:o[i + 1]] for i in range(10))
    main = _bf(jnp.concatenate([gqkv, rq, rk, rv, rg, gz, ma, mb], axis=1))
    small = jnp.concatenate([ga, gb], axis=1)
    small_cols = _bf(jnp.pad(small, ((0, 0), (0, LANES - 2 * GDN_HEADS))))
    small_rows = _bf(jnp.pad(small.T, ((0, SMALL_ROWS - 2 * GDN_HEADS), (0, 0))))
    return main, small_cols, small_rows


def _pick(n, prefs):
    for p in prefs:
        if n % p == 0:
            return p
    raise ValueError(f"no tile in {prefs} divides {n}")


def kernel(x, norm_mix, w_in, conv_w, a_log, dt_bias, gdn_norm, w_branch, w_out, norm_ffn,
           dense_w_gate, dense_w_up, dense_w_down, router, moe_w_gate, moe_w_up, moe_w_down, final_norm):
    batch, seq, d = x.shape
    depth = norm_mix.shape[0]
    assert d == D_MODEL and seq % GDN_CHUNK == 0
    t = batch * seq
    blk = _pick(seq, (256, 128, 64))
    tm = _pick(t, (1024, 512, 256))
    tm_rows = _pick(t, (512, 256))
    tf_dense = 256
    rows = 512
    tabs = _mixer_tables(seq, blk)

    h = x.reshape(t, d)
    out = None
    for layer in range(depth):
        w_main, w_small, w_small_t = _pack_in_weights(w_in[layer])
        proj, gab, gabt = _in_projection(h, norm_mix[layer][None, :], w_main, w_small, w_small_t, tm, 512)
        prow = jnp.zeros((8, LANES), F32)
        prow = prow.at[0, :GDN_HEADS].set(a_log[layer]).at[1, :GDN_HEADS].set(dt_bias[layer])
        pcol = jnp.zeros((SMALL_ROWS, LANES), F32)
        pcol = pcol.at[:GDN_HEADS, 0].set(a_log[layer]).at[:GDN_HEADS, 1].set(dt_bias[layer])
        convw = jnp.pad(conv_w[layer], ((0, 8 - CONV_K), (0, 0)))
        ret, gdn = _token_mixers(proj, gab, gabt, tabs, convw, prow, pcol, gdn_norm[layer][None, :],
                                 batch, seq, blk)
        i = layer // 2
        is_moe = layer % 2 == 1
        router_w = _bf(jnp.pad(router[i], ((0, 0), (0, LANES - N_EXPERTS)))) if is_moe else None
        h, second = _merge_project(ret, gdn, proj, h, _bf(w_branch[layer, 0]), _bf(w_branch[layer, 1]),
                                   _bf(w_out[layer]), norm_ffn[layer][None, :], router_w, tm_rows)
        if not is_moe:
            h = _dense_ffn(second, h, _column_slabs(dense_w_gate[i], tf_dense),
                           _column_slabs(dense_w_up[i], tf_dense), _bf(dense_w_down[i]), tm, tf_dense)
            if layer == depth - 1:
                raise NotImplementedError("final norm after a dense layer is not wired")
        else:
            if layer != depth - 1:
                raise NotImplementedError("routed layer must be last")
            out = _moe_layer(second, h, norm_ffn[layer][None, :], final_norm[None, :],
                             _bf(moe_w_gate[i]), _bf(moe_w_up[i]), _bf(moe_w_down[i]),
                             rows, tm_rows, tm_rows, 256)
    return out.reshape(batch, seq, d)
```

```python
import functools
import math

import jax
import jax.numpy as jnp
import numpy as np
from jax import lax
from jax.experimental import pallas as pl
from jax.experimental.pallas import tpu as pltpu

F32 = jnp.float32
BF16 = jnp.bfloat16

D_MODEL = 1024
RET_HEADS = 4
GDN_HEADS = 4
HEAD_DIM = 128
HEADS_W = 512
CONV_K = 4
CONV_CH = 3 * HEADS_W
GDN_CHUNK = 64
ROPE_BASE = 10000.0
D_FF = 2816
N_EXPERTS = 8
TOP_K = 2
EPS = 1e-6
LANES = 128
SMALL_ROWS = 16

MAIN_COLS = CONV_CH + 5 * HEADS_W + 2 * D_MODEL
VMEM_LIMIT = 56 * 1024 * 1024

HIGHEST = lax.Precision.HIGHEST


def _bf(x):
    return x.astype(BF16)


def _dot(a, b):
    return jnp.dot(a, b, preferred_element_type=F32)


def _dot_nt(a, b):
    return lax.dot_general(a, b, (((1,), (1,)), ((), ())), preferred_element_type=F32)


def _dot_tn(a, b):
    return lax.dot_general(a, b, (((0,), (0,)), ((), ())), preferred_element_type=F32)


def _dot_f32(a, b):
    return jnp.dot(a, b, preferred_element_type=F32, precision=HIGHEST)


def _sigmoid(x):
    return 1.0 / (1.0 + jnp.exp(-x))


def _silu(x):
    return x * _sigmoid(x)


def _softplus(x):
    return jnp.maximum(x, 0.0) + jnp.log(1.0 + jnp.exp(-jnp.abs(x)))


def _rms_rows(x, w):
    ms = jnp.mean(x * x, axis=-1, keepdims=True)
    return x * lax.rsqrt(ms + EPS) * w


def _params(*sem):
    return pltpu.CompilerParams(dimension_semantics=sem, vmem_limit_bytes=VMEM_LIMIT)


def _inproj_kernel(x_ref, nw_ref, w_ref, ws_ref, wst_ref, o_ref, os_ref, ost_ref, hn_ref):
    @pl.when(pl.program_id(1) == 0)
    def _():
        hn = _bf(_rms_rows(x_ref[...], nw_ref[...]))
        hn_ref[...] = hn
        os_ref[...] = _dot(hn, ws_ref[...])
        ost_ref[...] = _dot_nt(wst_ref[...], hn)

    o_ref[...] = _dot(hn_ref[...], w_ref[...])


def _in_projection(h, norm_w, w_main, w_small, w_small_t, tm, tn):
    t = h.shape[0]
    return pl.pallas_call(
        _inproj_kernel,
        grid=(t // tm, MAIN_COLS // tn),
        in_specs=[
            pl.BlockSpec((tm, D_MODEL), lambda i, j: (i, 0)),
            pl.BlockSpec((1, D_MODEL), lambda i, j: (0, 0)),
            pl.BlockSpec((D_MODEL, tn), lambda i, j: (0, j)),
            pl.BlockSpec((D_MODEL, LANES), lambda i, j: (0, 0)),
            pl.BlockSpec((SMALL_ROWS, D_MODEL), lambda i, j: (0, 0)),
        ],
        out_specs=[
            pl.BlockSpec((tm, tn), lambda i, j: (i, j)),
            pl.BlockSpec((tm, LANES), lambda i, j: (i, 0)),
            pl.BlockSpec((SMALL_ROWS, tm), lambda i, j: (0, i)),
        ],
        out_shape=[
            jax.ShapeDtypeStruct((t, MAIN_COLS), F32),
            jax.ShapeDtypeStruct((t, LANES), F32),
            jax.ShapeDtypeStruct((SMALL_ROWS, t), F32),
        ],
        scratch_shapes=[pltpu.VMEM((tm, D_MODEL), BF16)],
        compiler_params=_params("parallel", "arbitrary"),
        name="in_projection",
    )(h, norm_w, w_main, w_small, w_small_t)


def _split_hi_lo(x):
    hi = _bf(x)
    return hi, _bf(x - hi.astype(F32))


def _dup_lhs(x2, left):
    hi = _bf(x2).astype(F32)
    w = _bf(jnp.where(left, x2, x2 - hi))
    return jnp.concatenate([w, w], axis=1)


def _dup_rhs(x):
    hi, lo = _split_hi_lo(x)
    return jnp.concatenate([hi, hi, lo, lo], axis=0)


def _mixer_kernel(gqkv_ref, rq_ref, rk_ref, rv_ref, rg_ref, gz_ref, gab_ref, gabt_ref,
                  cos_ref, sin_ref, dmat_ref, qdec_ref, kdec_ref, cdec_ref,
                  convw_ref, prow_ref, pcol_ref, gnorm_ref, tril_ref, triu_ref,
                  ret_ref, gdn_ref,
                  rstate_ref, gstate_ref, xs_ref, *, blk):
    c = GDN_CHUNK
    n_chunks = blk // c
    heads = range(GDN_HEADS)
    inst = [(h, n) for h in heads for n in range(n_chunks)]

    @pl.when(pl.program_id(1) == 0)
    def _():
        rstate_ref[...] = jnp.zeros_like(rstate_ref)
        gstate_ref[...] = jnp.zeros_like(gstate_ref)
        xs_ref[0:8, :] = jnp.zeros((8, CONV_CH), F32)

    hsl = [slice(h * HEAD_DIM, (h + 1) * HEAD_DIM) for h in range(RET_HEADS)]

    cos2 = cos_ref[...]
    sin2 = sin_ref[...]
    half = HEAD_DIM // 2
    rq = [rq_ref[0, :, sl] for sl in hsl]
    rk = [rk_ref[0, :, sl] for sl in hsl]
    rv = [_bf(rv_ref[0, :, sl]) for sl in hsl]
    rq = [q * cos2 + pltpu.roll(q, half, 1) * sin2 for q in rq]
    rk = [(k * cos2 + pltpu.roll(k, half, 1) * sin2) * (HEAD_DIM ** -0.5) for k in rk]
    rstate = [rstate_ref[h] for h in range(RET_HEADS)]
    scores = [_dot_nt(_bf(rq[h]), _bf(rk[h])) * dmat_ref[h] for h in range(RET_HEADS)]
    inter = [_dot(_bf(rq[h] * qdec_ref[h]), _bf(rstate[h])) for h in range(RET_HEADS)]
    kv = [_dot_tn(_bf(rk[h] * kdec_ref[h]), rv[h]) for h in range(RET_HEADS)]
    ro = [_dot(_bf(scores[h]), rv[h]) + inter[h] for h in range(RET_HEADS)]
    for h in range(RET_HEADS):
        rstate_ref[h] = rstate[h] * cdec_ref[h] + kv[h]
        o = ro[h]
        o = o * lax.rsqrt(jnp.mean(o * o, axis=-1, keepdims=True) + EPS)
        ret_ref[0, :, hsl[h]] = _bf(_silu(rg_ref[0, :, hsl[h]]) * o)

    xs_ref[8:8 + blk, :] = gqkv_ref[0]
    conv = convw_ref[3:4, :] * xs_ref[pl.ds(8, blk), :]
    for s in range(1, CONV_K):
        conv = conv + convw_ref[3 - s:4 - s, :] * xs_ref[pl.ds(8 - s, blk), :]
    xs_ref[0:8, :] = xs_ref[blk:blk + 8, :]
    conv = _silu(conv)

    gab = gab_ref[0]
    g_cols = -jnp.exp(prow_ref[0:1, :]) * _softplus(gab + prow_ref[1:2, :])
    beta_cols = _sigmoid(gab)
    gc_cols = _dot_f32(tril_ref[...], g_cols)
    egc_cols = jnp.exp(gc_cols)
    g_rows = -jnp.exp(pcol_ref[:, 0:1]) * _softplus(gabt_ref[...] + pcol_ref[:, 1:2])
    gc_rows = _dot_f32(g_rows, triu_ref[...])

    ri = lax.broadcasted_iota(jnp.int32, (c, 2 * c), 0)
    lane = lax.broadcasted_iota(jnp.int32, (c, 2 * c), 1)
    ci = lane & (c - 1)
    left = lane < c
    ge = ri >= ci
    gt = ri > ci
    eye = jnp.where(ri == ci, 1.0, 0.0).astype(F32)
    level_masks = []
    for lg in range(int(math.log2(c))):
        b = 1 << lg
        same_block = (ri >> (lg + 1)) == (ci >> (lg + 1))
        level_masks.append(same_block & ((ri & (2 * b - 1)) >= b) & ((ci & (2 * b - 1)) < b))

    qh, kh, vh = [], [], []
    for h in heads:
        q = conv[:, hsl[h]]
        k = conv[:, HEADS_W + h * HEAD_DIM:HEADS_W + (h + 1) * HEAD_DIM]
        vh.append(conv[:, 2 * HEADS_W + h * HEAD_DIM:2 * HEADS_W + (h + 1) * HEAD_DIM])
        qh.append(q * lax.rsqrt(jnp.sum(q * q, axis=-1, keepdims=True) + EPS) * (HEAD_DIM ** -0.5))
        kh.append(k * lax.rsqrt(jnp.sum(k * k, axis=-1, keepdims=True) + EPS))

    rows = [slice(n * c, (n + 1) * c) for n in range(n_chunks)]
    kc = {(h, n): kh[h][rows[n]] for h, n in inst}
    qc = {(h, n): qh[h][rows[n]] for h, n in inst}
    bcol = {(h, n): beta_cols[rows[n], 4 + h:5 + h] for h, n in inst}
    gcol = {(h, n): gc_cols[rows[n], h:h + 1] for h, n in inst}
    egcol = {(h, n): egc_cols[rows[n], h:h + 1] for h, n in inst}
    glast = {(h, n): gc_cols[n * c + c - 1:(n + 1) * c, h:h + 1] for h, n in inst}
    grow = {(h, n): gc_rows[h:h + 1, 2 * c * n:2 * c * (n + 1)] for h, n in inst}
    kb = {i: kc[i] * bcol[i] for i in inst}
    kcb = {i: _bf(kc[i]) for i in inst}
    kcb2 = {i: jnp.concatenate([kcb[i], kcb[i]], axis=0) for i in inst}
    decay = {i: jnp.where(ge, jnp.exp(jnp.where(ge, gcol[i] - grow[i], 0.0)), 0.0) for i in inst}
    a2 = {i: jnp.where(gt, _dot_nt(_bf(kb[i]), kcb2[i]) * decay[i], 0.0) for i in inst}
    attn = {i: _bf(jnp.where(ge, _dot_nt(_bf(qc[i]), kcb2[i]) * decay[i], 0.0)[:, :c]) for i in inst}

    x2 = {i: eye - jnp.where(level_masks[0], a2[i], 0.0) for i in inst}
    for m in level_masks[1:]:
        xl = {i: _dup_lhs(x2[i], left) for i in inst}
        xr = {i: _dup_rhs(x2[i]) for i in inst}
        t = {i: _dot(xl[i], _dup_rhs(jnp.where(m, a2[i], 0.0))) for i in inst}
        x2 = {i: x2[i] - _dot(_dup_lhs(t[i], left), xr[i]) for i in inst}

    uw = {}
    for h, n in inst:
        i = (h, n)
        rhs = jnp.concatenate([vh[h][rows[n]] * bcol[i], kb[i] * egcol[i]], axis=1)
        uw[i] = _dot(_dup_lhs(x2[i], left), _dup_rhs(rhs))
    wq = {i: _bf(jnp.concatenate([uw[i][:, HEAD_DIM:], qc[i] * egcol[i]], axis=0)) for i in inst}
    kd = {i: _bf(kc[i] * jnp.exp(glast[i] - gcol[i])) for i in inst}
    eg = {i: jnp.exp(glast[i]) for i in inst}

    state = [gstate_ref[h] for h in heads]
    outs = [[] for _ in heads]
    for n in range(n_chunks):
        sb = [_bf(state[h]) for h in heads]
        ws = [_dot(wq[(h, n)], sb[h]) for h in heads]
        vnb = [_bf(uw[(h, n)][:, :HEAD_DIM] - ws[h][:c]) for h in heads]
        for h in heads:
            outs[h].append(ws[h][c:] + _dot(attn[(h, n)], vnb[h]))
        state = [state[h] * eg[(h, n)] + _dot_tn(kd[(h, n)], vnb[h]) for h in heads]

    gnorm = gnorm_ref[...]
    for h in heads:
        gstate_ref[h] = state[h]
        o = jnp.concatenate(outs[h], axis=0)
        o = o * lax.rsqrt(jnp.mean(o * o, axis=-1, keepdims=True) + EPS)
        gdn_ref[0, :, hsl[h]] = _bf(o * gnorm * _silu(gz_ref[0, :, hsl[h]]))


def _token_mixers(proj, gab, gabt, tabs, conv_w, prow, pcol, gnorm, batch, seq, blk):
    proj3 = proj.reshape(batch, seq, MAIN_COLS)
    gab3 = gab.reshape(batch, seq, LANES)
    n_blk = seq // blk
    cos2, sin2, dmat, qdec, kdec, cdec, tril, triu = tabs
    w512 = lambda col: pl.BlockSpec((1, blk, HEADS_W), lambda b, n, col=col: (b, n, col))
    full = lambda shape: pl.BlockSpec(shape, lambda b, n: (0,) * len(shape))
    out = pl.pallas_call(
        functools.partial(_mixer_kernel, blk=blk),
        grid=(batch, seq // blk),
        in_specs=[
            pl.BlockSpec((1, blk, CONV_CH), lambda b, n: (b, n, 0)),
            w512(3), w512(4), w512(5), w512(6), w512(7),
            pl.BlockSpec((1, blk, LANES), lambda b, n: (b, n, 0)),
            pl.BlockSpec((SMALL_ROWS, blk), lambda b, n: (0, b * n_blk + n)),
            pl.BlockSpec((blk, HEAD_DIM), lambda b, n: (n, 0)),
            pl.BlockSpec((blk, HEAD_DIM), lambda b, n: (n, 0)),
            full((RET_HEADS, blk, blk)),
            full((RET_HEADS, blk, HEAD_DIM)),
            full((RET_HEADS, blk, HEAD_DIM)),
            full((RET_HEADS, 1, HEAD_DIM)),
            full((8, CONV_CH)),
            full((8, LANES)),
            full((SMALL_ROWS, LANES)),
            full((1, HEAD_DIM)),
            full((blk, blk)),
            full((blk, 2 * blk)),
        ],
        out_specs=[
            pl.BlockSpec((1, blk, HEADS_W), lambda b, n: (b, n, 0)),
            pl.BlockSpec((1, blk, HEADS_W), lambda b, n: (b, n, 0)),
        ],
        out_shape=[
            jax.ShapeDtypeStruct((batch, seq, HEADS_W), BF16),
            jax.ShapeDtypeStruct((batch, seq, HEADS_W), BF16),
        ],
        scratch_shapes=[
            pltpu.VMEM((RET_HEADS, HEAD_DIM, HEAD_DIM), F32),
            pltpu.VMEM((GDN_HEADS, HEAD_DIM, HEAD_DIM), F32),
            pltpu.VMEM((blk + 8, CONV_CH), F32),
        ],
        compiler_params=_params("parallel", "arbitrary"),
        name="token_mixers",
    )(proj3, proj3, proj3, proj3, proj3, proj3, gab3, gabt,
      cos2, sin2, dmat, qdec, kdec, cdec, conv_w, prow, pcol, gnorm, tril, triu)
    ret, gdn = out
    return ret.reshape(batch * seq, HEADS_W), gdn.reshape(batch * seq, HEADS_W)


def _mixer_tables(seq, blk):
    inv_freq = ROPE_BASE ** (-jnp.arange(0, HEAD_DIM, 2, dtype=F32) / HEAD_DIM)
    ang = jnp.arange(seq, dtype=F32)[:, None] * inv_freq[None, :]
    cos, sin = jnp.cos(ang), jnp.sin(ang)
    cos2 = jnp.concatenate([cos, cos], axis=-1)
    sin2 = jnp.concatenate([-sin, sin], axis=-1)
    log_gamma = jnp.log(1.0 - jnp.exp2(-5.0 - jnp.arange(RET_HEADS, dtype=F32)))
    pos = jnp.arange(blk, dtype=F32)
    diff = pos[:, None] - pos[None, :]
    dmat = jnp.where(diff >= 0, jnp.exp(jnp.maximum(diff, 0.0)[None] * log_gamma[:, None, None]), 0.0)
    ones = jnp.ones((1, 1, HEAD_DIM), F32)
    qdec = jnp.exp((pos + 1.0)[None, :] * log_gamma[:, None])[:, :, None] * ones
    kdec = jnp.exp((blk - 1 - pos)[None, :] * log_gamma[:, None])[:, :, None] * ones
    cdec = jnp.exp(blk * log_gamma)[:, None, None] * ones
    idx = np.arange(blk)
    same = (idx[:, None] // GDN_CHUNK) == (idx[None, :] // GDN_CHUNK)
    tril = jnp.asarray((same & (idx[:, None] >= idx[None, :])).astype(np.float32))
    col = np.arange(2 * blk)
    col_chunk, col_pos = col // (2 * GDN_CHUNK), col % GDN_CHUNK
    triu = jnp.asarray(((idx[:, None] // GDN_CHUNK == col_chunk[None, :])
                        & (idx[:, None] % GDN_CHUNK <= col_pos[None, :])).astype(np.float32))
    return cos2, sin2, dmat, qdec, kdec, cdec, tril, triu


def _merge_kernel(ret_ref, gdn_ref, ga_ref, gb_ref, h_ref, wa_ref, wb_ref, wo_ref, nw_ref, *rest,
                  with_router):
    if with_router:
        wr_ref, ho_ref, logit_ref = rest
    else:
        ho_ref, hn_ref = rest
    ya = _dot(ret_ref[...], wa_ref[...])
    yb = _dot(gdn_ref[...], wb_ref[...])
    merged = _sigmoid(ga_ref[...]) * ya + _sigmoid(gb_ref[...]) * yb
    h_new = h_ref[...] + _dot(_bf(merged), wo_ref[...])
    ho_ref[...] = h_new
    hn = _bf(_rms_rows(h_new, nw_ref[...]))
    if with_router:
        logit_ref[...] = _dot(hn, wr_ref[...])
    else:
        hn_ref[...] = hn


def _merge_project(ret, gdn, proj, h, wa, wb, wo, norm_w, router_w, tm):
    t = h.shape[0]
    with_router = router_w is not None
    in_specs = [
        pl.BlockSpec((tm, HEADS_W), lambda i: (i, 0)),
        pl.BlockSpec((tm, HEADS_W), lambda i: (i, 0)),
        pl.BlockSpec((tm, D_MODEL), lambda i: (i, 4)),
        pl.BlockSpec((tm, D_MODEL), lambda i: (i, 5)),
        pl.BlockSpec((tm, D_MODEL), lambda i: (i, 0)),
        pl.BlockSpec((HEADS_W, D_MODEL), lambda i: (0, 0)),
        pl.BlockSpec((HEADS_W, D_MODEL), lambda i: (0, 0)),
        pl.BlockSpec((D_MODEL, D_MODEL), lambda i: (0, 0)),
        pl.BlockSpec((1, D_MODEL), lambda i: (0, 0)),
    ]
    args = [ret, gdn, proj, proj, h, wa, wb, wo, norm_w]
    if with_router:
        in_specs.append(pl.BlockSpec((D_MODEL, LANES), lambda i: (0, 0)))
        args.append(router_w)
        second = (pl.BlockSpec((tm, LANES), lambda i: (i, 0)), jax.ShapeDtypeStruct((t, LANES), F32))
    else:
        second = (pl.BlockSpec((tm, D_MODEL), lambda i: (i, 0)), jax.ShapeDtypeStruct((t, D_MODEL), BF16))
    return pl.pallas_call(
        functools.partial(_merge_kernel, with_router=with_router),
        grid=(t // tm,),
        in_specs=in_specs,
        out_specs=[pl.BlockSpec((tm, D_MODEL), lambda i: (i, 0)), second[0]],
        out_shape=[jax.ShapeDtypeStruct((t, D_MODEL), F32), second[1]],
        compiler_params=_params("parallel"),
        name="merge_project",
    )(*args)


def _ffn_kernel(hn_ref, h_ref, wg_ref, wu_ref, wd_ref, o_ref, acc_ref):
    f = pl.program_id(1)

    @pl.when(f == 0)
    def _():
        acc_ref[...] = h_ref[...]

    x = hn_ref[...]
    act = _silu(_dot(x, wg_ref[0])) * _dot(x, wu_ref[0])
    acc_ref[...] += _dot(_bf(act), wd_ref[...])

    @pl.when(f == pl.num_programs(1) - 1)
    def _():
        o_ref[...] = acc_ref[...]


def _dense_ffn(hn, h, wg, wu, wd, tm, tf):
    t = h.shape[0]
    n_f = D_FF // tf
    return pl.pallas_call(
        _ffn_kernel,
        grid=(t // tm, n_f),
        in_specs=[
            pl.BlockSpec((tm, D_MODEL), lambda i, f: (i, 0)),
            pl.BlockSpec((tm, D_MODEL), lambda i, f: (i, 0)),
            pl.BlockSpec((1, D_MODEL, tf), lambda i, f: (f, 0, 0)),
            pl.BlockSpec((1, D_MODEL, tf), lambda i, f: (f, 0, 0)),
            pl.BlockSpec((tf, D_MODEL), lambda i, f: (f, 0)),
        ],
        out_specs=pl.BlockSpec((tm, D_MODEL), lambda i, f: (i, 0)),
        out_shape=jax.ShapeDtypeStruct((t, D_MODEL), F32),
        scratch_shapes=[pltpu.VMEM((tm, D_MODEL), F32)],
        compiler_params=_params("parallel", "arbitrary"),
        name="dense_ffn",
    )(hn, h, wg, wu, wd)


def _column_slabs(w, tf):
    d, n = w.shape
    return _bf(w).reshape(d, n // tf, tf).transpose(1, 0, 2)


def _route_kernel(logit_ref, tri_ref, meta_ref, count_ref, run_ref):
    @pl.when(pl.program_id(0) == 0)
    def _():
        run_ref[...] = jnp.zeros_like(run_ref)

    tm = logit_ref.shape[0]
    lane = lax.broadcasted_iota(jnp.int32, (tm, LANES), 1)
    neg = jnp.float32(-jnp.inf)
    l1 = jnp.where(lane < N_EXPERTS, logit_ref[...], neg)
    m1 = jnp.max(l1, axis=-1, keepdims=True)
    i1 = jnp.min(jnp.where(l1 == m1, lane, LANES), axis=-1, keepdims=True)
    l2 = jnp.where(lane == i1, neg, l1)
    m2 = jnp.max(l2, axis=-1, keepdims=True)
    i2 = jnp.min(jnp.where(l2 == m2, lane, LANES), axis=-1, keepdims=True)
    e2 = jnp.exp(m2 - m1)
    g1 = 1.0 / (1.0 + e2)
    g2 = e2 / (1.0 + e2)
    sel1 = lane == i1
    sel2 = lane == i2
    onehot = jnp.where(sel1 | sel2, 1.0, 0.0).astype(F32)
    before = _dot(tri_ref[...], _bf(onehot)) + run_ref[0:1, :]
    r1 = jnp.sum(jnp.where(sel1, before, 0.0), axis=-1, keepdims=True)
    r2 = jnp.sum(jnp.where(sel2, before, 0.0), axis=-1, keepdims=True)
    run_new = run_ref[0:1, :] + jnp.sum(onehot, axis=0, keepdims=True)
    run_ref[0:1, :] = run_new
    count_ref[...] = jnp.broadcast_to(run_new, count_ref.shape)
    meta = jnp.where(lane == 0, i1.astype(F32), 0.0)
    meta = jnp.where(lane == 1, i2.astype(F32), meta)
    meta = jnp.where(lane == 2, r1, meta)
    meta = jnp.where(lane == 3, r2, meta)
    meta = jnp.where(lane == 4, g1, meta)
    meta = jnp.where(lane == 5, g2, meta)
    meta_ref[...] = meta


def _route(logits, tm):
    t = logits.shape[0]
    idx = np.arange(tm)
    tri = jnp.asarray((idx[:, None] > idx[None, :]).astype(np.float32), dtype=BF16)
    return pl.pallas_call(
        _route_kernel,
        grid=(t // tm,),
        in_specs=[pl.BlockSpec((tm, LANES), lambda i: (i, 0)),
                  pl.BlockSpec((tm, tm), lambda i: (0, 0))],
        out_specs=[pl.BlockSpec((tm, LANES), lambda i: (i, 0)),
                   pl.BlockSpec((8, LANES), lambda i: (0, 0))],
        out_shape=[jax.ShapeDtypeStruct((t, LANES), F32),
                   jax.ShapeDtypeStruct((8, LANES), F32)],
        scratch_shapes=[pltpu.VMEM((8, LANES), F32)],
        compiler_params=_params("arbitrary"),
        name="route",
    )(logits, tri)


def _row_wait(src_hbm, dst_vmem, sem, rows):
    pltpu.make_async_copy(src_hbm.at[pl.ds(0, rows)], dst_vmem, sem).wait()


def _dispatch_kernel(dest_ref, h_ref, nw_ref, xg_in_ref, xg_ref, hn_ref, sem):
    del xg_in_ref
    tm = h_ref.shape[0]
    hn_ref[...] = _rms_rows(h_ref[...], nw_ref[...])

    def body(r, carry):
        for j in range(TOP_K):
            d = dest_ref[TOP_K * r + j]
            pltpu.make_async_copy(hn_ref.at[pl.ds(r, 1)], xg_ref.at[pl.ds(d, 1)], sem).start()
        return carry

    lax.fori_loop(0, tm, body, 0)
    for _ in range(TOP_K):
        pltpu.make_async_copy(hn_ref, xg_ref.at[pl.ds(0, tm)], sem).wait()


def _dispatch(dest_flat, h, norm_w, n_rows, tm):
    t = h.shape[0]
    xg0 = jnp.zeros((n_rows, D_MODEL), F32)
    return pl.pallas_call(
        _dispatch_kernel,
        grid=(t // tm,),
        in_specs=[
            pl.BlockSpec((TOP_K * tm,), lambda i: (i,), memory_space=pltpu.SMEM),
            pl.BlockSpec((tm, D_MODEL), lambda i: (i, 0)),
            pl.BlockSpec((1, D_MODEL), lambda i: (0, 0)),
            pl.BlockSpec(memory_space=pl.ANY),
        ],
        out_specs=pl.BlockSpec(memory_space=pl.ANY),
        out_shape=jax.ShapeDtypeStruct((n_rows, D_MODEL), F32),
        scratch_shapes=[pltpu.VMEM((tm, D_MODEL), F32), pltpu.SemaphoreType.DMA(())],
        input_output_aliases={3: 0},
        compiler_params=_params("arbitrary"),
        name="dispatch",
    )(dest_flat, h, norm_w, xg0)


def _expert_kernel(be_ref, x_ref, wg_ref, wu_ref, wd_ref, y_ref, *, tf):
    del be_ref
    x = _bf(x_ref[...])
    acc = jnp.zeros(y_ref.shape, F32)
    for f in range(D_FF // tf):
        cols = slice(f * tf, (f + 1) * tf)
        act = _silu(_dot(x, wg_ref[0, :, cols])) * _dot(x, wu_ref[0, :, cols])
        acc = acc + _dot(_bf(act), wd_ref[0, cols, :])
    y_ref[...] = acc


def _experts(block_e, xg, wg, wu, wd, rows, tf):
    n_rows = xg.shape[0]
    grid_spec = pltpu.PrefetchScalarGridSpec(
        num_scalar_prefetch=1,
        grid=(n_rows // rows,),
        in_specs=[
            pl.BlockSpec((rows, D_MODEL), lambda i, be: (i, 0)),
            pl.BlockSpec((1, D_MODEL, D_FF), lambda i, be: (be[i], 0, 0)),
            pl.BlockSpec((1, D_MODEL, D_FF), lambda i, be: (be[i], 0, 0)),
            pl.BlockSpec((1, D_FF, D_MODEL), lambda i, be: (be[i], 0, 0)),
        ],
        out_specs=pl.BlockSpec((rows, D_MODEL), lambda i, be: (i, 0)),
    )
    return pl.pallas_call(
        functools.partial(_expert_kernel, tf=tf),
        grid_spec=grid_spec,
        out_shape=jax.ShapeDtypeStruct((n_rows, D_MODEL), F32),
        compiler_params=_params("arbitrary"),
        name="experts",
    )(block_e, xg, wg, wu, wd)


def _combine_kernel(dest_ref, meta_ref, h_ref, nw_ref, y_ref, o_ref, buf_ref, sem):
    tm = h_ref.shape[0]

    def body(r, carry):
        for j in range(TOP_K):
            d = dest_ref[TOP_K * r + j]
            pltpu.make_async_copy(y_ref.at[pl.ds(d, 1)], buf_ref.at[j, pl.ds(r, 1)], sem).start()
        return carry

    lax.fori_loop(0, tm, body, 0)
    for j in range(TOP_K):
        _row_wait(y_ref, buf_ref.at[j], sem, tm)
    meta = meta_ref[...]
    moe = meta[:, 4:5] * buf_ref[0] + meta[:, 5:6] * buf_ref[1]
    o_ref[...] = _rms_rows(h_ref[...] + moe, nw_ref[...])


def _combine(dest_flat, meta, h, norm_w, y, tm):
    t = h.shape[0]
    return pl.pallas_call(
        _combine_kernel,
        grid=(t // tm,),
        in_specs=[
            pl.BlockSpec((TOP_K * tm,), lambda i: (i,), memory_space=pltpu.SMEM),
            pl.BlockSpec((tm, LANES), lambda i: (i, 0)),
            pl.BlockSpec((tm, D_MODEL), lambda i: (i, 0)),
            pl.BlockSpec((1, D_MODEL), lambda i: (0, 0)),
            pl.BlockSpec(memory_space=pl.ANY),
        ],
        out_specs=pl.BlockSpec((tm, D_MODEL), lambda i: (i, 0)),
        out_shape=jax.ShapeDtypeStruct((t, D_MODEL), F32),
        scratch_shapes=[pltpu.VMEM((TOP_K, tm, D_MODEL), F32), pltpu.SemaphoreType.DMA(())],
        compiler_params=_params("arbitrary"),
        name="combine",
    )(dest_flat, meta, h, norm_w, y)


def _moe_layer(logits, h, ffn_norm_w, final_norm_w, wg, wu, wd, rows, tm_route, tm_rows, tf):
    t = h.shape[0]
    meta, counts = _route(logits, tm_route)
    counts = counts[0, :N_EXPERTS].astype(jnp.int32)
    padded = ((counts + rows - 1) // rows) * rows
    pend = jnp.cumsum(padded)
    pstart = pend - padded
    n_rows = t * TOP_K + N_EXPERTS * rows
    expert = meta[:, 0:TOP_K].astype(jnp.int32)
    rank = meta[:, TOP_K:2 * TOP_K].astype(jnp.int32)
    dest = (pstart[expert] + rank).reshape(t * TOP_K)
    block_start = jnp.arange(n_rows // rows, dtype=jnp.int32) * rows
    block_e = jnp.clip(jnp.searchsorted(pend, block_start, side="right"), 0, N_EXPERTS - 1).astype(jnp.int32)
    xg = _dispatch(dest, h, ffn_norm_w, n_rows, tm_rows)
    y = _experts(block_e, xg, wg, wu, wd, rows, tf)
    return _combine(dest, meta, h, final_norm_w, y, tm_rows)


def _pack_in_weights(w_in):
    o = np.cumsum((0, HEADS_W, HEADS_W, HEADS_W, HEADS_W, CONV_CH, HEADS_W, GDN_HEADS, GDN_HEADS, D_MODEL, D_MODEL))
    rq, rk, rv, rg, gqkv, gz, ga, gb, ma, mb = (w_in[:, o[i]:o[i + 1]] for i in range(10))
    main = _bf(jnp.concatenate([gqkv, rq, rk, rv, rg, gz, ma, mb], axis=1))
    small = jnp.concatenate([ga, gb], axis=1)
    small_cols = _bf(jnp.pad(small, ((0, 0), (0, LANES - 2 * GDN_HEADS))))
    small_rows = _bf(jnp.pad(small.T, ((0, SMALL_ROWS - 2 * GDN_HEADS), (0, 0))))
    return main, small_cols, small_rows


def _pick(n, prefs):
    for p in prefs:
        if n % p == 0:
            return p
    raise ValueError(f"no tile in {prefs} divides {n}")


def kernel(x, norm_mix, w_in, conv_w, a_log, dt_bias, gdn_norm, w_branch, w_out, norm_ffn,
           dense_w_gate, dense_w_up, dense_w_down, router, moe_w_gate, moe_w_up, moe_w_down, final_norm):
    batch, seq, d = x.shape
    depth = norm_mix.shape[0]
    assert d == D_MODEL and seq % GDN_CHUNK == 0
    t = batch * seq
    blk = _pick(seq, (256, 128, 64))
    tm = _pick(t, (1024, 512, 256))
    tm_rows = _pick(t, (512, 256))
    tf_dense = 256
    rows = 512
    tabs = _mixer_tables(seq, blk)

    h = x.reshape(t, d)
    out = None
    for layer in range(depth):
        w_main, w_small, w_small_t = _pack_in_weights(w_in[layer])
        proj, gab, gabt = _in_projection(h, norm_mix[layer][None, :], w_main, w_small, w_small_t, tm, 512)
        prow = jnp.zeros((8, LANES), F32)
        prow = prow.at[0, :GDN_HEADS].set(a_log[layer]).at[1, :GDN_HEADS].set(dt_bias[layer])
        pcol = jnp.zeros((SMALL_ROWS, LANES), F32)
        pcol = pcol.at[:GDN_HEADS, 0].set(a_log[layer]).at[:GDN_HEADS, 1].set(dt_bias[layer])
        convw = jnp.pad(conv_w[layer], ((0, 8 - CONV_K), (0, 0)))
        ret, gdn = _token_mixers(proj, gab, gabt, tabs, convw, prow, pcol, gdn_norm[layer][None, :],
                                 batch, seq, blk)
        i = layer // 2
        is_moe = layer % 2 == 1
        router_w = _bf(jnp.pad(router[i], ((0, 0), (0, LANES - N_EXPERTS)))) if is_moe else None
        h, second = _merge_project(ret, gdn, proj, h, _bf(w_branch[layer, 0]), _bf(w_branch[layer, 1]),
                                   _bf(w_out[layer]), norm_ffn[layer][None, :], router_w, tm_rows)
        if not is_moe:
            h = _dense_ffn(second, h, _column_slabs(dense_w_gate[i], tf_dense),
                           _column_slabs(dense_w_up[i], tf_dense), _bf(dense_w_down[i]), tm, tf_dense)
            if layer == depth - 1:
                raise NotImplementedError("final norm after a dense layer is not wired")
        else:
            if layer != depth - 1:
                raise NotImplementedError("routed layer must be last")
            out = _moe_layer(second, h, norm_ffn[layer][None, :], final_norm[None, :],
                             _bf(moe_w_gate[i]), _bf(moe_w_up[i]), _bf(moe_w_down[i]),
                             rows, tm_rows, tm_rows, 256)
    return out.reshape(batch, seq, d)
```

```python
import functools
import math

import jax
import jax.numpy as jnp
import numpy as np
from jax import lax
from jax.experimental import pallas as pl
from jax.experimental.pallas import tpu as pltpu

F32 = jnp.float32
BF16 = jnp.bfloat16

D_MODEL = 1024
RET_HEADS = 4
GDN_HEADS = 4
HEAD_DIM = 128
HEADS_W = 512
CONV_K = 4
CONV_CH = 3 * HEADS_W
GDN_CHUNK = 64
ROPE_BASE = 10000.0
D_FF = 2816
N_EXPERTS = 8
TOP_K = 2
EPS = 1e-6
LANES = 128
SMALL_ROWS = 16

MAIN_COLS = CONV_CH + 5 * HEADS_W + 2 * D_MODEL
VMEM_LIMIT = 56 * 1024 * 1024

HIGHEST = lax.Precision.HIGHEST


def _bf(x):
    return x.astype(BF16)


def _dot(a, b):
    return jnp.dot(a, b, preferred_element_type=F32)


def _dot_nt(a, b):
    return lax.dot_general(a, b, (((1,), (1,)), ((), ())), preferred_element_type=F32)


def _dot_tn(a, b):
    return lax.dot_general(a, b, (((0,), (0,)), ((), ())), preferred_element_type=F32)


def _dot_f32(a, b):
    return jnp.dot(a, b, preferred_element_type=F32, precision=HIGHEST)


def _sigmoid(x):
    return 1.0 / (1.0 + jnp.exp(-x))


def _silu(x):
    return x * _sigmoid(x)


def _softplus(x):
    return jnp.maximum(x, 0.0) + jnp.log(1.0 + jnp.exp(-jnp.abs(x)))


def _rms_rows(x, w):
    ms = jnp.mean(x * x, axis=-1, keepdims=True)
    return x * lax.rsqrt(ms + EPS) * w


def _params(*sem):
    return pltpu.CompilerParams(dimension_semantics=sem, vmem_limit_bytes=VMEM_LIMIT)


def _inproj_kernel(x_ref, nw_ref, w_ref, ws_ref, wst_ref, o_ref, os_ref, ost_ref, hn_ref):
    @pl.when(pl.program_id(1) == 0)
    def _():
        hn = _bf(_rms_rows(x_ref[...], nw_ref[...]))
        hn_ref[...] = hn
        os_ref[...] = _dot(hn, ws_ref[...])
        ost_ref[...] = _dot_nt(wst_ref[...], hn)

    o_ref[...] = _bf(_dot(hn_ref[...], w_ref[...]))


def _in_projection(h, norm_w, w_main, w_small, w_small_t, tm, tn):
    t = h.shape[0]
    return pl.pallas_call(
        _inproj_kernel,
        grid=(t // tm, MAIN_COLS // tn),
        in_specs=[
            pl.BlockSpec((tm, D_MODEL), lambda i, j: (i, 0)),
            pl.BlockSpec((1, D_MODEL), lambda i, j: (0, 0)),
            pl.BlockSpec((D_MODEL, tn), lambda i, j: (0, j)),
            pl.BlockSpec((D_MODEL, LANES), lambda i, j: (0, 0)),
            pl.BlockSpec((SMALL_ROWS, D_MODEL), lambda i, j: (0, 0)),
        ],
        out_specs=[
            pl.BlockSpec((tm, tn), lambda i, j: (i, j)),
            pl.BlockSpec((tm, LANES), lambda i, j: (i, 0)),
            pl.BlockSpec((SMALL_ROWS, tm), lambda i, j: (0, i)),
        ],
        out_shape=[
            jax.ShapeDtypeStruct((t, MAIN_COLS), BF16),
            jax.ShapeDtypeStruct((t, LANES), F32),
            jax.ShapeDtypeStruct((SMALL_ROWS, t), F32),
        ],
        scratch_shapes=[pltpu.VMEM((tm, D_MODEL), BF16)],
        compiler_params=_params("parallel", "arbitrary"),
        name="in_projection",
    )(h, norm_w, w_main, w_small, w_small_t)


def _hi_lo(x):
    hi = _bf(x)
    lo_f = x - hi.astype(F32)
    return hi, _bf(lo_f), lo_f


def _dup_lhs(x2, lo_f, left):
    w = _bf(jnp.where(left, x2, lo_f))
    return jnp.concatenate([w, w], axis=1)


def _dup_rhs(hi, lo):
    return jnp.concatenate([hi, hi, lo, lo], axis=0)


def _mixer_kernel(gqkv_ref, rq_ref, rk_ref, rv_ref, rg_ref, gz_ref, gab_ref, gabt_ref,
                  cos_ref, sin_ref, dmat_ref, qdec_ref, kdec_ref, cdec_ref,
                  convw_ref, prow_ref, pcol_ref, gnorm_ref, tril_ref, triu_ref,
                  ret_ref, gdn_ref,
                  rstate_ref, gstate_ref, xs_ref, *, blk):
    c = GDN_CHUNK
    n_chunks = blk // c
    heads = range(GDN_HEADS)
    inst = [(h, n) for h in heads for n in range(n_chunks)]

    @pl.when(pl.program_id(1) == 0)
    def _():
        rstate_ref[...] = jnp.zeros_like(rstate_ref)
        gstate_ref[...] = jnp.zeros_like(gstate_ref)
        xs_ref[0:8, :] = jnp.zeros((8, CONV_CH), F32)

    hsl = [slice(h * HEAD_DIM, (h + 1) * HEAD_DIM) for h in range(RET_HEADS)]

    cos2 = cos_ref[...]
    sin2 = sin_ref[...]
    half = HEAD_DIM // 2
    rq = [rq_ref[0, :, sl].astype(F32) for sl in hsl]
    rk = [rk_ref[0, :, sl].astype(F32) for sl in hsl]
    rv = [rv_ref[0, :, sl] for sl in hsl]
    rq = [q * cos2 + pltpu.roll(q, half, 1) * sin2 for q in rq]
    rk = [(k * cos2 + pltpu.roll(k, half, 1) * sin2) * (HEAD_DIM ** -0.5) for k in rk]
    rstate = [rstate_ref[h] for h in range(RET_HEADS)]
    scores = [_dot_nt(_bf(rq[h]), _bf(rk[h])) * dmat_ref[h] for h in range(RET_HEADS)]
    inter = [_dot(_bf(rq[h] * qdec_ref[h]), _bf(rstate[h])) for h in range(RET_HEADS)]
    kv = [_dot_tn(_bf(rk[h] * kdec_ref[h]), rv[h]) for h in range(RET_HEADS)]
    ro = [_dot(_bf(scores[h]), rv[h]) + inter[h] for h in range(RET_HEADS)]
    for h in range(RET_HEADS):
        rstate_ref[h] = rstate[h] * cdec_ref[h] + kv[h]
        o = ro[h]
        o = o * lax.rsqrt(jnp.mean(o * o, axis=-1, keepdims=True) + EPS)
        ret_ref[0, :, hsl[h]] = _bf(_silu(rg_ref[0, :, hsl[h]].astype(F32)) * o)

    xs_ref[8:8 + blk, :] = gqkv_ref[0].astype(F32)
    conv = convw_ref[3:4, :] * xs_ref[pl.ds(8, blk), :]
    for s in range(1, CONV_K):
        conv = conv + convw_ref[3 - s:4 - s, :] * xs_ref[pl.ds(8 - s, blk), :]
    xs_ref[0:8, :] = xs_ref[blk:blk + 8, :]
    conv = _silu(conv)

    gab = gab_ref[0]
    g_cols = -jnp.exp(prow_ref[0:1, :]) * _softplus(gab + prow_ref[1:2, :])
    beta_cols = _sigmoid(gab)
    gc_cols = _dot_f32(tril_ref[...], g_cols)
    egc_cols = jnp.exp(gc_cols)
    g_rows = -jnp.exp(pcol_ref[:, 0:1]) * _softplus(gabt_ref[...] + pcol_ref[:, 1:2])
    gc_rows = _dot_f32(g_rows, triu_ref[...])

    ri = lax.broadcasted_iota(jnp.int32, (c, 2 * c), 0)
    lane = lax.broadcasted_iota(jnp.int32, (c, 2 * c), 1)
    ci = lane & (c - 1)
    left = lane < c
    ge = ri >= ci
    gt = ri > ci
    eye = jnp.where(ri == ci, 1.0, 0.0).astype(F32)
    level_masks = []
    for lg in range(int(math.log2(c))):
        b = 1 << lg
        same_block = (ri >> (lg + 1)) == (ci >> (lg + 1))
        level_masks.append(same_block & ((ri & (2 * b - 1)) >= b) & ((ci & (2 * b - 1)) < b))

    qh, kh, vh = [], [], []
    for h in heads:
        q = conv[:, hsl[h]]
        k = conv[:, HEADS_W + h * HEAD_DIM:HEADS_W + (h + 1) * HEAD_DIM]
        vh.append(conv[:, 2 * HEADS_W + h * HEAD_DIM:2 * HEADS_W + (h + 1) * HEAD_DIM])
        qh.append(q * lax.rsqrt(jnp.sum(q * q, axis=-1, keepdims=True) + EPS) * (HEAD_DIM ** -0.5))
        kh.append(k * lax.rsqrt(jnp.sum(k * k, axis=-1, keepdims=True) + EPS))

    rows = [slice(n * c, (n + 1) * c) for n in range(n_chunks)]
    kc = {(h, n): kh[h][rows[n]] for h, n in inst}
    qc = {(h, n): qh[h][rows[n]] for h, n in inst}
    bcol = {(h, n): beta_cols[rows[n], 4 + h:5 + h] for h, n in inst}
    gcol = {(h, n): gc_cols[rows[n], h:h + 1] for h, n in inst}
    egcol = {(h, n): egc_cols[rows[n], h:h + 1] for h, n in inst}
    glast = {(h, n): gc_cols[n * c + c - 1:(n + 1) * c, h:h + 1] for h, n in inst}
    grow = {(h, n): gc_rows[h:h + 1, 2 * c * n:2 * c * (n + 1)] for h, n in inst}
    kb = {i: kc[i] * bcol[i] for i in inst}
    kcb = {i: _bf(kc[i]) for i in inst}
    kcb2 = {i: jnp.concatenate([kcb[i], kcb[i]], axis=0) for i in inst}
    decay = {i: jnp.where(ge, jnp.exp(jnp.where(ge, gcol[i] - grow[i], 0.0)), 0.0) for i in inst}
    a2 = {i: jnp.where(gt, _dot_nt(_bf(kb[i]), kcb2[i]) * decay[i], 0.0) for i in inst}
    attn = {i: _bf(jnp.where(ge, _dot_nt(_bf(qc[i]), kcb2[i]) * decay[i], 0.0)[:, :c]) for i in inst}

    x2 = {i: eye - jnp.where(level_masks[0], a2[i], 0.0) for i in inst}
    a_hl = {i: _hi_lo(a2[i]) for i in inst}
    for m in level_masks[1:]:
        mb = jnp.where(m, 1.0, 0.0).astype(BF16)
        x_hl = {i: _hi_lo(x2[i]) for i in inst}
        t = {i: _dot(_dup_lhs(x2[i], x_hl[i][2], left), _dup_rhs(a_hl[i][0] * mb, a_hl[i][1] * mb)) for i in inst}
        t_hl = {i: _hi_lo(t[i]) for i in inst}
        x2 = {i: x2[i] - _dot(_dup_lhs(t[i], t_hl[i][2], left), _dup_rhs(x_hl[i][0], x_hl[i][1])) for i in inst}

    uw = {}
    for h, n in inst:
        i = (h, n)
        rhs = _hi_lo(jnp.concatenate([vh[h][rows[n]] * bcol[i], kb[i] * egcol[i]], axis=1))
        uw[i] = _dot(_dup_lhs(x2[i], _hi_lo(x2[i])[2], left), _dup_rhs(rhs[0], rhs[1]))
    wq = {i: _bf(jnp.concatenate([uw[i][:, HEAD_DIM:], qc[i] * egcol[i]], axis=0)) for i in inst}
    kd = {i: _bf(kc[i] * jnp.exp(glast[i] - gcol[i])) for i in inst}
    eg = {i: jnp.exp(glast[i]) for i in inst}

    state = [gstate_ref[h] for h in heads]
    outs = [[] for _ in heads]
    for n in range(n_chunks):
        sb = [_bf(state[h]) for h in heads]
        ws = [_dot(wq[(h, n)], sb[h]) for h in heads]
        vnb = [_bf(uw[(h, n)][:, :HEAD_DIM] - ws[h][:c]) for h in heads]
        for h in heads:
            outs[h].append(ws[h][c:] + _dot(attn[(h, n)], vnb[h]))
        state = [state[h] * eg[(h, n)] + _dot_tn(kd[(h, n)], vnb[h]) for h in heads]

    gnorm = gnorm_ref[...]
    for h in heads:
        gstate_ref[h] = state[h]
        o = jnp.concatenate(outs[h], axis=0)
        o = o * lax.rsqrt(jnp.mean(o * o, axis=-1, keepdims=True) + EPS)
        gdn_ref[0, :, hsl[h]] = _bf(o * gnorm * _silu(gz_ref[0, :, hsl[h]].astype(F32)))


def _token_mixers(proj, gab, gabt, tabs, conv_w, prow, pcol, gnorm, batch, seq, blk):
    proj3 = proj.reshape(batch, seq, MAIN_COLS)
    gab3 = gab.reshape(batch, seq, LANES)
    n_blk = seq // blk
    cos2, sin2, dmat, qdec, kdec, cdec, tril, triu = tabs
    w512 = lambda col: pl.BlockSpec((1, blk, HEADS_W), lambda b, n, col=col: (b, n, col))
    full = lambda shape: pl.BlockSpec(shape, lambda b, n: (0,) * len(shape))
    out = pl.pallas_call(
        functools.partial(_mixer_kernel, blk=blk),
        grid=(batch, seq // blk),
        in_specs=[
            pl.BlockSpec((1, blk, CONV_CH), lambda b, n: (b, n, 0)),
            w512(3), w512(4), w512(5), w512(6), w512(7),
            pl.BlockSpec((1, blk, LANES), lambda b, n: (b, n, 0)),
            pl.BlockSpec((SMALL_ROWS, blk), lambda b, n: (0, b * n_blk + n)),
            pl.BlockSpec((blk, HEAD_DIM), lambda b, n: (n, 0)),
            pl.BlockSpec((blk, HEAD_DIM), lambda b, n: (n, 0)),
            full((RET_HEADS, blk, blk)),
            full((RET_HEADS, blk, HEAD_DIM)),
            full((RET_HEADS, blk, HEAD_DIM)),
            full((RET_HEADS, 1, HEAD_DIM)),
            full((8, CONV_CH)),
            full((8, LANES)),
            full((SMALL_ROWS, LANES)),
            full((1, HEAD_DIM)),
            full((blk, blk)),
            full((blk, 2 * blk)),
        ],
        out_specs=[
            pl.BlockSpec((1, blk, HEADS_W), lambda b, n: (b, n, 0)),
            pl.BlockSpec((1, blk, HEADS_W), lambda b, n: (b, n, 0)),
        ],
        out_shape=[
            jax.ShapeDtypeStruct((batch, seq, HEADS_W), BF16),
            jax.ShapeDtypeStruct((batch, seq, HEADS_W), BF16),
        ],
        scratch_shapes=[
            pltpu.VMEM((RET_HEADS, HEAD_DIM, HEAD_DIM), F32),
            pltpu.VMEM((GDN_HEADS, HEAD_DIM, HEAD_DIM), F32),
            pltpu.VMEM((blk + 8, CONV_CH), F32),
        ],
        compiler_params=_params("parallel", "arbitrary"),
        name="token_mixers",
    )(proj3, proj3, proj3, proj3, proj3, proj3, gab3, gabt,
      cos2, sin2, dmat, qdec, kdec, cdec, conv_w, prow, pcol, gnorm, tril, triu)
    ret, gdn = out
    return ret.reshape(batch * seq, HEADS_W), gdn.reshape(batch * seq, HEADS_W)


def _mixer_tables(seq, blk):
    inv_freq = ROPE_BASE ** (-jnp.arange(0, HEAD_DIM, 2, dtype=F32) / HEAD_DIM)
    ang = jnp.arange(seq, dtype=F32)[:, None] * inv_freq[None, :]
    cos, sin = jnp.cos(ang), jnp.sin(ang)
    cos2 = jnp.concatenate([cos, cos], axis=-1)
    sin2 = jnp.concatenate([-sin, sin], axis=-1)
    log_gamma = jnp.log(1.0 - jnp.exp2(-5.0 - jnp.arange(RET_HEADS, dtype=F32)))
    pos = jnp.arange(blk, dtype=F32)
    diff = pos[:, None] - pos[None, :]
    dmat = jnp.where(diff >= 0, jnp.exp(jnp.maximum(diff, 0.0)[None] * log_gamma[:, None, None]), 0.0)
    ones = jnp.ones((1, 1, HEAD_DIM), F32)
    qdec = jnp.exp((pos + 1.0)[None, :] * log_gamma[:, None])[:, :, None] * ones
    kdec = jnp.exp((blk - 1 - pos)[None, :] * log_gamma[:, None])[:, :, None] * ones
    cdec = jnp.exp(blk * log_gamma)[:, None, None] * ones
    idx = np.arange(blk)
    same = (idx[:, None] // GDN_CHUNK) == (idx[None, :] // GDN_CHUNK)
    tril = jnp.asarray((same & (idx[:, None] >= idx[None, :])).astype(np.float32))
    col = np.arange(2 * blk)
    col_chunk, col_pos = col // (2 * GDN_CHUNK), col % GDN_CHUNK
    triu = jnp.asarray(((idx[:, None] // GDN_CHUNK == col_chunk[None, :])
                        & (idx[:, None] % GDN_CHUNK <= col_pos[None, :])).astype(np.float32))
    return cos2, sin2, dmat, qdec, kdec, cdec, tril, triu


def _merge_kernel(ret_ref, gdn_ref, ga_ref, gb_ref, h_ref, wa_ref, wb_ref, wo_ref, nw_ref, *rest,
                  with_router):
    if with_router:
        wr_ref, ho_ref, logit_ref = rest
    else:
        ho_ref, hn_ref = rest
    ya = _dot(ret_ref[...], wa_ref[...])
    yb = _dot(gdn_ref[...], wb_ref[...])
    merged = _sigmoid(ga_ref[...].astype(F32)) * ya + _sigmoid(gb_ref[...].astype(F32)) * yb
    h_new = h_ref[...] + _dot(_bf(merged), wo_ref[...])
    ho_ref[...] = h_new
    hn = _bf(_rms_rows(h_new, nw_ref[...]))
    if with_router:
        logit_ref[...] = _dot(hn, wr_ref[...])
    else:
        hn_ref[...] = hn


def _merge_project(ret, gdn, proj, h, wa, wb, wo, norm_w, router_w, tm):
    t = h.shape[0]
    with_router = router_w is not None
    in_specs = [
        pl.BlockSpec((tm, HEADS_W), lambda i: (i, 0)),
        pl.BlockSpec((tm, HEADS_W), lambda i: (i, 0)),
        pl.BlockSpec((tm, D_MODEL), lambda i: (i, 4)),
        pl.BlockSpec((tm, D_MODEL), lambda i: (i, 5)),
        pl.BlockSpec((tm, D_MODEL), lambda i: (i, 0)),
        pl.BlockSpec((HEADS_W, D_MODEL), lambda i: (0, 0)),
        pl.BlockSpec((HEADS_W, D_MODEL), lambda i: (0, 0)),
        pl.BlockSpec((D_MODEL, D_MODEL), lambda i: (0, 0)),
        pl.BlockSpec((1, D_MODEL), lambda i: (0, 0)),
    ]
    args = [ret, gdn, proj, proj, h, wa, wb, wo, norm_w]
    if with_router:
        in_specs.append(pl.BlockSpec((D_MODEL, LANES), lambda i: (0, 0)))
        args.append(router_w)
        second = (pl.BlockSpec((tm, LANES), lambda i: (i, 0)), jax.ShapeDtypeStruct((t, LANES), F32))
    else:
        second = (pl.BlockSpec((tm, D_MODEL), lambda i: (i, 0)), jax.ShapeDtypeStruct((t, D_MODEL), BF16))
    return pl.pallas_call(
        functools.partial(_merge_kernel, with_router=with_router),
        grid=(t // tm,),
        in_specs=in_specs,
        out_specs=[pl.BlockSpec((tm, D_MODEL), lambda i: (i, 0)), second[0]],
        out_shape=[jax.ShapeDtypeStruct((t, D_MODEL), F32), second[1]],
        compiler_params=_params("parallel"),
        name="merge_project",
    )(*args)


def _swiglu_rows(x, wg_ref, wu_ref, wd_ref, acc, tf):
    for f in range(D_FF // tf):
        cols = slice(f * tf, (f + 1) * tf)
        act = _silu(_dot(x, wg_ref[0, :, cols])) * _dot(x, wu_ref[0, :, cols])
        acc = acc + _dot(_bf(act), wd_ref[0, cols, :])
    return acc


def _ffn_kernel(hn_ref, h_ref, wg_ref, wu_ref, wd_ref, o_ref, *, tf):
    o_ref[...] = _swiglu_rows(hn_ref[...], wg_ref, wu_ref, wd_ref, h_ref[...], tf)


def _dense_ffn(hn, h, wg, wu, wd, tm, tf):
    t = h.shape[0]
    return pl.pallas_call(
        functools.partial(_ffn_kernel, tf=tf),
        grid=(t // tm,),
        in_specs=[
            pl.BlockSpec((tm, D_MODEL), lambda i: (i, 0)),
            pl.BlockSpec((tm, D_MODEL), lambda i: (i, 0)),
            pl.BlockSpec((1, D_MODEL, D_FF), lambda i: (0, 0, 0)),
            pl.BlockSpec((1, D_MODEL, D_FF), lambda i: (0, 0, 0)),
            pl.BlockSpec((1, D_FF, D_MODEL), lambda i: (0, 0, 0)),
        ],
        out_specs=pl.BlockSpec((tm, D_MODEL), lambda i: (i, 0)),
        out_shape=jax.ShapeDtypeStruct((t, D_MODEL), F32),
        compiler_params=_params("parallel"),
        name="dense_ffn",
    )(hn, h, wg, wu, wd)


def _route_kernel(logit_ref, tri_ref, meta_ref, count_ref, run_ref):
    @pl.when(pl.program_id(0) == 0)
    def _():
        run_ref[...] = jnp.zeros_like(run_ref)

    tm = logit_ref.shape[0]
    lane = lax.broadcasted_iota(jnp.int32, (tm, LANES), 1)
    neg = jnp.float32(-jnp.inf)
    l1 = jnp.where(lane < N_EXPERTS, logit_ref[...], neg)
    m1 = jnp.max(l1, axis=-1, keepdims=True)
    i1 = jnp.min(jnp.where(l1 == m1, lane, LANES), axis=-1, keepdims=True)
    l2 = jnp.where(lane == i1, neg, l1)
    m2 = jnp.max(l2, axis=-1, keepdims=True)
    i2 = jnp.min(jnp.where(l2 == m2, lane, LANES), axis=-1, keepdims=True)
    e2 = jnp.exp(m2 - m1)
    g1 = 1.0 / (1.0 + e2)
    g2 = e2 / (1.0 + e2)
    sel1 = lane == i1
    sel2 = lane == i2
    onehot = jnp.where(sel1 | sel2, 1.0, 0.0).astype(F32)
    before = _dot(tri_ref[...], _bf(onehot)) + run_ref[0:1, :]
    r1 = jnp.sum(jnp.where(sel1, before, 0.0), axis=-1, keepdims=True)
    r2 = jnp.sum(jnp.where(sel2, before, 0.0), axis=-1, keepdims=True)
    run_new = run_ref[0:1, :] + jnp.sum(onehot, axis=0, keepdims=True)
    run_ref[0:1, :] = run_new
    count_ref[...] = jnp.broadcast_to(run_new, count_ref.shape)
    meta = jnp.where(lane == 0, i1.astype(F32), 0.0)
    meta = jnp.where(lane == 1, i2.astype(F32), meta)
    meta = jnp.where(lane == 2, r1, meta)
    meta = jnp.where(lane == 3, r2, meta)
    meta = jnp.where(lane == 4, g1, meta)
    meta = jnp.where(lane == 5, g2, meta)
    meta_ref[...] = meta


def _route(logits, tm):
    t = logits.shape[0]
    idx = np.arange(tm)
    tri = jnp.asarray((idx[:, None] > idx[None, :]).astype(np.float32), dtype=BF16)
    return pl.pallas_call(
        _route_kernel,
        grid=(t // tm,),
        in_specs=[pl.BlockSpec((tm, LANES), lambda i: (i, 0)),
                  pl.BlockSpec((tm, tm), lambda i: (0, 0))],
        out_specs=[pl.BlockSpec((tm, LANES), lambda i: (i, 0)),
                   pl.BlockSpec((8, LANES), lambda i: (0, 0))],
        out_shape=[jax.ShapeDtypeStruct((t, LANES), F32),
                   jax.ShapeDtypeStruct((8, LANES), F32)],
        scratch_shapes=[pltpu.VMEM((8, LANES), F32)],
        compiler_params=_params("arbitrary"),
        name="route",
    )(logits, tri)


def _row_wait(src_hbm, dst_vmem, sem, rows):
    pltpu.make_async_copy(src_hbm.at[pl.ds(0, rows)], dst_vmem, sem).wait()


def _dispatch_kernel(sched_ref, dest_ref, h_ref, nw_ref, xg_ref, hn_ref, zero_ref, sem, zsem, *, rows):
    tm = h_ref.shape[0]

    @pl.when(pl.program_id(0) == 0)
    def _():
        zero_ref[...] = jnp.zeros_like(zero_ref)
        for e in range(N_EXPERTS):
            @pl.when(sched_ref[N_EXPERTS + e] > 0)
            def _():
                start = pl.multiple_of(sched_ref[e] - rows, rows)
                clear = pltpu.make_async_copy(zero_ref, xg_ref.at[pl.ds(start, rows)], zsem)
                clear.start()
                clear.wait()

        def clear_unused(b, carry):
            clear = pltpu.make_async_copy(zero_ref, xg_ref.at[pl.ds(pl.multiple_of(b * rows, rows), rows)], zsem)
            clear.start()
            clear.wait()
            return carry

        lax.fori_loop(sched_ref[2 * N_EXPERTS], xg_ref.shape[0] // rows, clear_unused, 0)

    hn_ref[...] = _rms_rows(h_ref[...], nw_ref[...])

    def body(r, carry):
        for j in range(TOP_K):
            d = dest_ref[TOP_K * r + j]
            pltpu.make_async_copy(hn_ref.at[pl.ds(r, 1)], xg_ref.at[pl.ds(d, 1)], sem).start()
        return carry

    lax.fori_loop(0, tm, body, 0)
    for _ in range(TOP_K):
        pltpu.make_async_copy(hn_ref, xg_ref.at[pl.ds(0, tm)], sem).wait()


def _dispatch(sched, dest_flat, h, norm_w, n_rows, tm, rows):
    t = h.shape[0]
    grid_spec = pltpu.PrefetchScalarGridSpec(
        num_scalar_prefetch=1,
        grid=(t // tm,),
        in_specs=[
            pl.BlockSpec((TOP_K * tm,), lambda i, sc: (i,), memory_space=pltpu.SMEM),
            pl.BlockSpec((tm, D_MODEL), lambda i, sc: (i, 0)),
            pl.BlockSpec((1, D_MODEL), lambda i, sc: (0, 0)),
        ],
        out_specs=pl.BlockSpec(memory_space=pl.ANY),
        scratch_shapes=[pltpu.VMEM((tm, D_MODEL), F32), pltpu.VMEM((rows, D_MODEL), F32),
                        pltpu.SemaphoreType.DMA(()), pltpu.SemaphoreType.DMA(())],
    )
    return pl.pallas_call(
        functools.partial(_dispatch_kernel, rows=rows),
        grid_spec=grid_spec,
        out_shape=jax.ShapeDtypeStruct((n_rows, D_MODEL), F32),
        compiler_params=_params("arbitrary"),
        name="dispatch",
    )(sched, dest_flat, h, norm_w)


def _expert_kernel(be_ref, sched_ref, x_ref, wg_ref, wu_ref, wd_ref, y_ref, *, tf):
    del be_ref

    @pl.when(pl.program_id(0) < sched_ref[2 * N_EXPERTS])
    def _():
        y_ref[...] = _swiglu_rows(_bf(x_ref[...]), wg_ref, wu_ref, wd_ref, jnp.zeros(y_ref.shape, F32), tf)

    @pl.when(pl.program_id(0) >= sched_ref[2 * N_EXPERTS])
    def _():
        y_ref[...] = jnp.zeros_like(y_ref)


def _experts(block_e, sched, xg, wg, wu, wd, rows, tf):
    n_rows = xg.shape[0]
    used = lambda i, sc: jnp.minimum(i, sc[2 * N_EXPERTS] - 1)
    grid_spec = pltpu.PrefetchScalarGridSpec(
        num_scalar_prefetch=2,
        grid=(n_rows // rows,),
        in_specs=[
            pl.BlockSpec((rows, D_MODEL), lambda i, be, sc: (used(i, sc), 0)),
            pl.BlockSpec((1, D_MODEL, D_FF), lambda i, be, sc: (be[i], 0, 0)),
            pl.BlockSpec((1, D_MODEL, D_FF), lambda i, be, sc: (be[i], 0, 0)),
            pl.BlockSpec((1, D_FF, D_MODEL), lambda i, be, sc: (be[i], 0, 0)),
        ],
        out_specs=pl.BlockSpec((rows, D_MODEL), lambda i, be, sc: (i, 0)),
    )
    return pl.pallas_call(
        functools.partial(_expert_kernel, tf=tf),
        grid_spec=grid_spec,
        out_shape=jax.ShapeDtypeStruct((n_rows, D_MODEL), F32),
        compiler_params=_params("arbitrary"),
        name="experts",
    )(block_e, sched, xg, wg, wu, wd)


def _combine_kernel(dest_ref, meta_ref, h_ref, nw_ref, y_ref, o_ref, buf_ref, sem):
    tm = h_ref.shape[0]

    def body(r, carry):
        for j in range(TOP_K):
            d = dest_ref[TOP_K * r + j]
            pltpu.make_async_copy(y_ref.at[pl.ds(d, 1)], buf_ref.at[j, pl.ds(r, 1)], sem).start()
        return carry

    lax.fori_loop(0, tm, body, 0)
    for j in range(TOP_K):
        _row_wait(y_ref, buf_ref.at[j], sem, tm)
    meta = meta_ref[...]
    moe = meta[:, 4:5] * buf_ref[0] + meta[:, 5:6] * buf_ref[1]
    o_ref[...] = _rms_rows(h_ref[...] + moe, nw_ref[...])


def _combine(dest_flat, meta, h, norm_w, y, tm):
    t = h.shape[0]
    return pl.pallas_call(
        _combine_kernel,
        grid=(t // tm,),
        in_specs=[
            pl.BlockSpec((TOP_K * tm,), lambda i: (i,), memory_space=pltpu.SMEM),
            pl.BlockSpec((tm, LANES), lambda i: (i, 0)),
            pl.BlockSpec((tm, D_MODEL), lambda i: (i, 0)),
            pl.BlockSpec((1, D_MODEL), lambda i: (0, 0)),
            pl.BlockSpec(memory_space=pl.ANY),
        ],
        out_specs=pl.BlockSpec((tm, D_MODEL), lambda i: (i, 0)),
        out_shape=jax.ShapeDtypeStruct((t, D_MODEL), F32),
        scratch_shapes=[pltpu.VMEM((TOP_K, tm, D_MODEL), F32), pltpu.SemaphoreType.DMA(())],
        compiler_params=_params("arbitrary"),
        name="combine",
    )(dest_flat, meta, h, norm_w, y)


def _moe_layer(logits, h, ffn_norm_w, final_norm_w, wg, wu, wd, rows, tm_route, tm_rows, tf):
    t = h.shape[0]
    meta, counts = _route(logits, tm_route)
    counts = counts[0, :N_EXPERTS].astype(jnp.int32)
    padded = ((counts + rows - 1) // rows) * rows
    pend = jnp.cumsum(padded)
    pstart = pend - padded
    n_rows = t * TOP_K + N_EXPERTS * rows
    expert = meta[:, 0:TOP_K].astype(jnp.int32)
    rank = meta[:, TOP_K:2 * TOP_K].astype(jnp.int32)
    dest = (pstart[expert] + rank).reshape(t * TOP_K)
    n_used = pend[N_EXPERTS - 1] // rows
    block_start = jnp.minimum(jnp.arange(n_rows // rows, dtype=jnp.int32), n_used - 1) * rows
    block_e = jnp.sum(block_start[:, None] >= pend[None, :], axis=1).astype(jnp.int32)
    sched = jnp.concatenate([pend, padded, n_used[None]]).astype(jnp.int32)
    xg = _dispatch(sched, dest, h, ffn_norm_w, n_rows, tm_rows, rows)
    y = _experts(block_e, sched, xg, wg, wu, wd, rows, tf)
    return _combine(dest, meta, h, final_norm_w, y, tm_rows)


def _pack_in_weights(w_in):
    o = np.cumsum((0, HEADS_W, HEADS_W, HEADS_W, HEADS_W, CONV_CH, HEADS_W, GDN_HEADS, GDN_HEADS, D_MODEL, D_MODEL))
    rq, rk, rv, rg, gqkv, gz, ga, gb, ma, mb = (w_in[:, o[i]:o[i + 1]] for i in range(10))
    main = _bf(jnp.concatenate([gqkv, rq, rk, rv, rg, gz, ma, mb], axis=1))
    small = jnp.concatenate([ga, gb], axis=1)
    small_cols = _bf(jnp.pad(small, ((0, 0), (0, LANES - 2 * GDN_HEADS))))
    small_rows = _bf(jnp.pad(small.T, ((0, SMALL_ROWS - 2 * GDN_HEADS), (0, 0))))
    return main, small_cols, small_rows


def _pick(n, prefs):
    for p in prefs:
        if n % p == 0:
            return p
    raise ValueError(f"no tile in {prefs} divides {n}")


def kernel(x, norm_mix, w_in, conv_w, a_log, dt_bias, gdn_norm, w_branch, w_out, norm_ffn,
           dense_w_gate, dense_w_up, dense_w_down, router, moe_w_gate, moe_w_up, moe_w_down, final_norm):
    batch, seq, d = x.shape
    depth = norm_mix.shape[0]
    assert d == D_MODEL and seq % GDN_CHUNK == 0
    t = batch * seq
    blk = _pick(seq, (256, 128, 64))
    tm = _pick(t, (1024, 512, 256))
    tm_rows = _pick(t, (512, 256))
    tf_dense = 256
    rows = 512
    tabs = _mixer_tables(seq, blk)

    h = x.reshape(t, d)
    out = None
    for layer in range(depth):
        w_main, w_small, w_small_t = _pack_in_weights(w_in[layer])
        proj, gab, gabt = _in_projection(h, norm_mix[layer][None, :], w_main, w_small, w_small_t, tm, CONV_CH)
        prow = jnp.zeros((8, LANES), F32)
        prow = prow.at[0, :GDN_HEADS].set(a_log[layer]).at[1, :GDN_HEADS].set(dt_bias[layer])
        pcol = jnp.zeros((SMALL_ROWS, LANES), F32)
        pcol = pcol.at[:GDN_HEADS, 0].set(a_log[layer]).at[:GDN_HEADS, 1].set(dt_bias[layer])
        convw = jnp.pad(conv_w[layer], ((0, 8 - CONV_K), (0, 0)))
        ret, gdn = _token_mixers(proj, gab, gabt, tabs, convw, prow, pcol, gdn_norm[layer][None, :],
                                 batch, seq, blk)
        i = layer // 2
        is_moe = layer % 2 == 1
        router_w = _bf(jnp.pad(router[i], ((0, 0), (0, LANES - N_EXPERTS)))) if is_moe else None
        h, second = _merge_project(ret, gdn, proj, h, _bf(w_branch[layer, 0]), _bf(w_branch[layer, 1]),
                                   _bf(w_out[layer]), norm_ffn[layer][None, :], router_w, tm_rows)
        if not is_moe:
            h = _dense_ffn(second, h, _bf(dense_w_gate[i])[None], _bf(dense_w_up[i])[None],
                           _bf(dense_w_down[i])[None], tm_rows, tf_dense)
            if layer == depth - 1:
                raise NotImplementedError("final norm after a dense layer is not wired")
        else:
            if layer != depth - 1:
                raise NotImplementedError("routed layer must be last")
            out = _moe_layer(second, h, norm_ffn[layer][None, :], final_norm[None, :],
                             _bf(moe_w_gate[i]), _bf(moe_w_up[i]), _bf(moe_w_down[i]),
                             rows, tm_rows, tm_rows, 256)
    return out.reshape(batch, seq, d)
```

```python
import functools
import math

import jax
import jax.numpy as jnp
import numpy as np
from jax import lax
from jax.experimental import pallas as pl
from jax.experimental.pallas import tpu as pltpu

F32 = jnp.float32
BF16 = jnp.bfloat16

D_MODEL = 1024
RET_HEADS = 4
GDN_HEADS = 4
HEAD_DIM = 128
HEADS_W = 512
CONV_K = 4
CONV_CH = 3 * HEADS_W
GDN_CHUNK = 64
ROPE_BASE = 10000.0
D_FF = 2816
N_EXPERTS = 8
TOP_K = 2
EPS = 1e-6
LANES = 128
SMALL_ROWS = 16
ROW_DMA_UNROLL = 8

MAIN_COLS = CONV_CH + 5 * HEADS_W + 2 * D_MODEL
VMEM_LIMIT = 56 * 1024 * 1024

HIGHEST = lax.Precision.HIGHEST


def _bf(x):
    return x.astype(BF16)


def _dot(a, b):
    return jnp.dot(a, b, preferred_element_type=F32)


def _dot_nt(a, b):
    return lax.dot_general(a, b, (((1,), (1,)), ((), ())), preferred_element_type=F32)


def _dot_tn(a, b):
    return lax.dot_general(a, b, (((0,), (0,)), ((), ())), preferred_element_type=F32)


def _dot_f32(a, b):
    return jnp.dot(a, b, preferred_element_type=F32, precision=HIGHEST)


def _sigmoid(x):
    return 1.0 / (1.0 + jnp.exp(-x))


def _silu(x):
    return x * _sigmoid(x)


def _softplus(x):
    return jnp.maximum(x, 0.0) + jnp.log(1.0 + jnp.exp(-jnp.abs(x)))


def _rms_rows(x, w):
    ms = jnp.mean(x * x, axis=-1, keepdims=True)
    return x * lax.rsqrt(ms + EPS) * w


def _params(*sem):
    return pltpu.CompilerParams(dimension_semantics=sem, vmem_limit_bytes=VMEM_LIMIT)


def _inproj_kernel(x_ref, nw_ref, w_ref, ws_ref, wst_ref, o_ref, os_ref, ost_ref, hn_ref):
    @pl.when(pl.program_id(1) == 0)
    def _():
        hn = _bf(_rms_rows(x_ref[...], nw_ref[...]))
        hn_ref[...] = hn
        os_ref[...] = _dot(hn, ws_ref[...])
        ost_ref[...] = _dot_nt(wst_ref[...], hn)

    o_ref[...] = _bf(_dot(hn_ref[...], w_ref[...]))


def _in_projection(h, norm_w, w_main, w_small, w_small_t, tm, tn):
    t = h.shape[0]
    return pl.pallas_call(
        _inproj_kernel,
        grid=(t // tm, MAIN_COLS // tn),
        in_specs=[
            pl.BlockSpec((tm, D_MODEL), lambda i, j: (i, 0)),
            pl.BlockSpec((1, D_MODEL), lambda i, j: (0, 0)),
            pl.BlockSpec((D_MODEL, tn), lambda i, j: (0, j)),
            pl.BlockSpec((D_MODEL, LANES), lambda i, j: (0, 0)),
            pl.BlockSpec((SMALL_ROWS, D_MODEL), lambda i, j: (0, 0)),
        ],
        out_specs=[
            pl.BlockSpec((tm, tn), lambda i, j: (i, j)),
            pl.BlockSpec((tm, LANES), lambda i, j: (i, 0)),
            pl.BlockSpec((SMALL_ROWS, tm), lambda i, j: (0, i)),
        ],
        out_shape=[
            jax.ShapeDtypeStruct((t, MAIN_COLS), BF16),
            jax.ShapeDtypeStruct((t, LANES), F32),
            jax.ShapeDtypeStruct((SMALL_ROWS, t), F32),
        ],
        scratch_shapes=[pltpu.VMEM((tm, D_MODEL), BF16)],
        compiler_params=_params("parallel", "arbitrary"),
        name="in_projection",
    )(h, norm_w, w_main, w_small, w_small_t)


def _hi_lo(x):
    hi = _bf(x)
    lo_f = x - hi.astype(F32)
    return hi, _bf(lo_f), lo_f


def _dup_lhs(x2, lo_f, left):
    w = _bf(jnp.where(left, x2, lo_f))
    return jnp.concatenate([w, w], axis=1)


def _dup_rhs(hi, lo):
    return jnp.concatenate([hi, hi, lo, lo], axis=0)


def _mixer_kernel(gqkv_ref, rq_ref, rk_ref, rv_ref, rg_ref, gz_ref, gab_ref, gabt_ref,
                  cos_ref, sin_ref, dmat_ref, qdec_ref, kdec_ref, cdec_ref,
                  convw_ref, prow_ref, pcol_ref, gnorm_ref, tril_ref, triu_ref,
                  ret_ref, gdn_ref,
                  rstate_ref, gstate_ref, xs_ref, *, blk):
    c = GDN_CHUNK
    n_chunks = blk // c
    heads = range(GDN_HEADS)
    inst = [(h, n) for h in heads for n in range(n_chunks)]

    @pl.when(pl.program_id(1) == 0)
    def _():
        rstate_ref[...] = jnp.zeros_like(rstate_ref)
        gstate_ref[...] = jnp.zeros_like(gstate_ref)
        xs_ref[0:8, :] = jnp.zeros((8, CONV_CH), F32)

    hsl = [slice(h * HEAD_DIM, (h + 1) * HEAD_DIM) for h in range(RET_HEADS)]

    cos2 = cos_ref[...]
    sin2 = sin_ref[...]
    half = HEAD_DIM // 2
    rq = [rq_ref[0, :, sl].astype(F32) for sl in hsl]
    rk = [rk_ref[0, :, sl].astype(F32) for sl in hsl]
    rv = [rv_ref[0, :, sl] for sl in hsl]
    rq = [q * cos2 + pltpu.roll(q, half, 1) * sin2 for q in rq]
    rk = [(k * cos2 + pltpu.roll(k, half, 1) * sin2) * (HEAD_DIM ** -0.5) for k in rk]
    rstate = [rstate_ref[h] for h in range(RET_HEADS)]
    scores = [_dot_nt(_bf(rq[h]), _bf(rk[h])) * dmat_ref[h] for h in range(RET_HEADS)]
    inter = [_dot(_bf(rq[h] * qdec_ref[h]), _bf(rstate[h])) for h in range(RET_HEADS)]
    kv = [_dot_tn(_bf(rk[h] * kdec_ref[h]), rv[h]) for h in range(RET_HEADS)]
    ro = [_dot(_bf(scores[h]), rv[h]) + inter[h] for h in range(RET_HEADS)]
    for h in range(RET_HEADS):
        rstate_ref[h] = rstate[h] * cdec_ref[h] + kv[h]
        o = ro[h]
        o = o * lax.rsqrt(jnp.mean(o * o, axis=-1, keepdims=True) + EPS)
        ret_ref[0, :, hsl[h]] = _bf(_silu(rg_ref[0, :, hsl[h]].astype(F32)) * o)

    xs_ref[8:8 + blk, :] = gqkv_ref[0].astype(F32)
    conv = convw_ref[3:4, :] * xs_ref[pl.ds(8, blk), :]
    for s in range(1, CONV_K):
        conv = conv + convw_ref[3 - s:4 - s, :] * xs_ref[pl.ds(8 - s, blk), :]
    xs_ref[0:8, :] = xs_ref[blk:blk + 8, :]
    conv = _silu(conv)

    gab = gab_ref[0]
    g_cols = -jnp.exp(prow_ref[0:1, :]) * _softplus(gab + prow_ref[1:2, :])
    beta_cols = _sigmoid(gab)
    gc_cols = _dot_f32(tril_ref[...], g_cols)
    egc_cols = jnp.exp(gc_cols)
    g_rows = -jnp.exp(pcol_ref[:, 0:1]) * _softplus(gabt_ref[...] + pcol_ref[:, 1:2])
    gc_rows = _dot_f32(g_rows, triu_ref[...])

    ri = lax.broadcasted_iota(jnp.int32, (c, 2 * c), 0)
    lane = lax.broadcasted_iota(jnp.int32, (c, 2 * c), 1)
    ci = lane & (c - 1)
    left = lane < c
    ge = ri >= ci
    gt = ri > ci
    eye = jnp.where(ri == ci, 1.0, 0.0).astype(F32)
    level_masks = []
    for lg in range(int(math.log2(c))):
        b = 1 << lg
        same_block = (ri >> (lg + 1)) == (ci >> (lg + 1))
        level_masks.append(same_block & ((ri & (2 * b - 1)) >= b) & ((ci & (2 * b - 1)) < b))

    qh, kh, vh = [], [], []
    for h in heads:
        q = conv[:, hsl[h]]
        k = conv[:, HEADS_W + h * HEAD_DIM:HEADS_W + (h + 1) * HEAD_DIM]
        vh.append(conv[:, 2 * HEADS_W + h * HEAD_DIM:2 * HEADS_W + (h + 1) * HEAD_DIM])
        qh.append(q * lax.rsqrt(jnp.sum(q * q, axis=-1, keepdims=True) + EPS) * (HEAD_DIM ** -0.5))
        kh.append(k * lax.rsqrt(jnp.sum(k * k, axis=-1, keepdims=True) + EPS))

    rows = [slice(n * c, (n + 1) * c) for n in range(n_chunks)]
    kc = {(h, n): kh[h][rows[n]] for h, n in inst}
    qc = {(h, n): qh[h][rows[n]] for h, n in inst}
    bcol = {(h, n): beta_cols[rows[n], 4 + h:5 + h] for h, n in inst}
    gcol = {(h, n): gc_cols[rows[n], h:h + 1] for h, n in inst}
    egcol = {(h, n): egc_cols[rows[n], h:h + 1] for h, n in inst}
    glast = {(h, n): gc_cols[n * c + c - 1:(n + 1) * c, h:h + 1] for h, n in inst}
    grow = {(h, n): gc_rows[h:h + 1, 2 * c * n:2 * c * (n + 1)] for h, n in inst}
    kb = {i: kc[i] * bcol[i] for i in inst}
    kcb = {i: _bf(kc[i]) for i in inst}
    kcb2 = {i: jnp.concatenate([kcb[i], kcb[i]], axis=0) for i in inst}
    decay = {i: jnp.where(ge, jnp.exp(jnp.where(ge, gcol[i] - grow[i], 0.0)), 0.0) for i in inst}
    a2 = {i: jnp.where(gt, _dot_nt(_bf(kb[i]), kcb2[i]) * decay[i], 0.0) for i in inst}
    attn = {i: _bf(jnp.where(ge, _dot_nt(_bf(qc[i]), kcb2[i]) * decay[i], 0.0)[:, :c]) for i in inst}

    x2 = {i: eye - jnp.where(level_masks[0], a2[i], 0.0) for i in inst}
    a_hl = {i: _hi_lo(a2[i]) for i in inst}
    for m in level_masks[1:]:
        mb = jnp.where(m, 1.0, 0.0).astype(BF16)
        x_hl = {i: _hi_lo(x2[i]) for i in inst}
        t = {i: _dot(_dup_lhs(x2[i], x_hl[i][2], left), _dup_rhs(a_hl[i][0] * mb, a_hl[i][1] * mb)) for i in inst}
        t_hl = {i: _hi_lo(t[i]) for i in inst}
        x2 = {i: x2[i] - _dot(_dup_lhs(t[i], t_hl[i][2], left), _dup_rhs(x_hl[i][0], x_hl[i][1])) for i in inst}

    uw = {}
    for h, n in inst:
        i = (h, n)
        rhs = _hi_lo(jnp.concatenate([vh[h][rows[n]] * bcol[i], kb[i] * egcol[i]], axis=1))
        uw[i] = _dot(_dup_lhs(x2[i], _hi_lo(x2[i])[2], left), _dup_rhs(rhs[0], rhs[1]))
    wq = {i: _bf(jnp.concatenate([uw[i][:, HEAD_DIM:], qc[i] * egcol[i]], axis=0)) for i in inst}
    kd = {i: _bf(kc[i] * jnp.exp(glast[i] - gcol[i])) for i in inst}
    eg = {i: jnp.exp(glast[i]) for i in inst}

    state = [gstate_ref[h] for h in heads]
    outs = [[] for _ in heads]
    for n in range(n_chunks):
        sb = [_bf(state[h]) for h in heads]
        ws = [_dot(wq[(h, n)], sb[h]) for h in heads]
        vnb = [_bf(uw[(h, n)][:, :HEAD_DIM] - ws[h][:c]) for h in heads]
        for h in heads:
            outs[h].append(ws[h][c:] + _dot(attn[(h, n)], vnb[h]))
        state = [state[h] * eg[(h, n)] + _dot_tn(kd[(h, n)], vnb[h]) for h in heads]

    gnorm = gnorm_ref[...]
    for h in heads:
        gstate_ref[h] = state[h]
        o = jnp.concatenate(outs[h], axis=0)
        o = o * lax.rsqrt(jnp.mean(o * o, axis=-1, keepdims=True) + EPS)
        gdn_ref[0, :, hsl[h]] = _bf(o * gnorm * _silu(gz_ref[0, :, hsl[h]].astype(F32)))


def _token_mixers(proj, gab, gabt, tabs, conv_w, prow, pcol, gnorm, batch, seq, blk):
    proj3 = proj.reshape(batch, seq, MAIN_COLS)
    gab3 = gab.reshape(batch, seq, LANES)
    n_blk = seq // blk
    cos2, sin2, dmat, qdec, kdec, cdec, tril, triu = tabs
    w512 = lambda col: pl.BlockSpec((1, blk, HEADS_W), lambda b, n, col=col: (b, n, col))
    full = lambda shape: pl.BlockSpec(shape, lambda b, n: (0,) * len(shape))
    out = pl.pallas_call(
        functools.partial(_mixer_kernel, blk=blk),
        grid=(batch, seq // blk),
        in_specs=[
            pl.BlockSpec((1, blk, CONV_CH), lambda b, n: (b, n, 0)),
            w512(3), w512(4), w512(5), w512(6), w512(7),
            pl.BlockSpec((1, blk, LANES), lambda b, n: (b, n, 0)),
            pl.BlockSpec((SMALL_ROWS, blk), lambda b, n: (0, b * n_blk + n)),
            pl.BlockSpec((blk, HEAD_DIM), lambda b, n: (n, 0)),
            pl.BlockSpec((blk, HEAD_DIM), lambda b, n: (n, 0)),
            full((RET_HEADS, blk, blk)),
            full((RET_HEADS, blk, HEAD_DIM)),
            full((RET_HEADS, blk, HEAD_DIM)),
            full((RET_HEADS, 1, HEAD_DIM)),
            full((8, CONV_CH)),
            full((8, LANES)),
            full((SMALL_ROWS, LANES)),
            full((1, HEAD_DIM)),
            full((blk, blk)),
            full((blk, 2 * blk)),
        ],
        out_specs=[
            pl.BlockSpec((1, blk, HEADS_W), lambda b, n: (b, n, 0)),
            pl.BlockSpec((1, blk, HEADS_W), lambda b, n: (b, n, 0)),
        ],
        out_shape=[
            jax.ShapeDtypeStruct((batch, seq, HEADS_W), BF16),
            jax.ShapeDtypeStruct((batch, seq, HEADS_W), BF16),
        ],
        scratch_shapes=[
            pltpu.VMEM((RET_HEADS, HEAD_DIM, HEAD_DIM), F32),
            pltpu.VMEM((GDN_HEADS, HEAD_DIM, HEAD_DIM), F32),
            pltpu.VMEM((blk + 8, CONV_CH), F32),
        ],
        compiler_params=_params("parallel", "arbitrary"),
        name="token_mixers",
    )(proj3, proj3, proj3, proj3, proj3, proj3, gab3, gabt,
      cos2, sin2, dmat, qdec, kdec, cdec, conv_w, prow, pcol, gnorm, tril, triu)
    ret, gdn = out
    return ret.reshape(batch * seq, HEADS_W), gdn.reshape(batch * seq, HEADS_W)


def _mixer_tables(seq, blk):
    inv_freq = ROPE_BASE ** (-jnp.arange(0, HEAD_DIM, 2, dtype=F32) / HEAD_DIM)
    ang = jnp.arange(seq, dtype=F32)[:, None] * inv_freq[None, :]
    cos, sin = jnp.cos(ang), jnp.sin(ang)
    cos2 = jnp.concatenate([cos, cos], axis=-1)
    sin2 = jnp.concatenate([-sin, sin], axis=-1)
    log_gamma = jnp.log(1.0 - jnp.exp2(-5.0 - jnp.arange(RET_HEADS, dtype=F32)))
    pos = jnp.arange(blk, dtype=F32)
    diff = pos[:, None] - pos[None, :]
    dmat = jnp.where(diff >= 0, jnp.exp(jnp.maximum(diff, 0.0)[None] * log_gamma[:, None, None]), 0.0)
    ones = jnp.ones((1, 1, HEAD_DIM), F32)
    qdec = jnp.exp((pos + 1.0)[None, :] * log_gamma[:, None])[:, :, None] * ones
    kdec = jnp.exp((blk - 1 - pos)[None, :] * log_gamma[:, None])[:, :, None] * ones
    cdec = jnp.exp(blk * log_gamma)[:, None, None] * ones
    idx = np.arange(blk)
    same = (idx[:, None] // GDN_CHUNK) == (idx[None, :] // GDN_CHUNK)
    tril = jnp.asarray((same & (idx[:, None] >= idx[None, :])).astype(np.float32))
    col = np.arange(2 * blk)
    col_chunk, col_pos = col // (2 * GDN_CHUNK), col % GDN_CHUNK
    triu = jnp.asarray(((idx[:, None] // GDN_CHUNK == col_chunk[None, :])
                        & (idx[:, None] % GDN_CHUNK <= col_pos[None, :])).astype(np.float32))
    return cos2, sin2, dmat, qdec, kdec, cdec, tril, triu


def _merge_kernel(ret_ref, gdn_ref, ga_ref, gb_ref, h_ref, wa_ref, wb_ref, wo_ref, nw_ref, *rest,
                  with_router):
    if with_router:
        wr_ref, ho_ref, logit_ref = rest
    else:
        ho_ref, hn_ref = rest
    ya = _dot(ret_ref[...], wa_ref[...])
    yb = _dot(gdn_ref[...], wb_ref[...])
    merged = _sigmoid(ga_ref[...].astype(F32)) * ya + _sigmoid(gb_ref[...].astype(F32)) * yb
    h_new = h_ref[...] + _dot(_bf(merged), wo_ref[...])
    ho_ref[...] = h_new
    hn = _bf(_rms_rows(h_new, nw_ref[...]))
    if with_router:
        logit_ref[...] = _dot(hn, wr_ref[...])
    else:
        hn_ref[...] = hn


def _merge_project(ret, gdn, proj, h, wa, wb, wo, norm_w, router_w, tm):
    t = h.shape[0]
    with_router = router_w is not None
    in_specs = [
        pl.BlockSpec((tm, HEADS_W), lambda i: (i, 0)),
        pl.BlockSpec((tm, HEADS_W), lambda i: (i, 0)),
        pl.BlockSpec((tm, D_MODEL), lambda i: (i, 4)),
        pl.BlockSpec((tm, D_MODEL), lambda i: (i, 5)),
        pl.BlockSpec((tm, D_MODEL), lambda i: (i, 0)),
        pl.BlockSpec((HEADS_W, D_MODEL), lambda i: (0, 0)),
        pl.BlockSpec((HEADS_W, D_MODEL), lambda i: (0, 0)),
        pl.BlockSpec((D_MODEL, D_MODEL), lambda i: (0, 0)),
        pl.BlockSpec((1, D_MODEL), lambda i: (0, 0)),
    ]
    args = [ret, gdn, proj, proj, h, wa, wb, wo, norm_w]
    if with_router:
        in_specs.append(pl.BlockSpec((D_MODEL, LANES), lambda i: (0, 0)))
        args.append(router_w)
        second = (pl.BlockSpec((tm, LANES), lambda i: (i, 0)), jax.ShapeDtypeStruct((t, LANES), F32))
    else:
        second = (pl.BlockSpec((tm, D_MODEL), lambda i: (i, 0)), jax.ShapeDtypeStruct((t, D_MODEL), BF16))
    return pl.pallas_call(
        functools.partial(_merge_kernel, with_router=with_router),
        grid=(t // tm,),
        in_specs=in_specs,
        out_specs=[pl.BlockSpec((tm, D_MODEL), lambda i: (i, 0)), second[0]],
        out_shape=[jax.ShapeDtypeStruct((t, D_MODEL), F32), second[1]],
        compiler_params=_params("parallel"),
        name="merge_project",
    )(*args)


def _swiglu_rows(x, wg_ref, wu_ref, wd_ref, acc, tf):
    for f in range(D_FF // tf):
        cols = slice(f * tf, (f + 1) * tf)
        act = _silu(_dot(x, wg_ref[0, :, cols])) * _dot(x, wu_ref[0, :, cols])
        acc = acc + _dot(_bf(act), wd_ref[0, cols, :])
    return acc


def _ffn_kernel(hn_ref, h_ref, wg_ref, wu_ref, wd_ref, o_ref, *, tf):
    o_ref[...] = _swiglu_rows(hn_ref[...], wg_ref, wu_ref, wd_ref, h_ref[...], tf)


def _dense_ffn(hn, h, wg, wu, wd, tm, tf):
    t = h.shape[0]
    return pl.pallas_call(
        functools.partial(_ffn_kernel, tf=tf),
        grid=(t // tm,),
        in_specs=[
            pl.BlockSpec((tm, D_MODEL), lambda i: (i, 0)),
            pl.BlockSpec((tm, D_MODEL), lambda i: (i, 0)),
            pl.BlockSpec((1, D_MODEL, D_FF), lambda i: (0, 0, 0)),
            pl.BlockSpec((1, D_MODEL, D_FF), lambda i: (0, 0, 0)),
            pl.BlockSpec((1, D_FF, D_MODEL), lambda i: (0, 0, 0)),
        ],
        out_specs=pl.BlockSpec((tm, D_MODEL), lambda i: (i, 0)),
        out_shape=jax.ShapeDtypeStruct((t, D_MODEL), F32),
        compiler_params=_params("parallel"),
        name="dense_ffn",
    )(hn, h, wg, wu, wd)


def _route_kernel(logit_ref, tri_ref, meta_ref, count_ref, run_ref):
    @pl.when(pl.program_id(0) == 0)
    def _():
        run_ref[...] = jnp.zeros_like(run_ref)

    tm = logit_ref.shape[0]
    lane = lax.broadcasted_iota(jnp.int32, (tm, LANES), 1)
    neg = jnp.float32(-jnp.inf)
    l1 = jnp.where(lane < N_EXPERTS, logit_ref[...], neg)
    m1 = jnp.max(l1, axis=-1, keepdims=True)
    i1 = jnp.min(jnp.where(l1 == m1, lane, LANES), axis=-1, keepdims=True)
    l2 = jnp.where(lane == i1, neg, l1)
    m2 = jnp.max(l2, axis=-1, keepdims=True)
    i2 = jnp.min(jnp.where(l2 == m2, lane, LANES), axis=-1, keepdims=True)
    e2 = jnp.exp(m2 - m1)
    g1 = 1.0 / (1.0 + e2)
    g2 = e2 / (1.0 + e2)
    sel1 = lane == i1
    sel2 = lane == i2
    onehot = jnp.where(sel1 | sel2, 1.0, 0.0).astype(F32)
    before = _dot(tri_ref[...], _bf(onehot)) + run_ref[0:1, :]
    r1 = jnp.sum(jnp.where(sel1, before, 0.0), axis=-1, keepdims=True)
    r2 = jnp.sum(jnp.where(sel2, before, 0.0), axis=-1, keepdims=True)
    run_new = run_ref[0:1, :] + jnp.sum(onehot, axis=0, keepdims=True)
    run_ref[0:1, :] = run_new
    count_ref[...] = jnp.broadcast_to(run_new, count_ref.shape)
    meta = jnp.where(lane == 0, i1.astype(F32), 0.0)
    meta = jnp.where(lane == 1, i2.astype(F32), meta)
    meta = jnp.where(lane == 2, r1, meta)
    meta = jnp.where(lane == 3, r2, meta)
    meta = jnp.where(lane == 4, g1, meta)
    meta = jnp.where(lane == 5, g2, meta)
    meta_ref[...] = meta


def _route(logits, tm):
    t = logits.shape[0]
    idx = np.arange(tm)
    tri = jnp.asarray((idx[:, None] > idx[None, :]).astype(np.float32), dtype=BF16)
    return pl.pallas_call(
        _route_kernel,
        grid=(t // tm,),
        in_specs=[pl.BlockSpec((tm, LANES), lambda i: (i, 0)),
                  pl.BlockSpec((tm, tm), lambda i: (0, 0))],
        out_specs=[pl.BlockSpec((tm, LANES), lambda i: (i, 0)),
                   pl.BlockSpec((8, LANES), lambda i: (0, 0))],
        out_shape=[jax.ShapeDtypeStruct((t, LANES), F32),
                   jax.ShapeDtypeStruct((8, LANES), F32)],
        scratch_shapes=[pltpu.VMEM((8, LANES), F32)],
        compiler_params=_params("arbitrary"),
        name="route",
    )(logits, tri)


def _row_wait(src_hbm, dst_vmem, sem, rows):
    pltpu.make_async_copy(src_hbm.at[pl.ds(0, rows)], dst_vmem, sem).wait()


def _dispatch_kernel(sched_ref, dest_ref, h_ref, nw_ref, xg_ref, hn_ref, zero_ref, sem, zsem, *, rows):
    tm = h_ref.shape[0]

    @pl.when(pl.program_id(0) == 0)
    def _():
        zero_ref[...] = jnp.zeros_like(zero_ref)
        for e in range(N_EXPERTS):
            @pl.when(sched_ref[N_EXPERTS + e] > 0)
            def _():
                start = pl.multiple_of(sched_ref[e] - rows, rows)
                clear = pltpu.make_async_copy(zero_ref, xg_ref.at[pl.ds(start, rows)], zsem)
                clear.start()
                clear.wait()

        def clear_unused(b, carry):
            clear = pltpu.make_async_copy(zero_ref, xg_ref.at[pl.ds(pl.multiple_of(b * rows, rows), rows)], zsem)
            clear.start()
            clear.wait()
            return carry

        lax.fori_loop(sched_ref[2 * N_EXPERTS], xg_ref.shape[0] // rows, clear_unused, 0)

    step = pl.program_id(0)
    slot = step % 2
    rows_ref = hn_ref.at[slot]
    rows_ref[...] = _rms_rows(h_ref[...], nw_ref[...])

    def body(r, carry):
        for j in range(TOP_K):
            d = dest_ref[TOP_K * r + j]
            pltpu.make_async_copy(rows_ref.at[pl.ds(r, 1)], xg_ref.at[pl.ds(d, 1)], sem.at[slot]).start()
        return carry

    lax.fori_loop(0, tm, body, 0, unroll=ROW_DMA_UNROLL)

    def drain(s):
        for _ in range(TOP_K):
            pltpu.make_async_copy(hn_ref.at[s], xg_ref.at[pl.ds(0, tm)], sem.at[s]).wait()

    @pl.when(step > 0)
    def _():
        drain(1 - slot)

    @pl.when(step == pl.num_programs(0) - 1)
    def _():
        drain(slot)


def _dispatch(sched, dest_flat, h, norm_w, n_rows, tm, rows):
    t = h.shape[0]
    grid_spec = pltpu.PrefetchScalarGridSpec(
        num_scalar_prefetch=1,
        grid=(t // tm,),
        in_specs=[
            pl.BlockSpec((TOP_K * tm,), lambda i, sc: (i,), memory_space=pltpu.SMEM),
            pl.BlockSpec((tm, D_MODEL), lambda i, sc: (i, 0)),
            pl.BlockSpec((1, D_MODEL), lambda i, sc: (0, 0)),
        ],
        out_specs=pl.BlockSpec(memory_space=pl.ANY),
        scratch_shapes=[pltpu.VMEM((2, tm, D_MODEL), F32), pltpu.VMEM((rows, D_MODEL), F32),
                        pltpu.SemaphoreType.DMA((2,)), pltpu.SemaphoreType.DMA(())],
    )
    return pl.pallas_call(
        functools.partial(_dispatch_kernel, rows=rows),
        grid_spec=grid_spec,
        out_shape=jax.ShapeDtypeStruct((n_rows, D_MODEL), F32),
        compiler_params=_params("arbitrary"),
        name="dispatch",
    )(sched, dest_flat, h, norm_w)


def _expert_kernel(be_ref, sched_ref, x_ref, wg_ref, wu_ref, wd_ref, y_ref, *, tf):
    del be_ref

    @pl.when(pl.program_id(0) < sched_ref[2 * N_EXPERTS])
    def _():
        y_ref[...] = _swiglu_rows(_bf(x_ref[...]), wg_ref, wu_ref, wd_ref, jnp.zeros(y_ref.shape, F32), tf)

    @pl.when(pl.program_id(0) >= sched_ref[2 * N_EXPERTS])
    def _():
        y_ref[...] = jnp.zeros_like(y_ref)


def _experts(block_e, sched, xg, wg, wu, wd, rows, tf):
    n_rows = xg.shape[0]
    used = lambda i, sc: jnp.minimum(i, sc[2 * N_EXPERTS] - 1)
    grid_spec = pltpu.PrefetchScalarGridSpec(
        num_scalar_prefetch=2,
        grid=(n_rows // rows,),
        in_specs=[
            pl.BlockSpec((rows, D_MODEL), lambda i, be, sc: (used(i, sc), 0)),
            pl.BlockSpec((1, D_MODEL, D_FF), lambda i, be, sc: (be[i], 0, 0)),
            pl.BlockSpec((1, D_MODEL, D_FF), lambda i, be, sc: (be[i], 0, 0)),
            pl.BlockSpec((1, D_FF, D_MODEL), lambda i, be, sc: (be[i], 0, 0)),
        ],
        out_specs=pl.BlockSpec((rows, D_MODEL), lambda i, be, sc: (i, 0)),
    )
    return pl.pallas_call(
        functools.partial(_expert_kernel, tf=tf),
        grid_spec=grid_spec,
        out_shape=jax.ShapeDtypeStruct((n_rows, D_MODEL), F32),
        compiler_params=_params("arbitrary"),
        name="experts",
    )(block_e, sched, xg, wg, wu, wd)


def _combine_kernel(dest_ref, next_dest_ref, meta_ref, h_ref, nw_ref, y_ref, o_ref, buf_ref, sem):
    tm = h_ref.shape[0]
    step = pl.program_id(0)
    slot = step % 2

    def fetch(idx_ref, s):
        def body(r, carry):
            for j in range(TOP_K):
                d = idx_ref[TOP_K * r + j]
                pltpu.make_async_copy(y_ref.at[pl.ds(d, 1)], buf_ref.at[s, j, pl.ds(r, 1)], sem.at[s]).start()
            return carry

        lax.fori_loop(0, tm, body, 0, unroll=ROW_DMA_UNROLL)

    @pl.when(step == 0)
    def _():
        fetch(dest_ref, 0)

    @pl.when(step + 1 < pl.num_programs(0))
    def _():
        fetch(next_dest_ref, 1 - slot)

    for j in range(TOP_K):
        _row_wait(y_ref, buf_ref.at[slot, j], sem.at[slot], tm)
    meta = meta_ref[...]
    moe = meta[:, 4:5] * buf_ref[slot, 0] + meta[:, 5:6] * buf_ref[slot, 1]
    o_ref[...] = _rms_rows(h_ref[...] + moe, nw_ref[...])


def _combine(dest_flat, meta, h, norm_w, y, tm):
    t = h.shape[0]
    last = t // tm - 1
    return pl.pallas_call(
        _combine_kernel,
        grid=(t // tm,),
        in_specs=[
            pl.BlockSpec((TOP_K * tm,), lambda i: (i,), memory_space=pltpu.SMEM),
            pl.BlockSpec((TOP_K * tm,), lambda i: (jnp.minimum(i + 1, last),), memory_space=pltpu.SMEM),
            pl.BlockSpec((tm, LANES), lambda i: (i, 0)),
            pl.BlockSpec((tm, D_MODEL), lambda i: (i, 0)),
            pl.BlockSpec((1, D_MODEL), lambda i: (0, 0)),
            pl.BlockSpec(memory_space=pl.ANY),
        ],
        out_specs=pl.BlockSpec((tm, D_MODEL), lambda i: (i, 0)),
        out_shape=jax.ShapeDtypeStruct((t, D_MODEL), F32),
        scratch_shapes=[pltpu.VMEM((2, TOP_K, tm, D_MODEL), F32), pltpu.SemaphoreType.DMA((2,))],
        compiler_params=_params("arbitrary"),
        name="combine",
    )(dest_flat, dest_flat, meta, h, norm_w, y)


def _moe_layer(logits, h, ffn_norm_w, final_norm_w, wg, wu, wd, rows, tm_route, tm_rows, tf):
    t = h.shape[0]
    meta, counts = _route(logits, tm_route)
    counts = counts[0, :N_EXPERTS].astype(jnp.int32)
    padded = ((counts + rows - 1) // rows) * rows
    pend = jnp.cumsum(padded)
    pstart = pend - padded
    n_rows = t * TOP_K + N_EXPERTS * rows
    expert = meta[:, 0:TOP_K].astype(jnp.int32)
    rank = meta[:, TOP_K:2 * TOP_K].astype(jnp.int32)
    dest = (pstart[expert] + rank).reshape(t * TOP_K)
    n_used = pend[N_EXPERTS - 1] // rows
    block_start = jnp.minimum(jnp.arange(n_rows // rows, dtype=jnp.int32), n_used - 1) * rows
    block_e = jnp.sum(block_start[:, None] >= pend[None, :], axis=1).astype(jnp.int32)
    sched = jnp.concatenate([pend, padded, n_used[None]]).astype(jnp.int32)
    xg = _dispatch(sched, dest, h, ffn_norm_w, n_rows, tm_rows, rows)
    y = _experts(block_e, sched, xg, wg, wu, wd, rows, tf)
    return _combine(dest, meta, h, final_norm_w, y, tm_rows)


def _pack_in_weights(w_in):
    o = np.cumsum((0, HEADS_W, HEADS_W, HEADS_W, HEADS_W, CONV_CH, HEADS_W, GDN_HEADS, GDN_HEADS, D_MODEL, D_MODEL))
    rq, rk, rv, rg, gqkv, gz, ga, gb, ma, mb = (w_in[:, o[i]:o[i + 1]] for i in range(10))
    main = _bf(jnp.concatenate([gqkv, rq, rk, rv, rg, gz, ma, mb], axis=1))
    small = jnp.concatenate([ga, gb], axis=1)
    small_cols = _bf(jnp.pad(small, ((0, 0), (0, LANES - 2 * GDN_HEADS))))
    small_rows = _bf(jnp.pad(small.T, ((0, SMALL_ROWS - 2 * GDN_HEADS), (0, 0))))
    return main, small_cols, small_rows


def _pick(n, prefs):
    for p in prefs:
        if n % p == 0:
            return p
    raise ValueError(f"no tile in {prefs} divides {n}")


def kernel(x, norm_mix, w_in, conv_w, a_log, dt_bias, gdn_norm, w_branch, w_out, norm_ffn,
           dense_w_gate, dense_w_up, dense_w_down, router, moe_w_gate, moe_w_up, moe_w_down, final_norm):
    batch, seq, d = x.shape
    depth = norm_mix.shape[0]
    assert d == D_MODEL and seq % GDN_CHUNK == 0
    t = batch * seq
    blk = _pick(seq, (256, 128, 64))
    tm = _pick(t, (1024, 512, 256))
    tm_rows = _pick(t, (512, 256))
    tf_dense = 256
    rows = 512
    tabs = _mixer_tables(seq, blk)

    h = x.reshape(t, d)
    out = None
    for layer in range(depth):
        w_main, w_small, w_small_t = _pack_in_weights(w_in[layer])
        proj, gab, gabt = _in_projection(h, norm_mix[layer][None, :], w_main, w_small, w_small_t, tm, CONV_CH)
        prow = jnp.zeros((8, LANES), F32)
        prow = prow.at[0, :GDN_HEADS].set(a_log[layer]).at[1, :GDN_HEADS].set(dt_bias[layer])
        pcol = jnp.zeros((SMALL_ROWS, LANES), F32)
        pcol = pcol.at[:GDN_HEADS, 0].set(a_log[layer]).at[:GDN_HEADS, 1].set(dt_bias[layer])
        convw = jnp.pad(conv_w[layer], ((0, 8 - CONV_K), (0, 0)))
        ret, gdn = _token_mixers(proj, gab, gabt, tabs, convw, prow, pcol, gdn_norm[layer][None, :],
                                 batch, seq, blk)
        i = layer // 2
        is_moe = layer % 2 == 1
        router_w = _bf(jnp.pad(router[i], ((0, 0), (0, LANES - N_EXPERTS)))) if is_moe else None
        h, second = _merge_project(ret, gdn, proj, h, _bf(w_branch[layer, 0]), _bf(w_branch[layer, 1]),
                                   _bf(w_out[layer]), norm_ffn[layer][None, :], router_w, tm_rows)
        if not is_moe:
            h = _dense_ffn(second, h, _bf(dense_w_gate[i])[None], _bf(dense_w_up[i])[None],
                           _bf(dense_w_down[i])[None], tm_rows, tf_dense)
            if layer == depth - 1:
                raise NotImplementedError("final norm after a dense layer is not wired")
        else:
            if layer != depth - 1:
                raise NotImplementedError("routed layer must be last")
            out = _moe_layer(second, h, norm_ffn[layer][None, :], final_norm[None, :],
                             _bf(moe_w_gate[i]), _bf(moe_w_up[i]), _bf(moe_w_down[i]),
                             rows, tm_rows, tm_rows, 256)
    return out.reshape(batch, seq, d)
```

```python
import functools
import math

import jax
import jax.numpy as jnp
import numpy as np
from jax import lax
from jax.experimental import pallas as pl
from jax.experimental.pallas import tpu as pltpu

F32 = jnp.float32
BF16 = jnp.bfloat16

D_MODEL = 1024
RET_HEADS = 4
GDN_HEADS = 4
HEAD_DIM = 128
HEADS_W = 512
CONV_K = 4
CONV_CH = 3 * HEADS_W
GDN_CHUNK = 64
ROPE_BASE = 10000.0
D_FF = 2816
N_EXPERTS = 8
TOP_K = 2
EPS = 1e-6
LANES = 128
SMALL_ROWS = 16
ROW_DMA_UNROLL = 8
CONV_ROWS = 256

MAIN_COLS = CONV_CH + 5 * HEADS_W + 2 * D_MODEL
VMEM_LIMIT = 56 * 1024 * 1024

HIGHEST = lax.Precision.HIGHEST


def _bf(x):
    return x.astype(BF16)


def _dot(a, b):
    return jnp.dot(a, b, preferred_element_type=F32)


def _dot_nt(a, b):
    return lax.dot_general(a, b, (((1,), (1,)), ((), ())), preferred_element_type=F32)


def _dot_tn(a, b):
    return lax.dot_general(a, b, (((0,), (0,)), ((), ())), preferred_element_type=F32)


def _dot_f32(a, b):
    return jnp.dot(a, b, preferred_element_type=F32, precision=HIGHEST)


def _sigmoid(x):
    return 1.0 / (1.0 + jnp.exp(-x))


def _silu(x):
    return x * _sigmoid(x)


def _softplus(x):
    return jnp.maximum(x, 0.0) + jnp.log(1.0 + jnp.exp(-jnp.abs(x)))


def _rms_rows(x, w):
    ms = jnp.mean(x * x, axis=-1, keepdims=True)
    return x * lax.rsqrt(ms + EPS) * w


def _params(*sem):
    return pltpu.CompilerParams(dimension_semantics=sem, vmem_limit_bytes=VMEM_LIMIT)


def _inproj_kernel(x_ref, nw_ref, w_ref, ws_ref, wst_ref, cw_ref, o_ref, os_ref, ost_ref,
                   hn_ref, xs_ref, stage_ref, *, tiles_per_seq, n_col):
    tm = x_ref.shape[0]
    tile, s = pl.program_id(0), pl.program_id(1)

    @pl.when(s == 0)
    def _():
        hn = _bf(_rms_rows(x_ref[...], nw_ref[...]))
        hn_ref[...] = hn
        os_ref[...] = _dot(hn, ws_ref[...])
        ost_ref[...] = _dot_nt(wst_ref[...], hn)

        @pl.when(tile % tiles_per_seq == 0)
        def _():
            xs_ref[0:8, :] = jnp.zeros((8, CONV_CH), F32)

        xs_ref[8:8 + tm, :] = _dot(hn, w_ref[...])

    @pl.when(s == 1)
    def _():
        for r0 in range(0, tm, CONV_ROWS):
            rows = slice(r0, r0 + CONV_ROWS)
            for c0 in range(0, CONV_CH, LANES):
                if c0 % HEADS_W == 0:
                    wide = slice(c0, c0 + HEADS_W)
                    stage_ref[1, rows, wide] = _bf(_dot(hn_ref[rows, :], w_ref[:, wide]))
                cols = slice(c0, c0 + LANES)
                conv = cw_ref[CONV_K - 1:CONV_K, cols] * xs_ref[pl.ds(r0 + 8, CONV_ROWS), cols]
                for d in range(1, CONV_K):
                    conv = conv + cw_ref[CONV_K - 1 - d:CONV_K - d, cols] * xs_ref[pl.ds(r0 + 8 - d, CONV_ROWS), cols]
                o_ref[rows, cols] = _bf(_silu(conv))
        xs_ref[0:8, :] = xs_ref[tm:tm + 8, :]

    for k in range(2, n_col + 1):
        @pl.when(s == k)
        def _(k=k):
            if k < n_col:
                stage_ref[k % 2] = _bf(_dot(hn_ref[...], w_ref[...]))
            o_ref[...] = stage_ref[(k - 1) % 2]


def _in_projection(h, norm_w, w_main, w_small, w_small_t, conv_w, tm, tn, seq):
    t = h.shape[0]
    assert tn == CONV_CH and seq % tm == 0
    n_col = MAIN_COLS // tn
    return pl.pallas_call(
        functools.partial(_inproj_kernel, tiles_per_seq=seq // tm, n_col=n_col),
        grid=(t // tm, n_col + 1),
        in_specs=[
            pl.BlockSpec((tm, D_MODEL), lambda i, j: (i, 0)),
            pl.BlockSpec((1, D_MODEL), lambda i, j: (0, 0)),
            pl.BlockSpec((D_MODEL, tn), lambda i, j: (0, jnp.minimum(j, n_col - 1))),
            pl.BlockSpec((D_MODEL, LANES), lambda i, j: (0, 0)),
            pl.BlockSpec((SMALL_ROWS, D_MODEL), lambda i, j: (0, 0)),
            pl.BlockSpec((8, CONV_CH), lambda i, j: (0, 0)),
        ],
        out_specs=[
            pl.BlockSpec((tm, tn), lambda i, j: (i, jnp.maximum(j - 1, 0))),
            pl.BlockSpec((tm, LANES), lambda i, j: (i, 0)),
            pl.BlockSpec((SMALL_ROWS, tm), lambda i, j: (0, i)),
        ],
        out_shape=[
            jax.ShapeDtypeStruct((t, MAIN_COLS), BF16),
            jax.ShapeDtypeStruct((t, LANES), F32),
            jax.ShapeDtypeStruct((SMALL_ROWS, t), F32),
        ],
        scratch_shapes=[pltpu.VMEM((tm, D_MODEL), BF16), pltpu.VMEM((tm + 8, CONV_CH), F32),
                        pltpu.VMEM((2, tm, tn), BF16)],
        compiler_params=_params("arbitrary", "arbitrary"),
        name="in_projection",
    )(h, norm_w, w_main, w_small, w_small_t, conv_w)


def _hi_lo(x):
    hi = _bf(x)
    lo_f = x - hi.astype(F32)
    return hi, _bf(lo_f), lo_f


def _dup_lhs(x2, lo_f, left):
    w = _bf(jnp.where(left, x2, lo_f))
    return jnp.concatenate([w, w], axis=1)


def _dup_rhs(hi, lo):
    return jnp.concatenate([hi, hi, lo, lo], axis=0)


def _mixer_kernel(*refs, blk, nb):
    gqkv_ref, rq_ref, rk_ref, rv_ref, rg_ref, gz_ref, gab_ref = refs[:7]
    gabt_refs = refs[7:7 + nb]
    (cos_ref, sin_ref, dmat_ref, qdec_ref, kdec_ref, cdec_ref, prow_ref, pcol_ref, gnorm_ref, tril_ref, triu_ref,
     ret_ref, gdn_ref, rstate_ref, gstate_ref) = refs[7 + nb:]
    c = GDN_CHUNK
    n_chunks = blk // c
    seqs = range(nb)
    chains = [(s, h) for s in seqs for h in range(GDN_HEADS)]
    inst = [(s, h, n) for s, h in chains for n in range(n_chunks)]
    slot = {ch: ch[0] * GDN_HEADS + ch[1] for ch in chains}

    @pl.when(pl.program_id(1) == 0)
    def _():
        rstate_ref[...] = jnp.zeros_like(rstate_ref)
        gstate_ref[...] = jnp.zeros_like(gstate_ref)

    hsl = [slice(h * HEAD_DIM, (h + 1) * HEAD_DIM) for h in range(RET_HEADS)]

    cos2 = cos_ref[...]
    sin2 = sin_ref[...]
    half = HEAD_DIM // 2
    rq = {(s, h): rq_ref[s, :, hsl[h]].astype(F32) for s, h in chains}
    rk = {(s, h): rk_ref[s, :, hsl[h]].astype(F32) for s, h in chains}
    rv = {(s, h): rv_ref[s, :, hsl[h]] for s, h in chains}
    rq = {ch: q * cos2 + pltpu.roll(q, half, 1) * sin2 for ch, q in rq.items()}
    rk = {ch: (k * cos2 + pltpu.roll(k, half, 1) * sin2) * (HEAD_DIM ** -0.5) for ch, k in rk.items()}
    rstate = {ch: rstate_ref[slot[ch]] for ch in chains}
    scores = {ch: _dot_nt(_bf(rq[ch]), _bf(rk[ch])) * dmat_ref[ch[1]] for ch in chains}
    inter = {ch: _dot(_bf(rq[ch] * qdec_ref[ch[1]]), _bf(rstate[ch])) for ch in chains}
    kv = {ch: _dot_tn(_bf(rk[ch] * kdec_ref[ch[1]]), rv[ch]) for ch in chains}
    ro = {ch: _dot(_bf(scores[ch]), rv[ch]) + inter[ch] for ch in chains}
    for ch in chains:
        s, h = ch
        rstate_ref[slot[ch]] = rstate[ch] * cdec_ref[h] + kv[ch]
        o = ro[ch]
        o = o * lax.rsqrt(jnp.mean(o * o, axis=-1, keepdims=True) + EPS)
        ret_ref[s, :, hsl[h]] = _bf(_silu(rg_ref[s, :, hsl[h]].astype(F32)) * o)

    neg_a_row = -jnp.exp(prow_ref[0:1, :])
    neg_a_col = -jnp.exp(pcol_ref[:, 0:1])
    beta_cols, gc_cols, egc_cols, gc_rows = [], [], [], []
    for s in seqs:
        gab = gab_ref[s]
        beta_cols.append(_sigmoid(gab))
        gc_cols.append(_dot_f32(tril_ref[...], neg_a_row * _softplus(gab + prow_ref[1:2, :])))
        egc_cols.append(jnp.exp(gc_cols[s]))
        gc_rows.append(_dot_f32(neg_a_col * _softplus(gabt_refs[s][...] + pcol_ref[:, 1:2]), triu_ref[...]))

    ri = lax.broadcasted_iota(jnp.int32, (c, 2 * c), 0)
    lane = lax.broadcasted_iota(jnp.int32, (c, 2 * c), 1)
    ci = lane & (c - 1)
    left = lane < c
    ge = ri >= ci
    gt = ri > ci
    eye = jnp.where(ri == ci, 1.0, 0.0).astype(F32)
    level_masks = []
    for lg in range(int(math.log2(c))):
        b = 1 << lg
        same_block = (ri >> (lg + 1)) == (ci >> (lg + 1))
        level_masks.append(same_block & ((ri & (2 * b - 1)) >= b) & ((ci & (2 * b - 1)) < b))

    qh, kh, vh = {}, {}, {}
    for s, h in chains:
        q = gqkv_ref[s, :, hsl[h]].astype(F32)
        k = gqkv_ref[s, :, HEADS_W + h * HEAD_DIM:HEADS_W + (h + 1) * HEAD_DIM].astype(F32)
        vh[s, h] = gqkv_ref[s, :, 2 * HEADS_W + h * HEAD_DIM:2 * HEADS_W + (h + 1) * HEAD_DIM].astype(F32)
        qh[s, h] = q * lax.rsqrt(jnp.sum(q * q, axis=-1, keepdims=True) + EPS) * (HEAD_DIM ** -0.5)
        kh[s, h] = k * lax.rsqrt(jnp.sum(k * k, axis=-1, keepdims=True) + EPS)

    rows = [slice(n * c, (n + 1) * c) for n in range(n_chunks)]
    kc = {(s, h, n): kh[s, h][rows[n]] for s, h, n in inst}
    qc = {(s, h, n): qh[s, h][rows[n]] for s, h, n in inst}
    bcol = {(s, h, n): beta_cols[s][rows[n], 4 + h:5 + h] for s, h, n in inst}
    gcol = {(s, h, n): gc_cols[s][rows[n], h:h + 1] for s, h, n in inst}
    egcol = {(s, h, n): egc_cols[s][rows[n], h:h + 1] for s, h, n in inst}
    glast = {(s, h, n): gc_cols[s][n * c + c - 1:(n + 1) * c, h:h + 1] for s, h, n in inst}
    grow = {(s, h, n): gc_rows[s][h:h + 1, 2 * c * n:2 * c * (n + 1)] for s, h, n in inst}
    kb = {i: kc[i] * bcol[i] for i in inst}
    kcb = {i: _bf(kc[i]) for i in inst}
    kcb2 = {i: jnp.concatenate([kcb[i], kcb[i]], axis=0) for i in inst}
    decay = {i: jnp.where(ge, jnp.exp(jnp.where(ge, gcol[i] - grow[i], 0.0)), 0.0) for i in inst}
    a2 = {i: jnp.where(gt, _dot_nt(_bf(kb[i]), kcb2[i]) * decay[i], 0.0) for i in inst}
    attn = {i: _bf(jnp.where(ge, _dot_nt(_bf(qc[i]), kcb2[i]) * decay[i], 0.0)[:, :c]) for i in inst}

    x2 = {i: eye - jnp.where(level_masks[0], a2[i], 0.0) for i in inst}
    a_hl = {i: _hi_lo(a2[i]) for i in inst}
    for m in level_masks[1:]:
        mb = jnp.where(m, 1.0, 0.0).astype(BF16)
        x_hl = {i: _hi_lo(x2[i]) for i in inst}
        t = {i: _dot(_dup_lhs(x2[i], x_hl[i][2], left), _dup_rhs(a_hl[i][0] * mb, a_hl[i][1] * mb)) for i in inst}
        t_hl = {i: _hi_lo(t[i]) for i in inst}
        x2 = {i: x2[i] - _dot(_dup_lhs(t[i], t_hl[i][2], left), _dup_rhs(x_hl[i][0], x_hl[i][1])) for i in inst}

    uw = {}
    for i in inst:
        s, h, n = i
        rhs = _hi_lo(jnp.concatenate([vh[s, h][rows[n]] * bcol[i], kb[i] * egcol[i]], axis=1))
        uw[i] = _dot(_dup_lhs(x2[i], _hi_lo(x2[i])[2], left), _dup_rhs(rhs[0], rhs[1]))
    wq = {i: _bf(jnp.concatenate([uw[i][:, HEAD_DIM:], qc[i] * egcol[i]], axis=0)) for i in inst}
    kd = {i: _bf(kc[i] * jnp.exp(glast[i] - gcol[i])) for i in inst}
    eg = {i: jnp.exp(glast[i]) for i in inst}

    state = {ch: gstate_ref[slot[ch]] for ch in chains}
    outs = {ch: [] for ch in chains}
    for n in range(n_chunks):
        sb = {ch: _bf(state[ch]) for ch in chains}
        ws = {ch: _dot(wq[ch + (n,)], sb[ch]) for ch in chains}
        vnb = {ch: _bf(uw[ch + (n,)][:, :HEAD_DIM] - ws[ch][:c]) for ch in chains}
        for ch in chains:
            outs[ch].append(ws[ch][c:] + _dot(attn[ch + (n,)], vnb[ch]))
        state = {ch: state[ch] * eg[ch + (n,)] + _dot_tn(kd[ch + (n,)], vnb[ch]) for ch in chains}

    gnorm = gnorm_ref[...]
    for ch in chains:
        s, h = ch
        gstate_ref[slot[ch]] = state[ch]
        o = jnp.concatenate(outs[ch], axis=0)
        o = o * lax.rsqrt(jnp.mean(o * o, axis=-1, keepdims=True) + EPS)
        gdn_ref[s, :, hsl[h]] = _bf(o * gnorm * _silu(gz_ref[s, :, hsl[h]].astype(F32)))


def _token_mixers(proj, gab, gabt, tabs, prow, pcol, gnorm, batch, seq, blk, nb):
    proj3 = proj.reshape(batch, seq, MAIN_COLS)
    gab3 = gab.reshape(batch, seq, LANES)
    n_blk = seq // blk
    cos2, sin2, dmat, qdec, kdec, cdec, tril, triu = tabs
    w512 = lambda col: pl.BlockSpec((nb, blk, HEADS_W), lambda b, n, col=col: (b, n, col))
    full = lambda shape: pl.BlockSpec(shape, lambda b, n: (0,) * len(shape))
    gabt_specs = [pl.BlockSpec((SMALL_ROWS, blk), lambda b, n, s=s: (0, (b * nb + s) * n_blk + n)) for s in range(nb)]
    out = pl.pallas_call(
        functools.partial(_mixer_kernel, blk=blk, nb=nb),
        grid=(batch // nb, n_blk),
        in_specs=[
            pl.BlockSpec((nb, blk, CONV_CH), lambda b, n: (b, n, 0)),
            w512(3), w512(4), w512(5), w512(6), w512(7),
            pl.BlockSpec((nb, blk, LANES), lambda b, n: (b, n, 0)),
            *gabt_specs,
            pl.BlockSpec((blk, HEAD_DIM), lambda b, n: (n, 0)),
            pl.BlockSpec((blk, HEAD_DIM), lambda b, n: (n, 0)),
            full((RET_HEADS, blk, blk)),
            full((RET_HEADS, blk, HEAD_DIM)),
            full((RET_HEADS, blk, HEAD_DIM)),
            full((RET_HEADS, 1, HEAD_DIM)),
            full((8, LANES)),
            full((SMALL_ROWS, LANES)),
            full((1, HEAD_DIM)),
            full((blk, blk)),
            full((blk, 2 * blk)),
        ],
        out_specs=[
            pl.BlockSpec((nb, blk, HEADS_W), lambda b, n: (b, n, 0)),
            pl.BlockSpec((nb, blk, HEADS_W), lambda b, n: (b, n, 0)),
        ],
        out_shape=[
            jax.ShapeDtypeStruct((batch, seq, HEADS_W), BF16),
            jax.ShapeDtypeStruct((batch, seq, HEADS_W), BF16),
        ],
        scratch_shapes=[
            pltpu.VMEM((nb * RET_HEADS, HEAD_DIM, HEAD_DIM), F32),
            pltpu.VMEM((nb * GDN_HEADS, HEAD_DIM, HEAD_DIM), F32),
        ],
        compiler_params=_params("parallel", "arbitrary"),
        name="token_mixers",
    )(proj3, proj3, proj3, proj3, proj3, proj3, gab3, *([gabt] * nb),
      cos2, sin2, dmat, qdec, kdec, cdec, prow, pcol, gnorm, tril, triu)
    ret, gdn = out
    return ret.reshape(batch * seq, HEADS_W), gdn.reshape(batch * seq, HEADS_W)


def _mixer_tables(seq, blk):
    inv_freq = ROPE_BASE ** (-jnp.arange(0, HEAD_DIM, 2, dtype=F32) / HEAD_DIM)
    ang = jnp.arange(seq, dtype=F32)[:, None] * inv_freq[None, :]
    cos, sin = jnp.cos(ang), jnp.sin(ang)
    cos2 = jnp.concatenate([cos, cos], axis=-1)
    sin2 = jnp.concatenate([-sin, sin], axis=-1)
    log_gamma = jnp.log(1.0 - jnp.exp2(-5.0 - jnp.arange(RET_HEADS, dtype=F32)))
    pos = jnp.arange(blk, dtype=F32)
    diff = pos[:, None] - pos[None, :]
    dmat = jnp.where(diff >= 0, jnp.exp(jnp.maximum(diff, 0.0)[None] * log_gamma[:, None, None]), 0.0)
    ones = jnp.ones((1, 1, HEAD_DIM), F32)
    qdec = jnp.exp((pos + 1.0)[None, :] * log_gamma[:, None])[:, :, None] * ones
    kdec = jnp.exp((blk - 1 - pos)[None, :] * log_gamma[:, None])[:, :, None] * ones
    cdec = jnp.exp(blk * log_gamma)[:, None, None] * ones
    idx = np.arange(blk)
    same = (idx[:, None] // GDN_CHUNK) == (idx[None, :] // GDN_CHUNK)
    tril = jnp.asarray((same & (idx[:, None] >= idx[None, :])).astype(np.float32))
    col = np.arange(2 * blk)
    col_chunk, col_pos = col // (2 * GDN_CHUNK), col % GDN_CHUNK
    triu = jnp.asarray(((idx[:, None] // GDN_CHUNK == col_chunk[None, :])
                        & (idx[:, None] % GDN_CHUNK <= col_pos[None, :])).astype(np.float32))
    return cos2, sin2, dmat, qdec, kdec, cdec, tril, triu


def _merge_kernel(ret_ref, gdn_ref, ga_ref, gb_ref, h_ref, wa_ref, wb_ref, wo_ref, nw_ref, *rest,
                  with_router):
    if with_router:
        wr_ref, ho_ref, logit_ref = rest
    else:
        ho_ref, hn_ref = rest
    ya = _dot(ret_ref[...], wa_ref[...])
    yb = _dot(gdn_ref[...], wb_ref[...])
    merged = _sigmoid(ga_ref[...].astype(F32)) * ya + _sigmoid(gb_ref[...].astype(F32)) * yb
    h_new = h_ref[...] + _dot(_bf(merged), wo_ref[...])
    ho_ref[...] = h_new
    hn = _bf(_rms_rows(h_new, nw_ref[...]))
    if with_router:
        logit_ref[...] = _dot(hn, wr_ref[...])
    else:
        hn_ref[...] = hn


def _merge_project(ret, gdn, proj, h, wa, wb, wo, norm_w, router_w, tm):
    t = h.shape[0]
    with_router = router_w is not None
    in_specs = [
        pl.BlockSpec((tm, HEADS_W), lambda i: (i, 0)),
        pl.BlockSpec((tm, HEADS_W), lambda i: (i, 0)),
        pl.BlockSpec((tm, D_MODEL), lambda i: (i, 4)),
        pl.BlockSpec((tm, D_MODEL), lambda i: (i, 5)),
        pl.BlockSpec((tm, D_MODEL), lambda i: (i, 0)),
        pl.BlockSpec((HEADS_W, D_MODEL), lambda i: (0, 0)),
        pl.BlockSpec((HEADS_W, D_MODEL), lambda i: (0, 0)),
        pl.BlockSpec((D_MODEL, D_MODEL), lambda i: (0, 0)),
        pl.BlockSpec((1, D_MODEL), lambda i: (0, 0)),
    ]
    args = [ret, gdn, proj, proj, h, wa, wb, wo, norm_w]
    if with_router:
        in_specs.append(pl.BlockSpec((D_MODEL, LANES), lambda i: (0, 0)))
        args.append(router_w)
        second = (pl.BlockSpec((tm, LANES), lambda i: (i, 0)), jax.ShapeDtypeStruct((t, LANES), F32))
    else:
        second = (pl.BlockSpec((tm, D_MODEL), lambda i: (i, 0)), jax.ShapeDtypeStruct((t, D_MODEL), BF16))
    return pl.pallas_call(
        functools.partial(_merge_kernel, with_router=with_router),
        grid=(t // tm,),
        in_specs=in_specs,
        out_specs=[pl.BlockSpec((tm, D_MODEL), lambda i: (i, 0)), second[0]],
        out_shape=[jax.ShapeDtypeStruct((t, D_MODEL), F32), second[1]],
        compiler_params=_params("parallel"),
        name="merge_project",
    )(*args)


def _swiglu_rows(x, wg_ref, wu_ref, wd_ref, acc, tf):
    for f in range(D_FF // tf):
        cols = slice(f * tf, (f + 1) * tf)
        act = _silu(_dot(x, wg_ref[0, :, cols])) * _dot(x, wu_ref[0, :, cols])
        acc = acc + _dot(_bf(act), wd_ref[0, cols, :])
    return acc


def _ffn_kernel(hn_ref, h_ref, wg_ref, wu_ref, wd_ref, o_ref, *, tf):
    o_ref[...] = _swiglu_rows(hn_ref[...], wg_ref, wu_ref, wd_ref, h_ref[...], tf)


def _dense_ffn(hn, h, wg, wu, wd, tm, tf):
    t = h.shape[0]
    return pl.pallas_call(
        functools.partial(_ffn_kernel, tf=tf),
        grid=(t // tm,),
        in_specs=[
            pl.BlockSpec((tm, D_MODEL), lambda i: (i, 0)),
            pl.BlockSpec((tm, D_MODEL), lambda i: (i, 0)),
            pl.BlockSpec((1, D_MODEL, D_FF), lambda i: (0, 0, 0)),
            pl.BlockSpec((1, D_MODEL, D_FF), lambda i: (0, 0, 0)),
            pl.BlockSpec((1, D_FF, D_MODEL), lambda i: (0, 0, 0)),
        ],
        out_specs=pl.BlockSpec((tm, D_MODEL), lambda i: (i, 0)),
        out_shape=jax.ShapeDtypeStruct((t, D_MODEL), F32),
        compiler_params=_params("parallel"),
        name="dense_ffn",
    )(hn, h, wg, wu, wd)


def _route_kernel(logit_ref, tri_ref, meta_ref, count_ref, run_ref):
    @pl.when(pl.program_id(0) == 0)
    def _():
        run_ref[...] = jnp.zeros_like(run_ref)

    tm = logit_ref.shape[0]
    lane = lax.broadcasted_iota(jnp.int32, (tm, LANES), 1)
    neg = jnp.float32(-jnp.inf)
    l1 = jnp.where(lane < N_EXPERTS, logit_ref[...], neg)
    m1 = jnp.max(l1, axis=-1, keepdims=True)
    i1 = jnp.min(jnp.where(l1 == m1, lane, LANES), axis=-1, keepdims=True)
    l2 = jnp.where(lane == i1, neg, l1)
    m2 = jnp.max(l2, axis=-1, keepdims=True)
    i2 = jnp.min(jnp.where(l2 == m2, lane, LANES), axis=-1, keepdims=True)
    e2 = jnp.exp(m2 - m1)
    g1 = 1.0 / (1.0 + e2)
    g2 = e2 / (1.0 + e2)
    sel1 = lane == i1
    sel2 = lane == i2
    onehot = jnp.where(sel1 | sel2, 1.0, 0.0).astype(F32)
    before = _dot(tri_ref[...], _bf(onehot)) + run_ref[0:1, :]
    r1 = jnp.sum(jnp.where(sel1, before, 0.0), axis=-1, keepdims=True)
    r2 = jnp.sum(jnp.where(sel2, before, 0.0), axis=-1, keepdims=True)
    run_new = run_ref[0:1, :] + jnp.sum(onehot, axis=0, keepdims=True)
    run_ref[0:1, :] = run_new
    count_ref[...] = jnp.broadcast_to(run_new, count_ref.shape)
    meta = jnp.where(lane == 0, i1.astype(F32), 0.0)
    meta = jnp.where(lane == 1, i2.astype(F32), meta)
    meta = jnp.where(lane == 2, r1, meta)
    meta = jnp.where(lane == 3, r2, meta)
    meta = jnp.where(lane == 4, g1, meta)
    meta = jnp.where(lane == 5, g2, meta)
    meta_ref[...] = meta


def _route(logits, tm):
    t = logits.shape[0]
    idx = np.arange(tm)
    tri = jnp.asarray((idx[:, None] > idx[None, :]).astype(np.float32), dtype=BF16)
    return pl.pallas_call(
        _route_kernel,
        grid=(t // tm,),
        in_specs=[pl.BlockSpec((tm, LANES), lambda i: (i, 0)),
                  pl.BlockSpec((tm, tm), lambda i: (0, 0))],
        out_specs=[pl.BlockSpec((tm, LANES), lambda i: (i, 0)),
                   pl.BlockSpec((8, LANES), lambda i: (0, 0))],
        out_shape=[jax.ShapeDtypeStruct((t, LANES), F32),
                   jax.ShapeDtypeStruct((8, LANES), F32)],
        scratch_shapes=[pltpu.VMEM((8, LANES), F32)],
        compiler_params=_params("arbitrary"),
        name="route",
    )(logits, tri)


def _row_wait(src_hbm, dst_vmem, sem, rows):
    pltpu.make_async_copy(src_hbm.at[pl.ds(0, rows)], dst_vmem, sem).wait()


def _dispatch_kernel(sched_ref, dest_ref, h_ref, nw_ref, xg_ref, hn_ref, zero_ref, sem, zsem, *, rows):
    tm = h_ref.shape[0]

    @pl.when(pl.program_id(0) == 0)
    def _():
        zero_ref[...] = jnp.zeros_like(zero_ref)
        for e in range(N_EXPERTS):
            @pl.when(sched_ref[N_EXPERTS + e] > 0)
            def _():
                start = pl.multiple_of(sched_ref[e] - rows, rows)
                clear = pltpu.make_async_copy(zero_ref, xg_ref.at[pl.ds(start, rows)], zsem)
                clear.start()
                clear.wait()

        def clear_unused(b, carry):
            clear = pltpu.make_async_copy(zero_ref, xg_ref.at[pl.ds(pl.multiple_of(b * rows, rows), rows)], zsem)
            clear.start()
            clear.wait()
            return carry

        lax.fori_loop(sched_ref[2 * N_EXPERTS], xg_ref.shape[0] // rows, clear_unused, 0)

    step = pl.program_id(0)
    slot = step % 2
    rows_ref = hn_ref.at[slot]
    rows_ref[...] = _rms_rows(h_ref[...], nw_ref[...])

    def body(r, carry):
        for j in range(TOP_K):
            d = dest_ref[TOP_K * r + j]
            pltpu.make_async_copy(rows_ref.at[pl.ds(r, 1)], xg_ref.at[pl.ds(d, 1)], sem.at[slot]).start()
        return carry

    lax.fori_loop(0, tm, body, 0, unroll=ROW_DMA_UNROLL)

    def drain(s):
        for _ in range(TOP_K):
            pltpu.make_async_copy(hn_ref.at[s], xg_ref.at[pl.ds(0, tm)], sem.at[s]).wait()

    @pl.when(step > 0)
    def _():
        drain(1 - slot)

    @pl.when(step == pl.num_programs(0) - 1)
    def _():
        drain(slot)


def _dispatch(sched, dest_flat, h, norm_w, n_rows, tm, rows):
    t = h.shape[0]
    grid_spec = pltpu.PrefetchScalarGridSpec(
        num_scalar_prefetch=1,
        grid=(t // tm,),
        in_specs=[
            pl.BlockSpec((TOP_K * tm,), lambda i, sc: (i,), memory_space=pltpu.SMEM),
            pl.BlockSpec((tm, D_MODEL), lambda i, sc: (i, 0)),
            pl.BlockSpec((1, D_MODEL), lambda i, sc: (0, 0)),
        ],
        out_specs=pl.BlockSpec(memory_space=pl.ANY),
        scratch_shapes=[pltpu.VMEM((2, tm, D_MODEL), F32), pltpu.VMEM((rows, D_MODEL), F32),
                        pltpu.SemaphoreType.DMA((2,)), pltpu.SemaphoreType.DMA(())],
    )
    return pl.pallas_call(
        functools.partial(_dispatch_kernel, rows=rows),
        grid_spec=grid_spec,
        out_shape=jax.ShapeDtypeStruct((n_rows, D_MODEL), F32),
        compiler_params=_params("arbitrary"),
        name="dispatch",
    )(sched, dest_flat, h, norm_w)


def _expert_kernel(be_ref, sched_ref, x_ref, wg_ref, wu_ref, wd_ref, y_ref, *, tf):
    del be_ref

    @pl.when(pl.program_id(0) < sched_ref[2 * N_EXPERTS])
    def _():
        y_ref[...] = _swiglu_rows(_bf(x_ref[...]), wg_ref, wu_ref, wd_ref, jnp.zeros(y_ref.shape, F32), tf)

    @pl.when(pl.program_id(0) >= sched_ref[2 * N_EXPERTS])
    def _():
        y_ref[...] = jnp.zeros_like(y_ref)


def _experts(block_e, sched, xg, wg, wu, wd, rows, tf):
    n_rows = xg.shape[0]
    used = lambda i, sc: jnp.minimum(i, sc[2 * N_EXPERTS] - 1)
    grid_spec = pltpu.PrefetchScalarGridSpec(
        num_scalar_prefetch=2,
        grid=(n_rows // rows,),
        in_specs=[
            pl.BlockSpec((rows, D_MODEL), lambda i, be, sc: (used(i, sc), 0)),
            pl.BlockSpec((1, D_MODEL, D_FF), lambda i, be, sc: (be[i], 0, 0)),
            pl.BlockSpec((1, D_MODEL, D_FF), lambda i, be, sc: (be[i], 0, 0)),
            pl.BlockSpec((1, D_FF, D_MODEL), lambda i, be, sc: (be[i], 0, 0)),
        ],
        out_specs=pl.BlockSpec((rows, D_MODEL), lambda i, be, sc: (i, 0)),
    )
    return pl.pallas_call(
        functools.partial(_expert_kernel, tf=tf),
        grid_spec=grid_spec,
        out_shape=jax.ShapeDtypeStruct((n_rows, D_MODEL), F32),
        compiler_params=_params("arbitrary"),
        name="experts",
    )(block_e, sched, xg, wg, wu, wd)


def _combine_kernel(dest_ref, next_dest_ref, meta_ref, h_ref, nw_ref, y_ref, o_ref, buf_ref, sem):
    tm = h_ref.shape[0]
    step = pl.program_id(0)
    slot = step % 2

    def fetch(idx_ref, s):
        def body(r, carry):
            for j in range(TOP_K):
                d = idx_ref[TOP_K * r + j]
                pltpu.make_async_copy(y_ref.at[pl.ds(d, 1)], buf_ref.at[s, j, pl.ds(r, 1)], sem.at[s]).start()
            return carry

        lax.fori_loop(0, tm, body, 0, unroll=ROW_DMA_UNROLL)

    @pl.when(step == 0)
    def _():
        fetch(dest_ref, 0)

    @pl.when(step + 1 < pl.num_programs(0))
    def _():
        fetch(next_dest_ref, 1 - slot)

    for j in range(TOP_K):
        _row_wait(y_ref, buf_ref.at[slot, j], sem.at[slot], tm)
    meta = meta_ref[...]
    moe = meta[:, 4:5] * buf_ref[slot, 0] + meta[:, 5:6] * buf_ref[slot, 1]
    o_ref[...] = _rms_rows(h_ref[...] + moe, nw_ref[...])


def _combine(dest_flat, meta, h, norm_w, y, tm):
    t = h.shape[0]
    last = t // tm - 1
    return pl.pallas_call(
        _combine_kernel,
        grid=(t // tm,),
        in_specs=[
            pl.BlockSpec((TOP_K * tm,), lambda i: (i,), memory_space=pltpu.SMEM),
            pl.BlockSpec((TOP_K * tm,), lambda i: (jnp.minimum(i + 1, last),), memory_space=pltpu.SMEM),
            pl.BlockSpec((tm, LANES), lambda i: (i, 0)),
            pl.BlockSpec((tm, D_MODEL), lambda i: (i, 0)),
            pl.BlockSpec((1, D_MODEL), lambda i: (0, 0)),
            pl.BlockSpec(memory_space=pl.ANY),
        ],
        out_specs=pl.BlockSpec((tm, D_MODEL), lambda i: (i, 0)),
        out_shape=jax.ShapeDtypeStruct((t, D_MODEL), F32),
        scratch_shapes=[pltpu.VMEM((2, TOP_K, tm, D_MODEL), F32), pltpu.SemaphoreType.DMA((2,))],
        compiler_params=_params("arbitrary"),
        name="combine",
    )(dest_flat, dest_flat, meta, h, norm_w, y)


def _moe_layer(logits, h, ffn_norm_w, final_norm_w, wg, wu, wd, rows, tm_route, tm_rows, tf):
    t = h.shape[0]
    meta, counts = _route(logits, tm_route)
    counts = counts[0, :N_EXPERTS].astype(jnp.int32)
    padded = ((counts + rows - 1) // rows) * rows
    pend = jnp.cumsum(padded)
    pstart = pend - padded
    n_rows = t * TOP_K + N_EXPERTS * rows
    expert = meta[:, 0:TOP_K].astype(jnp.int32)
    rank = meta[:, TOP_K:2 * TOP_K].astype(jnp.int32)
    dest = (pstart[expert] + rank).reshape(t * TOP_K)
    n_used = pend[N_EXPERTS - 1] // rows
    block_start = jnp.minimum(jnp.arange(n_rows // rows, dtype=jnp.int32), n_used - 1) * rows
    block_e = jnp.sum(block_start[:, None] >= pend[None, :], axis=1).astype(jnp.int32)
    sched = jnp.concatenate([pend, padded, n_used[None]]).astype(jnp.int32)
    xg = _dispatch(sched, dest, h, ffn_norm_w, n_rows, tm_rows, rows)
    y = _experts(block_e, sched, xg, wg, wu, wd, rows, tf)
    return _combine(dest, meta, h, final_norm_w, y, tm_rows)


def _pack_in_weights(w_in):
    o = np.cumsum((0, HEADS_W, HEADS_W, HEADS_W, HEADS_W, CONV_CH, HEADS_W, GDN_HEADS, GDN_HEADS, D_MODEL, D_MODEL))
    rq, rk, rv, rg, gqkv, gz, ga, gb, ma, mb = (w_in[:, o[i]:o[i + 1]] for i in range(10))
    main = _bf(jnp.concatenate([gqkv, rq, rk, rv, rg, gz, ma, mb], axis=1))
    small = jnp.concatenate([ga, gb], axis=1)
    small_cols = _bf(jnp.pad(small, ((0, 0), (0, LANES - 2 * GDN_HEADS))))
    small_rows = _bf(jnp.pad(small.T, ((0, SMALL_ROWS - 2 * GDN_HEADS), (0, 0))))
    return main, small_cols, small_rows


def _pick(n, prefs):
    for p in prefs:
        if n % p == 0:
            return p
    raise ValueError(f"no tile in {prefs} divides {n}")


def kernel(x, norm_mix, w_in, conv_w, a_log, dt_bias, gdn_norm, w_branch, w_out, norm_ffn,
           dense_w_gate, dense_w_up, dense_w_down, router, moe_w_gate, moe_w_up, moe_w_down, final_norm):
    batch, seq, d = x.shape
    depth = norm_mix.shape[0]
    assert d == D_MODEL and seq % GDN_CHUNK == 0
    t = batch * seq
    blk = _pick(seq, (128, 64))
    nb = _pick(batch, (4, 2, 1))
    tm = _pick(seq, (1024, 512, 256))
    tm_rows = _pick(t, (512, 256))
    tf_dense = 256
    rows = 512
    tabs = _mixer_tables(seq, blk)

    h = x.reshape(t, d)
    out = None
    for layer in range(depth):
        w_main, w_small, w_small_t = _pack_in_weights(w_in[layer])
        convw = jnp.pad(conv_w[layer], ((0, 8 - CONV_K), (0, 0)))
        proj, gab, gabt = _in_projection(h, norm_mix[layer][None, :], w_main, w_small, w_small_t, convw, tm, CONV_CH, seq)
        prow = jnp.zeros((8, LANES), F32)
        prow = prow.at[0, :GDN_HEADS].set(a_log[layer]).at[1, :GDN_HEADS].set(dt_bias[layer])
        pcol = jnp.zeros((SMALL_ROWS, LANES), F32)
        pcol = pcol.at[:GDN_HEADS, 0].set(a_log[layer]).at[:GDN_HEADS, 1].set(dt_bias[layer])
        ret, gdn = _token_mixers(proj, gab, gabt, tabs, prow, pcol, gdn_norm[layer][None, :],
                                 batch, seq, blk, nb)
        i = layer // 2
        is_moe = layer % 2 == 1
        router_w = _bf(jnp.pad(router[i], ((0, 0), (0, LANES - N_EXPERTS)))) if is_moe else None
        h, second = _merge_project(ret, gdn, proj, h, _bf(w_branch[layer, 0]), _bf(w_branch[layer, 1]),
                                   _bf(w_out[layer]), norm_ffn[layer][None, :], router_w, tm_rows)
        if not is_moe:
            h = _dense_ffn(second, h, _bf(dense_w_gate[i])[None], _bf(dense_w_up[i])[None],
                           _bf(dense_w_down[i])[None], tm_rows, tf_dense)
            if layer == depth - 1:
                raise NotImplementedError("final norm after a dense layer is not wired")
        else:
            if layer != depth - 1:
                raise NotImplementedError("routed layer must be last")
            out = _moe_layer(second, h, norm_ffn[layer][None, :], final_norm[None, :],
                             _bf(moe_w_gate[i]), _bf(moe_w_up[i]), _bf(moe_w_down[i]),
                             rows, tm_rows, tm_rows, 256)
    return out.reshape(batch, seq, d)
```

```python
import functools
import math

import jax
import jax.numpy as jnp
import numpy as np
from jax import lax
from jax.experimental import pallas as pl
from jax.experimental.pallas import tpu as pltpu

F32 = jnp.float32
BF16 = jnp.bfloat16

D_MODEL = 1024
RET_HEADS = 4
GDN_HEADS = 4
HEAD_DIM = 128
HEADS_W = 512
CONV_K = 4
CONV_CH = 3 * HEADS_W
GDN_CHUNK = 64
ROPE_BASE = 10000.0
D_FF = 2816
N_EXPERTS = 8
TOP_K = 2
EPS = 1e-6
LANES = 128
SMALL_ROWS = 16
ROW_DMA_UNROLL = 8
CONV_ROWS = 256

MAIN_COLS = CONV_CH + 5 * HEADS_W + 2 * D_MODEL
VMEM_LIMIT = 56 * 1024 * 1024

HIGHEST = lax.Precision.HIGHEST


def _bf(x):
    return x.astype(BF16)


def _dot(a, b):
    return jnp.dot(a, b, preferred_element_type=F32)


def _dot_nt(a, b):
    return lax.dot_general(a, b, (((1,), (1,)), ((), ())), preferred_element_type=F32)


def _dot_tn(a, b):
    return lax.dot_general(a, b, (((0,), (0,)), ((), ())), preferred_element_type=F32)


def _dot_f32(a, b):
    return jnp.dot(a, b, preferred_element_type=F32, precision=HIGHEST)


def _sigmoid(x):
    return 1.0 / (1.0 + jnp.exp(-x))


def _silu(x):
    return x * _sigmoid(x)


def _softplus(x):
    return jnp.maximum(x, 0.0) + jnp.log(1.0 + jnp.exp(-jnp.abs(x)))


def _rms_rows(x, w):
    ms = jnp.mean(x * x, axis=-1, keepdims=True)
    return x * lax.rsqrt(ms + EPS) * w


def _params(*sem):
    return pltpu.CompilerParams(dimension_semantics=sem, vmem_limit_bytes=VMEM_LIMIT)


def _inproj_kernel(x_ref, nw_ref, w_ref, ws_ref, wst_ref, cw_ref, oc_ref, o_ref, os_ref, ost_ref,
                   hn_ref, xs_ref, *, tiles_per_seq, n_col):
    tm = x_ref.shape[0]
    tile, step = pl.program_id(0), pl.program_id(1)

    @pl.when(step == 0)
    def _():
        hn = _bf(_rms_rows(x_ref[...], nw_ref[...]))
        hn_ref[...] = hn
        os_ref[...] = _dot(hn, ws_ref[...])
        ost_ref[...] = _dot_nt(wst_ref[...], hn)

        @pl.when(tile % tiles_per_seq == 0)
        def _():
            xs_ref[0:8, :] = jnp.zeros((8, CONV_CH), F32)

        xs_ref[8:8 + tm, :] = _dot(hn, w_ref[...])

    def conv_piece(r0, c0):
        cols = slice(c0, c0 + LANES)
        acc = cw_ref[CONV_K - 1:CONV_K, cols] * xs_ref[pl.ds(r0 + 8, CONV_ROWS), cols]
        for d in range(1, CONV_K):
            acc = acc + cw_ref[CONV_K - 1 - d:CONV_K - d, cols] * xs_ref[pl.ds(r0 + 8 - d, CONV_ROWS), cols]
        oc_ref[r0:r0 + CONV_ROWS, cols] = _bf(_silu(acc))

    conv_pieces = [(r0, c0) for r0 in range(0, tm, CONV_ROWS) for c0 in range(0, CONV_CH, LANES)]
    dot_pieces = [(r0, c0) for r0 in range(0, tm, CONV_ROWS) for c0 in range(0, o_ref.shape[1], HEADS_W)]
    share = -(-len(conv_pieces) // (n_col - 1))
    for k in range(1, n_col):
        @pl.when(step == k)
        def _(k=k):
            mine = conv_pieces[(k - 1) * share:k * share]
            per_dot = -(-len(mine) // len(dot_pieces))
            for j, (r0, c0) in enumerate(dot_pieces):
                rows, cols = slice(r0, r0 + CONV_ROWS), slice(c0, c0 + HEADS_W)
                o_ref[rows, cols] = _bf(_dot(hn_ref[rows, :], w_ref[:, cols]))
                for piece in mine[j * per_dot:(j + 1) * per_dot]:
                    conv_piece(*piece)
            if k == n_col - 1:
                xs_ref[0:8, :] = xs_ref[tm:tm + 8, :]


def _in_projection(h, norm_w, w_main, w_small, w_small_t, conv_w, tm, tn, seq):
    t = h.shape[0]
    assert tn == CONV_CH and seq % tm == 0 and tm % CONV_ROWS == 0
    n_col = MAIN_COLS // tn
    return pl.pallas_call(
        functools.partial(_inproj_kernel, tiles_per_seq=seq // tm, n_col=n_col),
        grid=(t // tm, n_col),
        in_specs=[
            pl.BlockSpec((tm, D_MODEL), lambda i, j: (i, 0)),
            pl.BlockSpec((1, D_MODEL), lambda i, j: (0, 0)),
            pl.BlockSpec((D_MODEL, tn), lambda i, j: (0, j)),
            pl.BlockSpec((D_MODEL, LANES), lambda i, j: (0, 0)),
            pl.BlockSpec((SMALL_ROWS, D_MODEL), lambda i, j: (0, 0)),
            pl.BlockSpec((8, CONV_CH), lambda i, j: (0, 0)),
        ],
        out_specs=[
            pl.BlockSpec((tm, CONV_CH), lambda i, j: (i, 0)),
            pl.BlockSpec((tm, tn), lambda i, j: (i, jnp.maximum(j - 1, 0))),
            pl.BlockSpec((tm, LANES), lambda i, j: (i, 0)),
            pl.BlockSpec((SMALL_ROWS, tm), lambda i, j: (0, i)),
        ],
        out_shape=[
            jax.ShapeDtypeStruct((t, CONV_CH), BF16),
            jax.ShapeDtypeStruct((t, MAIN_COLS - CONV_CH), BF16),
            jax.ShapeDtypeStruct((t, LANES), F32),
            jax.ShapeDtypeStruct((SMALL_ROWS, t), F32),
        ],
        scratch_shapes=[pltpu.VMEM((tm, D_MODEL), BF16), pltpu.VMEM((tm + 8, CONV_CH), F32)],
        compiler_params=_params("arbitrary", "arbitrary"),
        name="in_projection",
    )(h, norm_w, w_main, w_small, w_small_t, conv_w)


def _hi_lo(x):
    hi = _bf(x)
    lo_f = x - hi.astype(F32)
    return hi, _bf(lo_f), lo_f


def _dup_lhs(x2, lo_f, left):
    w = _bf(jnp.where(left, x2, lo_f))
    return jnp.concatenate([w, w], axis=1)


def _dup_rhs(hi, lo):
    return jnp.concatenate([hi, hi, lo, lo], axis=0)


def _mixer_kernel(*refs, blk, nb):
    gqkv_ref, rq_ref, rk_ref, rv_ref, rg_ref, gz_ref, gab_ref = refs[:7]
    gabt_refs = refs[7:7 + nb]
    (cos_ref, sin_ref, dmat_ref, qdec_ref, kdec_ref, cdec_ref, prow_ref, pcol_ref, gnorm_ref, tril_ref, triu_ref,
     ret_ref, gdn_ref, rstate_ref, gstate_ref) = refs[7 + nb:]
    c = GDN_CHUNK
    n_chunks = blk // c
    seqs = range(nb)
    chains = [(s, h) for s in seqs for h in range(GDN_HEADS)]
    inst = [(s, h, n) for s, h in chains for n in range(n_chunks)]
    slot = {ch: ch[0] * GDN_HEADS + ch[1] for ch in chains}

    @pl.when(pl.program_id(1) == 0)
    def _():
        rstate_ref[...] = jnp.zeros_like(rstate_ref)
        gstate_ref[...] = jnp.zeros_like(gstate_ref)

    hsl = [slice(h * HEAD_DIM, (h + 1) * HEAD_DIM) for h in range(RET_HEADS)]

    cos2 = cos_ref[...]
    sin2 = sin_ref[...]
    half = HEAD_DIM // 2
    rq = {(s, h): rq_ref[s, :, hsl[h]].astype(F32) for s, h in chains}
    rk = {(s, h): rk_ref[s, :, hsl[h]].astype(F32) for s, h in chains}
    rv = {(s, h): rv_ref[s, :, hsl[h]] for s, h in chains}
    rq = {ch: q * cos2 + pltpu.roll(q, half, 1) * sin2 for ch, q in rq.items()}
    rk = {ch: (k * cos2 + pltpu.roll(k, half, 1) * sin2) * (HEAD_DIM ** -0.5) for ch, k in rk.items()}
    rstate = {ch: rstate_ref[slot[ch]] for ch in chains}
    scores = {ch: _dot_nt(_bf(rq[ch]), _bf(rk[ch])) * dmat_ref[ch[1]] for ch in chains}
    inter = {ch: _dot(_bf(rq[ch] * qdec_ref[ch[1]]), _bf(rstate[ch])) for ch in chains}
    kv = {ch: _dot_tn(_bf(rk[ch] * kdec_ref[ch[1]]), rv[ch]) for ch in chains}
    ro = {ch: _dot(_bf(scores[ch]), rv[ch]) + inter[ch] for ch in chains}
    for ch in chains:
        s, h = ch
        rstate_ref[slot[ch]] = rstate[ch] * cdec_ref[h] + kv[ch]
        o = ro[ch]
        o = o * lax.rsqrt(jnp.mean(o * o, axis=-1, keepdims=True) + EPS)
        ret_ref[s, :, hsl[h]] = _bf(_silu(rg_ref[s, :, hsl[h]].astype(F32)) * o)

    neg_a_row = -jnp.exp(prow_ref[0:1, :])
    neg_a_col = -jnp.exp(pcol_ref[:, 0:1])
    beta_cols, gc_cols, egc_cols, gc_rows = [], [], [], []
    for s in seqs:
        gab = gab_ref[s]
        beta_cols.append(_sigmoid(gab))
        gc_cols.append(_dot_f32(tril_ref[...], neg_a_row * _softplus(gab + prow_ref[1:2, :])))
        egc_cols.append(jnp.exp(gc_cols[s]))
        gc_rows.append(_dot_f32(neg_a_col * _softplus(gabt_refs[s][...] + pcol_ref[:, 1:2]), triu_ref[...]))

    ri = lax.broadcasted_iota(jnp.int32, (c, 2 * c), 0)
    lane = lax.broadcasted_iota(jnp.int32, (c, 2 * c), 1)
    ci = lane & (c - 1)
    left = lane < c
    ge = ri >= ci
    gt = ri > ci
    eye = jnp.where(ri == ci, 1.0, 0.0).astype(F32)
    level_masks = []
    for lg in range(int(math.log2(c))):
        b = 1 << lg
        same_block = (ri >> (lg + 1)) == (ci >> (lg + 1))
        level_masks.append(same_block & ((ri & (2 * b - 1)) >= b) & ((ci & (2 * b - 1)) < b))

    qh, kh, vh = {}, {}, {}
    for s, h in chains:
        q = gqkv_ref[s, :, hsl[h]].astype(F32)
        k = gqkv_ref[s, :, HEADS_W + h * HEAD_DIM:HEADS_W + (h + 1) * HEAD_DIM].astype(F32)
        vh[s, h] = gqkv_ref[s, :, 2 * HEADS_W + h * HEAD_DIM:2 * HEADS_W + (h + 1) * HEAD_DIM].astype(F32)
        qh[s, h] = q * lax.rsqrt(jnp.sum(q * q, axis=-1, keepdims=True) + EPS) * (HEAD_DIM ** -0.5)
        kh[s, h] = k * lax.rsqrt(jnp.sum(k * k, axis=-1, keepdims=True) + EPS)

    rows = [slice(n * c, (n + 1) * c) for n in range(n_chunks)]
    kc = {(s, h, n): kh[s, h][rows[n]] for s, h, n in inst}
    qc = {(s, h, n): qh[s, h][rows[n]] for s, h, n in inst}
    bcol = {(s, h, n): beta_cols[s][rows[n], 4 + h:5 + h] for s, h, n in inst}
    gcol = {(s, h, n): gc_cols[s][rows[n], h:h + 1] for s, h, n in inst}
    egcol = {(s, h, n): egc_cols[s][rows[n], h:h + 1] for s, h, n in inst}
    glast = {(s, h, n): gc_cols[s][n * c + c - 1:(n + 1) * c, h:h + 1] for s, h, n in inst}
    grow = {(s, h, n): gc_rows[s][h:h + 1, 2 * c * n:2 * c * (n + 1)] for s, h, n in inst}
    kb = {i: kc[i] * bcol[i] for i in inst}
    kcb = {i: _bf(kc[i]) for i in inst}
    kcb2 = {i: jnp.concatenate([kcb[i], kcb[i]], axis=0) for i in inst}
    decay = {i: jnp.where(ge, jnp.exp(jnp.where(ge, gcol[i] - grow[i], 0.0)), 0.0) for i in inst}
    a2 = {i: jnp.where(gt, _dot_nt(_bf(kb[i]), kcb2[i]) * decay[i], 0.0) for i in inst}
    attn = {i: _bf(jnp.where(ge, _dot_nt(_bf(qc[i]), kcb2[i]) * decay[i], 0.0)[:, :c]) for i in inst}

    x2 = {i: eye - jnp.where(level_masks[0], a2[i], 0.0) for i in inst}
    a_hl = {i: _hi_lo(a2[i]) for i in inst}
    for m in level_masks[1:]:
        mb = jnp.where(m, 1.0, 0.0).astype(BF16)
        x_hl = {i: _hi_lo(x2[i]) for i in inst}
        t = {i: _dot(_dup_lhs(x2[i], x_hl[i][2], left), _dup_rhs(a_hl[i][0] * mb, a_hl[i][1] * mb)) for i in inst}
        t_hl = {i: _hi_lo(t[i]) for i in inst}
        x2 = {i: x2[i] - _dot(_dup_lhs(t[i], t_hl[i][2], left), _dup_rhs(x_hl[i][0], x_hl[i][1])) for i in inst}

    uw = {}
    for i in inst:
        s, h, n = i
        rhs = _hi_lo(jnp.concatenate([vh[s, h][rows[n]] * bcol[i], kb[i] * egcol[i]], axis=1))
        uw[i] = _dot(_dup_lhs(x2[i], _hi_lo(x2[i])[2], left), _dup_rhs(rhs[0], rhs[1]))
    wq = {i: _bf(jnp.concatenate([uw[i][:, HEAD_DIM:], qc[i] * egcol[i]], axis=0)) for i in inst}
    kd = {i: _bf(kc[i] * jnp.exp(glast[i] - gcol[i])) for i in inst}
    eg = {i: jnp.exp(glast[i]) for i in inst}

    state = {ch: gstate_ref[slot[ch]] for ch in chains}
    outs = {ch: [] for ch in chains}
    for n in range(n_chunks):
        sb = {ch: _bf(state[ch]) for ch in chains}
        ws = {ch: _dot(wq[ch + (n,)], sb[ch]) for ch in chains}
        vnb = {ch: _bf(uw[ch + (n,)][:, :HEAD_DIM] - ws[ch][:c]) for ch in chains}
        for ch in chains:
            outs[ch].append(ws[ch][c:] + _dot(attn[ch + (n,)], vnb[ch]))
        state = {ch: state[ch] * eg[ch + (n,)] + _dot_tn(kd[ch + (n,)], vnb[ch]) for ch in chains}

    gnorm = gnorm_ref[...]
    for ch in chains:
        s, h = ch
        gstate_ref[slot[ch]] = state[ch]
        o = jnp.concatenate(outs[ch], axis=0)
        o = o * lax.rsqrt(jnp.mean(o * o, axis=-1, keepdims=True) + EPS)
        gdn_ref[s, :, hsl[h]] = _bf(o * gnorm * _silu(gz_ref[s, :, hsl[h]].astype(F32)))


def _token_mixers(conv, proj, gab, gabt, tabs, prow, pcol, gnorm, batch, seq, blk, nb):
    conv3 = conv.reshape(batch, seq, CONV_CH)
    proj3 = proj.reshape(batch, seq, MAIN_COLS - CONV_CH)
    gab3 = gab.reshape(batch, seq, LANES)
    n_blk = seq // blk
    cos2, sin2, dmat, qdec, kdec, cdec, tril, triu = tabs
    w512 = lambda col: pl.BlockSpec((nb, blk, HEADS_W), lambda b, n, col=col: (b, n, col))
    full = lambda shape: pl.BlockSpec(shape, lambda b, n: (0,) * len(shape))
    gabt_specs = [pl.BlockSpec((SMALL_ROWS, blk), lambda b, n, s=s: (0, (b * nb + s) * n_blk + n)) for s in range(nb)]
    out = pl.pallas_call(
        functools.partial(_mixer_kernel, blk=blk, nb=nb),
        grid=(batch // nb, n_blk),
        in_specs=[
            pl.BlockSpec((nb, blk, CONV_CH), lambda b, n: (b, n, 0)),
            w512(4), w512(5), w512(6), w512(7), w512(8),
            pl.BlockSpec((nb, blk, LANES), lambda b, n: (b, n, 0)),
            *gabt_specs,
            pl.BlockSpec((blk, HEAD_DIM), lambda b, n: (n, 0)),
            pl.BlockSpec((blk, HEAD_DIM), lambda b, n: (n, 0)),
            full((RET_HEADS, blk, blk)),
            full((RET_HEADS, blk, HEAD_DIM)),
            full((RET_HEADS, blk, HEAD_DIM)),
            full((RET_HEADS, 1, HEAD_DIM)),
            full((8, LANES)),
            full((SMALL_ROWS, LANES)),
            full((1, HEAD_DIM)),
            full((blk, blk)),
            full((blk, 2 * blk)),
        ],
        out_specs=[
            pl.BlockSpec((nb, blk, HEADS_W), lambda b, n: (b, n, 0)),
            pl.BlockSpec((nb, blk, HEADS_W), lambda b, n: (b, n, 0)),
        ],
        out_shape=[
            jax.ShapeDtypeStruct((batch, seq, HEADS_W), BF16),
            jax.ShapeDtypeStruct((batch, seq, HEADS_W), BF16),
        ],
        scratch_shapes=[
            pltpu.VMEM((nb * RET_HEADS, HEAD_DIM, HEAD_DIM), F32),
            pltpu.VMEM((nb * GDN_HEADS, HEAD_DIM, HEAD_DIM), F32),
        ],
        compiler_params=_params("parallel", "arbitrary"),
        name="token_mixers",
    )(conv3, proj3, proj3, proj3, proj3, proj3, gab3, *([gabt] * nb),
      cos2, sin2, dmat, qdec, kdec, cdec, prow, pcol, gnorm, tril, triu)
    ret, gdn = out
    return ret.reshape(batch * seq, HEADS_W), gdn.reshape(batch * seq, HEADS_W)


def _mixer_tables(seq, blk):
    inv_freq = ROPE_BASE ** (-jnp.arange(0, HEAD_DIM, 2, dtype=F32) / HEAD_DIM)
    ang = jnp.arange(seq, dtype=F32)[:, None] * inv_freq[None, :]
    cos, sin = jnp.cos(ang), jnp.sin(ang)
    cos2 = jnp.concatenate([cos, cos], axis=-1)
    sin2 = jnp.concatenate([-sin, sin], axis=-1)
    log_gamma = jnp.log(1.0 - jnp.exp2(-5.0 - jnp.arange(RET_HEADS, dtype=F32)))
    pos = jnp.arange(blk, dtype=F32)
    diff = pos[:, None] - pos[None, :]
    dmat = jnp.where(diff >= 0, jnp.exp(jnp.maximum(diff, 0.0)[None] * log_gamma[:, None, None]), 0.0)
    ones = jnp.ones((1, 1, HEAD_DIM), F32)
    qdec = jnp.exp((pos + 1.0)[None, :] * log_gamma[:, None])[:, :, None] * ones
    kdec = jnp.exp((blk - 1 - pos)[None, :] * log_gamma[:, None])[:, :, None] * ones
    cdec = jnp.exp(blk * log_gamma)[:, None, None] * ones
    idx = np.arange(blk)
    same = (idx[:, None] // GDN_CHUNK) == (idx[None, :] // GDN_CHUNK)
    tril = jnp.asarray((same & (idx[:, None] >= idx[None, :])).astype(np.float32))
    col = np.arange(2 * blk)
    col_chunk, col_pos = col // (2 * GDN_CHUNK), col % GDN_CHUNK
    triu = jnp.asarray(((idx[:, None] // GDN_CHUNK == col_chunk[None, :])
                        & (idx[:, None] % GDN_CHUNK <= col_pos[None, :])).astype(np.float32))
    return cos2, sin2, dmat, qdec, kdec, cdec, tril, triu


def _merge_kernel(ret_ref, gdn_ref, ga_ref, gb_ref, h_ref, wa_ref, wb_ref, wo_ref, nw_ref, *rest,
                  with_router):
    if with_router:
        wr_ref, ho_ref, logit_ref = rest
    else:
        ho_ref, hn_ref = rest
    ya = _dot(ret_ref[...], wa_ref[...])
    yb = _dot(gdn_ref[...], wb_ref[...])
    merged = _sigmoid(ga_ref[...].astype(F32)) * ya + _sigmoid(gb_ref[...].astype(F32)) * yb
    h_new = h_ref[...] + _dot(_bf(merged), wo_ref[...])
    ho_ref[...] = h_new
    hn = _bf(_rms_rows(h_new, nw_ref[...]))
    if with_router:
        logit_ref[...] = _dot(hn, wr_ref[...])
    else:
        hn_ref[...] = hn


def _merge_project(ret, gdn, proj, h, wa, wb, wo, norm_w, router_w, tm):
    t = h.shape[0]
    with_router = router_w is not None
    in_specs = [
        pl.BlockSpec((tm, HEADS_W), lambda i: (i, 0)),
        pl.BlockSpec((tm, HEADS_W), lambda i: (i, 0)),
        pl.BlockSpec((tm, D_MODEL), lambda i: (i, 0)),
        pl.BlockSpec((tm, D_MODEL), lambda i: (i, 1)),
        pl.BlockSpec((tm, D_MODEL), lambda i: (i, 0)),
        pl.BlockSpec((HEADS_W, D_MODEL), lambda i: (0, 0)),
        pl.BlockSpec((HEADS_W, D_MODEL), lambda i: (0, 0)),
        pl.BlockSpec((D_MODEL, D_MODEL), lambda i: (0, 0)),
        pl.BlockSpec((1, D_MODEL), lambda i: (0, 0)),
    ]
    args = [ret, gdn, proj, proj, h, wa, wb, wo, norm_w]
    if with_router:
        in_specs.append(pl.BlockSpec((D_MODEL, LANES), lambda i: (0, 0)))
        args.append(router_w)
        second = (pl.BlockSpec((tm, LANES), lambda i: (i, 0)), jax.ShapeDtypeStruct((t, LANES), F32))
    else:
        second = (pl.BlockSpec((tm, D_MODEL), lambda i: (i, 0)), jax.ShapeDtypeStruct((t, D_MODEL), BF16))
    return pl.pallas_call(
        functools.partial(_merge_kernel, with_router=with_router),
        grid=(t // tm,),
        in_specs=in_specs,
        out_specs=[pl.BlockSpec((tm, D_MODEL), lambda i: (i, 0)), second[0]],
        out_shape=[jax.ShapeDtypeStruct((t, D_MODEL), F32), second[1]],
        compiler_params=_params("parallel"),
        name="merge_project",
    )(*args)


def _swiglu_rows(x, wg_ref, wu_ref, wd_ref, acc, tf):
    for f in range(D_FF // tf):
        cols = slice(f * tf, (f + 1) * tf)
        act = _silu(_dot(x, wg_ref[0, :, cols])) * _dot(x, wu_ref[0, :, cols])
        acc = acc + _dot(_bf(act), wd_ref[0, cols, :])
    return acc


def _ffn_kernel(hn_ref, h_ref, wg_ref, wu_ref, wd_ref, o_ref, *, tf):
    o_ref[...] = _swiglu_rows(hn_ref[...], wg_ref, wu_ref, wd_ref, h_ref[...], tf)


def _dense_ffn(hn, h, wg, wu, wd, tm, tf):
    t = h.shape[0]
    return pl.pallas_call(
        functools.partial(_ffn_kernel, tf=tf),
        grid=(t // tm,),
        in_specs=[
            pl.BlockSpec((tm, D_MODEL), lambda i: (i, 0)),
            pl.BlockSpec((tm, D_MODEL), lambda i: (i, 0)),
            pl.BlockSpec((1, D_MODEL, D_FF), lambda i: (0, 0, 0)),
            pl.BlockSpec((1, D_MODEL, D_FF), lambda i: (0, 0, 0)),
            pl.BlockSpec((1, D_FF, D_MODEL), lambda i: (0, 0, 0)),
        ],
        out_specs=pl.BlockSpec((tm, D_MODEL), lambda i: (i, 0)),
        out_shape=jax.ShapeDtypeStruct((t, D_MODEL), F32),
        compiler_params=_params("parallel"),
        name="dense_ffn",
    )(hn, h, wg, wu, wd)


def _route_kernel(logit_ref, tri_ref, meta_ref, count_ref, run_ref):
    @pl.when(pl.program_id(0) == 0)
    def _():
        run_ref[...] = jnp.zeros_like(run_ref)

    tm = logit_ref.shape[0]
    lane = lax.broadcasted_iota(jnp.int32, (tm, LANES), 1)
    neg = jnp.float32(-jnp.inf)
    l1 = jnp.where(lane < N_EXPERTS, logit_ref[...], neg)
    m1 = jnp.max(l1, axis=-1, keepdims=True)
    i1 = jnp.min(jnp.where(l1 == m1, lane, LANES), axis=-1, keepdims=True)
    l2 = jnp.where(lane == i1, neg, l1)
    m2 = jnp.max(l2, axis=-1, keepdims=True)
    i2 = jnp.min(jnp.where(l2 == m2, lane, LANES), axis=-1, keepdims=True)
    e2 = jnp.exp(m2 - m1)
    g1 = 1.0 / (1.0 + e2)
    g2 = e2 / (1.0 + e2)
    sel1 = lane == i1
    sel2 = lane == i2
    onehot = jnp.where(sel1 | sel2, 1.0, 0.0).astype(F32)
    before = _dot(tri_ref[...], _bf(onehot)) + run_ref[0:1, :]
    r1 = jnp.sum(jnp.where(sel1, before, 0.0), axis=-1, keepdims=True)
    r2 = jnp.sum(jnp.where(sel2, before, 0.0), axis=-1, keepdims=True)
    run_new = run_ref[0:1, :] + jnp.sum(onehot, axis=0, keepdims=True)
    run_ref[0:1, :] = run_new
    count_ref[...] = jnp.broadcast_to(run_new, count_ref.shape)
    meta = jnp.where(lane == 0, i1.astype(F32), 0.0)
    meta = jnp.where(lane == 1, i2.astype(F32), meta)
    meta = jnp.where(lane == 2, r1, meta)
    meta = jnp.where(lane == 3, r2, meta)
    meta = jnp.where(lane == 4, g1, meta)
    meta = jnp.where(lane == 5, g2, meta)
    meta_ref[...] = meta


def _route(logits, tm):
    t = logits.shape[0]
    idx = np.arange(tm)
    tri = jnp.asarray((idx[:, None] > idx[None, :]).astype(np.float32), dtype=BF16)
    return pl.pallas_call(
        _route_kernel,
        grid=(t // tm,),
        in_specs=[pl.BlockSpec((tm, LANES), lambda i: (i, 0)),
                  pl.BlockSpec((tm, tm), lambda i: (0, 0))],
        out_specs=[pl.BlockSpec((tm, LANES), lambda i: (i, 0)),
                   pl.BlockSpec((8, LANES), lambda i: (0, 0))],
        out_shape=[jax.ShapeDtypeStruct((t, LANES), F32),
                   jax.ShapeDtypeStruct((8, LANES), F32)],
        scratch_shapes=[pltpu.VMEM((8, LANES), F32)],
        compiler_params=_params("arbitrary"),
        name="route",
    )(logits, tri)


def _row_wait(src_hbm, dst_vmem, sem, rows):
    pltpu.make_async_copy(src_hbm.at[pl.ds(0, rows)], dst_vmem, sem).wait()


def _dispatch_kernel(sched_ref, dest_ref, h_ref, nw_ref, xg_ref, hn_ref, zero_ref, sem, zsem, *, rows):
    tm = h_ref.shape[0]

    @pl.when(pl.program_id(0) == 0)
    def _():
        zero_ref[...] = jnp.zeros_like(zero_ref)
        for e in range(N_EXPERTS):
            @pl.when(sched_ref[N_EXPERTS + e] > 0)
            def _():
                start = pl.multiple_of(sched_ref[e] - rows, rows)
                clear = pltpu.make_async_copy(zero_ref, xg_ref.at[pl.ds(start, rows)], zsem)
                clear.start()
                clear.wait()

        def clear_unused(b, carry):
            clear = pltpu.make_async_copy(zero_ref, xg_ref.at[pl.ds(pl.multiple_of(b * rows, rows), rows)], zsem)
            clear.start()
            clear.wait()
            return carry

        lax.fori_loop(sched_ref[2 * N_EXPERTS], xg_ref.shape[0] // rows, clear_unused, 0)

    step = pl.program_id(0)
    slot = step % 2
    rows_ref = hn_ref.at[slot]
    rows_ref[...] = _rms_rows(h_ref[...], nw_ref[...])

    def body(r, carry):
        for j in range(TOP_K):
            d = dest_ref[TOP_K * r + j]
            pltpu.make_async_copy(rows_ref.at[pl.ds(r, 1)], xg_ref.at[pl.ds(d, 1)], sem.at[slot]).start()
        return carry

    lax.fori_loop(0, tm, body, 0, unroll=ROW_DMA_UNROLL)

    def drain(s):
        for _ in range(TOP_K):
            pltpu.make_async_copy(hn_ref.at[s], xg_ref.at[pl.ds(0, tm)], sem.at[s]).wait()

    @pl.when(step > 0)
    def _():
        drain(1 - slot)

    @pl.when(step == pl.num_programs(0) - 1)
    def _():
        drain(slot)


def _dispatch(sched, dest_flat, h, norm_w, n_rows, tm, rows):
    t = h.shape[0]
    grid_spec = pltpu.PrefetchScalarGridSpec(
        num_scalar_prefetch=1,
        grid=(t // tm,),
        in_specs=[
            pl.BlockSpec((TOP_K * tm,), lambda i, sc: (i,), memory_space=pltpu.SMEM),
            pl.BlockSpec((tm, D_MODEL), lambda i, sc: (i, 0)),
            pl.BlockSpec((1, D_MODEL), lambda i, sc: (0, 0)),
        ],
        out_specs=pl.BlockSpec(memory_space=pl.ANY),
        scratch_shapes=[pltpu.VMEM((2, tm, D_MODEL), F32), pltpu.VMEM((rows, D_MODEL), F32),
                        pltpu.SemaphoreType.DMA((2,)), pltpu.SemaphoreType.DMA(())],
    )
    return pl.pallas_call(
        functools.partial(_dispatch_kernel, rows=rows),
        grid_spec=grid_spec,
        out_shape=jax.ShapeDtypeStruct((n_rows, D_MODEL), F32),
        compiler_params=_params("arbitrary"),
        name="dispatch",
    )(sched, dest_flat, h, norm_w)


def _expert_kernel(be_ref, sched_ref, x_ref, wg_ref, wu_ref, wd_ref, y_ref, *, tf):
    del be_ref

    @pl.when(pl.program_id(0) < sched_ref[2 * N_EXPERTS])
    def _():
        y_ref[...] = _swiglu_rows(_bf(x_ref[...]), wg_ref, wu_ref, wd_ref, jnp.zeros(y_ref.shape, F32), tf)

    @pl.when(pl.program_id(0) >= sched_ref[2 * N_EXPERTS])
    def _():
        y_ref[...] = jnp.zeros_like(y_ref)


def _experts(block_e, sched, xg, wg, wu, wd, rows, tf):
    n_rows = xg.shape[0]
    used = lambda i, sc: jnp.minimum(i, sc[2 * N_EXPERTS] - 1)
    grid_spec = pltpu.PrefetchScalarGridSpec(
        num_scalar_prefetch=2,
        grid=(n_rows // rows,),
        in_specs=[
            pl.BlockSpec((rows, D_MODEL), lambda i, be, sc: (used(i, sc), 0)),
            pl.BlockSpec((1, D_MODEL, D_FF), lambda i, be, sc: (be[i], 0, 0)),
            pl.BlockSpec((1, D_MODEL, D_FF), lambda i, be, sc: (be[i], 0, 0)),
            pl.BlockSpec((1, D_FF, D_MODEL), lambda i, be, sc: (be[i], 0, 0)),
        ],
        out_specs=pl.BlockSpec((rows, D_MODEL), lambda i, be, sc: (i, 0)),
    )
    return pl.pallas_call(
        functools.partial(_expert_kernel, tf=tf),
        grid_spec=grid_spec,
        out_shape=jax.ShapeDtypeStruct((n_rows, D_MODEL), F32),
        compiler_params=_params("arbitrary"),
        name="experts",
    )(block_e, sched, xg, wg, wu, wd)


def _combine_kernel(dest_ref, next_dest_ref, meta_ref, h_ref, nw_ref, y_ref, o_ref, buf_ref, sem):
    tm = h_ref.shape[0]
    step = pl.program_id(0)
    slot = step % 2

    def fetch(idx_ref, s):
        def body(r, carry):
            for j in range(TOP_K):
                d = idx_ref[TOP_K * r + j]
                pltpu.make_async_copy(y_ref.at[pl.ds(d, 1)], buf_ref.at[s, j, pl.ds(r, 1)], sem.at[s]).start()
            return carry

        lax.fori_loop(0, tm, body, 0, unroll=ROW_DMA_UNROLL)

    @pl.when(step == 0)
    def _():
        fetch(dest_ref, 0)

    @pl.when(step + 1 < pl.num_programs(0))
    def _():
        fetch(next_dest_ref, 1 - slot)

    for j in range(TOP_K):
        _row_wait(y_ref, buf_ref.at[slot, j], sem.at[slot], tm)
    meta = meta_ref[...]
    moe = meta[:, 4:5] * buf_ref[slot, 0] + meta[:, 5:6] * buf_ref[slot, 1]
    o_ref[...] = _rms_rows(h_ref[...] + moe, nw_ref[...])


def _combine(dest_flat, meta, h, norm_w, y, tm):
    t = h.shape[0]
    last = t // tm - 1
    return pl.pallas_call(
        _combine_kernel,
        grid=(t // tm,),
        in_specs=[
            pl.BlockSpec((TOP_K * tm,), lambda i: (i,), memory_space=pltpu.SMEM),
            pl.BlockSpec((TOP_K * tm,), lambda i: (jnp.minimum(i + 1, last),), memory_space=pltpu.SMEM),
            pl.BlockSpec((tm, LANES), lambda i: (i, 0)),
            pl.BlockSpec((tm, D_MODEL), lambda i: (i, 0)),
            pl.BlockSpec((1, D_MODEL), lambda i: (0, 0)),
            pl.BlockSpec(memory_space=pl.ANY),
        ],
        out_specs=pl.BlockSpec((tm, D_MODEL), lambda i: (i, 0)),
        out_shape=jax.ShapeDtypeStruct((t, D_MODEL), F32),
        scratch_shapes=[pltpu.VMEM((2, TOP_K, tm, D_MODEL), F32), pltpu.SemaphoreType.DMA((2,))],
        compiler_params=_params("arbitrary"),
        name="combine",
    )(dest_flat, dest_flat, meta, h, norm_w, y)


def _moe_layer(logits, h, ffn_norm_w, final_norm_w, wg, wu, wd, rows, tm_route, tm_rows, tf):
    t = h.shape[0]
    meta, counts = _route(logits, tm_route)
    counts = counts[0, :N_EXPERTS].astype(jnp.int32)
    padded = ((counts + rows - 1) // rows) * rows
    pend = jnp.cumsum(padded)
    pstart = pend - padded
    n_rows = t * TOP_K + N_EXPERTS * rows
    expert = meta[:, 0:TOP_K].astype(jnp.int32)
    rank = meta[:, TOP_K:2 * TOP_K].astype(jnp.int32)
    dest = (pstart[expert] + rank).reshape(t * TOP_K)
    n_used = pend[N_EXPERTS - 1] // rows
    block_start = jnp.minimum(jnp.arange(n_rows // rows, dtype=jnp.int32), n_used - 1) * rows
    block_e = jnp.sum(block_start[:, None] >= pend[None, :], axis=1).astype(jnp.int32)
    sched = jnp.concatenate([pend, padded, n_used[None]]).astype(jnp.int32)
    xg = _dispatch(sched, dest, h, ffn_norm_w, n_rows, tm_rows, rows)
    y = _experts(block_e, sched, xg, wg, wu, wd, rows, tf)
    return _combine(dest, meta, h, final_norm_w, y, tm_rows)


def _pack_in_weights(w_in):
    o = np.cumsum((0, HEADS_W, HEADS_W, HEADS_W, HEADS_W, CONV_CH, HEADS_W, GDN_HEADS, GDN_HEADS, D_MODEL, D_MODEL))
    rq, rk, rv, rg, gqkv, gz, ga, gb, ma, mb = (w_in[:, o[i]:o[i + 1]] for i in range(10))
    main = _bf(jnp.concatenate([gqkv, ma, mb, rq, rk, rv, rg, gz], axis=1))
    small = jnp.concatenate([ga, gb], axis=1)
    small_cols = _bf(jnp.pad(small, ((0, 0), (0, LANES - 2 * GDN_HEADS))))
    small_rows = _bf(jnp.pad(small.T, ((0, SMALL_ROWS - 2 * GDN_HEADS), (0, 0))))
    return main, small_cols, small_rows


def _pick(n, prefs):
    for p in prefs:
        if n % p == 0:
            return p
    raise ValueError(f"no tile in {prefs} divides {n}")


def kernel(x, norm_mix, w_in, conv_w, a_log, dt_bias, gdn_norm, w_branch, w_out, norm_ffn,
           dense_w_gate, dense_w_up, dense_w_down, router, moe_w_gate, moe_w_up, moe_w_down, final_norm):
    batch, seq, d = x.shape
    depth = norm_mix.shape[0]
    assert d == D_MODEL and seq % GDN_CHUNK == 0
    t = batch * seq
    blk = _pick(seq, (128, 64))
    nb = _pick(batch, (4, 2, 1))
    tm = _pick(seq, (1024, 512, 256))
    tm_rows = _pick(t, (512, 256))
    tf_dense = 256
    rows = 512
    tabs = _mixer_tables(seq, blk)

    h = x.reshape(t, d)
    out = None
    for layer in range(depth):
        w_main, w_small, w_small_t = _pack_in_weights(w_in[layer])
        convw = jnp.pad(conv_w[layer], ((0, 8 - CONV_K), (0, 0)))
        conv, proj, gab, gabt = _in_projection(h, norm_mix[layer][None, :], w_main, w_small, w_small_t, convw, tm, CONV_CH, seq)
        prow = jnp.zeros((8, LANES), F32)
        prow = prow.at[0, :GDN_HEADS].set(a_log[layer]).at[1, :GDN_HEADS].set(dt_bias[layer])
        pcol = jnp.zeros((SMALL_ROWS, LANES), F32)
        pcol = pcol.at[:GDN_HEADS, 0].set(a_log[layer]).at[:GDN_HEADS, 1].set(dt_bias[layer])
        ret, gdn = _token_mixers(conv, proj, gab, gabt, tabs, prow, pcol, gdn_norm[layer][None, :],
                                 batch, seq, blk, nb)
        i = layer // 2
        is_moe = layer % 2 == 1
        router_w = _bf(jnp.pad(router[i], ((0, 0), (0, LANES - N_EXPERTS)))) if is_moe else None
        h, second = _merge_project(ret, gdn, proj, h, _bf(w_branch[layer, 0]), _bf(w_branch[layer, 1]),
                                   _bf(w_out[layer]), norm_ffn[layer][None, :], router_w, tm_rows)
        if not is_moe:
            h = _dense_ffn(second, h, _bf(dense_w_gate[i])[None], _bf(dense_w_up[i])[None],
                           _bf(dense_w_down[i])[None], tm_rows, tf_dense)
            if layer == depth - 1:
                raise NotImplementedError("final norm after a dense layer is not wired")
        else:
            if layer != depth - 1:
                raise NotImplementedError("routed layer must be last")
            out = _moe_layer(second, h, norm_ffn[layer][None, :], final_norm[None, :],
                             _bf(moe_w_gate[i]), _bf(moe_w_up[i]), _bf(moe_w_down[i]),
                             rows, tm_rows, tm_rows, 256)
    return out.reshape(batch, seq, d)
```

```python
import functools
import math

import jax
import jax.numpy as jnp
import numpy as np
from jax import lax
from jax.experimental import pallas as pl
from jax.experimental.pallas import tpu as pltpu

F32 = jnp.float32
BF16 = jnp.bfloat16

D_MODEL = 1024
RET_HEADS = 4
GDN_HEADS = 4
HEAD_DIM = 128
HEADS_W = 512
CONV_K = 4
CONV_CH = 3 * HEADS_W
GDN_CHUNK = 64
ROPE_BASE = 10000.0
D_FF = 2816
N_EXPERTS = 8
TOP_K = 2
EPS = 1e-6
LANES = 128
SMALL_ROWS = 16
ROW_DMA_UNROLL = 8
CONV_ROWS = 256
MIXER_GROUP = 16

MAIN_COLS = CONV_CH + 5 * HEADS_W + 2 * D_MODEL
VMEM_LIMIT = 56 * 1024 * 1024

HIGHEST = lax.Precision.HIGHEST


def _bf(x):
    return x.astype(BF16)


def _dot(a, b):
    return jnp.dot(a, b, preferred_element_type=F32)


def _dot_nt(a, b):
    return lax.dot_general(a, b, (((1,), (1,)), ((), ())), preferred_element_type=F32)


def _dot_tn(a, b):
    return lax.dot_general(a, b, (((0,), (0,)), ((), ())), preferred_element_type=F32)


def _dot_f32(a, b):
    return jnp.dot(a, b, preferred_element_type=F32, precision=HIGHEST)


def _sigmoid(x):
    return 1.0 / (1.0 + jnp.exp(-x))


def _silu(x):
    return x * _sigmoid(x)


def _softplus(x):
    return jnp.maximum(x, 0.0) + jnp.log(1.0 + jnp.exp(-jnp.abs(x)))


def _rms_rows(x, w):
    ms = jnp.mean(x * x, axis=-1, keepdims=True)
    return x * lax.rsqrt(ms + EPS) * w


def _params(*sem):
    return pltpu.CompilerParams(dimension_semantics=sem, vmem_limit_bytes=VMEM_LIMIT)


def _inproj_kernel(x_ref, nw_ref, w_ref, ws_ref, wst_ref, cw_ref, oc_ref, o_ref, os_ref, ost_ref,
                   hn_ref, xs_ref, *, tiles_per_seq, n_col):
    tm = x_ref.shape[0]
    tile, step = pl.program_id(0), pl.program_id(1)

    @pl.when(step == 0)
    def _():
        hn = _bf(_rms_rows(x_ref[...], nw_ref[...]))
        hn_ref[...] = hn
        os_ref[...] = _dot(hn, ws_ref[...])
        ost_ref[...] = _dot_nt(wst_ref[...], hn)

        @pl.when(tile % tiles_per_seq == 0)
        def _():
            xs_ref[0:8, :] = jnp.zeros((8, CONV_CH), F32)

        xs_ref[8:8 + tm, :] = _dot(hn, w_ref[...])

    def conv_piece(r0, c0):
        cols = slice(c0, c0 + LANES)
        acc = cw_ref[CONV_K - 1:CONV_K, cols] * xs_ref[pl.ds(r0 + 8, CONV_ROWS), cols]
        for d in range(1, CONV_K):
            acc = acc + cw_ref[CONV_K - 1 - d:CONV_K - d, cols] * xs_ref[pl.ds(r0 + 8 - d, CONV_ROWS), cols]
        oc_ref[r0:r0 + CONV_ROWS, cols] = _bf(_silu(acc))

    conv_pieces = [(r0, c0) for r0 in range(0, tm, CONV_ROWS) for c0 in range(0, CONV_CH, LANES)]
    dot_pieces = [(r0, c0) for r0 in range(0, tm, CONV_ROWS) for c0 in range(0, o_ref.shape[1], HEADS_W)]
    share = -(-len(conv_pieces) // (n_col - 1))
    for k in range(1, n_col):
        @pl.when(step == k)
        def _(k=k):
            mine = conv_pieces[(k - 1) * share:k * share]
            per_dot = -(-len(mine) // len(dot_pieces))
            for j, (r0, c0) in enumerate(dot_pieces):
                rows, cols = slice(r0, r0 + CONV_ROWS), slice(c0, c0 + HEADS_W)
                o_ref[rows, cols] = _bf(_dot(hn_ref[rows, :], w_ref[:, cols]))
                for piece in mine[j * per_dot:(j + 1) * per_dot]:
                    conv_piece(*piece)
            if k == n_col - 1:
                xs_ref[0:8, :] = xs_ref[tm:tm + 8, :]


def _in_projection(h, norm_w, w_main, w_small, w_small_t, conv_w, tm, tn, seq):
    t = h.shape[0]
    assert tn == CONV_CH and seq % tm == 0 and tm % CONV_ROWS == 0
    n_col = MAIN_COLS // tn
    return pl.pallas_call(
        functools.partial(_inproj_kernel, tiles_per_seq=seq // tm, n_col=n_col),
        grid=(t // tm, n_col),
        in_specs=[
            pl.BlockSpec((tm, D_MODEL), lambda i, j: (i, 0)),
            pl.BlockSpec((1, D_MODEL), lambda i, j: (0, 0)),
            pl.BlockSpec((D_MODEL, tn), lambda i, j: (0, j)),
            pl.BlockSpec((D_MODEL, LANES), lambda i, j: (0, 0)),
            pl.BlockSpec((SMALL_ROWS, D_MODEL), lambda i, j: (0, 0)),
            pl.BlockSpec((8, CONV_CH), lambda i, j: (0, 0)),
        ],
        out_specs=[
            pl.BlockSpec((tm, CONV_CH), lambda i, j: (i, 0)),
            pl.BlockSpec((tm, tn), lambda i, j: (i, jnp.maximum(j - 1, 0))),
            pl.BlockSpec((tm, LANES), lambda i, j: (i, 0)),
            pl.BlockSpec((SMALL_ROWS, tm), lambda i, j: (0, i)),
        ],
        out_shape=[
            jax.ShapeDtypeStruct((t, CONV_CH), BF16),
            jax.ShapeDtypeStruct((t, MAIN_COLS - CONV_CH), BF16),
            jax.ShapeDtypeStruct((t, LANES), F32),
            jax.ShapeDtypeStruct((SMALL_ROWS, t), F32),
        ],
        scratch_shapes=[pltpu.VMEM((tm, D_MODEL), BF16), pltpu.VMEM((tm + 8, CONV_CH), F32)],
        compiler_params=_params("arbitrary", "arbitrary"),
        name="in_projection",
    )(h, norm_w, w_main, w_small, w_small_t, conv_w)


def _hi_lo(x):
    hi = _bf(x)
    lo_f = x - hi.astype(F32)
    return hi, _bf(lo_f), lo_f


def _dup_lhs(x2, lo_f, left):
    w = _bf(jnp.where(left, x2, lo_f))
    return jnp.concatenate([w, w], axis=1)


def _dup_rhs(hi, lo):
    return jnp.concatenate([hi, hi, lo, lo], axis=0)


def _mixer_kernel(*refs, blk, nb):
    gqkv_ref, rq_ref, rk_ref, rv_ref, rg_ref, gz_ref, gab_ref = refs[:7]
    gabt_refs = refs[7:7 + nb]
    (cos_ref, sin_ref, dmat_ref, qdec_ref, kdec_ref, cdec_ref, prow_ref, pcol_ref, gnorm_ref, tril_ref, triu_ref,
     ret_ref, gdn_ref, rstate_ref, gstate_ref) = refs[7 + nb:]
    c = GDN_CHUNK
    n_chunks = blk // c
    seqs = range(nb)
    chains = [(s, h) for s in seqs for h in range(GDN_HEADS)]
    inst = [(s, h, n) for s, h in chains for n in range(n_chunks)]
    slot = {ch: ch[0] * GDN_HEADS + ch[1] for ch in chains}

    @pl.when(pl.program_id(1) == 0)
    def _():
        rstate_ref[...] = jnp.zeros_like(rstate_ref)
        gstate_ref[...] = jnp.zeros_like(gstate_ref)

    hsl = [slice(h * HEAD_DIM, (h + 1) * HEAD_DIM) for h in range(RET_HEADS)]

    cos2 = cos_ref[...]
    sin2 = sin_ref[...]
    half = HEAD_DIM // 2
    rq = {(s, h): rq_ref[s, :, hsl[h]].astype(F32) for s, h in chains}
    rk = {(s, h): rk_ref[s, :, hsl[h]].astype(F32) for s, h in chains}
    rv = {(s, h): rv_ref[s, :, hsl[h]] for s, h in chains}
    rq = {ch: q * cos2 + pltpu.roll(q, half, 1) * sin2 for ch, q in rq.items()}
    rk = {ch: (k * cos2 + pltpu.roll(k, half, 1) * sin2) * (HEAD_DIM ** -0.5) for ch, k in rk.items()}
    rstate = {ch: rstate_ref[slot[ch]] for ch in chains}
    scores = {ch: _dot_nt(_bf(rq[ch]), _bf(rk[ch])) * dmat_ref[ch[1]] for ch in chains}
    inter = {ch: _dot(_bf(rq[ch] * qdec_ref[ch[1]]), _bf(rstate[ch])) for ch in chains}
    kv = {ch: _dot_tn(_bf(rk[ch] * kdec_ref[ch[1]]), rv[ch]) for ch in chains}
    ro = {ch: _dot(_bf(scores[ch]), rv[ch]) + inter[ch] for ch in chains}
    for ch in chains:
        s, h = ch
        rstate_ref[slot[ch]] = rstate[ch] * cdec_ref[h] + kv[ch]
        o = ro[ch]
        o = o * lax.rsqrt(jnp.mean(o * o, axis=-1, keepdims=True) + EPS)
        ret_ref[s, :, hsl[h]] = _bf(_silu(rg_ref[s, :, hsl[h]].astype(F32)) * o)

    neg_a_row = -jnp.exp(prow_ref[0:1, :])
    neg_a_col = -jnp.exp(pcol_ref[:, 0:1])
    beta_cols, gc_cols, egc_cols, gc_rows = [], [], [], []
    for s in seqs:
        gab = gab_ref[s]
        beta_cols.append(_sigmoid(gab))
        gc_cols.append(_dot_f32(tril_ref[...], neg_a_row * _softplus(gab + prow_ref[1:2, :])))
        egc_cols.append(jnp.exp(gc_cols[s]))
        gc_rows.append(_dot_f32(neg_a_col * _softplus(gabt_refs[s][...] + pcol_ref[:, 1:2]), triu_ref[...]))

    ri = lax.broadcasted_iota(jnp.int32, (c, 2 * c), 0)
    lane = lax.broadcasted_iota(jnp.int32, (c, 2 * c), 1)
    ci = lane & (c - 1)
    left = lane < c
    ge = ri >= ci
    gt = ri > ci
    eye = jnp.where(ri == ci, 1.0, 0.0).astype(F32)
    level_masks = []
    for lg in range(int(math.log2(c))):
        b = 1 << lg
        same_block = (ri >> (lg + 1)) == (ci >> (lg + 1))
        level_masks.append(same_block & ((ri & (2 * b - 1)) >= b) & ((ci & (2 * b - 1)) < b))

    qh, kh, vh = {}, {}, {}
    for s, h in chains:
        q = gqkv_ref[s, :, hsl[h]].astype(F32)
        k = gqkv_ref[s, :, HEADS_W + h * HEAD_DIM:HEADS_W + (h + 1) * HEAD_DIM].astype(F32)
        vh[s, h] = gqkv_ref[s, :, 2 * HEADS_W + h * HEAD_DIM:2 * HEADS_W + (h + 1) * HEAD_DIM].astype(F32)
        qh[s, h] = q * lax.rsqrt(jnp.sum(q * q, axis=-1, keepdims=True) + EPS) * (HEAD_DIM ** -0.5)
        kh[s, h] = k * lax.rsqrt(jnp.sum(k * k, axis=-1, keepdims=True) + EPS)

    rows = [slice(n * c, (n + 1) * c) for n in range(n_chunks)]
    uw, wq, kd, eg, attn = {}, {}, {}, {}, {}
    for g0 in range(0, len(inst), MIXER_GROUP):
        grp = inst[g0:g0 + MIXER_GROUP]
        kc = {(s, h, n): kh[s, h][rows[n]] for s, h, n in grp}
        qc = {(s, h, n): qh[s, h][rows[n]] for s, h, n in grp}
        bcol = {(s, h, n): beta_cols[s][rows[n], 4 + h:5 + h] for s, h, n in grp}
        gcol = {(s, h, n): gc_cols[s][rows[n], h:h + 1] for s, h, n in grp}
        egcol = {(s, h, n): egc_cols[s][rows[n], h:h + 1] for s, h, n in grp}
        glast = {(s, h, n): gc_cols[s][n * c + c - 1:(n + 1) * c, h:h + 1] for s, h, n in grp}
        grow = {(s, h, n): gc_rows[s][h:h + 1, 2 * c * n:2 * c * (n + 1)] for s, h, n in grp}
        kb = {i: kc[i] * bcol[i] for i in grp}
        kcb = {i: _bf(kc[i]) for i in grp}
        kcb2 = {i: jnp.concatenate([kcb[i], kcb[i]], axis=0) for i in grp}
        decay = {i: jnp.where(ge, jnp.exp(jnp.where(ge, gcol[i] - grow[i], 0.0)), 0.0) for i in grp}
        a2 = {i: jnp.where(gt, _dot_nt(_bf(kb[i]), kcb2[i]) * decay[i], 0.0) for i in grp}
        for i in grp:
            attn[i] = _bf(jnp.where(ge, _dot_nt(_bf(qc[i]), kcb2[i]) * decay[i], 0.0)[:, :c])

        x2 = {i: eye - jnp.where(level_masks[0], a2[i], 0.0) for i in grp}
        a_hl = {i: _hi_lo(a2[i]) for i in grp}
        for m in level_masks[1:]:
            mb = jnp.where(m, 1.0, 0.0).astype(BF16)
            x_hl = {i: _hi_lo(x2[i]) for i in grp}
            t = {i: _dot(_dup_lhs(x2[i], x_hl[i][2], left), _dup_rhs(a_hl[i][0] * mb, a_hl[i][1] * mb)) for i in grp}
            t_hl = {i: _hi_lo(t[i]) for i in grp}
            x2 = {i: x2[i] - _dot(_dup_lhs(t[i], t_hl[i][2], left), _dup_rhs(x_hl[i][0], x_hl[i][1])) for i in grp}

        for i in grp:
            s, h, n = i
            rhs = _hi_lo(jnp.concatenate([vh[s, h][rows[n]] * bcol[i], kb[i] * egcol[i]], axis=1))
            uw[i] = _dot(_dup_lhs(x2[i], _hi_lo(x2[i])[2], left), _dup_rhs(rhs[0], rhs[1]))
        for i in grp:
            wq[i] = _bf(jnp.concatenate([uw[i][:, HEAD_DIM:], qc[i] * egcol[i]], axis=0))
            kd[i] = _bf(kc[i] * jnp.exp(glast[i] - gcol[i]))
            eg[i] = jnp.exp(glast[i])

    state = {ch: gstate_ref[slot[ch]] for ch in chains}
    outs = {ch: [] for ch in chains}
    for n in range(n_chunks):
        sb = {ch: _bf(state[ch]) for ch in chains}
        ws = {ch: _dot(wq[ch + (n,)], sb[ch]) for ch in chains}
        vnb = {ch: _bf(uw[ch + (n,)][:, :HEAD_DIM] - ws[ch][:c]) for ch in chains}
        for ch in chains:
            outs[ch].append(ws[ch][c:] + _dot(attn[ch + (n,)], vnb[ch]))
        state = {ch: state[ch] * eg[ch + (n,)] + _dot_tn(kd[ch + (n,)], vnb[ch]) for ch in chains}

    gnorm = gnorm_ref[...]
    for ch in chains:
        s, h = ch
        gstate_ref[slot[ch]] = state[ch]
        o = jnp.concatenate(outs[ch], axis=0)
        o = o * lax.rsqrt(jnp.mean(o * o, axis=-1, keepdims=True) + EPS)
        gdn_ref[s, :, hsl[h]] = _bf(o * gnorm * _silu(gz_ref[s, :, hsl[h]].astype(F32)))


def _token_mixers(conv, proj, gab, gabt, tabs, prow, pcol, gnorm, batch, seq, blk, nb):
    conv3 = conv.reshape(batch, seq, CONV_CH)
    proj3 = proj.reshape(batch, seq, MAIN_COLS - CONV_CH)
    gab3 = gab.reshape(batch, seq, LANES)
    n_blk = seq // blk
    cos2, sin2, dmat, qdec, kdec, cdec, tril, triu = tabs
    w512 = lambda col: pl.BlockSpec((nb, blk, HEADS_W), lambda b, n, col=col: (b, n, col))
    full = lambda shape: pl.BlockSpec(shape, lambda b, n: (0,) * len(shape))
    gabt_specs = [pl.BlockSpec((SMALL_ROWS, blk), lambda b, n, s=s: (0, (b * nb + s) * n_blk + n)) for s in range(nb)]
    out = pl.pallas_call(
        functools.partial(_mixer_kernel, blk=blk, nb=nb),
        grid=(batch // nb, n_blk),
        in_specs=[
            pl.BlockSpec((nb, blk, CONV_CH), lambda b, n: (b, n, 0)),
            w512(4), w512(5), w512(6), w512(7), w512(8),
            pl.BlockSpec((nb, blk, LANES), lambda b, n: (b, n, 0)),
            *gabt_specs,
            pl.BlockSpec((blk, HEAD_DIM), lambda b, n: (n, 0)),
            pl.BlockSpec((blk, HEAD_DIM), lambda b, n: (n, 0)),
            full((RET_HEADS, blk, blk)),
            full((RET_HEADS, blk, HEAD_DIM)),
            full((RET_HEADS, blk, HEAD_DIM)),
            full((RET_HEADS, 1, HEAD_DIM)),
            full((8, LANES)),
            full((SMALL_ROWS, LANES)),
            full((1, HEAD_DIM)),
            full((blk, blk)),
            full((blk, 2 * blk)),
        ],
        out_specs=[
            pl.BlockSpec((nb, blk, HEADS_W), lambda b, n: (b, n, 0)),
            pl.BlockSpec((nb, blk, HEADS_W), lambda b, n: (b, n, 0)),
        ],
        out_shape=[
            jax.ShapeDtypeStruct((batch, seq, HEADS_W), BF16),
            jax.ShapeDtypeStruct((batch, seq, HEADS_W), BF16),
        ],
        scratch_shapes=[
            pltpu.VMEM((nb * RET_HEADS, HEAD_DIM, HEAD_DIM), F32),
            pltpu.VMEM((nb * GDN_HEADS, HEAD_DIM, HEAD_DIM), F32),
        ],
        compiler_params=_params("parallel", "arbitrary"),
        name="token_mixers",
    )(conv3, proj3, proj3, proj3, proj3, proj3, gab3, *([gabt] * nb),
      cos2, sin2, dmat, qdec, kdec, cdec, prow, pcol, gnorm, tril, triu)
    ret, gdn = out
    return ret.reshape(batch * seq, HEADS_W), gdn.reshape(batch * seq, HEADS_W)


def _mixer_tables(seq, blk):
    inv_freq = ROPE_BASE ** (-jnp.arange(0, HEAD_DIM, 2, dtype=F32) / HEAD_DIM)
    ang = jnp.arange(seq, dtype=F32)[:, None] * inv_freq[None, :]
    cos, sin = jnp.cos(ang), jnp.sin(ang)
    cos2 = jnp.concatenate([cos, cos], axis=-1)
    sin2 = jnp.concatenate([-sin, sin], axis=-1)
    log_gamma = jnp.log(1.0 - jnp.exp2(-5.0 - jnp.arange(RET_HEADS, dtype=F32)))
    pos = jnp.arange(blk, dtype=F32)
    diff = pos[:, None] - pos[None, :]
    dmat = jnp.where(diff >= 0, jnp.exp(jnp.maximum(diff, 0.0)[None] * log_gamma[:, None, None]), 0.0)
    ones = jnp.ones((1, 1, HEAD_DIM), F32)
    qdec = jnp.exp((pos + 1.0)[None, :] * log_gamma[:, None])[:, :, None] * ones
    kdec = jnp.exp((blk - 1 - pos)[None, :] * log_gamma[:, None])[:, :, None] * ones
    cdec = jnp.exp(blk * log_gamma)[:, None, None] * ones
    idx = np.arange(blk)
    same = (idx[:, None] // GDN_CHUNK) == (idx[None, :] // GDN_CHUNK)
    tril = jnp.asarray((same & (idx[:, None] >= idx[None, :])).astype(np.float32))
    col = np.arange(2 * blk)
    col_chunk, col_pos = col // (2 * GDN_CHUNK), col % GDN_CHUNK
    triu = jnp.asarray(((idx[:, None] // GDN_CHUNK == col_chunk[None, :])
                        & (idx[:, None] % GDN_CHUNK <= col_pos[None, :])).astype(np.float32))
    return cos2, sin2, dmat, qdec, kdec, cdec, tril, triu


def _swiglu_rows(x, wg_ref, wu_ref, wd_ref, acc, tf):
    for f in range(D_FF // tf):
        cols = slice(f * tf, (f + 1) * tf)
        act = _silu(_dot(x, wg_ref[0, :, cols])) * _dot(x, wu_ref[0, :, cols])
        acc = acc + _dot(_bf(act), wd_ref[0, cols, :])
    return acc


def _merged_rows(ret_ref, gdn_ref, ga_ref, gb_ref, h_ref, wa_ref, wb_ref, wo_ref, nw_ref):
    ya = _dot(ret_ref[...], wa_ref[...])
    yb = _dot(gdn_ref[...], wb_ref[...])
    merged = _sigmoid(ga_ref[...].astype(F32)) * ya + _sigmoid(gb_ref[...].astype(F32)) * yb
    h_new = h_ref[...] + _dot(_bf(merged), wo_ref[...])
    return h_new, _bf(_rms_rows(h_new, nw_ref[...]))


def _merge_ffn_kernel(*refs, tf):
    wg_ref, wu_ref, wd_ref, o_ref = refs[9:]
    h_new, hn = _merged_rows(*refs[:9])
    o_ref[...] = _swiglu_rows(hn, wg_ref, wu_ref, wd_ref, h_new, tf)


def _merge_router_kernel(*refs):
    wr_ref, ho_ref, logit_ref = refs[9:]
    h_new, hn = _merged_rows(*refs[:9])
    ho_ref[...] = h_new
    logit_ref[...] = _dot(hn, wr_ref[...])


def _resident(shape):
    return pl.BlockSpec(shape, lambda i: (0,) * len(shape), pipeline_mode=pl.Buffered(1))


def _merge_specs(tm):
    rows = lambda width, col: pl.BlockSpec((tm, width), lambda i: (i, col))
    return [rows(HEADS_W, 0), rows(HEADS_W, 0), rows(D_MODEL, 0), rows(D_MODEL, 1), rows(D_MODEL, 0),
            _resident((HEADS_W, D_MODEL)), _resident((HEADS_W, D_MODEL)), _resident((D_MODEL, D_MODEL)),
            _resident((1, D_MODEL))]


def _merge_dense_ffn(ret, gdn, proj, h, wa, wb, wo, norm_w, wg, wu, wd, tm, tf):
    t = h.shape[0]
    return pl.pallas_call(
        functools.partial(_merge_ffn_kernel, tf=tf),
        grid=(t // tm,),
        in_specs=_merge_specs(tm) + [_resident((1, D_MODEL, D_FF)), _resident((1, D_MODEL, D_FF)),
                                     _resident((1, D_FF, D_MODEL))],
        out_specs=pl.BlockSpec((tm, D_MODEL), lambda i: (i, 0)),
        out_shape=jax.ShapeDtypeStruct((t, D_MODEL), F32),
        compiler_params=_params("parallel"),
        name="merge_dense_ffn",
    )(ret, gdn, proj, proj, h, wa, wb, wo, norm_w, wg, wu, wd)


def _merge_router(ret, gdn, proj, h, wa, wb, wo, norm_w, router_w, tm):
    t = h.shape[0]
    return pl.pallas_call(
        _merge_router_kernel,
        grid=(t // tm,),
        in_specs=_merge_specs(tm) + [_resident((D_MODEL, LANES))],
        out_specs=[pl.BlockSpec((tm, D_MODEL), lambda i: (i, 0)), pl.BlockSpec((tm, LANES), lambda i: (i, 0))],
        out_shape=[jax.ShapeDtypeStruct((t, D_MODEL), F32), jax.ShapeDtypeStruct((t, LANES), F32)],
        compiler_params=_params("parallel"),
        name="merge_router",
    )(ret, gdn, proj, proj, h, wa, wb, wo, norm_w, router_w)


def _route_kernel(logit_ref, tri_ref, meta_ref, meta_t_ref, count_ref, run_ref):
    @pl.when(pl.program_id(0) == 0)
    def _():
        run_ref[...] = jnp.zeros_like(run_ref)

    tm = logit_ref.shape[0]
    lane = lax.broadcasted_iota(jnp.int32, (tm, LANES), 1)
    neg = jnp.float32(-jnp.inf)
    l1 = jnp.where(lane < N_EXPERTS, logit_ref[...], neg)
    m1 = jnp.max(l1, axis=-1, keepdims=True)
    i1 = jnp.min(jnp.where(l1 == m1, lane, LANES), axis=-1, keepdims=True)
    l2 = jnp.where(lane == i1, neg, l1)
    m2 = jnp.max(l2, axis=-1, keepdims=True)
    i2 = jnp.min(jnp.where(l2 == m2, lane, LANES), axis=-1, keepdims=True)
    e2 = jnp.exp(m2 - m1)
    g1 = 1.0 / (1.0 + e2)
    g2 = e2 / (1.0 + e2)
    sel1 = lane == i1
    sel2 = lane == i2
    onehot = jnp.where(sel1 | sel2, 1.0, 0.0).astype(F32)
    before = _dot(tri_ref[...], _bf(onehot)) + run_ref[0:1, :]
    r1 = jnp.sum(jnp.where(sel1, before, 0.0), axis=-1, keepdims=True)
    r2 = jnp.sum(jnp.where(sel2, before, 0.0), axis=-1, keepdims=True)
    run_new = run_ref[0:1, :] + jnp.sum(onehot, axis=0, keepdims=True)
    run_ref[0:1, :] = run_new
    count_ref[...] = jnp.broadcast_to(run_new, count_ref.shape)
    meta = jnp.where(lane == 0, i1.astype(F32), 0.0)
    meta = jnp.where(lane == 1, i2.astype(F32), meta)
    meta = jnp.where(lane == 2, r1, meta)
    meta = jnp.where(lane == 3, r2, meta)
    meta = jnp.where(lane == 4, g1, meta)
    meta = jnp.where(lane == 5, g2, meta)
    meta_ref[...] = meta
    meta_t_ref[...] = meta.T[:8, :]


def _route(logits, tm):
    t = logits.shape[0]
    idx = np.arange(tm)
    tri = jnp.asarray((idx[:, None] > idx[None, :]).astype(np.float32), dtype=BF16)
    return pl.pallas_call(
        _route_kernel,
        grid=(t // tm,),
        in_specs=[pl.BlockSpec((tm, LANES), lambda i: (i, 0)),
                  pl.BlockSpec((tm, tm), lambda i: (0, 0))],
        out_specs=[pl.BlockSpec((tm, LANES), lambda i: (i, 0)),
                   pl.BlockSpec((8, tm), lambda i: (0, i)),
                   pl.BlockSpec((8, LANES), lambda i: (0, 0))],
        out_shape=[jax.ShapeDtypeStruct((t, LANES), F32),
                   jax.ShapeDtypeStruct((8, t), F32),
                   jax.ShapeDtypeStruct((8, LANES), F32)],
        scratch_shapes=[pltpu.VMEM((8, LANES), F32)],
        compiler_params=_params("arbitrary"),
        name="route",
    )(logits, tri)


def _row_wait(src_hbm, dst_vmem, sem, rows):
    pltpu.make_async_copy(src_hbm.at[pl.ds(0, rows)], dst_vmem, sem).wait()


def _dispatch_kernel(sched_ref, dest_ref, h_ref, nw_ref, xg_ref, hn_ref, zero_ref, sem, zsem, *, rows):
    tm = h_ref.shape[0]

    @pl.when(pl.program_id(0) == 0)
    def _():
        zero_ref[...] = jnp.zeros_like(zero_ref)
        for e in range(N_EXPERTS):
            @pl.when(sched_ref[N_EXPERTS + e] > 0)
            def _():
                start = pl.multiple_of(sched_ref[e] - rows, rows)
                clear = pltpu.make_async_copy(zero_ref, xg_ref.at[pl.ds(start, rows)], zsem)
                clear.start()
                clear.wait()

        def clear_unused(b, carry):
            clear = pltpu.make_async_copy(zero_ref, xg_ref.at[pl.ds(pl.multiple_of(b * rows, rows), rows)], zsem)
            clear.start()
            clear.wait()
            return carry

        lax.fori_loop(sched_ref[2 * N_EXPERTS], xg_ref.shape[0] // rows, clear_unused, 0)

    step = pl.program_id(0)
    slot = step % 2
    rows_ref = hn_ref.at[slot]
    rows_ref[...] = _rms_rows(h_ref[...], nw_ref[...])

    def body(r, carry):
        for j in range(TOP_K):
            d = dest_ref[j * tm + r]
            pltpu.make_async_copy(rows_ref.at[pl.ds(r, 1)], xg_ref.at[pl.ds(d, 1)], sem.at[slot]).start()
        return carry

    lax.fori_loop(0, tm, body, 0, unroll=ROW_DMA_UNROLL)

    def drain(s):
        for _ in range(TOP_K):
            pltpu.make_async_copy(hn_ref.at[s], xg_ref.at[pl.ds(0, tm)], sem.at[s]).wait()

    @pl.when(step > 0)
    def _():
        drain(1 - slot)

    @pl.when(step == pl.num_programs(0) - 1)
    def _():
        drain(slot)


def _dispatch(sched, dest_flat, h, norm_w, n_rows, tm, rows):
    t = h.shape[0]
    grid_spec = pltpu.PrefetchScalarGridSpec(
        num_scalar_prefetch=1,
        grid=(t // tm,),
        in_specs=[
            pl.BlockSpec((TOP_K * tm,), lambda i, sc: (i,), memory_space=pltpu.SMEM),
            pl.BlockSpec((tm, D_MODEL), lambda i, sc: (i, 0)),
            pl.BlockSpec((1, D_MODEL), lambda i, sc: (0, 0)),
        ],
        out_specs=pl.BlockSpec(memory_space=pl.ANY),
        scratch_shapes=[pltpu.VMEM((2, tm, D_MODEL), F32), pltpu.VMEM((rows, D_MODEL), F32),
                        pltpu.SemaphoreType.DMA((2,)), pltpu.SemaphoreType.DMA(())],
    )
    return pl.pallas_call(
        functools.partial(_dispatch_kernel, rows=rows),
        grid_spec=grid_spec,
        out_shape=jax.ShapeDtypeStruct((n_rows, D_MODEL), F32),
        compiler_params=_params("arbitrary"),
        name="dispatch",
    )(sched, dest_flat, h, norm_w)


def _expert_kernel(be_ref, sched_ref, x_ref, wg_ref, wu_ref, wd_ref, y_ref, *, tf):
    del be_ref

    @pl.when(pl.program_id(0) < sched_ref[2 * N_EXPERTS])
    def _():
        y_ref[...] = _swiglu_rows(_bf(x_ref[...]), wg_ref, wu_ref, wd_ref, jnp.zeros(y_ref.shape, F32), tf)

    @pl.when(pl.program_id(0) >= sched_ref[2 * N_EXPERTS])
    def _():
        y_ref[...] = jnp.zeros_like(y_ref)


def _experts(block_e, sched, xg, wg, wu, wd, rows, tf):
    n_rows = xg.shape[0]
    used = lambda i, sc: jnp.minimum(i, sc[2 * N_EXPERTS] - 1)
    grid_spec = pltpu.PrefetchScalarGridSpec(
        num_scalar_prefetch=2,
        grid=(n_rows // rows,),
        in_specs=[
            pl.BlockSpec((rows, D_MODEL), lambda i, be, sc: (used(i, sc), 0)),
            pl.BlockSpec((1, D_MODEL, D_FF), lambda i, be, sc: (be[i], 0, 0)),
            pl.BlockSpec((1, D_MODEL, D_FF), lambda i, be, sc: (be[i], 0, 0)),
            pl.BlockSpec((1, D_FF, D_MODEL), lambda i, be, sc: (be[i], 0, 0)),
        ],
        out_specs=pl.BlockSpec((rows, D_MODEL), lambda i, be, sc: (i, 0)),
    )
    return pl.pallas_call(
        functools.partial(_expert_kernel, tf=tf),
        grid_spec=grid_spec,
        out_shape=jax.ShapeDtypeStruct((n_rows, D_MODEL), F32),
        compiler_params=_params("arbitrary"),
        name="experts",
    )(block_e, sched, xg, wg, wu, wd)


def _combine_kernel(dest_ref, next_dest_ref, meta_ref, h_ref, nw_ref, y_ref, o_ref, buf_ref, sem):
    tm = h_ref.shape[0]
    step = pl.program_id(0)
    slot = step % 2

    def fetch(idx_ref, s):
        def body(r, carry):
            for j in range(TOP_K):
                d = idx_ref[j * tm + r]
                pltpu.make_async_copy(y_ref.at[pl.ds(d, 1)], buf_ref.at[s, j, pl.ds(r, 1)], sem.at[s]).start()
            return carry

        lax.fori_loop(0, tm, body, 0, unroll=ROW_DMA_UNROLL)

    @pl.when(step == 0)
    def _():
        fetch(dest_ref, 0)

    @pl.when(step + 1 < pl.num_programs(0))
    def _():
        fetch(next_dest_ref, 1 - slot)

    for j in range(TOP_K):
        _row_wait(y_ref, buf_ref.at[slot, j], sem.at[slot], tm)
    meta = meta_ref[...]
    moe = meta[:, 4:5] * buf_ref[slot, 0] + meta[:, 5:6] * buf_ref[slot, 1]
    o_ref[...] = _rms_rows(h_ref[...] + moe, nw_ref[...])


def _combine(dest_flat, meta, h, norm_w, y, tm):
    t = h.shape[0]
    last = t // tm - 1
    return pl.pallas_call(
        _combine_kernel,
        grid=(t // tm,),
        in_specs=[
            pl.BlockSpec((TOP_K * tm,), lambda i: (i,), memory_space=pltpu.SMEM),
            pl.BlockSpec((TOP_K * tm,), lambda i: (jnp.minimum(i + 1, last),), memory_space=pltpu.SMEM),
            pl.BlockSpec((tm, LANES), lambda i: (i, 0)),
            pl.BlockSpec((tm, D_MODEL), lambda i: (i, 0)),
            pl.BlockSpec((1, D_MODEL), lambda i: (0, 0)),
            pl.BlockSpec(memory_space=pl.ANY),
        ],
        out_specs=pl.BlockSpec((tm, D_MODEL), lambda i: (i, 0)),
        out_shape=jax.ShapeDtypeStruct((t, D_MODEL), F32),
        scratch_shapes=[pltpu.VMEM((2, TOP_K, tm, D_MODEL), F32), pltpu.SemaphoreType.DMA((2,))],
        compiler_params=_params("arbitrary"),
        name="combine",
    )(dest_flat, dest_flat, meta, h, norm_w, y)


def _moe_layer(logits, h, ffn_norm_w, final_norm_w, wg, wu, wd, rows, tm_route, tm_rows, tf):
    t = h.shape[0]
    meta, meta_t, counts = _route(logits, tm_route)
    counts = counts[0, :N_EXPERTS].astype(jnp.int32)
    padded = ((counts + rows - 1) // rows) * rows
    pend = jnp.cumsum(padded)
    pstart = pend - padded
    n_rows = t * TOP_K + N_EXPERTS * rows
    expert = meta_t[0:TOP_K].astype(jnp.int32)
    rank = meta_t[TOP_K:2 * TOP_K].astype(jnp.int32)
    dest = pstart[expert] + rank
    dest = dest.reshape(TOP_K, t // tm_rows, tm_rows).transpose(1, 0, 2).reshape(t * TOP_K)
    n_used = pend[N_EXPERTS - 1] // rows
    block_start = jnp.minimum(jnp.arange(n_rows // rows, dtype=jnp.int32), n_used - 1) * rows
    block_e = jnp.sum(block_start[:, None] >= pend[None, :], axis=1).astype(jnp.int32)
    sched = jnp.concatenate([pend, padded, n_used[None]]).astype(jnp.int32)
    xg = _dispatch(sched, dest, h, ffn_norm_w, n_rows, tm_rows, rows)
    y = _experts(block_e, sched, xg, wg, wu, wd, rows, tf)
    return _combine(dest, meta, h, final_norm_w, y, tm_rows)


def _pack_in_weights(w_in):
    o = np.cumsum((0, HEADS_W, HEADS_W, HEADS_W, HEADS_W, CONV_CH, HEADS_W, GDN_HEADS, GDN_HEADS, D_MODEL, D_MODEL))
    rq, rk, rv, rg, gqkv, gz, ga, gb, ma, mb = (w_in[:, o[i]:o[i + 1]] for i in range(10))
    main = _bf(jnp.concatenate([gqkv, ma, mb, rq, rk, rv, rg, gz], axis=1))
    small = jnp.concatenate([ga, gb], axis=1)
    small_cols = _bf(jnp.pad(small, ((0, 0), (0, LANES - 2 * GDN_HEADS))))
    small_rows = _bf(jnp.pad(small.T, ((0, SMALL_ROWS - 2 * GDN_HEADS), (0, 0))))
    return main, small_cols, small_rows


def _pick(n, prefs):
    for p in prefs:
        if n % p == 0:
            return p
    raise ValueError(f"no tile in {prefs} divides {n}")


def kernel(x, norm_mix, w_in, conv_w, a_log, dt_bias, gdn_norm, w_branch, w_out, norm_ffn,
           dense_w_gate, dense_w_up, dense_w_down, router, moe_w_gate, moe_w_up, moe_w_down, final_norm):
    batch, seq, d = x.shape
    depth = norm_mix.shape[0]
    assert d == D_MODEL and seq % GDN_CHUNK == 0
    assert depth % 2 == 0, "the final RMSNorm is fused into the last (routed) layer's combine"
    t = batch * seq
    blk = _pick(seq, (128, 64))
    nb = _pick(batch, (4, 2, 1))
    tm = _pick(seq, (1024, 512, 256))
    tm_rows = _pick(t, (512, 256))
    tf_dense = 256
    rows = 512
    tabs = _mixer_tables(seq, blk)

    h = x.reshape(t, d)
    out = None
    for layer in range(depth):
        w_main, w_small, w_small_t = _pack_in_weights(w_in[layer])
        convw = jnp.pad(conv_w[layer], ((0, 8 - CONV_K), (0, 0)))
        conv, proj, gab, gabt = _in_projection(h, norm_mix[layer][None, :], w_main, w_small, w_small_t, convw, tm, CONV_CH, seq)
        prow = jnp.zeros((8, LANES), F32)
        prow = prow.at[0, :GDN_HEADS].set(a_log[layer]).at[1, :GDN_HEADS].set(dt_bias[layer])
        pcol = jnp.zeros((SMALL_ROWS, LANES), F32)
        pcol = pcol.at[:GDN_HEADS, 0].set(a_log[layer]).at[:GDN_HEADS, 1].set(dt_bias[layer])
        ret, gdn = _token_mixers(conv, proj, gab, gabt, tabs, prow, pcol, gdn_norm[layer][None, :],
                                 batch, seq, blk, nb)
        i = layer // 2
        is_moe = layer % 2 == 1
        wa, wb, wo = _bf(w_branch[layer, 0]), _bf(w_branch[layer, 1]), _bf(w_out[layer])
        if not is_moe:
            h = _merge_dense_ffn(ret, gdn, proj, h, wa, wb, wo, norm_ffn[layer][None, :], _bf(dense_w_gate[i])[None],
                                 _bf(dense_w_up[i])[None], _bf(dense_w_down[i])[None], tm_rows, tf_dense)
        else:
            router_w = _bf(jnp.pad(router[i], ((0, 0), (0, LANES - N_EXPERTS))))
            h, logits = _merge_router(ret, gdn, proj, h, wa, wb, wo, norm_ffn[layer][None, :], router_w, tm_rows)
            out = _moe_layer(logits, h, norm_ffn[layer][None, :], final_norm[None, :],
                             _bf(moe_w_gate[i]), _bf(moe_w_up[i]), _bf(moe_w_down[i]),
                             rows, tm_rows, tm_rows, 256)
    return out.reshape(batch, seq, d)
```

```python
import functools
import math

import jax
import jax.numpy as jnp
import numpy as np
from jax import lax
from jax.experimental import pallas as pl
from jax.experimental.pallas import tpu as pltpu

F32 = jnp.float32
BF16 = jnp.bfloat16

D_MODEL = 1024
RET_HEADS = 4
GDN_HEADS = 4
HEAD_DIM = 128
HEADS_W = 512
CONV_K = 4
CONV_CH = 3 * HEADS_W
GDN_CHUNK = 64
ROPE_BASE = 10000.0
D_FF = 2816
N_EXPERTS = 8
TOP_K = 2
EPS = 1e-6
LANES = 128
SMALL_ROWS = 16
ROW_DMA_UNROLL = 8
CONV_ROWS = 256
MIXER_GROUP = 16

MAIN_COLS = CONV_CH + 5 * HEADS_W + 2 * D_MODEL
VMEM_LIMIT = 56 * 1024 * 1024

HIGHEST = lax.Precision.HIGHEST


def _bf(x):
    return x.astype(BF16)


def _dot(a, b):
    return jnp.dot(a, b, preferred_element_type=F32)


def _dot_nt(a, b):
    return lax.dot_general(a, b, (((1,), (1,)), ((), ())), preferred_element_type=F32)


def _dot_tn(a, b):
    return lax.dot_general(a, b, (((0,), (0,)), ((), ())), preferred_element_type=F32)


def _dot_f32(a, b):
    return jnp.dot(a, b, preferred_element_type=F32, precision=HIGHEST)


def _sigmoid(x):
    return 1.0 / (1.0 + jnp.exp(-x))


def _silu(x):
    return x * _sigmoid(x)


def _softplus(x):
    return jnp.maximum(x, 0.0) + jnp.log(1.0 + jnp.exp(-jnp.abs(x)))


def _rms_rows(x, w):
    ms = jnp.mean(x * x, axis=-1, keepdims=True)
    return x * lax.rsqrt(ms + EPS) * w


def _params(*sem):
    return pltpu.CompilerParams(dimension_semantics=sem, vmem_limit_bytes=VMEM_LIMIT)


def _inproj_kernel(x_ref, nw_ref, w_ref, ws_ref, wst_ref, cw_ref, oc_ref, o_ref, os_ref, ost_ref,
                   hn_ref, xs_ref, *, tiles_per_seq, n_col):
    tm = x_ref.shape[0]
    tile, step = pl.program_id(0), pl.program_id(1)

    @pl.when(step == 0)
    def _():
        hn = _bf(_rms_rows(x_ref[...], nw_ref[...]))
        hn_ref[...] = hn
        os_ref[...] = _dot(hn, ws_ref[...])
        ost_ref[...] = _dot_nt(wst_ref[...], hn)

        @pl.when(tile % tiles_per_seq == 0)
        def _():
            xs_ref[0:8, :] = jnp.zeros((8, CONV_CH), F32)

        xs_ref[8:8 + tm, :] = _dot(hn, w_ref[...])

    def conv_piece(r0, c0):
        cols = slice(c0, c0 + LANES)
        acc = cw_ref[CONV_K - 1:CONV_K, cols] * xs_ref[pl.ds(r0 + 8, CONV_ROWS), cols]
        for d in range(1, CONV_K):
            acc = acc + cw_ref[CONV_K - 1 - d:CONV_K - d, cols] * xs_ref[pl.ds(r0 + 8 - d, CONV_ROWS), cols]
        oc_ref[r0:r0 + CONV_ROWS, cols] = _bf(_silu(acc))

    conv_pieces = [(r0, c0) for r0 in range(0, tm, CONV_ROWS) for c0 in range(0, CONV_CH, LANES)]
    dot_pieces = [(r0, c0) for r0 in range(0, tm, CONV_ROWS) for c0 in range(0, o_ref.shape[1], HEADS_W)]
    share = -(-len(conv_pieces) // (n_col - 1))
    for k in range(1, n_col):
        @pl.when(step == k)
        def _(k=k):
            mine = conv_pieces[(k - 1) * share:k * share]
            per_dot = -(-len(mine) // len(dot_pieces))
            for j, (r0, c0) in enumerate(dot_pieces):
                rows, cols = slice(r0, r0 + CONV_ROWS), slice(c0, c0 + HEADS_W)
                o_ref[rows, cols] = _bf(_dot(hn_ref[rows, :], w_ref[:, cols]))
                for piece in mine[j * per_dot:(j + 1) * per_dot]:
                    conv_piece(*piece)
            if k == n_col - 1:
                xs_ref[0:8, :] = xs_ref[tm:tm + 8, :]


def _in_projection(h, norm_w, w_main, w_small, w_small_t, conv_w, tm, tn, seq):
    t = h.shape[0]
    assert tn == CONV_CH and seq % tm == 0 and tm % CONV_ROWS == 0
    n_col = MAIN_COLS // tn
    return pl.pallas_call(
        functools.partial(_inproj_kernel, tiles_per_seq=seq // tm, n_col=n_col),
        grid=(t // tm, n_col),
        in_specs=[
            pl.BlockSpec((tm, D_MODEL), lambda i, j: (i, 0)),
            pl.BlockSpec((1, D_MODEL), lambda i, j: (0, 0)),
            pl.BlockSpec((D_MODEL, tn), lambda i, j: (0, j)),
            pl.BlockSpec((D_MODEL, LANES), lambda i, j: (0, 0)),
            pl.BlockSpec((SMALL_ROWS, D_MODEL), lambda i, j: (0, 0)),
            pl.BlockSpec((8, CONV_CH), lambda i, j: (0, 0)),
        ],
        out_specs=[
            pl.BlockSpec((tm, CONV_CH), lambda i, j: (i, 0)),
            pl.BlockSpec((tm, tn), lambda i, j: (i, jnp.maximum(j - 1, 0))),
            pl.BlockSpec((tm, LANES), lambda i, j: (i, 0)),
            pl.BlockSpec((SMALL_ROWS, tm), lambda i, j: (0, i)),
        ],
        out_shape=[
            jax.ShapeDtypeStruct((t, CONV_CH), BF16),
            jax.ShapeDtypeStruct((t, MAIN_COLS - CONV_CH), BF16),
            jax.ShapeDtypeStruct((t, LANES), F32),
            jax.ShapeDtypeStruct((SMALL_ROWS, t), F32),
        ],
        scratch_shapes=[pltpu.VMEM((tm, D_MODEL), BF16), pltpu.VMEM((tm + 8, CONV_CH), F32)],
        compiler_params=_params("arbitrary", "arbitrary"),
        name="in_projection",
    )(h, norm_w, w_main, w_small, w_small_t, conv_w)


def _hi_lo(x):
    hi = _bf(x)
    lo_f = x - hi.astype(F32)
    return hi, _bf(lo_f), lo_f


def _dup_lhs(x2, lo_f, left):
    w = _bf(jnp.where(left, x2, lo_f))
    return jnp.concatenate([w, w], axis=1)


def _dup_rhs(hi, lo):
    return jnp.concatenate([hi, hi, lo, lo], axis=0)


def _mixer_kernel(*refs, blk, nb):
    gqkv_ref, rq_ref, rk_ref, rv_ref, rg_ref, gz_ref, gab_ref = refs[:7]
    gabt_refs = refs[7:7 + nb]
    (cos_ref, sin_ref, dmat_ref, qdec_ref, kdec_ref, cdec_ref, prow_ref, pcol_ref, gnorm_ref, tril_ref, triu_ref,
     ret_ref, gdn_ref, rstate_ref, gstate_ref) = refs[7 + nb:]
    c = GDN_CHUNK
    n_chunks = blk // c
    seqs = range(nb)
    chains = [(s, h) for s in seqs for h in range(GDN_HEADS)]
    inst = [(s, h, n) for s, h in chains for n in range(n_chunks)]
    slot = {ch: ch[0] * GDN_HEADS + ch[1] for ch in chains}

    @pl.when(pl.program_id(1) == 0)
    def _():
        rstate_ref[...] = jnp.zeros_like(rstate_ref)
        gstate_ref[...] = jnp.zeros_like(gstate_ref)

    hsl = [slice(h * HEAD_DIM, (h + 1) * HEAD_DIM) for h in range(RET_HEADS)]

    cos2 = cos_ref[...]
    sin2 = sin_ref[...]
    half = HEAD_DIM // 2
    rq = {(s, h): rq_ref[s, :, hsl[h]].astype(F32) for s, h in chains}
    rk = {(s, h): rk_ref[s, :, hsl[h]].astype(F32) for s, h in chains}
    rv = {(s, h): rv_ref[s, :, hsl[h]] for s, h in chains}
    rq = {ch: q * cos2 + pltpu.roll(q, half, 1) * sin2 for ch, q in rq.items()}
    rk = {ch: (k * cos2 + pltpu.roll(k, half, 1) * sin2) * (HEAD_DIM ** -0.5) for ch, k in rk.items()}
    rstate = {ch: rstate_ref[slot[ch]] for ch in chains}
    scores = {ch: _dot_nt(_bf(rq[ch]), _bf(rk[ch])) * dmat_ref[ch[1]] for ch in chains}
    inter = {ch: _dot(_bf(rq[ch] * qdec_ref[ch[1]]), _bf(rstate[ch])) for ch in chains}
    kv = {ch: _dot_tn(_bf(rk[ch] * kdec_ref[ch[1]]), rv[ch]) for ch in chains}
    ro = {ch: _dot(_bf(scores[ch]), rv[ch]) + inter[ch] for ch in chains}
    for ch in chains:
        s, h = ch
        rstate_ref[slot[ch]] = rstate[ch] * cdec_ref[h] + kv[ch]
        o = ro[ch]
        o = o * lax.rsqrt(jnp.mean(o * o, axis=-1, keepdims=True) + EPS)
        ret_ref[s, :, hsl[h]] = _bf(_silu(rg_ref[s, :, hsl[h]].astype(F32)) * o)

    neg_a_row = -jnp.exp(prow_ref[0:1, :])
    neg_a_col = -jnp.exp(pcol_ref[:, 0:1])
    beta_cols, gc_cols, egc_cols, gc_rows = [], [], [], []
    for s in seqs:
        gab = gab_ref[s]
        beta_cols.append(_sigmoid(gab))
        gc_cols.append(_dot_f32(tril_ref[...], neg_a_row * _softplus(gab + prow_ref[1:2, :])))
        egc_cols.append(jnp.exp(gc_cols[s]))
        gc_rows.append(_dot_f32(neg_a_col * _softplus(gabt_refs[s][...] + pcol_ref[:, 1:2]), triu_ref[...]))

    ri = lax.broadcasted_iota(jnp.int32, (c, 2 * c), 0)
    lane = lax.broadcasted_iota(jnp.int32, (c, 2 * c), 1)
    ci = lane & (c - 1)
    left = lane < c
    ge = ri >= ci
    gt = ri > ci
    eye = jnp.where(ri == ci, 1.0, 0.0).astype(F32)
    level_masks = []
    for lg in range(int(math.log2(c))):
        b = 1 << lg
        same_block = (ri >> (lg + 1)) == (ci >> (lg + 1))
        level_masks.append(same_block & ((ri & (2 * b - 1)) >= b) & ((ci & (2 * b - 1)) < b))

    qh, kh, vh = {}, {}, {}
    for s, h in chains:
        q = gqkv_ref[s, :, hsl[h]].astype(F32)
        k = gqkv_ref[s, :, HEADS_W + h * HEAD_DIM:HEADS_W + (h + 1) * HEAD_DIM].astype(F32)
        vh[s, h] = gqkv_ref[s, :, 2 * HEADS_W + h * HEAD_DIM:2 * HEADS_W + (h + 1) * HEAD_DIM].astype(F32)
        qh[s, h] = q * lax.rsqrt(jnp.sum(q * q, axis=-1, keepdims=True) + EPS) * (HEAD_DIM ** -0.5)
        kh[s, h] = k * lax.rsqrt(jnp.sum(k * k, axis=-1, keepdims=True) + EPS)

    rows = [slice(n * c, (n + 1) * c) for n in range(n_chunks)]
    uw, wq, kd, eg, attn = {}, {}, {}, {}, {}
    for g0 in range(0, len(inst), MIXER_GROUP):
        grp = inst[g0:g0 + MIXER_GROUP]
        kc = {(s, h, n): kh[s, h][rows[n]] for s, h, n in grp}
        qc = {(s, h, n): qh[s, h][rows[n]] for s, h, n in grp}
        bcol = {(s, h, n): beta_cols[s][rows[n], 4 + h:5 + h] for s, h, n in grp}
        gcol = {(s, h, n): gc_cols[s][rows[n], h:h + 1] for s, h, n in grp}
        egcol = {(s, h, n): egc_cols[s][rows[n], h:h + 1] for s, h, n in grp}
        glast = {(s, h, n): gc_cols[s][n * c + c - 1:(n + 1) * c, h:h + 1] for s, h, n in grp}
        grow = {(s, h, n): gc_rows[s][h:h + 1, 2 * c * n:2 * c * (n + 1)] for s, h, n in grp}
        kb = {i: kc[i] * bcol[i] for i in grp}
        kcb = {i: _bf(kc[i]) for i in grp}
        kcb2 = {i: jnp.concatenate([kcb[i], kcb[i]], axis=0) for i in grp}
        decay = {i: jnp.where(ge, jnp.exp(jnp.where(ge, gcol[i] - grow[i], 0.0)), 0.0) for i in grp}
        a2 = {i: jnp.where(gt, _dot_nt(_bf(kb[i]), kcb2[i]) * decay[i], 0.0) for i in grp}
        for i in grp:
            attn[i] = _bf(jnp.where(ge, _dot_nt(_bf(qc[i]), kcb2[i]) * decay[i], 0.0)[:, :c])

        x2 = {i: eye - jnp.where(level_masks[0], a2[i], 0.0) for i in grp}
        a_hl = {i: _hi_lo(a2[i]) for i in grp}
        for m in level_masks[1:]:
            mb = jnp.where(m, 1.0, 0.0).astype(BF16)
            x_hl = {i: _hi_lo(x2[i]) for i in grp}
            t = {i: _dot(_dup_lhs(x2[i], x_hl[i][2], left), _dup_rhs(a_hl[i][0] * mb, a_hl[i][1] * mb)) for i in grp}
            t_hl = {i: _hi_lo(t[i]) for i in grp}
            x2 = {i: x2[i] - _dot(_dup_lhs(t[i], t_hl[i][2], left), _dup_rhs(x_hl[i][0], x_hl[i][1])) for i in grp}

        for i in grp:
            s, h, n = i
            rhs = _hi_lo(jnp.concatenate([vh[s, h][rows[n]] * bcol[i], kb[i] * egcol[i]], axis=1))
            uw[i] = _dot(_dup_lhs(x2[i], _hi_lo(x2[i])[2], left), _dup_rhs(rhs[0], rhs[1]))
        for i in grp:
            wq[i] = _bf(jnp.concatenate([uw[i][:, HEAD_DIM:], qc[i] * egcol[i]], axis=0))
            kd[i] = _bf(kc[i] * jnp.exp(glast[i] - gcol[i]))
            eg[i] = jnp.exp(glast[i])

    state = {ch: gstate_ref[slot[ch]] for ch in chains}
    outs = {ch: [] for ch in chains}
    for n in range(n_chunks):
        sb = {ch: _bf(state[ch]) for ch in chains}
        ws = {ch: _dot(wq[ch + (n,)], sb[ch]) for ch in chains}
        vnb = {ch: _bf(uw[ch + (n,)][:, :HEAD_DIM] - ws[ch][:c]) for ch in chains}
        for ch in chains:
            outs[ch].append(ws[ch][c:] + _dot(attn[ch + (n,)], vnb[ch]))
        state = {ch: state[ch] * eg[ch + (n,)] + _dot_tn(kd[ch + (n,)], vnb[ch]) for ch in chains}

    gnorm = gnorm_ref[...]
    for ch in chains:
        s, h = ch
        gstate_ref[slot[ch]] = state[ch]
        o = jnp.concatenate(outs[ch], axis=0)
        o = o * lax.rsqrt(jnp.mean(o * o, axis=-1, keepdims=True) + EPS)
        gdn_ref[s, :, hsl[h]] = _bf(o * gnorm * _silu(gz_ref[s, :, hsl[h]].astype(F32)))


def _token_mixers(conv, proj, gab, gabt, tabs, prow, pcol, gnorm, batch, seq, blk, nb):
    conv3 = conv.reshape(batch, seq, CONV_CH)
    proj3 = proj.reshape(batch, seq, MAIN_COLS - CONV_CH)
    gab3 = gab.reshape(batch, seq, LANES)
    n_blk = seq // blk
    cos2, sin2, dmat, qdec, kdec, cdec, tril, triu = tabs
    w512 = lambda col: pl.BlockSpec((nb, blk, HEADS_W), lambda b, n, col=col: (b, n, col))
    full = lambda shape: pl.BlockSpec(shape, lambda b, n: (0,) * len(shape))
    gabt_specs = [pl.BlockSpec((SMALL_ROWS, blk), lambda b, n, s=s: (0, (b * nb + s) * n_blk + n)) for s in range(nb)]
    out = pl.pallas_call(
        functools.partial(_mixer_kernel, blk=blk, nb=nb),
        grid=(batch // nb, n_blk),
        in_specs=[
            pl.BlockSpec((nb, blk, CONV_CH), lambda b, n: (b, n, 0)),
            w512(4), w512(5), w512(6), w512(7), w512(8),
            pl.BlockSpec((nb, blk, LANES), lambda b, n: (b, n, 0)),
            *gabt_specs,
            pl.BlockSpec((blk, HEAD_DIM), lambda b, n: (n, 0)),
            pl.BlockSpec((blk, HEAD_DIM), lambda b, n: (n, 0)),
            full((RET_HEADS, blk, blk)),
            full((RET_HEADS, blk, HEAD_DIM)),
            full((RET_HEADS, blk, HEAD_DIM)),
            full((RET_HEADS, 1, HEAD_DIM)),
            full((8, LANES)),
            full((SMALL_ROWS, LANES)),
            full((1, HEAD_DIM)),
            full((blk, blk)),
            full((blk, 2 * blk)),
        ],
        out_specs=[
            pl.BlockSpec((nb, blk, HEADS_W), lambda b, n: (b, n, 0)),
            pl.BlockSpec((nb, blk, HEADS_W), lambda b, n: (b, n, 0)),
        ],
        out_shape=[
            jax.ShapeDtypeStruct((batch, seq, HEADS_W), BF16),
            jax.ShapeDtypeStruct((batch, seq, HEADS_W), BF16),
        ],
        scratch_shapes=[
            pltpu.VMEM((nb * RET_HEADS, HEAD_DIM, HEAD_DIM), F32),
            pltpu.VMEM((nb * GDN_HEADS, HEAD_DIM, HEAD_DIM), F32),
        ],
        compiler_params=_params("parallel", "arbitrary"),
        name="token_mixers",
    )(conv3, proj3, proj3, proj3, proj3, proj3, gab3, *([gabt] * nb),
      cos2, sin2, dmat, qdec, kdec, cdec, prow, pcol, gnorm, tril, triu)
    ret, gdn = out
    return ret.reshape(batch * seq, HEADS_W), gdn.reshape(batch * seq, HEADS_W)


def _mixer_tables(seq, blk):
    inv_freq = ROPE_BASE ** (-jnp.arange(0, HEAD_DIM, 2, dtype=F32) / HEAD_DIM)
    ang = jnp.arange(seq, dtype=F32)[:, None] * inv_freq[None, :]
    cos, sin = jnp.cos(ang), jnp.sin(ang)
    cos2 = jnp.concatenate([cos, cos], axis=-1)
    sin2 = jnp.concatenate([-sin, sin], axis=-1)
    log_gamma = jnp.log(1.0 - jnp.exp2(-5.0 - jnp.arange(RET_HEADS, dtype=F32)))
    pos = jnp.arange(blk, dtype=F32)
    diff = pos[:, None] - pos[None, :]
    dmat = jnp.where(diff >= 0, jnp.exp(jnp.maximum(diff, 0.0)[None] * log_gamma[:, None, None]), 0.0)
    ones = jnp.ones((1, 1, HEAD_DIM), F32)
    qdec = jnp.exp((pos + 1.0)[None, :] * log_gamma[:, None])[:, :, None] * ones
    kdec = jnp.exp((blk - 1 - pos)[None, :] * log_gamma[:, None])[:, :, None] * ones
    cdec = jnp.exp(blk * log_gamma)[:, None, None] * ones
    idx = np.arange(blk)
    same = (idx[:, None] // GDN_CHUNK) == (idx[None, :] // GDN_CHUNK)
    tril = jnp.asarray((same & (idx[:, None] >= idx[None, :])).astype(np.float32))
    col = np.arange(2 * blk)
    col_chunk, col_pos = col // (2 * GDN_CHUNK), col % GDN_CHUNK
    triu = jnp.asarray(((idx[:, None] // GDN_CHUNK == col_chunk[None, :])
                        & (idx[:, None] % GDN_CHUNK <= col_pos[None, :])).astype(np.float32))
    return cos2, sin2, dmat, qdec, kdec, cdec, tril, triu


def _swiglu_rows(x, wg_ref, wu_ref, wd_ref, acc, tf):
    for f in range(D_FF // tf):
        cols = slice(f * tf, (f + 1) * tf)
        act = _silu(_dot(x, wg_ref[0, :, cols])) * _dot(x, wu_ref[0, :, cols])
        acc = acc + _dot(_bf(act), wd_ref[0, cols, :])
    return acc


def _merged_rows(ret_ref, gdn_ref, ga_ref, gb_ref, h_ref, wa_ref, wb_ref, wo_ref, nw_ref):
    ya = _dot(ret_ref[...], wa_ref[...])
    yb = _dot(gdn_ref[...], wb_ref[...])
    merged = _sigmoid(ga_ref[...].astype(F32)) * ya + _sigmoid(gb_ref[...].astype(F32)) * yb
    h_new = h_ref[...] + _dot(_bf(merged), wo_ref[...])
    return h_new, _bf(_rms_rows(h_new, nw_ref[...]))


def _merge_ffn_kernel(*refs, tf):
    wg_ref, wu_ref, wd_ref, o_ref = refs[9:]
    h_new, hn = _merged_rows(*refs[:9])
    o_ref[...] = _swiglu_rows(hn, wg_ref, wu_ref, wd_ref, h_new, tf)


def _merge_router_kernel(*refs):
    wr_ref, tri_ref, ho_ref, meta_ref, meta_t_ref, count_ref, run_ref = refs[9:]

    @pl.when(pl.program_id(0) == 0)
    def _():
        run_ref[...] = jnp.zeros_like(run_ref)

    h_new, hn = _merged_rows(*refs[:9])
    ho_ref[...] = h_new
    meta, run_new = _route_rows(_dot(hn, wr_ref[...]), tri_ref, run_ref[0:1, :])
    run_ref[0:1, :] = run_new
    count_ref[...] = jnp.broadcast_to(run_new, count_ref.shape)
    meta_ref[...] = meta
    meta_t_ref[...] = meta.T[:8, :]


def _resident(shape):
    return pl.BlockSpec(shape, lambda i: (0,) * len(shape), pipeline_mode=pl.Buffered(1))


def _merge_specs(tm):
    rows = lambda width, col: pl.BlockSpec((tm, width), lambda i: (i, col))
    return [rows(HEADS_W, 0), rows(HEADS_W, 0), rows(D_MODEL, 0), rows(D_MODEL, 1), rows(D_MODEL, 0),
            _resident((HEADS_W, D_MODEL)), _resident((HEADS_W, D_MODEL)), _resident((D_MODEL, D_MODEL)),
            _resident((1, D_MODEL))]


def _merge_dense_ffn(ret, gdn, proj, h, wa, wb, wo, norm_w, wg, wu, wd, tm, tf):
    t = h.shape[0]
    return pl.pallas_call(
        functools.partial(_merge_ffn_kernel, tf=tf),
        grid=(t // tm,),
        in_specs=_merge_specs(tm) + [_resident((1, D_MODEL, D_FF)), _resident((1, D_MODEL, D_FF)),
                                     _resident((1, D_FF, D_MODEL))],
        out_specs=pl.BlockSpec((tm, D_MODEL), lambda i: (i, 0)),
        out_shape=jax.ShapeDtypeStruct((t, D_MODEL), F32),
        compiler_params=_params("parallel"),
        name="merge_dense_ffn",
    )(ret, gdn, proj, proj, h, wa, wb, wo, norm_w, wg, wu, wd)


def _merge_router(ret, gdn, proj, h, wa, wb, wo, norm_w, router_w, tm):
    t = h.shape[0]
    idx = np.arange(tm)
    tri = jnp.asarray((idx[:, None] > idx[None, :]).astype(np.float32), dtype=BF16)
    return pl.pallas_call(
        _merge_router_kernel,
        grid=(t // tm,),
        in_specs=_merge_specs(tm) + [_resident((D_MODEL, LANES)), _resident((tm, tm))],
        out_specs=[pl.BlockSpec((tm, D_MODEL), lambda i: (i, 0)),
                   pl.BlockSpec((tm, LANES), lambda i: (i, 0)),
                   pl.BlockSpec((8, tm), lambda i: (0, i)),
                   pl.BlockSpec((8, LANES), lambda i: (0, 0))],
        out_shape=[jax.ShapeDtypeStruct((t, D_MODEL), F32),
                   jax.ShapeDtypeStruct((t, LANES), F32),
                   jax.ShapeDtypeStruct((8, t), F32),
                   jax.ShapeDtypeStruct((8, LANES), F32)],
        scratch_shapes=[pltpu.VMEM((8, LANES), F32)],
        compiler_params=_params("arbitrary"),
        name="merge_router",
    )(ret, gdn, proj, proj, h, wa, wb, wo, norm_w, router_w, tri)


def _route_rows(logits, tri_ref, before_tile):
    tm = logits.shape[0]
    lane = lax.broadcasted_iota(jnp.int32, (tm, LANES), 1)
    neg = jnp.float32(-jnp.inf)
    l1 = jnp.where(lane < N_EXPERTS, logits, neg)
    m1 = jnp.max(l1, axis=-1, keepdims=True)
    i1 = jnp.min(jnp.where(l1 == m1, lane, LANES), axis=-1, keepdims=True)
    l2 = jnp.where(lane == i1, neg, l1)
    m2 = jnp.max(l2, axis=-1, keepdims=True)
    i2 = jnp.min(jnp.where(l2 == m2, lane, LANES), axis=-1, keepdims=True)
    e2 = jnp.exp(m2 - m1)
    g1 = 1.0 / (1.0 + e2)
    g2 = e2 / (1.0 + e2)
    sel1 = lane == i1
    sel2 = lane == i2
    onehot = jnp.where(sel1 | sel2, 1.0, 0.0).astype(F32)
    before = _dot(tri_ref[...], _bf(onehot)) + before_tile
    r1 = jnp.sum(jnp.where(sel1, before, 0.0), axis=-1, keepdims=True)
    r2 = jnp.sum(jnp.where(sel2, before, 0.0), axis=-1, keepdims=True)
    meta = jnp.where(lane == 0, i1.astype(F32), 0.0)
    meta = jnp.where(lane == 1, i2.astype(F32), meta)
    meta = jnp.where(lane == 2, r1, meta)
    meta = jnp.where(lane == 3, r2, meta)
    meta = jnp.where(lane == 4, g1, meta)
    meta = jnp.where(lane == 5, g2, meta)
    return meta, before_tile + jnp.sum(onehot, axis=0, keepdims=True)


def _row_wait(src_hbm, dst_vmem, sem, rows):
    pltpu.make_async_copy(src_hbm.at[pl.ds(0, rows)], dst_vmem, sem).wait()


def _dispatch_kernel(sched_ref, dest_ref, h_ref, nw_ref, xg_ref, hn_ref, zero_ref, sem, zsem, *, rows):
    tm = h_ref.shape[0]

    @pl.when(pl.program_id(0) == 0)
    def _():
        zero_ref[...] = jnp.zeros_like(zero_ref)
        for e in range(N_EXPERTS):
            @pl.when(sched_ref[N_EXPERTS + e] > 0)
            def _():
                start = pl.multiple_of(sched_ref[e] - rows, rows)
                clear = pltpu.make_async_copy(zero_ref, xg_ref.at[pl.ds(start, rows)], zsem)
                clear.start()
                clear.wait()

        def clear_unused(b, carry):
            clear = pltpu.make_async_copy(zero_ref, xg_ref.at[pl.ds(pl.multiple_of(b * rows, rows), rows)], zsem)
            clear.start()
            clear.wait()
            return carry

        lax.fori_loop(sched_ref[2 * N_EXPERTS], xg_ref.shape[0] // rows, clear_unused, 0)

    step = pl.program_id(0)
    slot = step % 2
    rows_ref = hn_ref.at[slot]
    rows_ref[...] = _rms_rows(h_ref[...], nw_ref[...])

    def body(r, carry):
        for j in range(TOP_K):
            d = dest_ref[j * tm + r]
            pltpu.make_async_copy(rows_ref.at[pl.ds(r, 1)], xg_ref.at[pl.ds(d, 1)], sem.at[slot]).start()
        return carry

    lax.fori_loop(0, tm, body, 0, unroll=ROW_DMA_UNROLL)

    def drain(s):
        for _ in range(TOP_K):
            pltpu.make_async_copy(hn_ref.at[s], xg_ref.at[pl.ds(0, tm)], sem.at[s]).wait()

    @pl.when(step > 0)
    def _():
        drain(1 - slot)

    @pl.when(step == pl.num_programs(0) - 1)
    def _():
        drain(slot)


def _dispatch(sched, dest_flat, h, norm_w, n_rows, tm, rows):
    t = h.shape[0]
    grid_spec = pltpu.PrefetchScalarGridSpec(
        num_scalar_prefetch=1,
        grid=(t // tm,),
        in_specs=[
            pl.BlockSpec((TOP_K * tm,), lambda i, sc: (i,), memory_space=pltpu.SMEM),
            pl.BlockSpec((tm, D_MODEL), lambda i, sc: (i, 0)),
            pl.BlockSpec((1, D_MODEL), lambda i, sc: (0, 0)),
        ],
        out_specs=pl.BlockSpec(memory_space=pl.ANY),
        scratch_shapes=[pltpu.VMEM((2, tm, D_MODEL), F32), pltpu.VMEM((rows, D_MODEL), F32),
                        pltpu.SemaphoreType.DMA((2,)), pltpu.SemaphoreType.DMA(())],
    )
    return pl.pallas_call(
        functools.partial(_dispatch_kernel, rows=rows),
        grid_spec=grid_spec,
        out_shape=jax.ShapeDtypeStruct((n_rows, D_MODEL), F32),
        compiler_params=_params("arbitrary"),
        name="dispatch",
    )(sched, dest_flat, h, norm_w)


def _expert_kernel(be_ref, sched_ref, x_ref, wg_ref, wu_ref, wd_ref, y_ref, *, tf):
    del be_ref

    @pl.when(pl.program_id(0) < sched_ref[2 * N_EXPERTS])
    def _():
        y_ref[...] = _swiglu_rows(_bf(x_ref[...]), wg_ref, wu_ref, wd_ref, jnp.zeros(y_ref.shape, F32), tf)

    @pl.when(pl.program_id(0) >= sched_ref[2 * N_EXPERTS])
    def _():
        y_ref[...] = jnp.zeros_like(y_ref)


def _experts(block_e, sched, xg, wg, wu, wd, rows, tf):
    n_rows = xg.shape[0]
    used = lambda i, sc: jnp.minimum(i, sc[2 * N_EXPERTS] - 1)
    grid_spec = pltpu.PrefetchScalarGridSpec(
        num_scalar_prefetch=2,
        grid=(n_rows // rows,),
        in_specs=[
            pl.BlockSpec((rows, D_MODEL), lambda i, be, sc: (used(i, sc), 0)),
            pl.BlockSpec((1, D_MODEL, D_FF), lambda i, be, sc: (be[i], 0, 0)),
            pl.BlockSpec((1, D_MODEL, D_FF), lambda i, be, sc: (be[i], 0, 0)),
            pl.BlockSpec((1, D_FF, D_MODEL), lambda i, be, sc: (be[i], 0, 0)),
        ],
        out_specs=pl.BlockSpec((rows, D_MODEL), lambda i, be, sc: (i, 0)),
    )
    return pl.pallas_call(
        functools.partial(_expert_kernel, tf=tf),
        grid_spec=grid_spec,
        out_shape=jax.ShapeDtypeStruct((n_rows, D_MODEL), F32),
        compiler_params=_params("arbitrary"),
        name="experts",
    )(block_e, sched, xg, wg, wu, wd)


def _combine_kernel(dest_ref, next_dest_ref, meta_ref, h_ref, nw_ref, y_ref, o_ref, buf_ref, sem):
    tm = h_ref.shape[0]
    step = pl.program_id(0)
    slot = step % 2

    def fetch(idx_ref, s):
        def body(r, carry):
            for j in range(TOP_K):
                d = idx_ref[j * tm + r]
                pltpu.make_async_copy(y_ref.at[pl.ds(d, 1)], buf_ref.at[s, j, pl.ds(r, 1)], sem.at[s]).start()
            return carry

        lax.fori_loop(0, tm, body, 0, unroll=ROW_DMA_UNROLL)

    @pl.when(step == 0)
    def _():
        fetch(dest_ref, 0)

    @pl.when(step + 1 < pl.num_programs(0))
    def _():
        fetch(next_dest_ref, 1 - slot)

    for j in range(TOP_K):
        _row_wait(y_ref, buf_ref.at[slot, j], sem.at[slot], tm)
    meta = meta_ref[...]
    moe = meta[:, 4:5] * buf_ref[slot, 0] + meta[:, 5:6] * buf_ref[slot, 1]
    o_ref[...] = _rms_rows(h_ref[...] + moe, nw_ref[...])


def _combine(dest_flat, meta, h, norm_w, y, tm):
    t = h.shape[0]
    last = t // tm - 1
    return pl.pallas_call(
        _combine_kernel,
        grid=(t // tm,),
        in_specs=[
            pl.BlockSpec((TOP_K * tm,), lambda i: (i,), memory_space=pltpu.SMEM),
            pl.BlockSpec((TOP_K * tm,), lambda i: (jnp.minimum(i + 1, last),), memory_space=pltpu.SMEM),
            pl.BlockSpec((tm, LANES), lambda i: (i, 0)),
            pl.BlockSpec((tm, D_MODEL), lambda i: (i, 0)),
            pl.BlockSpec((1, D_MODEL), lambda i: (0, 0)),
            pl.BlockSpec(memory_space=pl.ANY),
        ],
        out_specs=pl.BlockSpec((tm, D_MODEL), lambda i: (i, 0)),
        out_shape=jax.ShapeDtypeStruct((t, D_MODEL), F32),
        scratch_shapes=[pltpu.VMEM((2, TOP_K, tm, D_MODEL), F32), pltpu.SemaphoreType.DMA((2,))],
        compiler_params=_params("arbitrary"),
        name="combine",
    )(dest_flat, dest_flat, meta, h, norm_w, y)


def _moe_layer(meta, meta_t, counts, h, ffn_norm_w, final_norm_w, wg, wu, wd, rows, tm_rows, tf):
    t = h.shape[0]
    counts = counts[0, :N_EXPERTS].astype(jnp.int32)
    padded = ((counts + rows - 1) // rows) * rows
    pend = jnp.cumsum(padded)
    pstart = pend - padded
    n_rows = t * TOP_K + N_EXPERTS * rows
    expert = meta_t[0:TOP_K].astype(jnp.int32)
    rank = meta_t[TOP_K:2 * TOP_K].astype(jnp.int32)
    dest = rank
    for e in range(N_EXPERTS):
        dest = dest + jnp.where(expert == e, pstart[e], 0)
    dest = dest.reshape(TOP_K, t // tm_rows, tm_rows).transpose(1, 0, 2).reshape(t * TOP_K)
    n_used = pend[N_EXPERTS - 1] // rows
    block_start = jnp.minimum(jnp.arange(n_rows // rows, dtype=jnp.int32), n_used - 1) * rows
    block_e = jnp.sum(block_start[:, None] >= pend[None, :], axis=1).astype(jnp.int32)
    sched = jnp.concatenate([pend, padded, n_used[None]]).astype(jnp.int32)
    xg = _dispatch(sched, dest, h, ffn_norm_w, n_rows, tm_rows, rows)
    y = _experts(block_e, sched, xg, wg, wu, wd, rows, tf)
    return _combine(dest, meta, h, final_norm_w, y, tm_rows)


def _pack_in_weights(w_in):
    o = np.cumsum((0, HEADS_W, HEADS_W, HEADS_W, HEADS_W, CONV_CH, HEADS_W, GDN_HEADS, GDN_HEADS, D_MODEL, D_MODEL))
    rq, rk, rv, rg, gqkv, gz, ga, gb, ma, mb = (w_in[:, o[i]:o[i + 1]] for i in range(10))
    main = _bf(jnp.concatenate([gqkv, ma, mb, rq, rk, rv, rg, gz], axis=1))
    small = jnp.concatenate([ga, gb], axis=1)
    small_cols = _bf(jnp.pad(small, ((0, 0), (0, LANES - 2 * GDN_HEADS))))
    small_rows = _bf(jnp.pad(small.T, ((0, SMALL_ROWS - 2 * GDN_HEADS), (0, 0))))
    return main, small_cols, small_rows


def _pick(n, prefs):
    for p in prefs:
        if n % p == 0:
            return p
    raise ValueError(f"no tile in {prefs} divides {n}")


def kernel(x, norm_mix, w_in, conv_w, a_log, dt_bias, gdn_norm, w_branch, w_out, norm_ffn,
           dense_w_gate, dense_w_up, dense_w_down, router, moe_w_gate, moe_w_up, moe_w_down, final_norm):
    batch, seq, d = x.shape
    depth = norm_mix.shape[0]
    assert d == D_MODEL and seq % GDN_CHUNK == 0
    assert depth % 2 == 0, "the final RMSNorm is fused into the last (routed) layer's combine"
    t = batch * seq
    blk = _pick(seq, (128, 64))
    nb = _pick(batch, (4, 2, 1))
    tm = _pick(seq, (1024, 512, 256))
    tm_rows = _pick(t, (512, 256))
    tf_dense = 256
    rows = 512
    tabs = _mixer_tables(seq, blk)

    h = x.reshape(t, d)
    out = None
    for layer in range(depth):
        w_main, w_small, w_small_t = _pack_in_weights(w_in[layer])
        convw = jnp.pad(conv_w[layer], ((0, 8 - CONV_K), (0, 0)))
        conv, proj, gab, gabt = _in_projection(h, norm_mix[layer][None, :], w_main, w_small, w_small_t, convw, tm, CONV_CH, seq)
        prow = jnp.zeros((8, LANES), F32)
        prow = prow.at[0, :GDN_HEADS].set(a_log[layer]).at[1, :GDN_HEADS].set(dt_bias[layer])
        pcol = jnp.zeros((SMALL_ROWS, LANES), F32)
        pcol = pcol.at[:GDN_HEADS, 0].set(a_log[layer]).at[:GDN_HEADS, 1].set(dt_bias[layer])
        ret, gdn = _token_mixers(conv, proj, gab, gabt, tabs, prow, pcol, gdn_norm[layer][None, :],
                                 batch, seq, blk, nb)
        i = layer // 2
        is_moe = layer % 2 == 1
        wa, wb, wo = _bf(w_branch[layer, 0]), _bf(w_branch[layer, 1]), _bf(w_out[layer])
        if not is_moe:
            h = _merge_dense_ffn(ret, gdn, proj, h, wa, wb, wo, norm_ffn[layer][None, :], _bf(dense_w_gate[i])[None],
                                 _bf(dense_w_up[i])[None], _bf(dense_w_down[i])[None], tm_rows, tf_dense)
        else:
            router_w = _bf(jnp.pad(router[i], ((0, 0), (0, LANES - N_EXPERTS))))
            h, meta, meta_t, counts = _merge_router(ret, gdn, proj, h, wa, wb, wo, norm_ffn[layer][None, :],
                                                    router_w, tm_rows)
            out = _moe_layer(meta, meta_t, counts, h, norm_ffn[layer][None, :], final_norm[None, :],
                             _bf(moe_w_gate[i]), _bf(moe_w_up[i]), _bf(moe_w_down[i]), rows, tm_rows, 256)
    return out.reshape(batch, seq, d)
```

```python
import functools
import math

import jax
import jax.numpy as jnp
import numpy as np
from jax import lax
from jax.experimental import pallas as pl
from jax.experimental.pallas import tpu as pltpu

F32 = jnp.float32
BF16 = jnp.bfloat16

D_MODEL = 1024
RET_HEADS = 4
GDN_HEADS = 4
HEAD_DIM = 128
HEADS_W = 512
CONV_K = 4
CONV_CH = 3 * HEADS_W
GDN_CHUNK = 64
ROPE_BASE = 10000.0
D_FF = 2816
N_EXPERTS = 8
TOP_K = 2
EPS = 1e-6
LANES = 128
SMALL_ROWS = 16
ROW_DMA_UNROLL = 8
CONV_ROWS = 256
MIXER_GROUP = 16

MAIN_COLS = CONV_CH + 5 * HEADS_W + 2 * D_MODEL
VMEM_LIMIT = 56 * 1024 * 1024

HIGHEST = lax.Precision.HIGHEST


def _bf(x):
    return x.astype(BF16)


def _dot(a, b):
    return jnp.dot(a, b, preferred_element_type=F32)


def _dot_nt(a, b):
    return lax.dot_general(a, b, (((1,), (1,)), ((), ())), preferred_element_type=F32)


def _dot_tn(a, b):
    return lax.dot_general(a, b, (((0,), (0,)), ((), ())), preferred_element_type=F32)


def _dot_f32(a, b):
    return jnp.dot(a, b, preferred_element_type=F32, precision=HIGHEST)


def _sigmoid(x):
    return 1.0 / (1.0 + jnp.exp(-x))


def _silu(x):
    return x * _sigmoid(x)


def _softplus(x):
    return jnp.maximum(x, 0.0) + jnp.log(1.0 + jnp.exp(-jnp.abs(x)))


def _rms_rows(x, w):
    ms = jnp.mean(x * x, axis=-1, keepdims=True)
    return x * lax.rsqrt(ms + EPS) * w


def _params(*sem):
    return pltpu.CompilerParams(dimension_semantics=sem, vmem_limit_bytes=VMEM_LIMIT)


def _inproj_kernel(x_ref, nw_ref, w_ref, ws_ref, wst_ref, cw_ref, oc_ref, o_ref, os_ref, ost_ref,
                   hn_ref, xs_ref, *, tiles_per_seq, n_col):
    tm = x_ref.shape[0]
    tile, step = pl.program_id(0), pl.program_id(1)

    @pl.when(step == 0)
    def _():
        @pl.when(tile % tiles_per_seq == 0)
        def _():
            xs_ref[0:8, :] = jnp.zeros((8, CONV_CH), F32)

        hn = None
        for r0 in range(0, tm + CONV_ROWS, CONV_ROWS):
            prev, prev_rows = hn, slice(r0 - CONV_ROWS, r0)
            if r0 < tm:
                rows = slice(r0, r0 + CONV_ROWS)
                hn = _bf(_rms_rows(x_ref[rows, :], nw_ref[...]))
                hn_ref[rows, :] = hn
            if prev is not None:
                xs_ref[8 + r0 - CONV_ROWS:8 + r0, :] = _dot(prev, w_ref[...])
                os_ref[prev_rows, :] = _dot(prev, ws_ref[...])
                ost_ref[:, prev_rows] = _dot_nt(wst_ref[...], prev)

    def conv_piece(r0, c0):
        cols = slice(c0, c0 + LANES)
        acc = cw_ref[CONV_K - 1:CONV_K, cols] * xs_ref[pl.ds(r0 + 8, CONV_ROWS), cols]
        for d in range(1, CONV_K):
            acc = acc + cw_ref[CONV_K - 1 - d:CONV_K - d, cols] * xs_ref[pl.ds(r0 + 8 - d, CONV_ROWS), cols]
        oc_ref[r0:r0 + CONV_ROWS, cols] = _bf(_silu(acc))

    conv_pieces = [(r0, c0) for r0 in range(0, tm, CONV_ROWS) for c0 in range(0, CONV_CH, LANES)]
    dot_pieces = [(r0, c0) for r0 in range(0, tm, CONV_ROWS) for c0 in range(0, o_ref.shape[1], HEADS_W)]
    share = -(-len(conv_pieces) // (n_col - 1))
    for k in range(1, n_col):
        @pl.when(step == k)
        def _(k=k):
            mine = conv_pieces[(k - 1) * share:k * share]
            per_dot = -(-len(mine) // len(dot_pieces))
            for j, (r0, c0) in enumerate(dot_pieces):
                rows, cols = slice(r0, r0 + CONV_ROWS), slice(c0, c0 + HEADS_W)
                o_ref[rows, cols] = _bf(_dot(hn_ref[rows, :], w_ref[:, cols]))
                for piece in mine[j * per_dot:(j + 1) * per_dot]:
                    conv_piece(*piece)
            if k == n_col - 1:
                xs_ref[0:8, :] = xs_ref[tm:tm + 8, :]


def _in_projection(h, norm_w, w_main, w_small, w_small_t, conv_w, tm, tn, seq):
    t = h.shape[0]
    assert tn == CONV_CH and seq % tm == 0 and tm % CONV_ROWS == 0
    n_col = MAIN_COLS // tn
    return pl.pallas_call(
        functools.partial(_inproj_kernel, tiles_per_seq=seq // tm, n_col=n_col),
        grid=(t // tm, n_col),
        in_specs=[
            pl.BlockSpec((tm, D_MODEL), lambda i, j: (i, 0)),
            pl.BlockSpec((1, D_MODEL), lambda i, j: (0, 0)),
            pl.BlockSpec((D_MODEL, tn), lambda i, j: (0, j)),
            pl.BlockSpec((D_MODEL, LANES), lambda i, j: (0, 0)),
            pl.BlockSpec((SMALL_ROWS, D_MODEL), lambda i, j: (0, 0)),
            pl.BlockSpec((8, CONV_CH), lambda i, j: (0, 0)),
        ],
        out_specs=[
            pl.BlockSpec((tm, CONV_CH), lambda i, j: (i, 0)),
            pl.BlockSpec((tm, tn), lambda i, j: (i, jnp.maximum(j - 1, 0))),
            pl.BlockSpec((tm, LANES), lambda i, j: (i, 0)),
            pl.BlockSpec((SMALL_ROWS, tm), lambda i, j: (0, i)),
        ],
        out_shape=[
            jax.ShapeDtypeStruct((t, CONV_CH), BF16),
            jax.ShapeDtypeStruct((t, MAIN_COLS - CONV_CH), BF16),
            jax.ShapeDtypeStruct((t, LANES), F32),
            jax.ShapeDtypeStruct((SMALL_ROWS, t), F32),
        ],
        scratch_shapes=[pltpu.VMEM((tm, D_MODEL), BF16), pltpu.VMEM((tm + 8, CONV_CH), F32)],
        compiler_params=_params("arbitrary", "arbitrary"),
        name="in_projection",
    )(h, norm_w, w_main, w_small, w_small_t, conv_w)


def _hi_lo(x):
    hi = _bf(x)
    lo_f = x - hi.astype(F32)
    return hi, _bf(lo_f), lo_f


def _dup_lhs(x2, lo_f, left):
    w = _bf(jnp.where(left, x2, lo_f))
    return jnp.concatenate([w, w], axis=1)


def _dup_rhs(hi, lo):
    return jnp.concatenate([hi, hi, lo, lo], axis=0)


def _mixer_kernel(*refs, blk, nb):
    gqkv_ref, rq_ref, rk_ref, rv_ref, rg_ref, gz_ref, gab_ref = refs[:7]
    gabt_refs = refs[7:7 + nb]
    (cos_ref, sin_ref, dmat_ref, qdec_ref, kdec_ref, cdec_ref, prow_ref, pcol_ref, gnorm_ref, tril_ref, triu_ref,
     ret_ref, gdn_ref, rstate_ref, gstate_ref) = refs[7 + nb:]
    c = GDN_CHUNK
    n_chunks = blk // c
    seqs = range(nb)
    chains = [(s, h) for s in seqs for h in range(GDN_HEADS)]
    inst = [(s, h, n) for s, h in chains for n in range(n_chunks)]
    slot = {ch: ch[0] * GDN_HEADS + ch[1] for ch in chains}

    @pl.when(pl.program_id(1) == 0)
    def _():
        rstate_ref[...] = jnp.zeros_like(rstate_ref)
        gstate_ref[...] = jnp.zeros_like(gstate_ref)

    hsl = [slice(h * HEAD_DIM, (h + 1) * HEAD_DIM) for h in range(RET_HEADS)]

    cos2 = cos_ref[...]
    sin2 = sin_ref[...]
    half = HEAD_DIM // 2
    rq = {(s, h): rq_ref[s, :, hsl[h]].astype(F32) for s, h in chains}
    rk = {(s, h): rk_ref[s, :, hsl[h]].astype(F32) for s, h in chains}
    rv = {(s, h): rv_ref[s, :, hsl[h]] for s, h in chains}
    rq = {ch: q * cos2 + pltpu.roll(q, half, 1) * sin2 for ch, q in rq.items()}
    rk = {ch: (k * cos2 + pltpu.roll(k, half, 1) * sin2) * (HEAD_DIM ** -0.5) for ch, k in rk.items()}
    rstate = {ch: rstate_ref[slot[ch]] for ch in chains}
    scores = {ch: _dot_nt(_bf(rq[ch]), _bf(rk[ch])) * dmat_ref[ch[1]] for ch in chains}
    inter = {ch: _dot(_bf(rq[ch] * qdec_ref[ch[1]]), _bf(rstate[ch])) for ch in chains}
    kv = {ch: _dot_tn(_bf(rk[ch] * kdec_ref[ch[1]]), rv[ch]) for ch in chains}
    ro = {ch: _dot(_bf(scores[ch]), rv[ch]) + inter[ch] for ch in chains}
    for ch in chains:
        s, h = ch
        rstate_ref[slot[ch]] = rstate[ch] * cdec_ref[h] + kv[ch]
        o = ro[ch]
        o = o * lax.rsqrt(jnp.mean(o * o, axis=-1, keepdims=True) + EPS)
        ret_ref[s, :, hsl[h]] = _bf(_silu(rg_ref[s, :, hsl[h]].astype(F32)) * o)

    neg_a_row = -jnp.exp(prow_ref[0:1, :])
    neg_a_col = -jnp.exp(pcol_ref[:, 0:1])
    beta_cols, gc_cols, egc_cols, gc_rows = [], [], [], []
    for s in seqs:
        gab = gab_ref[s]
        beta_cols.append(_sigmoid(gab))
        gc_cols.append(_dot_f32(tril_ref[...], neg_a_row * _softplus(gab + prow_ref[1:2, :])))
        egc_cols.append(jnp.exp(gc_cols[s]))
        gc_rows.append(_dot_f32(neg_a_col * _softplus(gabt_refs[s][...] + pcol_ref[:, 1:2]), triu_ref[...]))

    ri = lax.broadcasted_iota(jnp.int32, (c, 2 * c), 0)
    lane = lax.broadcasted_iota(jnp.int32, (c, 2 * c), 1)
    ci = lane & (c - 1)
    left = lane < c
    ge = ri >= ci
    gt = ri > ci
    eye = jnp.where(ri == ci, 1.0, 0.0).astype(F32)
    level_masks = []
    for lg in range(int(math.log2(c))):
        b = 1 << lg
        same_block = (ri >> (lg + 1)) == (ci >> (lg + 1))
        level_masks.append(same_block & ((ri & (2 * b - 1)) >= b) & ((ci & (2 * b - 1)) < b))

    qh, kh, vh = {}, {}, {}
    for s, h in chains:
        q = gqkv_ref[s, :, hsl[h]].astype(F32)
        k = gqkv_ref[s, :, HEADS_W + h * HEAD_DIM:HEADS_W + (h + 1) * HEAD_DIM].astype(F32)
        vh[s, h] = gqkv_ref[s, :, 2 * HEADS_W + h * HEAD_DIM:2 * HEADS_W + (h + 1) * HEAD_DIM].astype(F32)
        qh[s, h] = q * lax.rsqrt(jnp.sum(q * q, axis=-1, keepdims=True) + EPS) * (HEAD_DIM ** -0.5)
        kh[s, h] = k * lax.rsqrt(jnp.sum(k * k, axis=-1, keepdims=True) + EPS)

    rows = [slice(n * c, (n + 1) * c) for n in range(n_chunks)]
    uw, wq, kd, eg, attn = {}, {}, {}, {}, {}
    for g0 in range(0, len(inst), MIXER_GROUP):
        grp = inst[g0:g0 + MIXER_GROUP]
        kc = {(s, h, n): kh[s, h][rows[n]] for s, h, n in grp}
        qc = {(s, h, n): qh[s, h][rows[n]] for s, h, n in grp}
        bcol = {(s, h, n): beta_cols[s][rows[n], 4 + h:5 + h] for s, h, n in grp}
        gcol = {(s, h, n): gc_cols[s][rows[n], h:h + 1] for s, h, n in grp}
        egcol = {(s, h, n): egc_cols[s][rows[n], h:h + 1] for s, h, n in grp}
        glast = {(s, h, n): gc_cols[s][n * c + c - 1:(n + 1) * c, h:h + 1] for s, h, n in grp}
        grow = {(s, h, n): gc_rows[s][h:h + 1, 2 * c * n:2 * c * (n + 1)] for s, h, n in grp}
        kb = {i: kc[i] * bcol[i] for i in grp}
        kcb = {i: _bf(kc[i]) for i in grp}
        kcb2 = {i: jnp.concatenate([kcb[i], kcb[i]], axis=0) for i in grp}
        decay = {i: jnp.where(ge, jnp.exp(jnp.where(ge, gcol[i] - grow[i], 0.0)), 0.0) for i in grp}
        a2 = {i: jnp.where(gt, _dot_nt(_bf(kb[i]), kcb2[i]) * decay[i], 0.0) for i in grp}
        for i in grp:
            attn[i] = _bf(jnp.where(ge, _dot_nt(_bf(qc[i]), kcb2[i]) * decay[i], 0.0)[:, :c])

        x2 = {i: eye - jnp.where(level_masks[0], a2[i], 0.0) for i in grp}
        a_hl = {i: _hi_lo(a2[i]) for i in grp}
        for m in level_masks[1:]:
            mb = jnp.where(m, 1.0, 0.0).astype(BF16)
            x_hl = {i: _hi_lo(x2[i]) for i in grp}
            t = {i: _dot(_dup_lhs(x2[i], x_hl[i][2], left), _dup_rhs(a_hl[i][0] * mb, a_hl[i][1] * mb)) for i in grp}
            t_hl = {i: _hi_lo(t[i]) for i in grp}
            x2 = {i: x2[i] - _dot(_dup_lhs(t[i], t_hl[i][2], left), _dup_rhs(x_hl[i][0], x_hl[i][1])) for i in grp}

        for i in grp:
            s, h, n = i
            rhs = _hi_lo(jnp.concatenate([vh[s, h][rows[n]] * bcol[i], kb[i] * egcol[i]], axis=1))
            uw[i] = _dot(_dup_lhs(x2[i], _hi_lo(x2[i])[2], left), _dup_rhs(rhs[0], rhs[1]))
        for i in grp:
            wq[i] = _bf(jnp.concatenate([uw[i][:, HEAD_DIM:], qc[i] * egcol[i]], axis=0))
            kd[i] = _bf(kc[i] * jnp.exp(glast[i] - gcol[i]))
            eg[i] = jnp.exp(glast[i])

    state = {ch: gstate_ref[slot[ch]] for ch in chains}
    outs = {ch: [] for ch in chains}
    for n in range(n_chunks):
        sb = {ch: _bf(state[ch]) for ch in chains}
        ws = {ch: _dot(wq[ch + (n,)], sb[ch]) for ch in chains}
        vnb = {ch: _bf(uw[ch + (n,)][:, :HEAD_DIM] - ws[ch][:c]) for ch in chains}
        for ch in chains:
            outs[ch].append(ws[ch][c:] + _dot(attn[ch + (n,)], vnb[ch]))
        state = {ch: state[ch] * eg[ch + (n,)] + _dot_tn(kd[ch + (n,)], vnb[ch]) for ch in chains}

    gnorm = gnorm_ref[...]
    for ch in chains:
        s, h = ch
        gstate_ref[slot[ch]] = state[ch]
        o = jnp.concatenate(outs[ch], axis=0)
        o = o * lax.rsqrt(jnp.mean(o * o, axis=-1, keepdims=True) + EPS)
        gdn_ref[s, :, hsl[h]] = _bf(o * gnorm * _silu(gz_ref[s, :, hsl[h]].astype(F32)))


def _token_mixers(conv, proj, gab, gabt, tabs, prow, pcol, gnorm, batch, seq, blk, nb):
    conv3 = conv.reshape(batch, seq, CONV_CH)
    proj3 = proj.reshape(batch, seq, MAIN_COLS - CONV_CH)
    gab3 = gab.reshape(batch, seq, LANES)
    n_blk = seq // blk
    cos2, sin2, dmat, qdec, kdec, cdec, tril, triu = tabs
    w512 = lambda col: pl.BlockSpec((nb, blk, HEADS_W), lambda b, n, col=col: (b, n, col))
    full = lambda shape: pl.BlockSpec(shape, lambda b, n: (0,) * len(shape))
    gabt_specs = [pl.BlockSpec((SMALL_ROWS, blk), lambda b, n, s=s: (0, (b * nb + s) * n_blk + n)) for s in range(nb)]
    out = pl.pallas_call(
        functools.partial(_mixer_kernel, blk=blk, nb=nb),
        grid=(batch // nb, n_blk),
        in_specs=[
            pl.BlockSpec((nb, blk, CONV_CH), lambda b, n: (b, n, 0)),
            w512(4), w512(5), w512(6), w512(7), w512(8),
            pl.BlockSpec((nb, blk, LANES), lambda b, n: (b, n, 0)),
            *gabt_specs,
            pl.BlockSpec((blk, HEAD_DIM), lambda b, n: (n, 0)),
            pl.BlockSpec((blk, HEAD_DIM), lambda b, n: (n, 0)),
            full((RET_HEADS, blk, blk)),
            full((RET_HEADS, blk, HEAD_DIM)),
            full((RET_HEADS, blk, HEAD_DIM)),
            full((RET_HEADS, 1, HEAD_DIM)),
            full((8, LANES)),
            full((SMALL_ROWS, LANES)),
            full((1, HEAD_DIM)),
            full((blk, blk)),
            full((blk, 2 * blk)),
        ],
        out_specs=[
            pl.BlockSpec((nb, blk, HEADS_W), lambda b, n: (b, n, 0)),
            pl.BlockSpec((nb, blk, HEADS_W), lambda b, n: (b, n, 0)),
        ],
        out_shape=[
            jax.ShapeDtypeStruct((batch, seq, HEADS_W), BF16),
            jax.ShapeDtypeStruct((batch, seq, HEADS_W), BF16),
        ],
        scratch_shapes=[
            pltpu.VMEM((nb * RET_HEADS, HEAD_DIM, HEAD_DIM), F32),
            pltpu.VMEM((nb * GDN_HEADS, HEAD_DIM, HEAD_DIM), F32),
        ],
        compiler_params=_params("parallel", "arbitrary"),
        name="token_mixers",
    )(conv3, proj3, proj3, proj3, proj3, proj3, gab3, *([gabt] * nb),
      cos2, sin2, dmat, qdec, kdec, cdec, prow, pcol, gnorm, tril, triu)
    ret, gdn = out
    return ret.reshape(batch * seq, HEADS_W), gdn.reshape(batch * seq, HEADS_W)


def _mixer_tables(seq, blk):
    inv_freq = ROPE_BASE ** (-jnp.arange(0, HEAD_DIM, 2, dtype=F32) / HEAD_DIM)
    ang = jnp.arange(seq, dtype=F32)[:, None] * inv_freq[None, :]
    cos, sin = jnp.cos(ang), jnp.sin(ang)
    cos2 = jnp.concatenate([cos, cos], axis=-1)
    sin2 = jnp.concatenate([-sin, sin], axis=-1)
    log_gamma = jnp.log(1.0 - jnp.exp2(-5.0 - jnp.arange(RET_HEADS, dtype=F32)))
    pos = jnp.arange(blk, dtype=F32)
    diff = pos[:, None] - pos[None, :]
    dmat = jnp.where(diff >= 0, jnp.exp(jnp.maximum(diff, 0.0)[None] * log_gamma[:, None, None]), 0.0)
    ones = jnp.ones((1, 1, HEAD_DIM), F32)
    qdec = jnp.exp((pos + 1.0)[None, :] * log_gamma[:, None])[:, :, None] * ones
    kdec = jnp.exp((blk - 1 - pos)[None, :] * log_gamma[:, None])[:, :, None] * ones
    cdec = jnp.exp(blk * log_gamma)[:, None, None] * ones
    idx = np.arange(blk)
    same = (idx[:, None] // GDN_CHUNK) == (idx[None, :] // GDN_CHUNK)
    tril = jnp.asarray((same & (idx[:, None] >= idx[None, :])).astype(np.float32))
    col = np.arange(2 * blk)
    col_chunk, col_pos = col // (2 * GDN_CHUNK), col % GDN_CHUNK
    triu = jnp.asarray(((idx[:, None] // GDN_CHUNK == col_chunk[None, :])
                        & (idx[:, None] % GDN_CHUNK <= col_pos[None, :])).astype(np.float32))
    return cos2, sin2, dmat, qdec, kdec, cdec, tril, triu


def _swiglu_rows(x, wg_ref, wu_ref, wd_ref, acc, tf):
    for f in range(D_FF // tf):
        cols = slice(f * tf, (f + 1) * tf)
        act = _silu(_dot(x, wg_ref[0, :, cols])) * _dot(x, wu_ref[0, :, cols])
        acc = acc + _dot(_bf(act), wd_ref[0, cols, :])
    return acc


def _merged_rows(ret_ref, gdn_ref, ga_ref, gb_ref, h_ref, wa_ref, wb_ref, wo_ref, nw_ref):
    ya = _dot(ret_ref[...], wa_ref[...])
    yb = _dot(gdn_ref[...], wb_ref[...])
    merged = _sigmoid(ga_ref[...].astype(F32)) * ya + _sigmoid(gb_ref[...].astype(F32)) * yb
    h_new = h_ref[...] + _dot(_bf(merged), wo_ref[...])
    return h_new, _bf(_rms_rows(h_new, nw_ref[...]))


def _merge_ffn_kernel(*refs, tf):
    wg_ref, wu_ref, wd_ref, o_ref = refs[9:]
    h_new, hn = _merged_rows(*refs[:9])
    o_ref[...] = _swiglu_rows(hn, wg_ref, wu_ref, wd_ref, h_new, tf)


def _merge_router_kernel(*refs):
    wr_ref, tri_ref, ho_ref, meta_ref, meta_t_ref, count_ref, run_ref = refs[9:]

    @pl.when(pl.program_id(0) == 0)
    def _():
        run_ref[...] = jnp.zeros_like(run_ref)

    h_new, hn = _merged_rows(*refs[:9])
    ho_ref[...] = h_new
    meta, run_new = _route_rows(_dot(hn, wr_ref[...]), tri_ref, run_ref[0:1, :])
    run_ref[0:1, :] = run_new
    count_ref[...] = jnp.broadcast_to(run_new, count_ref.shape)
    meta_ref[...] = meta
    meta_t_ref[...] = meta.T[:8, :]


def _resident(shape):
    return pl.BlockSpec(shape, lambda i: (0,) * len(shape), pipeline_mode=pl.Buffered(1))


def _merge_specs(tm):
    rows = lambda width, col: pl.BlockSpec((tm, width), lambda i: (i, col))
    return [rows(HEADS_W, 0), rows(HEADS_W, 0), rows(D_MODEL, 0), rows(D_MODEL, 1), rows(D_MODEL, 0),
            _resident((HEADS_W, D_MODEL)), _resident((HEADS_W, D_MODEL)), _resident((D_MODEL, D_MODEL)),
            _resident((1, D_MODEL))]


def _merge_dense_ffn(ret, gdn, proj, h, wa, wb, wo, norm_w, wg, wu, wd, tm, tf):
    t = h.shape[0]
    return pl.pallas_call(
        functools.partial(_merge_ffn_kernel, tf=tf),
        grid=(t // tm,),
        in_specs=_merge_specs(tm) + [_resident((1, D_MODEL, D_FF)), _resident((1, D_MODEL, D_FF)),
                                     _resident((1, D_FF, D_MODEL))],
        out_specs=pl.BlockSpec((tm, D_MODEL), lambda i: (i, 0)),
        out_shape=jax.ShapeDtypeStruct((t, D_MODEL), F32),
        compiler_params=_params("parallel"),
        name="merge_dense_ffn",
    )(ret, gdn, proj, proj, h, wa, wb, wo, norm_w, wg, wu, wd)


def _merge_router(ret, gdn, proj, h, wa, wb, wo, norm_w, router_w, tm):
    t = h.shape[0]
    idx = np.arange(tm)
    tri = jnp.asarray((idx[:, None] > idx[None, :]).astype(np.float32), dtype=BF16)
    return pl.pallas_call(
        _merge_router_kernel,
        grid=(t // tm,),
        in_specs=_merge_specs(tm) + [_resident((D_MODEL, LANES)), _resident((tm, tm))],
        out_specs=[pl.BlockSpec((tm, D_MODEL), lambda i: (i, 0)),
                   pl.BlockSpec((tm, LANES), lambda i: (i, 0)),
                   pl.BlockSpec((8, tm), lambda i: (0, i)),
                   pl.BlockSpec((8, LANES), lambda i: (0, 0))],
        out_shape=[jax.ShapeDtypeStruct((t, D_MODEL), F32),
                   jax.ShapeDtypeStruct((t, LANES), F32),
                   jax.ShapeDtypeStruct((8, t), F32),
                   jax.ShapeDtypeStruct((8, LANES), F32)],
        scratch_shapes=[pltpu.VMEM((8, LANES), F32)],
        compiler_params=_params("arbitrary"),
        name="merge_router",
    )(ret, gdn, proj, proj, h, wa, wb, wo, norm_w, router_w, tri)


def _route_rows(logits, tri_ref, before_tile):
    tm = logits.shape[0]
    lane = lax.broadcasted_iota(jnp.int32, (tm, LANES), 1)
    neg = jnp.float32(-jnp.inf)
    l1 = jnp.where(lane < N_EXPERTS, logits, neg)
    m1 = jnp.max(l1, axis=-1, keepdims=True)
    i1 = jnp.min(jnp.where(l1 == m1, lane, LANES), axis=-1, keepdims=True)
    l2 = jnp.where(lane == i1, neg, l1)
    m2 = jnp.max(l2, axis=-1, keepdims=True)
    i2 = jnp.min(jnp.where(l2 == m2, lane, LANES), axis=-1, keepdims=True)
    e2 = jnp.exp(m2 - m1)
    g1 = 1.0 / (1.0 + e2)
    g2 = e2 / (1.0 + e2)
    sel1 = lane == i1
    sel2 = lane == i2
    onehot = jnp.where(sel1 | sel2, 1.0, 0.0).astype(F32)
    before = _dot(tri_ref[...], _bf(onehot)) + before_tile
    r1 = jnp.sum(jnp.where(sel1, before, 0.0), axis=-1, keepdims=True)
    r2 = jnp.sum(jnp.where(sel2, before, 0.0), axis=-1, keepdims=True)
    meta = jnp.where(lane == 0, i1.astype(F32), 0.0)
    meta = jnp.where(lane == 1, i2.astype(F32), meta)
    meta = jnp.where(lane == 2, r1, meta)
    meta = jnp.where(lane == 3, r2, meta)
    meta = jnp.where(lane == 4, g1, meta)
    meta = jnp.where(lane == 5, g2, meta)
    return meta, before_tile + jnp.sum(onehot, axis=0, keepdims=True)


def _row_wait(src_hbm, dst_vmem, sem, rows):
    pltpu.make_async_copy(src_hbm.at[pl.ds(0, rows)], dst_vmem, sem).wait()


def _dispatch_kernel(sched_ref, dest_ref, h_ref, nw_ref, xg_ref, hn_ref, zero_ref, sem, zsem, *, rows):
    tm = h_ref.shape[0]

    @pl.when(pl.program_id(0) == 0)
    def _():
        zero_ref[...] = jnp.zeros_like(zero_ref)
        for e in range(N_EXPERTS):
            @pl.when(sched_ref[N_EXPERTS + e] > 0)
            def _():
                start = pl.multiple_of(sched_ref[e] - rows, rows)
                clear = pltpu.make_async_copy(zero_ref, xg_ref.at[pl.ds(start, rows)], zsem)
                clear.start()
                clear.wait()

        def clear_unused(b, carry):
            clear = pltpu.make_async_copy(zero_ref, xg_ref.at[pl.ds(pl.multiple_of(b * rows, rows), rows)], zsem)
            clear.start()
            clear.wait()
            return carry

        lax.fori_loop(sched_ref[2 * N_EXPERTS], xg_ref.shape[0] // rows, clear_unused, 0)

    step = pl.program_id(0)
    slot = step % 2
    rows_ref = hn_ref.at[slot]
    rows_ref[...] = _rms_rows(h_ref[...], nw_ref[...])

    def body(r, carry):
        for j in range(TOP_K):
            d = dest_ref[j * tm + r]
            pltpu.make_async_copy(rows_ref.at[pl.ds(r, 1)], xg_ref.at[pl.ds(d, 1)], sem.at[slot]).start(priority=j)
        return carry

    lax.fori_loop(0, tm, body, 0, unroll=ROW_DMA_UNROLL)

    def drain(s):
        for _ in range(TOP_K):
            pltpu.make_async_copy(hn_ref.at[s], xg_ref.at[pl.ds(0, tm)], sem.at[s]).wait()

    @pl.when(step > 0)
    def _():
        drain(1 - slot)

    @pl.when(step == pl.num_programs(0) - 1)
    def _():
        drain(slot)


def _dispatch(sched, dest_flat, h, norm_w, n_rows, tm, rows):
    t = h.shape[0]
    grid_spec = pltpu.PrefetchScalarGridSpec(
        num_scalar_prefetch=1,
        grid=(t // tm,),
        in_specs=[
            pl.BlockSpec((TOP_K * tm,), lambda i, sc: (i,), memory_space=pltpu.SMEM),
            pl.BlockSpec((tm, D_MODEL), lambda i, sc: (i, 0)),
            pl.BlockSpec((1, D_MODEL), lambda i, sc: (0, 0)),
        ],
        out_specs=pl.BlockSpec(memory_space=pl.ANY),
        scratch_shapes=[pltpu.VMEM((2, tm, D_MODEL), F32), pltpu.VMEM((rows, D_MODEL), F32),
                        pltpu.SemaphoreType.DMA((2,)), pltpu.SemaphoreType.DMA(())],
    )
    return pl.pallas_call(
        functools.partial(_dispatch_kernel, rows=rows),
        grid_spec=grid_spec,
        out_shape=jax.ShapeDtypeStruct((n_rows, D_MODEL), F32),
        compiler_params=_params("arbitrary"),
        name="dispatch",
    )(sched, dest_flat, h, norm_w)


def _expert_kernel(be_ref, sched_ref, x_ref, wg_ref, wu_ref, wd_ref, y_ref, *, tf):
    del be_ref

    @pl.when(pl.program_id(0) < sched_ref[2 * N_EXPERTS])
    def _():
        y_ref[...] = _swiglu_rows(_bf(x_ref[...]), wg_ref, wu_ref, wd_ref, jnp.zeros(y_ref.shape, F32), tf)

    @pl.when(pl.program_id(0) >= sched_ref[2 * N_EXPERTS])
    def _():
        y_ref[...] = jnp.zeros_like(y_ref)


def _experts(block_e, sched, xg, wg, wu, wd, rows, tf):
    n_rows = xg.shape[0]
    used = lambda i, sc: jnp.minimum(i, sc[2 * N_EXPERTS] - 1)
    grid_spec = pltpu.PrefetchScalarGridSpec(
        num_scalar_prefetch=2,
        grid=(n_rows // rows,),
        in_specs=[
            pl.BlockSpec((rows, D_MODEL), lambda i, be, sc: (used(i, sc), 0)),
            pl.BlockSpec((1, D_MODEL, D_FF), lambda i, be, sc: (be[i], 0, 0)),
            pl.BlockSpec((1, D_MODEL, D_FF), lambda i, be, sc: (be[i], 0, 0)),
            pl.BlockSpec((1, D_FF, D_MODEL), lambda i, be, sc: (be[i], 0, 0)),
        ],
        out_specs=pl.BlockSpec((rows, D_MODEL), lambda i, be, sc: (i, 0)),
    )
    return pl.pallas_call(
        functools.partial(_expert_kernel, tf=tf),
        grid_spec=grid_spec,
        out_shape=jax.ShapeDtypeStruct((n_rows, D_MODEL), F32),
        compiler_params=_params("arbitrary"),
        name="experts",
    )(block_e, sched, xg, wg, wu, wd)


def _combine_kernel(dest_ref, next_dest_ref, meta_ref, h_ref, nw_ref, y_ref, o_ref, buf_ref, sem):
    tm = h_ref.shape[0]
    step = pl.program_id(0)
    slot = step % 2

    def fetch(idx_ref, s):
        def body(r, carry):
            for j in range(TOP_K):
                d = idx_ref[j * tm + r]
                pltpu.make_async_copy(y_ref.at[pl.ds(d, 1)], buf_ref.at[s, j, pl.ds(r, 1)],
                                      sem.at[s]).start(priority=j)
            return carry

        lax.fori_loop(0, tm, body, 0, unroll=ROW_DMA_UNROLL)

    @pl.when(step == 0)
    def _():
        fetch(dest_ref, 0)

    @pl.when(step + 1 < pl.num_programs(0))
    def _():
        fetch(next_dest_ref, 1 - slot)

    for j in range(TOP_K):
        _row_wait(y_ref, buf_ref.at[slot, j], sem.at[slot], tm)
    meta = meta_ref[...]
    moe = meta[:, 4:5] * buf_ref[slot, 0] + meta[:, 5:6] * buf_ref[slot, 1]
    o_ref[...] = _rms_rows(h_ref[...] + moe, nw_ref[...])


def _combine(dest_flat, meta, h, norm_w, y, tm):
    t = h.shape[0]
    last = t // tm - 1
    return pl.pallas_call(
        _combine_kernel,
        grid=(t // tm,),
        in_specs=[
            pl.BlockSpec((TOP_K * tm,), lambda i: (i,), memory_space=pltpu.SMEM),
            pl.BlockSpec((TOP_K * tm,), lambda i: (jnp.minimum(i + 1, last),), memory_space=pltpu.SMEM),
            pl.BlockSpec((tm, LANES), lambda i: (i, 0)),
            pl.BlockSpec((tm, D_MODEL), lambda i: (i, 0)),
            pl.BlockSpec((1, D_MODEL), lambda i: (0, 0)),
            pl.BlockSpec(memory_space=pl.ANY),
        ],
        out_specs=pl.BlockSpec((tm, D_MODEL), lambda i: (i, 0)),
        out_shape=jax.ShapeDtypeStruct((t, D_MODEL), F32),
        scratch_shapes=[pltpu.VMEM((2, TOP_K, tm, D_MODEL), F32), pltpu.SemaphoreType.DMA((2,))],
        compiler_params=_params("arbitrary"),
        name="combine",
    )(dest_flat, dest_flat, meta, h, norm_w, y)


def _moe_layer(meta, meta_t, counts, h, ffn_norm_w, final_norm_w, wg, wu, wd, rows, tm_rows, tf):
    t = h.shape[0]
    counts = counts[0, :N_EXPERTS].astype(jnp.int32)
    padded = ((counts + rows - 1) // rows) * rows
    pend = jnp.cumsum(padded)
    pstart = pend - padded
    n_rows = t * TOP_K + N_EXPERTS * rows
    expert = meta_t[0:TOP_K].astype(jnp.int32)
    rank = meta_t[TOP_K:2 * TOP_K].astype(jnp.int32)
    dest = rank
    for e in range(N_EXPERTS):
        dest = dest + jnp.where(expert == e, pstart[e], 0)
    dest = dest.reshape(TOP_K, t // tm_rows, tm_rows).transpose(1, 0, 2).reshape(t * TOP_K)
    n_used = pend[N_EXPERTS - 1] // rows
    block_start = jnp.minimum(jnp.arange(n_rows // rows, dtype=jnp.int32), n_used - 1) * rows
    block_e = jnp.sum(block_start[:, None] >= pend[None, :], axis=1).astype(jnp.int32)
    sched = jnp.concatenate([pend, padded, n_used[None]]).astype(jnp.int32)
    xg = _dispatch(sched, dest, h, ffn_norm_w, n_rows, tm_rows, rows)
    y = _experts(block_e, sched, xg, wg, wu, wd, rows, tf)
    return _combine(dest, meta, h, final_norm_w, y, tm_rows)


def _pack_in_weights(w_in):
    o = np.cumsum((0, HEADS_W, HEADS_W, HEADS_W, HEADS_W, CONV_CH, HEADS_W, GDN_HEADS, GDN_HEADS, D_MODEL, D_MODEL))
    rq, rk, rv, rg, gqkv, gz, ga, gb, ma, mb = (w_in[:, o[i]:o[i + 1]] for i in range(10))
    main = _bf(jnp.concatenate([gqkv, ma, mb, rq, rk, rv, rg, gz], axis=1))
    small = jnp.concatenate([ga, gb], axis=1)
    small_cols = _bf(jnp.pad(small, ((0, 0), (0, LANES - 2 * GDN_HEADS))))
    small_rows = _bf(jnp.pad(small.T, ((0, SMALL_ROWS - 2 * GDN_HEADS), (0, 0))))
    return main, small_cols, small_rows


def _pick(n, prefs):
    for p in prefs:
        if n % p == 0:
            return p
    raise ValueError(f"no tile in {prefs} divides {n}")


def kernel(x, norm_mix, w_in, conv_w, a_log, dt_bias, gdn_norm, w_branch, w_out, norm_ffn,
           dense_w_gate, dense_w_up, dense_w_down, router, moe_w_gate, moe_w_up, moe_w_down, final_norm):
    batch, seq, d = x.shape
    depth = norm_mix.shape[0]
    assert d == D_MODEL and seq % GDN_CHUNK == 0
    assert depth % 2 == 0, "the final RMSNorm is fused into the last (routed) layer's combine"
    t = batch * seq
    blk = _pick(seq, (128, 64))
    nb = _pick(batch, (4, 2, 1))
    tm = _pick(seq, (1024, 512, 256))
    tm_rows = _pick(t, (512, 256))
    tf_dense = 256
    rows = 512
    tabs = _mixer_tables(seq, blk)

    h = x.reshape(t, d)
    out = None
    for layer in range(depth):
        w_main, w_small, w_small_t = _pack_in_weights(w_in[layer])
        convw = jnp.pad(conv_w[layer], ((0, 8 - CONV_K), (0, 0)))
        conv, proj, gab, gabt = _in_projection(h, norm_mix[layer][None, :], w_main, w_small, w_small_t, convw, tm, CONV_CH, seq)
        prow = jnp.zeros((8, LANES), F32)
        prow = prow.at[0, :GDN_HEADS].set(a_log[layer]).at[1, :GDN_HEADS].set(dt_bias[layer])
        pcol = jnp.zeros((SMALL_ROWS, LANES), F32)
        pcol = pcol.at[:GDN_HEADS, 0].set(a_log[layer]).at[:GDN_HEADS, 1].set(dt_bias[layer])
        ret, gdn = _token_mixers(conv, proj, gab, gabt, tabs, prow, pcol, gdn_norm[layer][None, :],
                                 batch, seq, blk, nb)
        i = layer // 2
        is_moe = layer % 2 == 1
        wa, wb, wo = _bf(w_branch[layer, 0]), _bf(w_branch[layer, 1]), _bf(w_out[layer])
        if not is_moe:
            h = _merge_dense_ffn(ret, gdn, proj, h, wa, wb, wo, norm_ffn[layer][None, :], _bf(dense_w_gate[i])[None],
                                 _bf(dense_w_up[i])[None], _bf(dense_w_down[i])[None], tm_rows, tf_dense)
        else:
            router_w = _bf(jnp.pad(router[i], ((0, 0), (0, LANES - N_EXPERTS))))
            h, meta, meta_t, counts = _merge_router(ret, gdn, proj, h, wa, wb, wo, norm_ffn[layer][None, :],
                                                    router_w, tm_rows)
            out = _moe_layer(meta, meta_t, counts, h, norm_ffn[layer][None, :], final_norm[None, :],
                             _bf(moe_w_gate[i]), _bf(moe_w_up[i]), _bf(moe_w_down[i]), rows, tm_rows, 256)
    return out.reshape(batch, seq, d)
```

```python
import functools
import math

import jax
import jax.numpy as jnp
import numpy as np
from jax import lax
from jax.experimental import pallas as pl
from jax.experimental.pallas import tpu as pltpu

F32 = jnp.float32
BF16 = jnp.bfloat16

D_MODEL = 1024
RET_HEADS = 4
GDN_HEADS = 4
HEAD_DIM = 128
HEADS_W = 512
CONV_K = 4
CONV_CH = 3 * HEADS_W
GDN_CHUNK = 64
ROPE_BASE = 10000.0
D_FF = 2816
N_EXPERTS = 8
TOP_K = 2
EPS = 1e-6
LANES = 128
SMALL_ROWS = 16
ROW_DMA_UNROLL = 8
CONV_ROWS = 256
MIXER_GROUP = 16

MAIN_COLS = CONV_CH + 5 * HEADS_W + 2 * D_MODEL
VMEM_LIMIT = 56 * 1024 * 1024

HIGHEST = lax.Precision.HIGHEST


def _bf(x):
    return x.astype(BF16)


def _dot(a, b):
    return jnp.dot(a, b, preferred_element_type=F32)


def _dot_nt(a, b):
    return lax.dot_general(a, b, (((1,), (1,)), ((), ())), preferred_element_type=F32)


def _dot_tn(a, b):
    return lax.dot_general(a, b, (((0,), (0,)), ((), ())), preferred_element_type=F32)


def _dot_f32(a, b):
    return jnp.dot(a, b, preferred_element_type=F32, precision=HIGHEST)


def _sigmoid(x):
    return 1.0 / (1.0 + jnp.exp(-x))


def _silu(x):
    return x * _sigmoid(x)


def _softplus(x):
    return jnp.maximum(x, 0.0) + jnp.log(1.0 + jnp.exp(-jnp.abs(x)))


def _rms_rows(x, w):
    ms = jnp.mean(x * x, axis=-1, keepdims=True)
    return x * lax.rsqrt(ms + EPS) * w


def _params(*sem):
    return pltpu.CompilerParams(dimension_semantics=sem, vmem_limit_bytes=VMEM_LIMIT)


def _inproj_kernel(x_ref, nw_ref, w_ref, ws_ref, wst_ref, cw_ref, oc_ref, o_ref, os_ref, ost_ref,
                   hn_ref, xs_ref, *, tiles_per_seq, n_col):
    tm = x_ref.shape[0]
    tile, step = pl.program_id(0), pl.program_id(1)

    @pl.when(step == 0)
    def _():
        @pl.when(tile % tiles_per_seq == 0)
        def _():
            xs_ref[0:8, :] = jnp.zeros((8, CONV_CH), F32)

        hn = None
        for r0 in range(0, tm + CONV_ROWS, CONV_ROWS):
            prev, prev_rows = hn, slice(r0 - CONV_ROWS, r0)
            if r0 < tm:
                rows = slice(r0, r0 + CONV_ROWS)
                hn = _bf(_rms_rows(x_ref[rows, :], nw_ref[...]))
                hn_ref[rows, :] = hn
            if prev is not None:
                for c in range(w_ref.shape[0]):
                    xs_ref[8 + r0 - CONV_ROWS:8 + r0, c * HEADS_W:(c + 1) * HEADS_W] = _dot(prev, w_ref[c])
                os_ref[prev_rows, :] = _dot(prev, ws_ref[...])
                ost_ref[:, prev_rows] = _dot_nt(wst_ref[...], prev)

    def conv_piece(r0, c0):
        cols = slice(c0, c0 + LANES)
        acc = cw_ref[CONV_K - 1:CONV_K, cols] * xs_ref[pl.ds(r0 + 8, CONV_ROWS), cols]
        for d in range(1, CONV_K):
            acc = acc + cw_ref[CONV_K - 1 - d:CONV_K - d, cols] * xs_ref[pl.ds(r0 + 8 - d, CONV_ROWS), cols]
        oc_ref[r0:r0 + CONV_ROWS, cols] = _bf(_silu(acc))

    conv_pieces = [(r0, c0) for r0 in range(0, tm, CONV_ROWS) for c0 in range(0, CONV_CH, LANES)]
    dot_pieces = [(r0, c0) for r0 in range(0, tm, CONV_ROWS) for c0 in range(0, o_ref.shape[1], HEADS_W)]
    share = -(-len(conv_pieces) // (n_col - 1))
    for k in range(1, n_col):
        @pl.when(step == k)
        def _(k=k):
            mine = conv_pieces[(k - 1) * share:k * share]
            per_dot = -(-len(mine) // len(dot_pieces))
            for j, (r0, c0) in enumerate(dot_pieces):
                rows, cols = slice(r0, r0 + CONV_ROWS), slice(c0, c0 + HEADS_W)
                o_ref[rows, cols] = _bf(_dot(hn_ref[rows, :], w_ref[c0 // HEADS_W]))
                for piece in mine[j * per_dot:(j + 1) * per_dot]:
                    conv_piece(*piece)
            if k == n_col - 1:
                xs_ref[0:8, :] = xs_ref[tm:tm + 8, :]


def _in_projection(h, norm_w, w_main, w_small, w_small_t, conv_w, tm, tn, seq):
    t = h.shape[0]
    assert tn == CONV_CH and seq % tm == 0 and tm % CONV_ROWS == 0
    n_col = MAIN_COLS // tn
    return pl.pallas_call(
        functools.partial(_inproj_kernel, tiles_per_seq=seq // tm, n_col=n_col),
        grid=(t // tm, n_col),
        in_specs=[
            pl.BlockSpec((tm, D_MODEL), lambda i, j: (i, 0)),
            pl.BlockSpec((1, D_MODEL), lambda i, j: (0, 0)),
            pl.BlockSpec((tn // HEADS_W, D_MODEL, HEADS_W), lambda i, j: (j, 0, 0)),
            pl.BlockSpec((D_MODEL, LANES), lambda i, j: (0, 0)),
            pl.BlockSpec((SMALL_ROWS, D_MODEL), lambda i, j: (0, 0)),
            pl.BlockSpec((8, CONV_CH), lambda i, j: (0, 0)),
        ],
        out_specs=[
            pl.BlockSpec((tm, CONV_CH), lambda i, j: (i, 0)),
            pl.BlockSpec((tm, tn), lambda i, j: (i, jnp.maximum(j - 1, 0))),
            pl.BlockSpec((tm, LANES), lambda i, j: (i, 0)),
            pl.BlockSpec((SMALL_ROWS, tm), lambda i, j: (0, i)),
        ],
        out_shape=[
            jax.ShapeDtypeStruct((t, CONV_CH), BF16),
            jax.ShapeDtypeStruct((t, MAIN_COLS - CONV_CH), BF16),
            jax.ShapeDtypeStruct((t, LANES), F32),
            jax.ShapeDtypeStruct((SMALL_ROWS, t), F32),
        ],
        scratch_shapes=[pltpu.VMEM((tm, D_MODEL), BF16), pltpu.VMEM((tm + 8, CONV_CH), F32)],
        compiler_params=_params("arbitrary", "arbitrary"),
        name="in_projection",
    )(h, norm_w, w_main, w_small, w_small_t, conv_w)


def _hi_lo(x):
    hi = _bf(x)
    lo_f = x - hi.astype(F32)
    return hi, _bf(lo_f), lo_f


def _dup_lhs(x2, lo_f, left):
    w = _bf(jnp.where(left, x2, lo_f))
    return jnp.concatenate([w, w], axis=1)


def _dup_rhs(hi, lo):
    return jnp.concatenate([hi, hi, lo, lo], axis=0)


def _mixer_kernel(*refs, blk, nb):
    gqkv_ref, rq_ref, rk_ref, rv_ref, rg_ref, gz_ref, gab_ref = refs[:7]
    gabt_refs = refs[7:7 + nb]
    (cos_ref, sin_ref, dmat_ref, qdec_ref, kdec_ref, cdec_ref, prow_ref, pcol_ref, gnorm_ref, tril_ref, triu_ref,
     ret_ref, gdn_ref, rstate_ref, gstate_ref) = refs[7 + nb:]
    c = GDN_CHUNK
    n_chunks = blk // c
    seqs = range(nb)
    chains = [(s, h) for s in seqs for h in range(GDN_HEADS)]
    inst = [(s, h, n) for s, h in chains for n in range(n_chunks)]
    slot = {ch: ch[0] * GDN_HEADS + ch[1] for ch in chains}

    @pl.when(pl.program_id(1) == 0)
    def _():
        rstate_ref[...] = jnp.zeros_like(rstate_ref)
        gstate_ref[...] = jnp.zeros_like(gstate_ref)

    hsl = [slice(h * HEAD_DIM, (h + 1) * HEAD_DIM) for h in range(RET_HEADS)]

    cos2 = cos_ref[...]
    sin2 = sin_ref[...]
    half = HEAD_DIM // 2
    rq = {(s, h): rq_ref[s, :, hsl[h]].astype(F32) for s, h in chains}
    rk = {(s, h): rk_ref[s, :, hsl[h]].astype(F32) for s, h in chains}
    rv = {(s, h): rv_ref[s, :, hsl[h]] for s, h in chains}
    rq = {ch: q * cos2 + pltpu.roll(q, half, 1) * sin2 for ch, q in rq.items()}
    rk = {ch: (k * cos2 + pltpu.roll(k, half, 1) * sin2) * (HEAD_DIM ** -0.5) for ch, k in rk.items()}
    rstate = {ch: rstate_ref[slot[ch]] for ch in chains}
    scores = {ch: _dot_nt(_bf(rq[ch]), _bf(rk[ch])) * dmat_ref[ch[1]] for ch in chains}
    inter = {ch: _dot(_bf(rq[ch] * qdec_ref[ch[1]]), _bf(rstate[ch])) for ch in chains}
    kv = {ch: _dot_tn(_bf(rk[ch] * kdec_ref[ch[1]]), rv[ch]) for ch in chains}
    ro = {ch: _dot(_bf(scores[ch]), rv[ch]) + inter[ch] for ch in chains}
    for ch in chains:
        s, h = ch
        rstate_ref[slot[ch]] = rstate[ch] * cdec_ref[h] + kv[ch]
        o = ro[ch]
        o = o * lax.rsqrt(jnp.mean(o * o, axis=-1, keepdims=True) + EPS)
        ret_ref[s, :, hsl[h]] = _bf(_silu(rg_ref[s, :, hsl[h]].astype(F32)) * o)

    neg_a_row = -jnp.exp(prow_ref[0:1, :])
    neg_a_col = -jnp.exp(pcol_ref[:, 0:1])
    beta_cols, gc_cols, egc_cols, gc_rows = [], [], [], []
    for s in seqs:
        gab = gab_ref[s]
        beta_cols.append(_sigmoid(gab))
        gc_cols.append(_dot_f32(tril_ref[...], neg_a_row * _softplus(gab + prow_ref[1:2, :])))
        egc_cols.append(jnp.exp(gc_cols[s]))
        gc_rows.append(_dot_f32(neg_a_col * _softplus(gabt_refs[s][...] + pcol_ref[:, 1:2]), triu_ref[...]))

    ri = lax.broadcasted_iota(jnp.int32, (c, 2 * c), 0)
    lane = lax.broadcasted_iota(jnp.int32, (c, 2 * c), 1)
    ci = lane & (c - 1)
    left = lane < c
    ge = ri >= ci
    gt = ri > ci
    eye = jnp.where(ri == ci, 1.0, 0.0).astype(F32)
    level_masks = []
    for lg in range(int(math.log2(c))):
        b = 1 << lg
        same_block = (ri >> (lg + 1)) == (ci >> (lg + 1))
        level_masks.append(same_block & ((ri & (2 * b - 1)) >= b) & ((ci & (2 * b - 1)) < b))

    qh, kh, vh = {}, {}, {}
    for s, h in chains:
        q = gqkv_ref[s, :, hsl[h]].astype(F32)
        k = gqkv_ref[s, :, HEADS_W + h * HEAD_DIM:HEADS_W + (h + 1) * HEAD_DIM].astype(F32)
        vh[s, h] = gqkv_ref[s, :, 2 * HEADS_W + h * HEAD_DIM:2 * HEADS_W + (h + 1) * HEAD_DIM].astype(F32)
        qh[s, h] = q * lax.rsqrt(jnp.sum(q * q, axis=-1, keepdims=True) + EPS) * (HEAD_DIM ** -0.5)
        kh[s, h] = k * lax.rsqrt(jnp.sum(k * k, axis=-1, keepdims=True) + EPS)

    rows = [slice(n * c, (n + 1) * c) for n in range(n_chunks)]
    uw, wq, kd, eg, attn = {}, {}, {}, {}, {}
    for g0 in range(0, len(inst), MIXER_GROUP):
        grp = inst[g0:g0 + MIXER_GROUP]
        kc = {(s, h, n): kh[s, h][rows[n]] for s, h, n in grp}
        qc = {(s, h, n): qh[s, h][rows[n]] for s, h, n in grp}
        bcol = {(s, h, n): beta_cols[s][rows[n], 4 + h:5 + h] for s, h, n in grp}
        gcol = {(s, h, n): gc_cols[s][rows[n], h:h + 1] for s, h, n in grp}
        egcol = {(s, h, n): egc_cols[s][rows[n], h:h + 1] for s, h, n in grp}
        glast = {(s, h, n): gc_cols[s][n * c + c - 1:(n + 1) * c, h:h + 1] for s, h, n in grp}
        grow = {(s, h, n): gc_rows[s][h:h + 1, 2 * c * n:2 * c * (n + 1)] for s, h, n in grp}
        kb = {i: kc[i] * bcol[i] for i in grp}
        kcb = {i: _bf(kc[i]) for i in grp}
        kcb2 = {i: jnp.concatenate([kcb[i], kcb[i]], axis=0) for i in grp}
        decay = {i: jnp.where(ge, jnp.exp(jnp.where(ge, gcol[i] - grow[i], 0.0)), 0.0) for i in grp}
        a2 = {i: jnp.where(gt, _dot_nt(_bf(kb[i]), kcb2[i]) * decay[i], 0.0) for i in grp}
        for i in grp:
            attn[i] = _bf(jnp.where(ge, _dot_nt(_bf(qc[i]), kcb2[i]) * decay[i], 0.0)[:, :c])

        x2 = {i: eye - jnp.where(level_masks[0], a2[i], 0.0) for i in grp}
        a_hl = {i: _hi_lo(a2[i]) for i in grp}
        for m in level_masks[1:]:
            mb = jnp.where(m, 1.0, 0.0).astype(BF16)
            x_hl = {i: _hi_lo(x2[i]) for i in grp}
            t = {i: _dot(_dup_lhs(x2[i], x_hl[i][2], left), _dup_rhs(a_hl[i][0] * mb, a_hl[i][1] * mb)) for i in grp}
            t_hl = {i: _hi_lo(t[i]) for i in grp}
            x2 = {i: x2[i] - _dot(_dup_lhs(t[i], t_hl[i][2], left), _dup_rhs(x_hl[i][0], x_hl[i][1])) for i in grp}

        for i in grp:
            s, h, n = i
            rhs = _hi_lo(jnp.concatenate([vh[s, h][rows[n]] * bcol[i], kb[i] * egcol[i]], axis=1))
            uw[i] = _dot(_dup_lhs(x2[i], _hi_lo(x2[i])[2], left), _dup_rhs(rhs[0], rhs[1]))
        for i in grp:
            wq[i] = _bf(jnp.concatenate([uw[i][:, HEAD_DIM:], qc[i] * egcol[i]], axis=0))
            kd[i] = _bf(kc[i] * jnp.exp(glast[i] - gcol[i]))
            eg[i] = jnp.exp(glast[i])

    state = {ch: gstate_ref[slot[ch]] for ch in chains}
    outs = {ch: [] for ch in chains}
    for n in range(n_chunks):
        sb = {ch: _bf(state[ch]) for ch in chains}
        ws = {ch: _dot(wq[ch + (n,)], sb[ch]) for ch in chains}
        vnb = {ch: _bf(uw[ch + (n,)][:, :HEAD_DIM] - ws[ch][:c]) for ch in chains}
        for ch in chains:
            outs[ch].append(ws[ch][c:] + _dot(attn[ch + (n,)], vnb[ch]))
        state = {ch: state[ch] * eg[ch + (n,)] + _dot_tn(kd[ch + (n,)], vnb[ch]) for ch in chains}

    gnorm = gnorm_ref[...]
    for ch in chains:
        s, h = ch
        gstate_ref[slot[ch]] = state[ch]
        o = jnp.concatenate(outs[ch], axis=0)
        o = o * lax.rsqrt(jnp.mean(o * o, axis=-1, keepdims=True) + EPS)
        gdn_ref[s, :, hsl[h]] = _bf(o * gnorm * _silu(gz_ref[s, :, hsl[h]].astype(F32)))


def _token_mixers(conv, proj, gab, gabt, tabs, prow, pcol, gnorm, batch, seq, blk, nb):
    conv3 = conv.reshape(batch, seq, CONV_CH)
    proj3 = proj.reshape(batch, seq, MAIN_COLS - CONV_CH)
    gab3 = gab.reshape(batch, seq, LANES)
    n_blk = seq // blk
    cos2, sin2, dmat, qdec, kdec, cdec, tril, triu = tabs
    w512 = lambda col: pl.BlockSpec((nb, blk, HEADS_W), lambda b, n, col=col: (b, n, col))
    full = lambda shape: pl.BlockSpec(shape, lambda b, n: (0,) * len(shape))
    gabt_specs = [pl.BlockSpec((SMALL_ROWS, blk), lambda b, n, s=s: (0, (b * nb + s) * n_blk + n)) for s in range(nb)]
    out = pl.pallas_call(
        functools.partial(_mixer_kernel, blk=blk, nb=nb),
        grid=(batch // nb, n_blk),
        in_specs=[
            pl.BlockSpec((nb, blk, CONV_CH), lambda b, n: (b, n, 0)),
            w512(4), w512(5), w512(6), w512(7), w512(8),
            pl.BlockSpec((nb, blk, LANES), lambda b, n: (b, n, 0)),
            *gabt_specs,
            pl.BlockSpec((blk, HEAD_DIM), lambda b, n: (n, 0)),
            pl.BlockSpec((blk, HEAD_DIM), lambda b, n: (n, 0)),
            full((RET_HEADS, blk, blk)),
            full((RET_HEADS, blk, HEAD_DIM)),
            full((RET_HEADS, blk, HEAD_DIM)),
            full((RET_HEADS, 1, HEAD_DIM)),
            full((8, LANES)),
            full((SMALL_ROWS, LANES)),
            full((1, HEAD_DIM)),
            full((blk, blk)),
            full((blk, 2 * blk)),
        ],
        out_specs=[
            pl.BlockSpec((nb, blk, HEADS_W), lambda b, n: (b, n, 0)),
            pl.BlockSpec((nb, blk, HEADS_W), lambda b, n: (b, n, 0)),
        ],
        out_shape=[
            jax.ShapeDtypeStruct((batch, seq, HEADS_W), BF16),
            jax.ShapeDtypeStruct((batch, seq, HEADS_W), BF16),
        ],
        scratch_shapes=[
            pltpu.VMEM((nb * RET_HEADS, HEAD_DIM, HEAD_DIM), F32),
            pltpu.VMEM((nb * GDN_HEADS, HEAD_DIM, HEAD_DIM), F32),
        ],
        compiler_params=_params("parallel", "arbitrary"),
        name="token_mixers",
    )(conv3, proj3, proj3, proj3, proj3, proj3, gab3, *([gabt] * nb),
      cos2, sin2, dmat, qdec, kdec, cdec, prow, pcol, gnorm, tril, triu)
    ret, gdn = out
    return ret.reshape(batch * seq, HEADS_W), gdn.reshape(batch * seq, HEADS_W)


def _mixer_tables(seq, blk):
    inv_freq = ROPE_BASE ** (-jnp.arange(0, HEAD_DIM, 2, dtype=F32) / HEAD_DIM)
    ang = jnp.arange(seq, dtype=F32)[:, None] * inv_freq[None, :]
    cos, sin = jnp.cos(ang), jnp.sin(ang)
    cos2 = jnp.concatenate([cos, cos], axis=-1)
    sin2 = jnp.concatenate([-sin, sin], axis=-1)
    log_gamma = jnp.log(1.0 - jnp.exp2(-5.0 - jnp.arange(RET_HEADS, dtype=F32)))
    pos = jnp.arange(blk, dtype=F32)
    diff = pos[:, None] - pos[None, :]
    dmat = jnp.where(diff >= 0, jnp.exp(jnp.maximum(diff, 0.0)[None] * log_gamma[:, None, None]), 0.0)
    ones = jnp.ones((1, 1, HEAD_DIM), F32)
    qdec = jnp.exp((pos + 1.0)[None, :] * log_gamma[:, None])[:, :, None] * ones
    kdec = jnp.exp((blk - 1 - pos)[None, :] * log_gamma[:, None])[:, :, None] * ones
    cdec = jnp.exp(blk * log_gamma)[:, None, None] * ones
    idx = np.arange(blk)
    same = (idx[:, None] // GDN_CHUNK) == (idx[None, :] // GDN_CHUNK)
    tril = jnp.asarray((same & (idx[:, None] >= idx[None, :])).astype(np.float32))
    col = np.arange(2 * blk)
    col_chunk, col_pos = col // (2 * GDN_CHUNK), col % GDN_CHUNK
    triu = jnp.asarray(((idx[:, None] // GDN_CHUNK == col_chunk[None, :])
                        & (idx[:, None] % GDN_CHUNK <= col_pos[None, :])).astype(np.float32))
    return cos2, sin2, dmat, qdec, kdec, cdec, tril, triu


def _swiglu_rows(x, wg_ref, wu_ref, wd_ref, acc, tf):
    for f in range(D_FF // tf):
        cols = slice(f * tf, (f + 1) * tf)
        act = _silu(_dot(x, wg_ref[0, :, cols])) * _dot(x, wu_ref[0, :, cols])
        acc = acc + _dot(_bf(act), wd_ref[0, cols, :])
    return acc


def _merged_rows(ret_ref, gdn_ref, ga_ref, gb_ref, h_ref, wa_ref, wb_ref, wo_ref, nw_ref):
    ya = _dot(ret_ref[...], wa_ref[...])
    yb = _dot(gdn_ref[...], wb_ref[...])
    merged = _sigmoid(ga_ref[...].astype(F32)) * ya + _sigmoid(gb_ref[...].astype(F32)) * yb
    h_new = h_ref[...] + _dot(_bf(merged), wo_ref[...])
    return h_new, _bf(_rms_rows(h_new, nw_ref[...]))


def _merge_ffn_kernel(*refs, tf):
    wg_ref, wu_ref, wd_ref, o_ref = refs[9:]
    h_new, hn = _merged_rows(*refs[:9])
    o_ref[...] = _swiglu_rows(hn, wg_ref, wu_ref, wd_ref, h_new, tf)


def _merge_router_kernel(*refs):
    wr_ref, tri_ref, ho_ref, meta_ref, meta_t_ref, count_ref, run_ref = refs[9:]

    @pl.when(pl.program_id(0) == 0)
    def _():
        run_ref[...] = jnp.zeros_like(run_ref)

    h_new, hn = _merged_rows(*refs[:9])
    ho_ref[...] = h_new
    meta, run_new = _route_rows(_dot(hn, wr_ref[...]), tri_ref, run_ref[0:1, :])
    run_ref[0:1, :] = run_new
    count_ref[...] = jnp.broadcast_to(run_new, count_ref.shape)
    meta_ref[...] = meta
    meta_t_ref[...] = meta.T[:8, :]


def _resident(shape):
    return pl.BlockSpec(shape, lambda i: (0,) * len(shape), pipeline_mode=pl.Buffered(1))


def _merge_specs(tm):
    rows = lambda width, col: pl.BlockSpec((tm, width), lambda i: (i, col))
    return [rows(HEADS_W, 0), rows(HEADS_W, 0), rows(D_MODEL, 0), rows(D_MODEL, 1), rows(D_MODEL, 0),
            _resident((HEADS_W, D_MODEL)), _resident((HEADS_W, D_MODEL)), _resident((D_MODEL, D_MODEL)),
            _resident((1, D_MODEL))]


def _merge_dense_ffn(ret, gdn, proj, h, wa, wb, wo, norm_w, wg, wu, wd, tm, tf):
    t = h.shape[0]
    return pl.pallas_call(
        functools.partial(_merge_ffn_kernel, tf=tf),
        grid=(t // tm,),
        in_specs=_merge_specs(tm) + [_resident((1, D_MODEL, D_FF)), _resident((1, D_MODEL, D_FF)),
                                     _resident((1, D_FF, D_MODEL))],
        out_specs=pl.BlockSpec((tm, D_MODEL), lambda i: (i, 0)),
        out_shape=jax.ShapeDtypeStruct((t, D_MODEL), F32),
        compiler_params=_params("parallel"),
        name="merge_dense_ffn",
    )(ret, gdn, proj, proj, h, wa, wb, wo, norm_w, wg, wu, wd)


def _merge_router(ret, gdn, proj, h, wa, wb, wo, norm_w, router_w, tm):
    t = h.shape[0]
    idx = np.arange(tm)
    tri = jnp.asarray((idx[:, None] > idx[None, :]).astype(np.float32), dtype=BF16)
    return pl.pallas_call(
        _merge_router_kernel,
        grid=(t // tm,),
        in_specs=_merge_specs(tm) + [_resident((D_MODEL, LANES)), _resident((tm, tm))],
        out_specs=[pl.BlockSpec((tm, D_MODEL), lambda i: (i, 0)),
                   pl.BlockSpec((tm, LANES), lambda i: (i, 0)),
                   pl.BlockSpec((8, tm), lambda i: (0, i)),
                   pl.BlockSpec((8, LANES), lambda i: (0, 0))],
        out_shape=[jax.ShapeDtypeStruct((t, D_MODEL), F32),
                   jax.ShapeDtypeStruct((t, LANES), F32),
                   jax.ShapeDtypeStruct((8, t), F32),
                   jax.ShapeDtypeStruct((8, LANES), F32)],
        scratch_shapes=[pltpu.VMEM((8, LANES), F32)],
        compiler_params=_params("arbitrary"),
        name="merge_router",
    )(ret, gdn, proj, proj, h, wa, wb, wo, norm_w, router_w, tri)


def _route_rows(logits, tri_ref, before_tile):
    tm = logits.shape[0]
    lane = lax.broadcasted_iota(jnp.int32, (tm, LANES), 1)
    neg = jnp.float32(-jnp.inf)
    l1 = jnp.where(lane < N_EXPERTS, logits, neg)
    m1 = jnp.max(l1, axis=-1, keepdims=True)
    i1 = jnp.min(jnp.where(l1 == m1, lane, LANES), axis=-1, keepdims=True)
    l2 = jnp.where(lane == i1, neg, l1)
    m2 = jnp.max(l2, axis=-1, keepdims=True)
    i2 = jnp.min(jnp.where(l2 == m2, lane, LANES), axis=-1, keepdims=True)
    e2 = jnp.exp(m2 - m1)
    g1 = 1.0 / (1.0 + e2)
    g2 = e2 / (1.0 + e2)
    sel1 = lane == i1
    sel2 = lane == i2
    onehot = jnp.where(sel1 | sel2, 1.0, 0.0).astype(F32)
    before = _dot(tri_ref[...], _bf(onehot)) + before_tile
    r1 = jnp.sum(jnp.where(sel1, before, 0.0), axis=-1, keepdims=True)
    r2 = jnp.sum(jnp.where(sel2, before, 0.0), axis=-1, keepdims=True)
    meta = jnp.where(lane == 0, i1.astype(F32), 0.0)
    meta = jnp.where(lane == 1, i2.astype(F32), meta)
    meta = jnp.where(lane == 2, r1, meta)
    meta = jnp.where(lane == 3, r2, meta)
    meta = jnp.where(lane == 4, g1, meta)
    meta = jnp.where(lane == 5, g2, meta)
    return meta, before_tile + jnp.sum(onehot, axis=0, keepdims=True)


def _row_wait(src_hbm, dst_vmem, sem, rows):
    pltpu.make_async_copy(src_hbm.at[pl.ds(0, rows)], dst_vmem, sem).wait()


def _dispatch_kernel(sched_ref, dest_ref, h_ref, nw_ref, xg_ref, hn_ref, zero_ref, sem, zsem, *, rows):
    tm = h_ref.shape[0]

    @pl.when(pl.program_id(0) == 0)
    def _():
        zero_ref[...] = jnp.zeros_like(zero_ref)
        for e in range(N_EXPERTS):
            @pl.when(sched_ref[N_EXPERTS + e] > 0)
            def _():
                start = pl.multiple_of(sched_ref[e] - rows, rows)
                clear = pltpu.make_async_copy(zero_ref, xg_ref.at[pl.ds(start, rows)], zsem)
                clear.start()
                clear.wait()

        def clear_unused(b, carry):
            clear = pltpu.make_async_copy(zero_ref, xg_ref.at[pl.ds(pl.multiple_of(b * rows, rows), rows)], zsem)
            clear.start()
            clear.wait()
            return carry

        lax.fori_loop(sched_ref[2 * N_EXPERTS], xg_ref.shape[0] // rows, clear_unused, 0)

    step = pl.program_id(0)
    slot = step % 2
    rows_ref = hn_ref.at[slot]
    rows_ref[...] = _rms_rows(h_ref[...], nw_ref[...])

    def body(r, carry):
        for j in range(TOP_K):
            d = dest_ref[j * tm + r]
            pltpu.make_async_copy(rows_ref.at[pl.ds(r, 1)], xg_ref.at[pl.ds(d, 1)], sem.at[slot]).start(priority=j)
        return carry

    lax.fori_loop(0, tm, body, 0, unroll=ROW_DMA_UNROLL)

    def drain(s):
        for _ in range(TOP_K):
            pltpu.make_async_copy(hn_ref.at[s], xg_ref.at[pl.ds(0, tm)], sem.at[s]).wait()

    @pl.when(step > 0)
    def _():
        drain(1 - slot)

    @pl.when(step == pl.num_programs(0) - 1)
    def _():
        drain(slot)


def _dispatch(sched, dest_flat, h, norm_w, n_rows, tm, rows):
    t = h.shape[0]
    grid_spec = pltpu.PrefetchScalarGridSpec(
        num_scalar_prefetch=1,
        grid=(t // tm,),
        in_specs=[
            pl.BlockSpec((TOP_K * tm,), lambda i, sc: (i,), memory_space=pltpu.SMEM),
            pl.BlockSpec((tm, D_MODEL), lambda i, sc: (i, 0)),
            pl.BlockSpec((1, D_MODEL), lambda i, sc: (0, 0)),
        ],
        out_specs=pl.BlockSpec(memory_space=pl.ANY),
        scratch_shapes=[pltpu.VMEM((2, tm, D_MODEL), F32), pltpu.VMEM((rows, D_MODEL), F32),
                        pltpu.SemaphoreType.DMA((2,)), pltpu.SemaphoreType.DMA(())],
    )
    return pl.pallas_call(
        functools.partial(_dispatch_kernel, rows=rows),
        grid_spec=grid_spec,
        out_shape=jax.ShapeDtypeStruct((n_rows, D_MODEL), F32),
        compiler_params=_params("arbitrary"),
        name="dispatch",
    )(sched, dest_flat, h, norm_w)


def _expert_kernel(be_ref, sched_ref, x_ref, wg_ref, wu_ref, wd_ref, y_ref, *, tf):
    del be_ref

    @pl.when(pl.program_id(0) < sched_ref[2 * N_EXPERTS])
    def _():
        y_ref[...] = _swiglu_rows(_bf(x_ref[...]), wg_ref, wu_ref, wd_ref, jnp.zeros(y_ref.shape, F32), tf)

    @pl.when(pl.program_id(0) >= sched_ref[2 * N_EXPERTS])
    def _():
        y_ref[...] = jnp.zeros_like(y_ref)


def _experts(block_e, sched, xg, wg, wu, wd, rows, tf):
    n_rows = xg.shape[0]
    used = lambda i, sc: jnp.minimum(i, sc[2 * N_EXPERTS] - 1)
    grid_spec = pltpu.PrefetchScalarGridSpec(
        num_scalar_prefetch=2,
        grid=(n_rows // rows,),
        in_specs=[
            pl.BlockSpec((rows, D_MODEL), lambda i, be, sc: (used(i, sc), 0)),
            pl.BlockSpec((1, D_MODEL, D_FF), lambda i, be, sc: (be[i], 0, 0)),
            pl.BlockSpec((1, D_MODEL, D_FF), lambda i, be, sc: (be[i], 0, 0)),
            pl.BlockSpec((1, D_FF, D_MODEL), lambda i, be, sc: (be[i], 0, 0)),
        ],
        out_specs=pl.BlockSpec((rows, D_MODEL), lambda i, be, sc: (i, 0)),
    )
    return pl.pallas_call(
        functools.partial(_expert_kernel, tf=tf),
        grid_spec=grid_spec,
        out_shape=jax.ShapeDtypeStruct((n_rows, D_MODEL), F32),
        compiler_params=_params("arbitrary"),
        name="experts",
    )(block_e, sched, xg, wg, wu, wd)


def _combine_kernel(dest_ref, next_dest_ref, meta_ref, h_ref, nw_ref, y_ref, o_ref, buf_ref, sem):
    tm = h_ref.shape[0]
    step = pl.program_id(0)
    slot = step % 2

    def fetch(idx_ref, s):
        def body(r, carry):
            for j in range(TOP_K):
                d = idx_ref[j * tm + r]
                pltpu.make_async_copy(y_ref.at[pl.ds(d, 1)], buf_ref.at[s, j, pl.ds(r, 1)],
                                      sem.at[s]).start(priority=j)
            return carry

        lax.fori_loop(0, tm, body, 0, unroll=ROW_DMA_UNROLL)

    @pl.when(step == 0)
    def _():
        fetch(dest_ref, 0)

    @pl.when(step + 1 < pl.num_programs(0))
    def _():
        fetch(next_dest_ref, 1 - slot)

    for j in range(TOP_K):
        _row_wait(y_ref, buf_ref.at[slot, j], sem.at[slot], tm)
    meta = meta_ref[...]
    moe = meta[:, 4:5] * buf_ref[slot, 0] + meta[:, 5:6] * buf_ref[slot, 1]
    o_ref[...] = _rms_rows(h_ref[...] + moe, nw_ref[...])


def _combine(dest_flat, meta, h, norm_w, y, tm):
    t = h.shape[0]
    last = t // tm - 1
    return pl.pallas_call(
        _combine_kernel,
        grid=(t // tm,),
        in_specs=[
            pl.BlockSpec((TOP_K * tm,), lambda i: (i,), memory_space=pltpu.SMEM),
            pl.BlockSpec((TOP_K * tm,), lambda i: (jnp.minimum(i + 1, last),), memory_space=pltpu.SMEM),
            pl.BlockSpec((tm, LANES), lambda i: (i, 0)),
            pl.BlockSpec((tm, D_MODEL), lambda i: (i, 0)),
            pl.BlockSpec((1, D_MODEL), lambda i: (0, 0)),
            pl.BlockSpec(memory_space=pl.ANY),
        ],
        out_specs=pl.BlockSpec((tm, D_MODEL), lambda i: (i, 0)),
        out_shape=jax.ShapeDtypeStruct((t, D_MODEL), F32),
        scratch_shapes=[pltpu.VMEM((2, TOP_K, tm, D_MODEL), F32), pltpu.SemaphoreType.DMA((2,))],
        compiler_params=_params("arbitrary"),
        name="combine",
    )(dest_flat, dest_flat, meta, h, norm_w, y)


def _moe_layer(meta, meta_t, counts, h, ffn_norm_w, final_norm_w, wg, wu, wd, rows, tm_rows, tf):
    t = h.shape[0]
    counts = counts[0, :N_EXPERTS].astype(jnp.int32)
    padded = ((counts + rows - 1) // rows) * rows
    pend = jnp.cumsum(padded)
    pstart = pend - padded
    n_rows = t * TOP_K + N_EXPERTS * rows
    expert = meta_t[0:TOP_K].astype(jnp.int32)
    rank = meta_t[TOP_K:2 * TOP_K].astype(jnp.int32)
    dest = rank
    for e in range(N_EXPERTS):
        dest = dest + jnp.where(expert == e, pstart[e], 0)
    dest = dest.reshape(TOP_K, t // tm_rows, tm_rows).transpose(1, 0, 2).reshape(t * TOP_K)
    n_used = pend[N_EXPERTS - 1] // rows
    block_start = jnp.minimum(jnp.arange(n_rows // rows, dtype=jnp.int32), n_used - 1) * rows
    block_e = jnp.sum(block_start[:, None] >= pend[None, :], axis=1).astype(jnp.int32)
    sched = jnp.concatenate([pend, padded, n_used[None]]).astype(jnp.int32)
    xg = _dispatch(sched, dest, h, ffn_norm_w, n_rows, tm_rows, rows)
    y = _experts(block_e, sched, xg, wg, wu, wd, rows, tf)
    return _combine(dest, meta, h, final_norm_w, y, tm_rows)


def _pack_in_weights(w_in):
    o = np.cumsum((0, HEADS_W, HEADS_W, HEADS_W, HEADS_W, CONV_CH, HEADS_W, GDN_HEADS, GDN_HEADS, D_MODEL, D_MODEL))
    rq, rk, rv, rg, gqkv, gz, ga, gb, ma, mb = (w_in[:, o[i]:o[i + 1]] for i in range(10))
    slabs = [p[:, c:c + HEADS_W] for p in (gqkv, ma, mb, rq, rk, rv, rg, gz) for c in range(0, p.shape[1], HEADS_W)]
    main = _bf(jnp.stack(slabs, axis=0))
    small = jnp.concatenate([ga, gb], axis=1)
    small_cols = _bf(jnp.pad(small, ((0, 0), (0, LANES - 2 * GDN_HEADS))))
    small_rows = _bf(jnp.pad(small.T, ((0, SMALL_ROWS - 2 * GDN_HEADS), (0, 0))))
    return main, small_cols, small_rows


def _pick(n, prefs):
    for p in prefs:
        if n % p == 0:
            return p
    raise ValueError(f"no tile in {prefs} divides {n}")


def kernel(x, norm_mix, w_in, conv_w, a_log, dt_bias, gdn_norm, w_branch, w_out, norm_ffn,
           dense_w_gate, dense_w_up, dense_w_down, router, moe_w_gate, moe_w_up, moe_w_down, final_norm):
    batch, seq, d = x.shape
    depth = norm_mix.shape[0]
    assert d == D_MODEL and seq % GDN_CHUNK == 0
    assert depth % 2 == 0, "the final RMSNorm is fused into the last (routed) layer's combine"
    t = batch * seq
    blk = _pick(seq, (128, 64))
    nb = _pick(batch, (4, 2, 1))
    tm = _pick(seq, (1024, 512, 256))
    tm_rows = _pick(t, (512, 256))
    tf_dense = 256
    rows = 512
    tabs = _mixer_tables(seq, blk)

    h = x.reshape(t, d)
    out = None
    for layer in range(depth):
        w_main, w_small, w_small_t = _pack_in_weights(w_in[layer])
        convw = jnp.pad(conv_w[layer], ((0, 8 - CONV_K), (0, 0)))
        conv, proj, gab, gabt = _in_projection(h, norm_mix[layer][None, :], w_main, w_small, w_small_t, convw, tm, CONV_CH, seq)
        prow = jnp.zeros((8, LANES), F32)
        prow = prow.at[0, :GDN_HEADS].set(a_log[layer]).at[1, :GDN_HEADS].set(dt_bias[layer])
        pcol = jnp.zeros((SMALL_ROWS, LANES), F32)
        pcol = pcol.at[:GDN_HEADS, 0].set(a_log[layer]).at[:GDN_HEADS, 1].set(dt_bias[layer])
        ret, gdn = _token_mixers(conv, proj, gab, gabt, tabs, prow, pcol, gdn_norm[layer][None, :],
                                 batch, seq, blk, nb)
        i = layer // 2
        is_moe = layer % 2 == 1
        wa, wb, wo = _bf(w_branch[layer, 0]), _bf(w_branch[layer, 1]), _bf(w_out[layer])
        if not is_moe:
            h = _merge_dense_ffn(ret, gdn, proj, h, wa, wb, wo, norm_ffn[layer][None, :], _bf(dense_w_gate[i])[None],
                                 _bf(dense_w_up[i])[None], _bf(dense_w_down[i])[None], tm_rows, tf_dense)
        else:
            router_w = _bf(jnp.pad(router[i], ((0, 0), (0, LANES - N_EXPERTS))))
            h, meta, meta_t, counts = _merge_router(ret, gdn, proj, h, wa, wb, wo, norm_ffn[layer][None, :],
                                                    router_w, tm_rows)
            out = _moe_layer(meta, meta_t, counts, h, norm_ffn[layer][None, :], final_norm[None, :],
                             _bf(moe_w_gate[i]), _bf(moe_w_up[i]), _bf(moe_w_down[i]), rows, tm_rows, 256)
    return out.reshape(batch, seq, d)
```

```python
import functools
import math

import jax
import jax.numpy as jnp
import numpy as np
from jax import lax
from jax.experimental import pallas as pl
from jax.experimental.pallas import tpu as pltpu

F32 = jnp.float32
BF16 = jnp.bfloat16

D_MODEL = 1024
RET_HEADS = 4
GDN_HEADS = 4
HEAD_DIM = 128
HEADS_W = 512
CONV_K = 4
CONV_CH = 3 * HEADS_W
GDN_CHUNK = 64
ROPE_BASE = 10000.0
D_FF = 2816
N_EXPERTS = 8
TOP_K = 2
EPS = 1e-6
LANES = 128
SMALL_ROWS = 16
ROW_DMA_UNROLL = 8
CONV_ROWS = 256
MIXER_GROUP = 16

MAIN_COLS = CONV_CH + 5 * HEADS_W + 2 * D_MODEL
VMEM_LIMIT = 56 * 1024 * 1024

HIGHEST = lax.Precision.HIGHEST


def _bf(x):
    return x.astype(BF16)


def _dot(a, b):
    return jnp.dot(a, b, preferred_element_type=F32)


def _dot_nt(a, b):
    return lax.dot_general(a, b, (((1,), (1,)), ((), ())), preferred_element_type=F32)


def _dot_tn(a, b):
    return lax.dot_general(a, b, (((0,), (0,)), ((), ())), preferred_element_type=F32)


def _dot_f32(a, b):
    return jnp.dot(a, b, preferred_element_type=F32, precision=HIGHEST)


def _sigmoid(x):
    return 1.0 / (1.0 + jnp.exp(-x))


def _silu(x):
    return x * _sigmoid(x)


def _softplus(x):
    return jnp.maximum(x, 0.0) + jnp.log(1.0 + jnp.exp(-jnp.abs(x)))


def _rms_rows(x, w):
    ms = jnp.mean(x * x, axis=-1, keepdims=True)
    return x * lax.rsqrt(ms + EPS) * w


def _params(*sem):
    return pltpu.CompilerParams(dimension_semantics=sem, vmem_limit_bytes=VMEM_LIMIT)


def _inproj_kernel(x_ref, nw_ref, w_ref, ws_ref, wst_ref, cw_ref, oc_ref, o_ref, os_ref, ost_ref,
                   hn_ref, xs_ref, *, tiles_per_seq, n_col):
    tm = x_ref.shape[0]
    tile, step = pl.program_id(0), pl.program_id(1)

    @pl.when(step == 0)
    def _():
        @pl.when(tile % tiles_per_seq == 0)
        def _():
            xs_ref[0:8, :] = jnp.zeros((8, CONV_CH), F32)

        hn = None
        for r0 in range(0, tm + CONV_ROWS, CONV_ROWS):
            prev, prev_rows = hn, slice(r0 - CONV_ROWS, r0)
            if r0 < tm:
                rows = slice(r0, r0 + CONV_ROWS)
                hn = _bf(_rms_rows(x_ref[rows, :], nw_ref[...]))
                hn_ref[rows, :] = hn
            if prev is not None:
                for c in range(w_ref.shape[0]):
                    xs_ref[8 + r0 - CONV_ROWS:8 + r0, c * HEADS_W:(c + 1) * HEADS_W] = _dot(prev, w_ref[c])
                os_ref[prev_rows, :] = _dot(prev, ws_ref[...])
                ost_ref[:, prev_rows] = _dot_nt(wst_ref[...], prev)

    def conv_piece(r0, c0):
        cols = slice(c0, c0 + LANES)
        acc = cw_ref[CONV_K - 1:CONV_K, cols] * xs_ref[pl.ds(r0 + 8, CONV_ROWS), cols]
        for d in range(1, CONV_K):
            acc = acc + cw_ref[CONV_K - 1 - d:CONV_K - d, cols] * xs_ref[pl.ds(r0 + 8 - d, CONV_ROWS), cols]
        oc_ref[r0:r0 + CONV_ROWS, cols] = _bf(_silu(acc))

    conv_pieces = [(r0, c0) for r0 in range(0, tm, CONV_ROWS) for c0 in range(0, CONV_CH, LANES)]
    dot_pieces = [(r0, c0) for r0 in range(0, tm, CONV_ROWS) for c0 in range(0, o_ref.shape[1], HEADS_W)]
    share = -(-len(conv_pieces) // (n_col - 1))
    for k in range(1, n_col):
        @pl.when(step == k)
        def _(k=k):
            mine = conv_pieces[(k - 1) * share:k * share]
            per_dot = -(-len(mine) // len(dot_pieces))
            for j, (r0, c0) in enumerate(dot_pieces):
                rows, cols = slice(r0, r0 + CONV_ROWS), slice(c0, c0 + HEADS_W)
                o_ref[rows, cols] = _bf(_dot(hn_ref[rows, :], w_ref[c0 // HEADS_W]))
                for piece in mine[j * per_dot:(j + 1) * per_dot]:
                    conv_piece(*piece)
            if k == n_col - 1:
                xs_ref[0:8, :] = xs_ref[tm:tm + 8, :]


def _in_projection(h, norm_w, w_main, w_small, w_small_t, conv_w, tm, tn, seq):
    t = h.shape[0]
    assert tn == CONV_CH and seq % tm == 0 and tm % CONV_ROWS == 0
    n_col = MAIN_COLS // tn
    return pl.pallas_call(
        functools.partial(_inproj_kernel, tiles_per_seq=seq // tm, n_col=n_col),
        grid=(t // tm, n_col),
        in_specs=[
            pl.BlockSpec((tm, D_MODEL), lambda i, j: (i, 0)),
            pl.BlockSpec((1, D_MODEL), lambda i, j: (0, 0)),
            pl.BlockSpec((tn // HEADS_W, D_MODEL, HEADS_W), lambda i, j: (j, 0, 0)),
            pl.BlockSpec((D_MODEL, LANES), lambda i, j: (0, 0)),
            pl.BlockSpec((SMALL_ROWS, D_MODEL), lambda i, j: (0, 0)),
            pl.BlockSpec((8, CONV_CH), lambda i, j: (0, 0)),
        ],
        out_specs=[
            pl.BlockSpec((tm, CONV_CH), lambda i, j: (i, 0)),
            pl.BlockSpec((tm, tn), lambda i, j: (i, jnp.maximum(j - 1, 0))),
            pl.BlockSpec((tm, LANES), lambda i, j: (i, 0)),
            pl.BlockSpec((SMALL_ROWS, tm), lambda i, j: (0, i)),
        ],
        out_shape=[
            jax.ShapeDtypeStruct((t, CONV_CH), BF16),
            jax.ShapeDtypeStruct((t, MAIN_COLS - CONV_CH), BF16),
            jax.ShapeDtypeStruct((t, LANES), F32),
            jax.ShapeDtypeStruct((SMALL_ROWS, t), F32),
        ],
        scratch_shapes=[pltpu.VMEM((tm, D_MODEL), BF16), pltpu.VMEM((tm + 8, CONV_CH), F32)],
        compiler_params=_params("arbitrary", "arbitrary"),
        name="in_projection",
    )(h, norm_w, w_main, w_small, w_small_t, conv_w)


def _hi_lo(x):
    hi = _bf(x)
    lo_f = x - hi.astype(F32)
    return hi, _bf(lo_f), lo_f


def _three_lhs(hi, lo):
    return jnp.concatenate([hi, lo, hi], axis=1)


def _three_rhs(hi, lo):
    return jnp.concatenate([hi, hi, lo], axis=0)


def _block_diag(m_pair, half_l, half_r):
    return jnp.concatenate([m_pair * half_l, m_pair * half_r], axis=0)


def _mixer_kernel(*refs, blk, nb):
    gqkv_ref, rq_ref, rk_ref, rv_ref, rg_ref, gz_ref, gab_ref = refs[:7]
    gabt_refs = refs[7:7 + nb]
    (cos_ref, sin_ref, dmat_ref, qdec_ref, kdec_ref, cdec_ref, prow_ref, pcol_ref, gnorm_ref, tril_ref, triu_ref,
     ret_ref, gdn_ref, rstate_ref, gstate_ref) = refs[7 + nb:]
    c = GDN_CHUNK
    n_chunks = blk // c
    seqs = range(nb)
    chains = [(s, h) for s in seqs for h in range(GDN_HEADS)]
    inst = [(s, h, n) for s, h in chains for n in range(n_chunks)]
    slot = {ch: ch[0] * GDN_HEADS + ch[1] for ch in chains}

    @pl.when(pl.program_id(1) == 0)
    def _():
        rstate_ref[...] = jnp.zeros_like(rstate_ref)
        gstate_ref[...] = jnp.zeros_like(gstate_ref)

    hsl = [slice(h * HEAD_DIM, (h + 1) * HEAD_DIM) for h in range(RET_HEADS)]

    cos2 = cos_ref[...]
    sin2 = sin_ref[...]
    half = HEAD_DIM // 2
    rq = {(s, h): rq_ref[s, :, hsl[h]].astype(F32) for s, h in chains}
    rk = {(s, h): rk_ref[s, :, hsl[h]].astype(F32) for s, h in chains}
    rv = {(s, h): rv_ref[s, :, hsl[h]] for s, h in chains}
    rq = {ch: q * cos2 + pltpu.roll(q, half, 1) * sin2 for ch, q in rq.items()}
    rk = {ch: (k * cos2 + pltpu.roll(k, half, 1) * sin2) * (HEAD_DIM ** -0.5) for ch, k in rk.items()}
    rstate = {ch: rstate_ref[slot[ch]] for ch in chains}
    scores = {ch: _dot_nt(_bf(rq[ch]), _bf(rk[ch])) * dmat_ref[ch[1]] for ch in chains}
    inter = {ch: _dot(_bf(rq[ch] * qdec_ref[ch[1]]), _bf(rstate[ch])) for ch in chains}
    kv = {ch: _dot_tn(_bf(rk[ch] * kdec_ref[ch[1]]), rv[ch]) for ch in chains}
    ro = {ch: _dot(_bf(scores[ch]), rv[ch]) + inter[ch] for ch in chains}
    for ch in chains:
        s, h = ch
        rstate_ref[slot[ch]] = rstate[ch] * cdec_ref[h] + kv[ch]
        o = ro[ch]
        o = o * lax.rsqrt(jnp.mean(o * o, axis=-1, keepdims=True) + EPS)
        ret_ref[s, :, hsl[h]] = _bf(_silu(rg_ref[s, :, hsl[h]].astype(F32)) * o)

    neg_a_row = -jnp.exp(prow_ref[0:1, :])
    neg_a_col = -jnp.exp(pcol_ref[:, 0:1])
    beta_cols, gc_cols, egc_cols, gc_rows = [], [], [], []
    for s in seqs:
        gab = gab_ref[s]
        beta_cols.append(_sigmoid(gab))
        gc_cols.append(_dot_f32(tril_ref[...], neg_a_row * _softplus(gab + prow_ref[1:2, :])))
        egc_cols.append(jnp.exp(gc_cols[s]))
        gc_rows.append(_dot_f32(neg_a_col * _softplus(gabt_refs[s][...] + pcol_ref[:, 1:2]), triu_ref[...]))

    ri = lax.broadcasted_iota(jnp.int32, (c, 2 * c), 0)
    lane = lax.broadcasted_iota(jnp.int32, (c, 2 * c), 1)
    ci = lane & (c - 1)
    left = lane < c
    half_l = jnp.where(left, 1.0, 0.0).astype(BF16)
    half_r = jnp.where(left, 0.0, 1.0).astype(BF16)
    ge = ri >= ci
    gt = ri > ci
    eye = jnp.where(ri == ci, 1.0, 0.0).astype(F32)
    level_masks = []
    for lg in range(int(math.log2(c))):
        b = 1 << lg
        same_block = (ri >> (lg + 1)) == (ci >> (lg + 1))
        level_masks.append(same_block & ((ri & (2 * b - 1)) >= b) & ((ci & (2 * b - 1)) < b))

    qh, kh, vh = {}, {}, {}
    for s, h in chains:
        q = gqkv_ref[s, :, hsl[h]].astype(F32)
        k = gqkv_ref[s, :, HEADS_W + h * HEAD_DIM:HEADS_W + (h + 1) * HEAD_DIM].astype(F32)
        vh[s, h] = gqkv_ref[s, :, 2 * HEADS_W + h * HEAD_DIM:2 * HEADS_W + (h + 1) * HEAD_DIM].astype(F32)
        qh[s, h] = q * lax.rsqrt(jnp.sum(q * q, axis=-1, keepdims=True) + EPS) * (HEAD_DIM ** -0.5)
        kh[s, h] = k * lax.rsqrt(jnp.sum(k * k, axis=-1, keepdims=True) + EPS)

    rows = [slice(n * c, (n + 1) * c) for n in range(n_chunks)]
    uw, wq, kd, eg, attn = {}, {}, {}, {}, {}
    for g0 in range(0, len(inst), MIXER_GROUP):
        grp = inst[g0:g0 + MIXER_GROUP]
        kc = {(s, h, n): kh[s, h][rows[n]] for s, h, n in grp}
        qc = {(s, h, n): qh[s, h][rows[n]] for s, h, n in grp}
        bcol = {(s, h, n): beta_cols[s][rows[n], 4 + h:5 + h] for s, h, n in grp}
        gcol = {(s, h, n): gc_cols[s][rows[n], h:h + 1] for s, h, n in grp}
        egcol = {(s, h, n): egc_cols[s][rows[n], h:h + 1] for s, h, n in grp}
        glast = {(s, h, n): gc_cols[s][n * c + c - 1:(n + 1) * c, h:h + 1] for s, h, n in grp}
        grow = {(s, h, n): gc_rows[s][h:h + 1, 2 * c * n:2 * c * (n + 1)] for s, h, n in grp}
        kb = {i: kc[i] * bcol[i] for i in grp}
        kcb = {i: _bf(kc[i]) for i in grp}
        kcb2 = {i: jnp.concatenate([kcb[i], kcb[i]], axis=0) for i in grp}
        decay = {i: jnp.where(ge, jnp.exp(jnp.where(ge, gcol[i] - grow[i], 0.0)), 0.0) for i in grp}
        a2 = {i: jnp.where(gt, _dot_nt(_bf(kb[i]), kcb2[i]) * decay[i], 0.0) for i in grp}
        for i in grp:
            attn[i] = _bf(jnp.where(ge, _dot_nt(_bf(qc[i]), kcb2[i]) * decay[i], 0.0)[:, :c])

        pairs = [(grp[j], grp[j + 1]) for j in range(0, len(grp), 2)]
        a_pair = {p: jnp.where(left, a2[p[0]], a2[p[1]]) for p in pairs}
        z = {p: eye - jnp.where(level_masks[0], a_pair[p], 0.0) for p in pairs}
        a_hl = {p: _hi_lo(a_pair[p]) for p in pairs}
        for m in level_masks[1:]:
            mb = jnp.where(m, 1.0, 0.0).astype(BF16)
            z_hl = {p: _hi_lo(z[p]) for p in pairs}
            t = {p: _dot(_three_lhs(z_hl[p][0], z_hl[p][1]),
                         _three_rhs(_block_diag(a_hl[p][0] * mb, half_l, half_r), _block_diag(a_hl[p][1] * mb, half_l, half_r)))
                 for p in pairs}
            t_hl = {p: _hi_lo(t[p]) for p in pairs}
            z = {p: z[p] - _dot(_three_lhs(t_hl[p][0], t_hl[p][1]),
                                _three_rhs(_block_diag(z_hl[p][0], half_l, half_r), _block_diag(z_hl[p][1], half_l, half_r)))
                 for p in pairs}

        for p in pairs:
            stacked = []
            for i in p:
                s, h, n = i
                stacked.append(jnp.concatenate([vh[s, h][rows[n]] * bcol[i], kb[i] * egcol[i]], axis=1))
            r_hi, r_lo, _ = _hi_lo(jnp.concatenate(stacked, axis=0))
            rr = _three_rhs(r_hi, r_lo)
            z_hi, z_lo, _ = _hi_lo(z[p])
            uw[p[0]] = _dot(_three_lhs(z_hi * half_l, z_lo * half_l), rr)
            uw[p[1]] = _dot(_three_lhs(z_hi * half_r, z_lo * half_r), rr)
        for i in grp:
            wq[i] = _bf(jnp.concatenate([uw[i][:, HEAD_DIM:], qc[i] * egcol[i]], axis=0))
            kd[i] = _bf(kc[i] * jnp.exp(glast[i] - gcol[i]))
            eg[i] = jnp.exp(glast[i])

    state = {ch: gstate_ref[slot[ch]] for ch in chains}
    outs = {ch: [] for ch in chains}
    for n in range(n_chunks):
        sb = {ch: _bf(state[ch]) for ch in chains}
        ws = {ch: _dot(wq[ch + (n,)], sb[ch]) for ch in chains}
        vnb = {ch: _bf(uw[ch + (n,)][:, :HEAD_DIM] - ws[ch][:c]) for ch in chains}
        for ch in chains:
            outs[ch].append(ws[ch][c:] + _dot(attn[ch + (n,)], vnb[ch]))
        state = {ch: state[ch] * eg[ch + (n,)] + _dot_tn(kd[ch + (n,)], vnb[ch]) for ch in chains}

    gnorm = gnorm_ref[...]
    for ch in chains:
        s, h = ch
        gstate_ref[slot[ch]] = state[ch]
        o = jnp.concatenate(outs[ch], axis=0)
        o = o * lax.rsqrt(jnp.mean(o * o, axis=-1, keepdims=True) + EPS)
        gdn_ref[s, :, hsl[h]] = _bf(o * gnorm * _silu(gz_ref[s, :, hsl[h]].astype(F32)))


def _token_mixers(conv, proj, gab, gabt, tabs, prow, pcol, gnorm, batch, seq, blk, nb):
    conv3 = conv.reshape(batch, seq, CONV_CH)
    proj3 = proj.reshape(batch, seq, MAIN_COLS - CONV_CH)
    gab3 = gab.reshape(batch, seq, LANES)
    n_blk = seq // blk
    cos2, sin2, dmat, qdec, kdec, cdec, tril, triu = tabs
    w512 = lambda col: pl.BlockSpec((nb, blk, HEADS_W), lambda b, n, col=col: (b, n, col))
    full = lambda shape: pl.BlockSpec(shape, lambda b, n: (0,) * len(shape))
    gabt_specs = [pl.BlockSpec((SMALL_ROWS, blk), lambda b, n, s=s: (0, (b * nb + s) * n_blk + n)) for s in range(nb)]
    out = pl.pallas_call(
        functools.partial(_mixer_kernel, blk=blk, nb=nb),
        grid=(batch // nb, n_blk),
        in_specs=[
            pl.BlockSpec((nb, blk, CONV_CH), lambda b, n: (b, n, 0)),
            w512(4), w512(5), w512(6), w512(7), w512(8),
            pl.BlockSpec((nb, blk, LANES), lambda b, n: (b, n, 0)),
            *gabt_specs,
            pl.BlockSpec((blk, HEAD_DIM), lambda b, n: (n, 0)),
            pl.BlockSpec((blk, HEAD_DIM), lambda b, n: (n, 0)),
            full((RET_HEADS, blk, blk)),
            full((RET_HEADS, blk, HEAD_DIM)),
            full((RET_HEADS, blk, HEAD_DIM)),
            full((RET_HEADS, 1, HEAD_DIM)),
            full((8, LANES)),
            full((SMALL_ROWS, LANES)),
            full((1, HEAD_DIM)),
            full((blk, blk)),
            full((blk, 2 * blk)),
        ],
        out_specs=[
            pl.BlockSpec((nb, blk, HEADS_W), lambda b, n: (b, n, 0)),
            pl.BlockSpec((nb, blk, HEADS_W), lambda b, n: (b, n, 0)),
        ],
        out_shape=[
            jax.ShapeDtypeStruct((batch, seq, HEADS_W), BF16),
            jax.ShapeDtypeStruct((batch, seq, HEADS_W), BF16),
        ],
        scratch_shapes=[
            pltpu.VMEM((nb * RET_HEADS, HEAD_DIM, HEAD_DIM), F32),
            pltpu.VMEM((nb * GDN_HEADS, HEAD_DIM, HEAD_DIM), F32),
        ],
        compiler_params=_params("parallel", "arbitrary"),
        name="token_mixers",
    )(conv3, proj3, proj3, proj3, proj3, proj3, gab3, *([gabt] * nb),
      cos2, sin2, dmat, qdec, kdec, cdec, prow, pcol, gnorm, tril, triu)
    ret, gdn = out
    return ret.reshape(batch * seq, HEADS_W), gdn.reshape(batch * seq, HEADS_W)


def _mixer_tables(seq, blk):
    inv_freq = ROPE_BASE ** (-jnp.arange(0, HEAD_DIM, 2, dtype=F32) / HEAD_DIM)
    ang = jnp.arange(seq, dtype=F32)[:, None] * inv_freq[None, :]
    cos, sin = jnp.cos(ang), jnp.sin(ang)
    cos2 = jnp.concatenate([cos, cos], axis=-1)
    sin2 = jnp.concatenate([-sin, sin], axis=-1)
    log_gamma = jnp.log(1.0 - jnp.exp2(-5.0 - jnp.arange(RET_HEADS, dtype=F32)))
    pos = jnp.arange(blk, dtype=F32)
    diff = pos[:, None] - pos[None, :]
    dmat = jnp.where(diff >= 0, jnp.exp(jnp.maximum(diff, 0.0)[None] * log_gamma[:, None, None]), 0.0)
    ones = jnp.ones((1, 1, HEAD_DIM), F32)
    qdec = jnp.exp((pos + 1.0)[None, :] * log_gamma[:, None])[:, :, None] * ones
    kdec = jnp.exp((blk - 1 - pos)[None, :] * log_gamma[:, None])[:, :, None] * ones
    cdec = jnp.exp(blk * log_gamma)[:, None, None] * ones
    idx = np.arange(blk)
    same = (idx[:, None] // GDN_CHUNK) == (idx[None, :] // GDN_CHUNK)
    tril = jnp.asarray((same & (idx[:, None] >= idx[None, :])).astype(np.float32))
    col = np.arange(2 * blk)
    col_chunk, col_pos = col // (2 * GDN_CHUNK), col % GDN_CHUNK
    triu = jnp.asarray(((idx[:, None] // GDN_CHUNK == col_chunk[None, :])
                        & (idx[:, None] % GDN_CHUNK <= col_pos[None, :])).astype(np.float32))
    return cos2, sin2, dmat, qdec, kdec, cdec, tril, triu


def _swiglu_rows(x, wg_ref, wu_ref, wd_ref, acc, tf):
    for f in range(D_FF // tf):
        cols = slice(f * tf, (f + 1) * tf)
        act = _silu(_dot(x, wg_ref[0, :, cols])) * _dot(x, wu_ref[0, :, cols])
        acc = acc + _dot(_bf(act), wd_ref[0, cols, :])
    return acc


def _merged_rows(ret_ref, gdn_ref, ga_ref, gb_ref, h_ref, wa_ref, wb_ref, wo_ref, nw_ref):
    ya = _dot(ret_ref[...], wa_ref[...])
    yb = _dot(gdn_ref[...], wb_ref[...])
    merged = _sigmoid(ga_ref[...].astype(F32)) * ya + _sigmoid(gb_ref[...].astype(F32)) * yb
    h_new = h_ref[...] + _dot(_bf(merged), wo_ref[...])
    return h_new, _bf(_rms_rows(h_new, nw_ref[...]))


def _merge_ffn_kernel(*refs, tf):
    wg_ref, wu_ref, wd_ref, o_ref = refs[9:]
    h_new, hn = _merged_rows(*refs[:9])
    o_ref[...] = _swiglu_rows(hn, wg_ref, wu_ref, wd_ref, h_new, tf)


def _merge_router_kernel(*refs):
    wr_ref, tri_ref, ho_ref, meta_ref, meta_t_ref, count_ref, run_ref = refs[9:]

    @pl.when(pl.program_id(0) == 0)
    def _():
        run_ref[...] = jnp.zeros_like(run_ref)

    h_new, hn = _merged_rows(*refs[:9])
    ho_ref[...] = h_new
    meta, run_new = _route_rows(_dot(hn, wr_ref[...]), tri_ref, run_ref[0:1, :])
    run_ref[0:1, :] = run_new
    count_ref[...] = jnp.broadcast_to(run_new, count_ref.shape)
    meta_ref[...] = meta
    meta_t_ref[...] = meta.T[:8, :]


def _resident(shape):
    return pl.BlockSpec(shape, lambda i: (0,) * len(shape), pipeline_mode=pl.Buffered(1))


def _merge_specs(tm):
    rows = lambda width, col: pl.BlockSpec((tm, width), lambda i: (i, col))
    return [rows(HEADS_W, 0), rows(HEADS_W, 0), rows(D_MODEL, 0), rows(D_MODEL, 1), rows(D_MODEL, 0),
            _resident((HEADS_W, D_MODEL)), _resident((HEADS_W, D_MODEL)), _resident((D_MODEL, D_MODEL)),
            _resident((1, D_MODEL))]


def _merge_dense_ffn(ret, gdn, proj, h, wa, wb, wo, norm_w, wg, wu, wd, tm, tf):
    t = h.shape[0]
    return pl.pallas_call(
        functools.partial(_merge_ffn_kernel, tf=tf),
        grid=(t // tm,),
        in_specs=_merge_specs(tm) + [_resident((1, D_MODEL, D_FF)), _resident((1, D_MODEL, D_FF)),
                                     _resident((1, D_FF, D_MODEL))],
        out_specs=pl.BlockSpec((tm, D_MODEL), lambda i: (i, 0)),
        out_shape=jax.ShapeDtypeStruct((t, D_MODEL), F32),
        compiler_params=_params("parallel"),
        name="merge_dense_ffn",
    )(ret, gdn, proj, proj, h, wa, wb, wo, norm_w, wg, wu, wd)


def _merge_router(ret, gdn, proj, h, wa, wb, wo, norm_w, router_w, tm):
    t = h.shape[0]
    idx = np.arange(tm)
    tri = jnp.asarray((idx[:, None] > idx[None, :]).astype(np.float32), dtype=BF16)
    return pl.pallas_call(
        _merge_router_kernel,
        grid=(t // tm,),
        in_specs=_merge_specs(tm) + [_resident((D_MODEL, LANES)), _resident((tm, tm))],
        out_specs=[pl.BlockSpec((tm, D_MODEL), lambda i: (i, 0)),
                   pl.BlockSpec((tm, LANES), lambda i: (i, 0)),
                   pl.BlockSpec((8, tm), lambda i: (0, i)),
                   pl.BlockSpec((8, LANES), lambda i: (0, 0))],
        out_shape=[jax.ShapeDtypeStruct((t, D_MODEL), F32),
                   jax.ShapeDtypeStruct((t, LANES), F32),
                   jax.ShapeDtypeStruct((8, t), F32),
                   jax.ShapeDtypeStruct((8, LANES), F32)],
        scratch_shapes=[pltpu.VMEM((8, LANES), F32)],
        compiler_params=_params("arbitrary"),
        name="merge_router",
    )(ret, gdn, proj, proj, h, wa, wb, wo, norm_w, router_w, tri)


def _route_rows(logits, tri_ref, before_tile):
    tm = logits.shape[0]
    lane = lax.broadcasted_iota(jnp.int32, (tm, LANES), 1)
    neg = jnp.float32(-jnp.inf)
    l1 = jnp.where(lane < N_EXPERTS, logits, neg)
    m1 = jnp.max(l1, axis=-1, keepdims=True)
    i1 = jnp.min(jnp.where(l1 == m1, lane, LANES), axis=-1, keepdims=True)
    l2 = jnp.where(lane == i1, neg, l1)
    m2 = jnp.max(l2, axis=-1, keepdims=True)
    i2 = jnp.min(jnp.where(l2 == m2, lane, LANES), axis=-1, keepdims=True)
    e2 = jnp.exp(m2 - m1)
    g1 = 1.0 / (1.0 + e2)
    g2 = e2 / (1.0 + e2)
    sel1 = lane == i1
    sel2 = lane == i2
    onehot = jnp.where(sel1 | sel2, 1.0, 0.0).astype(F32)
    before = _dot(tri_ref[...], _bf(onehot)) + before_tile
    r1 = jnp.sum(jnp.where(sel1, before, 0.0), axis=-1, keepdims=True)
    r2 = jnp.sum(jnp.where(sel2, before, 0.0), axis=-1, keepdims=True)
    meta = jnp.where(lane == 0, i1.astype(F32), 0.0)
    meta = jnp.where(lane == 1, i2.astype(F32), meta)
    meta = jnp.where(lane == 2, r1, meta)
    meta = jnp.where(lane == 3, r2, meta)
    meta = jnp.where(lane == 4, g1, meta)
    meta = jnp.where(lane == 5, g2, meta)
    return meta, before_tile + jnp.sum(onehot, axis=0, keepdims=True)


def _row_wait(src_hbm, dst_vmem, sem, rows):
    pltpu.make_async_copy(src_hbm.at[pl.ds(0, rows)], dst_vmem, sem).wait()


def _dispatch_kernel(sched_ref, dest_ref, h_ref, nw_ref, xg_ref, hn_ref, zero_ref, sem, zsem, *, rows):
    tm = h_ref.shape[0]

    @pl.when(pl.program_id(0) == 0)
    def _():
        zero_ref[...] = jnp.zeros_like(zero_ref)
        for e in range(N_EXPERTS):
            @pl.when(sched_ref[N_EXPERTS + e] > 0)
            def _():
                start = pl.multiple_of(sched_ref[e] - rows, rows)
                clear = pltpu.make_async_copy(zero_ref, xg_ref.at[pl.ds(start, rows)], zsem)
                clear.start()
                clear.wait()

        def clear_unused(b, carry):
            clear = pltpu.make_async_copy(zero_ref, xg_ref.at[pl.ds(pl.multiple_of(b * rows, rows), rows)], zsem)
            clear.start()
            clear.wait()
            return carry

        lax.fori_loop(sched_ref[2 * N_EXPERTS], xg_ref.shape[0] // rows, clear_unused, 0)

    step = pl.program_id(0)
    slot = step % 2
    rows_ref = hn_ref.at[slot]
    rows_ref[...] = _rms_rows(h_ref[...], nw_ref[...])

    def body(r, carry):
        for j in range(TOP_K):
            d = dest_ref[j * tm + r]
            pltpu.make_async_copy(rows_ref.at[pl.ds(r, 1)], xg_ref.at[pl.ds(d, 1)], sem.at[slot]).start(priority=j)
        return carry

    lax.fori_loop(0, tm, body, 0, unroll=ROW_DMA_UNROLL)

    def drain(s):
        for _ in range(TOP_K):
            pltpu.make_async_copy(hn_ref.at[s], xg_ref.at[pl.ds(0, tm)], sem.at[s]).wait()

    @pl.when(step > 0)
    def _():
        drain(1 - slot)

    @pl.when(step == pl.num_programs(0) - 1)
    def _():
        drain(slot)


def _dispatch(sched, dest_flat, h, norm_w, n_rows, tm, rows):
    t = h.shape[0]
    grid_spec = pltpu.PrefetchScalarGridSpec(
        num_scalar_prefetch=1,
        grid=(t // tm,),
        in_specs=[
            pl.BlockSpec((TOP_K * tm,), lambda i, sc: (i,), memory_space=pltpu.SMEM),
            pl.BlockSpec((tm, D_MODEL), lambda i, sc: (i, 0)),
            pl.BlockSpec((1, D_MODEL), lambda i, sc: (0, 0)),
        ],
        out_specs=pl.BlockSpec(memory_space=pl.ANY),
        scratch_shapes=[pltpu.VMEM((2, tm, D_MODEL), F32), pltpu.VMEM((rows, D_MODEL), F32),
                        pltpu.SemaphoreType.DMA((2,)), pltpu.SemaphoreType.DMA(())],
    )
    return pl.pallas_call(
        functools.partial(_dispatch_kernel, rows=rows),
        grid_spec=grid_spec,
        out_shape=jax.ShapeDtypeStruct((n_rows, D_MODEL), F32),
        compiler_params=_params("arbitrary"),
        name="dispatch",
    )(sched, dest_flat, h, norm_w)


def _expert_kernel(be_ref, sched_ref, x_ref, wg_ref, wu_ref, wd_ref, y_ref, *, tf):
    del be_ref

    @pl.when(pl.program_id(0) < sched_ref[2 * N_EXPERTS])
    def _():
        y_ref[...] = _swiglu_rows(_bf(x_ref[...]), wg_ref, wu_ref, wd_ref, jnp.zeros(y_ref.shape, F32), tf)

    @pl.when(pl.program_id(0) >= sched_ref[2 * N_EXPERTS])
    def _():
        y_ref[...] = jnp.zeros_like(y_ref)


def _experts(block_e, sched, xg, wg, wu, wd, rows, tf):
    n_rows = xg.shape[0]
    used = lambda i, sc: jnp.minimum(i, sc[2 * N_EXPERTS] - 1)
    grid_spec = pltpu.PrefetchScalarGridSpec(
        num_scalar_prefetch=2,
        grid=(n_rows // rows,),
        in_specs=[
            pl.BlockSpec((rows, D_MODEL), lambda i, be, sc: (used(i, sc), 0)),
            pl.BlockSpec((1, D_MODEL, D_FF), lambda i, be, sc: (be[i], 0, 0)),
            pl.BlockSpec((1, D_MODEL, D_FF), lambda i, be, sc: (be[i], 0, 0)),
            pl.BlockSpec((1, D_FF, D_MODEL), lambda i, be, sc: (be[i], 0, 0)),
        ],
        out_specs=pl.BlockSpec((rows, D_MODEL), lambda i, be, sc: (i, 0)),
    )
    return pl.pallas_call(
        functools.partial(_expert_kernel, tf=tf),
        grid_spec=grid_spec,
        out_shape=jax.ShapeDtypeStruct((n_rows, D_MODEL), F32),
        compiler_params=_params("arbitrary"),
        name="experts",
    )(block_e, sched, xg, wg, wu, wd)


def _combine_kernel(dest_ref, next_dest_ref, meta_ref, h_ref, nw_ref, y_ref, o_ref, buf_ref, sem):
    tm = h_ref.shape[0]
    step = pl.program_id(0)
    slot = step % 2

    def fetch(idx_ref, s):
        def body(r, carry):
            for j in range(TOP_K):
                d = idx_ref[j * tm + r]
                pltpu.make_async_copy(y_ref.at[pl.ds(d, 1)], buf_ref.at[s, j, pl.ds(r, 1)],
                                      sem.at[s]).start(priority=j)
            return carry

        lax.fori_loop(0, tm, body, 0, unroll=ROW_DMA_UNROLL)

    @pl.when(step == 0)
    def _():
        fetch(dest_ref, 0)

    @pl.when(step + 1 < pl.num_programs(0))
    def _():
        fetch(next_dest_ref, 1 - slot)

    for j in range(TOP_K):
        _row_wait(y_ref, buf_ref.at[slot, j], sem.at[slot], tm)
    meta = meta_ref[...]
    moe = meta[:, 4:5] * buf_ref[slot, 0] + meta[:, 5:6] * buf_ref[slot, 1]
    o_ref[...] = _rms_rows(h_ref[...] + moe, nw_ref[...])


def _combine(dest_flat, meta, h, norm_w, y, tm):
    t = h.shape[0]
    last = t // tm - 1
    return pl.pallas_call(
        _combine_kernel,
        grid=(t // tm,),
        in_specs=[
            pl.BlockSpec((TOP_K * tm,), lambda i: (i,), memory_space=pltpu.SMEM),
            pl.BlockSpec((TOP_K * tm,), lambda i: (jnp.minimum(i + 1, last),), memory_space=pltpu.SMEM),
            pl.BlockSpec((tm, LANES), lambda i: (i, 0)),
            pl.BlockSpec((tm, D_MODEL), lambda i: (i, 0)),
            pl.BlockSpec((1, D_MODEL), lambda i: (0, 0)),
            pl.BlockSpec(memory_space=pl.ANY),
        ],
        out_specs=pl.BlockSpec((tm, D_MODEL), lambda i: (i, 0)),
        out_shape=jax.ShapeDtypeStruct((t, D_MODEL), F32),
        scratch_shapes=[pltpu.VMEM((2, TOP_K, tm, D_MODEL), F32), pltpu.SemaphoreType.DMA((2,))],
        compiler_params=_params("arbitrary"),
        name="combine",
    )(dest_flat, dest_flat, meta, h, norm_w, y)


def _moe_layer(meta, meta_t, counts, h, ffn_norm_w, final_norm_w, wg, wu, wd, rows, tm_rows, tf):
    t = h.shape[0]
    counts = counts[0, :N_EXPERTS].astype(jnp.int32)
    padded = ((counts + rows - 1) // rows) * rows
    pend = jnp.cumsum(padded)
    pstart = pend - padded
    n_rows = t * TOP_K + N_EXPERTS * rows
    expert = meta_t[0:TOP_K].astype(jnp.int32)
    rank = meta_t[TOP_K:2 * TOP_K].astype(jnp.int32)
    dest = rank
    for e in range(N_EXPERTS):
        dest = dest + jnp.where(expert == e, pstart[e], 0)
    dest = dest.reshape(TOP_K, t // tm_rows, tm_rows).transpose(1, 0, 2).reshape(t * TOP_K)
    n_used = pend[N_EXPERTS - 1] // rows
    block_start = jnp.minimum(jnp.arange(n_rows // rows, dtype=jnp.int32), n_used - 1) * rows
    block_e = jnp.sum(block_start[:, None] >= pend[None, :], axis=1).astype(jnp.int32)
    sched = jnp.concatenate([pend, padded, n_used[None]]).astype(jnp.int32)
    xg = _dispatch(sched, dest, h, ffn_norm_w, n_rows, tm_rows, rows)
    y = _experts(block_e, sched, xg, wg, wu, wd, rows, tf)
    return _combine(dest, meta, h, final_norm_w, y, tm_rows)


def _pack_in_weights(w_in):
    o = np.cumsum((0, HEADS_W, HEADS_W, HEADS_W, HEADS_W, CONV_CH, HEADS_W, GDN_HEADS, GDN_HEADS, D_MODEL, D_MODEL))
    rq, rk, rv, rg, gqkv, gz, ga, gb, ma, mb = (w_in[:, o[i]:o[i + 1]] for i in range(10))
    slabs = [p[:, c:c + HEADS_W] for p in (gqkv, ma, mb, rq, rk, rv, rg, gz) for c in range(0, p.shape[1], HEADS_W)]
    main = _bf(jnp.stack(slabs, axis=0))
    small = jnp.concatenate([ga, gb], axis=1)
    small_cols = _bf(jnp.pad(small, ((0, 0), (0, LANES - 2 * GDN_HEADS))))
    small_rows = _bf(jnp.pad(small.T, ((0, SMALL_ROWS - 2 * GDN_HEADS), (0, 0))))
    return main, small_cols, small_rows


def _pick(n, prefs):
    for p in prefs:
        if n % p == 0:
            return p
    raise ValueError(f"no tile in {prefs} divides {n}")


def kernel(x, norm_mix, w_in, conv_w, a_log, dt_bias, gdn_norm, w_branch, w_out, norm_ffn,
           dense_w_gate, dense_w_up, dense_w_down, router, moe_w_gate, moe_w_up, moe_w_down, final_norm):
    batch, seq, d = x.shape
    depth = norm_mix.shape[0]
    assert d == D_MODEL and seq % GDN_CHUNK == 0
    assert depth % 2 == 0, "the final RMSNorm is fused into the last (routed) layer's combine"
    t = batch * seq
    blk = _pick(seq, (128, 64))
    nb = _pick(batch, (4, 2, 1))
    tm = _pick(seq, (1024, 512, 256))
    tm_rows = _pick(t, (512, 256))
    tf_dense = 256
    rows = 512
    tabs = _mixer_tables(seq, blk)

    h = x.reshape(t, d)
    out = None
    for layer in range(depth):
        w_main, w_small, w_small_t = _pack_in_weights(w_in[layer])
        convw = jnp.pad(conv_w[layer], ((0, 8 - CONV_K), (0, 0)))
        conv, proj, gab, gabt = _in_projection(h, norm_mix[layer][None, :], w_main, w_small, w_small_t, convw, tm, CONV_CH, seq)
        prow = jnp.zeros((8, LANES), F32)
        prow = prow.at[0, :GDN_HEADS].set(a_log[layer]).at[1, :GDN_HEADS].set(dt_bias[layer])
        pcol = jnp.zeros((SMALL_ROWS, LANES), F32)
        pcol = pcol.at[:GDN_HEADS, 0].set(a_log[layer]).at[:GDN_HEADS, 1].set(dt_bias[layer])
        ret, gdn = _token_mixers(conv, proj, gab, gabt, tabs, prow, pcol, gdn_norm[layer][None, :],
                                 batch, seq, blk, nb)
        i = layer // 2
        is_moe = layer % 2 == 1
        wa, wb, wo = _bf(w_branch[layer, 0]), _bf(w_branch[layer, 1]), _bf(w_out[layer])
        if not is_moe:
            h = _merge_dense_ffn(ret, gdn, proj, h, wa, wb, wo, norm_ffn[layer][None, :], _bf(dense_w_gate[i])[None],
                                 _bf(dense_w_up[i])[None], _bf(dense_w_down[i])[None], tm_rows, tf_dense)
        else:
            router_w = _bf(jnp.pad(router[i], ((0, 0), (0, LANES - N_EXPERTS))))
            h, meta, meta_t, counts = _merge_router(ret, gdn, proj, h, wa, wb, wo, norm_ffn[layer][None, :],
                                                    router_w, tm_rows)
            out = _moe_layer(meta, meta_t, counts, h, norm_ffn[layer][None, :], final_norm[None, :],
                             _bf(moe_w_gate[i]), _bf(moe_w_up[i]), _bf(moe_w_down[i]), rows, tm_rows, 256)
    return out.reshape(batch, seq, d)
```

```python
import functools
import math

import jax
import jax.numpy as jnp
import numpy as np
from jax import lax
from jax.experimental import pallas as pl
from jax.experimental.pallas import tpu as pltpu

F32 = jnp.float32
BF16 = jnp.bfloat16

D_MODEL = 1024
RET_HEADS = 4
GDN_HEADS = 4
HEAD_DIM = 128
HEADS_W = 512
CONV_K = 4
CONV_CH = 3 * HEADS_W
GDN_CHUNK = 64
ROPE_BASE = 10000.0
D_FF = 2816
N_EXPERTS = 8
TOP_K = 2
EPS = 1e-6
LANES = 128
SMALL_ROWS = 16
ROW_GROUP = 8
CONV_ROWS = 256
MIXER_GROUP = 16

MAIN_COLS = CONV_CH + 5 * HEADS_W + 2 * D_MODEL
VMEM_LIMIT = 56 * 1024 * 1024

HIGHEST = lax.Precision.HIGHEST


def _bf(x):
    return x.astype(BF16)


def _dot(a, b):
    return jnp.dot(a, b, preferred_element_type=F32)


def _dot_nt(a, b):
    return lax.dot_general(a, b, (((1,), (1,)), ((), ())), preferred_element_type=F32)


def _dot_tn(a, b):
    return lax.dot_general(a, b, (((0,), (0,)), ((), ())), preferred_element_type=F32)


def _dot_f32(a, b):
    return jnp.dot(a, b, preferred_element_type=F32, precision=HIGHEST)


def _sigmoid(x):
    return 1.0 / (1.0 + jnp.exp(-x))


def _silu(x):
    return x * _sigmoid(x)


def _softplus(x):
    return jnp.maximum(x, 0.0) + jnp.log(1.0 + jnp.exp(-jnp.abs(x)))


def _rms_rows(x, w):
    ms = jnp.mean(x * x, axis=-1, keepdims=True)
    return x * lax.rsqrt(ms + EPS) * w


def _params(*sem):
    return pltpu.CompilerParams(dimension_semantics=sem, vmem_limit_bytes=VMEM_LIMIT)


def _inproj_kernel(x_ref, nw_ref, w_ref, ws_ref, wst_ref, cw_ref, oc_ref, o_ref, os_ref, ost_ref,
                   hn_ref, xs_ref, *, tiles_per_seq, n_col):
    tm = x_ref.shape[0]
    tile, step = pl.program_id(0), pl.program_id(1)

    @pl.when(step == 0)
    def _():
        @pl.when(tile % tiles_per_seq == 0)
        def _():
            xs_ref[0:8, :] = jnp.zeros((8, CONV_CH), F32)

        hn = None
        for r0 in range(0, tm + CONV_ROWS, CONV_ROWS):
            prev, prev_rows = hn, slice(r0 - CONV_ROWS, r0)
            if r0 < tm:
                rows = slice(r0, r0 + CONV_ROWS)
                hn = _bf(_rms_rows(x_ref[rows, :], nw_ref[...]))
                hn_ref[rows, :] = hn
            if prev is not None:
                for c in range(w_ref.shape[0]):
                    xs_ref[8 + r0 - CONV_ROWS:8 + r0, c * HEADS_W:(c + 1) * HEADS_W] = _dot(prev, w_ref[c])
                os_ref[prev_rows, :] = _dot(prev, ws_ref[...])
                ost_ref[:, prev_rows] = _dot_nt(wst_ref[...], prev)

    def conv_piece(r0, c0):
        cols = slice(c0, c0 + LANES)
        acc = cw_ref[CONV_K - 1:CONV_K, cols] * xs_ref[pl.ds(r0 + 8, CONV_ROWS), cols]
        for d in range(1, CONV_K):
            acc = acc + cw_ref[CONV_K - 1 - d:CONV_K - d, cols] * xs_ref[pl.ds(r0 + 8 - d, CONV_ROWS), cols]
        oc_ref[r0:r0 + CONV_ROWS, cols] = _bf(_silu(acc))

    conv_pieces = [(r0, c0) for r0 in range(0, tm, CONV_ROWS) for c0 in range(0, CONV_CH, LANES)]
    dot_pieces = [(r0, c0) for r0 in range(0, tm, CONV_ROWS) for c0 in range(0, o_ref.shape[1], HEADS_W)]
    share = -(-len(conv_pieces) // (n_col - 1))
    for k in range(1, n_col):
        @pl.when(step == k)
        def _(k=k):
            mine = conv_pieces[(k - 1) * share:k * share]
            per_dot = -(-len(mine) // len(dot_pieces))
            for j, (r0, c0) in enumerate(dot_pieces):
                rows, cols = slice(r0, r0 + CONV_ROWS), slice(c0, c0 + HEADS_W)
                o_ref[rows, cols] = _bf(_dot(hn_ref[rows, :], w_ref[c0 // HEADS_W]))
                for piece in mine[j * per_dot:(j + 1) * per_dot]:
                    conv_piece(*piece)
            if k == n_col - 1:
                xs_ref[0:8, :] = xs_ref[tm:tm + 8, :]


def _in_projection(h, norm_w, w_main, w_small, w_small_t, conv_w, tm, tn, seq):
    t = h.shape[0]
    assert tn == CONV_CH and seq % tm == 0 and tm % CONV_ROWS == 0
    n_col = MAIN_COLS // tn
    return pl.pallas_call(
        functools.partial(_inproj_kernel, tiles_per_seq=seq // tm, n_col=n_col),
        grid=(t // tm, n_col),
        in_specs=[
            pl.BlockSpec((tm, D_MODEL), lambda i, j: (i, 0)),
            pl.BlockSpec((1, D_MODEL), lambda i, j: (0, 0)),
            pl.BlockSpec((tn // HEADS_W, D_MODEL, HEADS_W), lambda i, j: (j, 0, 0)),
            pl.BlockSpec((D_MODEL, LANES), lambda i, j: (0, 0)),
            pl.BlockSpec((SMALL_ROWS, D_MODEL), lambda i, j: (0, 0)),
            pl.BlockSpec((8, CONV_CH), lambda i, j: (0, 0)),
        ],
        out_specs=[
            pl.BlockSpec((tm, CONV_CH), lambda i, j: (i, 0)),
            pl.BlockSpec((tm, tn), lambda i, j: (i, jnp.maximum(j - 1, 0))),
            pl.BlockSpec((tm, LANES), lambda i, j: (i, 0)),
            pl.BlockSpec((SMALL_ROWS, tm), lambda i, j: (0, i)),
        ],
        out_shape=[
            jax.ShapeDtypeStruct((t, CONV_CH), BF16),
            jax.ShapeDtypeStruct((t, MAIN_COLS - CONV_CH), BF16),
            jax.ShapeDtypeStruct((t, LANES), F32),
            jax.ShapeDtypeStruct((SMALL_ROWS, t), F32),
        ],
        scratch_shapes=[pltpu.VMEM((tm, D_MODEL), BF16), pltpu.VMEM((tm + 8, CONV_CH), F32)],
        compiler_params=_params("arbitrary", "arbitrary"),
        name="in_projection",
    )(h, norm_w, w_main, w_small, w_small_t, conv_w)


def _hi_lo(x):
    hi = _bf(x)
    lo_f = x - hi.astype(F32)
    return hi, _bf(lo_f), lo_f


def _three_lhs(hi, lo):
    return jnp.concatenate([hi, lo, hi], axis=1)


def _three_rhs(hi, lo):
    return jnp.concatenate([hi, hi, lo], axis=0)


def _block_diag(m_pair, half_l, half_r):
    return jnp.concatenate([m_pair * half_l, m_pair * half_r], axis=0)


def _mixer_kernel(*refs, blk, nb):
    gqkv_ref, rq_ref, rk_ref, rv_ref, rg_ref, gz_ref, gab_ref = refs[:7]
    gabt_refs = refs[7:7 + nb]
    (cos_ref, sin_ref, dmat_ref, qdec_ref, kdec_ref, cdec_ref, prow_ref, pcol_ref, gnorm_ref, tril_ref, triu_ref,
     ret_ref, gdn_ref, rstate_ref, gstate_ref) = refs[7 + nb:]
    c = GDN_CHUNK
    n_chunks = blk // c
    seqs = range(nb)
    chains = [(s, h) for s in seqs for h in range(GDN_HEADS)]
    inst = [(s, h, n) for s, h in chains for n in range(n_chunks)]
    slot = {ch: ch[0] * GDN_HEADS + ch[1] for ch in chains}

    @pl.when(pl.program_id(1) == 0)
    def _():
        rstate_ref[...] = jnp.zeros_like(rstate_ref)
        gstate_ref[...] = jnp.zeros_like(gstate_ref)

    hsl = [slice(h * HEAD_DIM, (h + 1) * HEAD_DIM) for h in range(RET_HEADS)]

    cos2 = cos_ref[...]
    sin2 = sin_ref[...]
    half = HEAD_DIM // 2
    rq = {(s, h): rq_ref[s, :, hsl[h]].astype(F32) for s, h in chains}
    rk = {(s, h): rk_ref[s, :, hsl[h]].astype(F32) for s, h in chains}
    rv = {(s, h): rv_ref[s, :, hsl[h]] for s, h in chains}
    rq = {ch: q * cos2 + pltpu.roll(q, half, 1) * sin2 for ch, q in rq.items()}
    rk = {ch: (k * cos2 + pltpu.roll(k, half, 1) * sin2) * (HEAD_DIM ** -0.5) for ch, k in rk.items()}
    rstate = {ch: rstate_ref[slot[ch]] for ch in chains}
    scores = {ch: _dot_nt(_bf(rq[ch]), _bf(rk[ch])) * dmat_ref[ch[1]] for ch in chains}
    inter = {ch: _dot(_bf(rq[ch] * qdec_ref[ch[1]]), _bf(rstate[ch])) for ch in chains}
    kv = {ch: _dot_tn(_bf(rk[ch] * kdec_ref[ch[1]]), rv[ch]) for ch in chains}
    ro = {ch: _dot(_bf(scores[ch]), rv[ch]) + inter[ch] for ch in chains}
    for ch in chains:
        s, h = ch
        rstate_ref[slot[ch]] = rstate[ch] * cdec_ref[h] + kv[ch]
        o = ro[ch]
        o = o * lax.rsqrt(jnp.mean(o * o, axis=-1, keepdims=True) + EPS)
        ret_ref[s, :, hsl[h]] = _bf(_silu(rg_ref[s, :, hsl[h]].astype(F32)) * o)

    neg_a_row = -jnp.exp(prow_ref[0:1, :])
    neg_a_col = -jnp.exp(pcol_ref[:, 0:1])
    beta_cols, gc_cols, egc_cols, gc_rows = [], [], [], []
    for s in seqs:
        gab = gab_ref[s]
        beta_cols.append(_sigmoid(gab))
        gc_cols.append(_dot_f32(tril_ref[...], neg_a_row * _softplus(gab + prow_ref[1:2, :])))
        egc_cols.append(jnp.exp(gc_cols[s]))
        gc_rows.append(_dot_f32(neg_a_col * _softplus(gabt_refs[s][...] + pcol_ref[:, 1:2]), triu_ref[...]))

    ri = lax.broadcasted_iota(jnp.int32, (c, 2 * c), 0)
    lane = lax.broadcasted_iota(jnp.int32, (c, 2 * c), 1)
    ci = lane & (c - 1)
    left = lane < c
    half_l = jnp.where(left, 1.0, 0.0).astype(BF16)
    half_r = jnp.where(left, 0.0, 1.0).astype(BF16)
    ge = ri >= ci
    gt = ri > ci
    eye = jnp.where(ri == ci, 1.0, 0.0).astype(F32)
    level_masks = []
    for lg in range(int(math.log2(c))):
        b = 1 << lg
        same_block = (ri >> (lg + 1)) == (ci >> (lg + 1))
        level_masks.append(same_block & ((ri & (2 * b - 1)) >= b) & ((ci & (2 * b - 1)) < b))

    qh, kh, vh = {}, {}, {}
    for s, h in chains:
        q = gqkv_ref[s, :, hsl[h]].astype(F32)
        k = gqkv_ref[s, :, HEADS_W + h * HEAD_DIM:HEADS_W + (h + 1) * HEAD_DIM].astype(F32)
        vh[s, h] = gqkv_ref[s, :, 2 * HEADS_W + h * HEAD_DIM:2 * HEADS_W + (h + 1) * HEAD_DIM].astype(F32)
        qh[s, h] = q * lax.rsqrt(jnp.sum(q * q, axis=-1, keepdims=True) + EPS) * (HEAD_DIM ** -0.5)
        kh[s, h] = k * lax.rsqrt(jnp.sum(k * k, axis=-1, keepdims=True) + EPS)

    rows = [slice(n * c, (n + 1) * c) for n in range(n_chunks)]
    uw, wq, kd, eg, attn = {}, {}, {}, {}, {}
    for g0 in range(0, len(inst), MIXER_GROUP):
        grp = inst[g0:g0 + MIXER_GROUP]
        kc = {(s, h, n): kh[s, h][rows[n]] for s, h, n in grp}
        qc = {(s, h, n): qh[s, h][rows[n]] for s, h, n in grp}
        bcol = {(s, h, n): beta_cols[s][rows[n], 4 + h:5 + h] for s, h, n in grp}
        gcol = {(s, h, n): gc_cols[s][rows[n], h:h + 1] for s, h, n in grp}
        egcol = {(s, h, n): egc_cols[s][rows[n], h:h + 1] for s, h, n in grp}
        glast = {(s, h, n): gc_cols[s][n * c + c - 1:(n + 1) * c, h:h + 1] for s, h, n in grp}
        grow = {(s, h, n): gc_rows[s][h:h + 1, 2 * c * n:2 * c * (n + 1)] for s, h, n in grp}
        kb = {i: kc[i] * bcol[i] for i in grp}
        kcb = {i: _bf(kc[i]) for i in grp}
        kcb2 = {i: jnp.concatenate([kcb[i], kcb[i]], axis=0) for i in grp}
        decay = {i: jnp.where(ge, jnp.exp(jnp.where(ge, gcol[i] - grow[i], 0.0)), 0.0) for i in grp}
        a2 = {i: jnp.where(gt, _dot_nt(_bf(kb[i]), kcb2[i]) * decay[i], 0.0) for i in grp}
        for i in grp:
            attn[i] = _bf(jnp.where(ge, _dot_nt(_bf(qc[i]), kcb2[i]) * decay[i], 0.0)[:, :c])

        pairs = [(grp[j], grp[j + 1]) for j in range(0, len(grp), 2)]
        a_pair = {p: jnp.where(left, a2[p[0]], a2[p[1]]) for p in pairs}
        z = {p: eye - jnp.where(level_masks[0], a_pair[p], 0.0) for p in pairs}
        a_hl = {p: _hi_lo(a_pair[p]) for p in pairs}
        for m in level_masks[1:]:
            mb = jnp.where(m, 1.0, 0.0).astype(BF16)
            z_hl = {p: _hi_lo(z[p]) for p in pairs}
            t = {p: _dot(_three_lhs(z_hl[p][0], z_hl[p][1]),
                         _three_rhs(_block_diag(a_hl[p][0] * mb, half_l, half_r), _block_diag(a_hl[p][1] * mb, half_l, half_r)))
                 for p in pairs}
            t_hl = {p: _hi_lo(t[p]) for p in pairs}
            z = {p: z[p] - _dot(_three_lhs(t_hl[p][0], t_hl[p][1]),
                                _three_rhs(_block_diag(z_hl[p][0], half_l, half_r), _block_diag(z_hl[p][1], half_l, half_r)))
                 for p in pairs}

        for p in pairs:
            stacked = []
            for i in p:
                s, h, n = i
                stacked.append(jnp.concatenate([vh[s, h][rows[n]] * bcol[i], kb[i] * egcol[i]], axis=1))
            r_hi, r_lo, _ = _hi_lo(jnp.concatenate(stacked, axis=0))
            rr = _three_rhs(r_hi, r_lo)
            z_hi, z_lo, _ = _hi_lo(z[p])
            uw[p[0]] = _dot(_three_lhs(z_hi * half_l, z_lo * half_l), rr)
            uw[p[1]] = _dot(_three_lhs(z_hi * half_r, z_lo * half_r), rr)
        for i in grp:
            wq[i] = _bf(jnp.concatenate([uw[i][:, HEAD_DIM:], qc[i] * egcol[i]], axis=0))
            kd[i] = _bf(kc[i] * jnp.exp(glast[i] - gcol[i]))
            eg[i] = jnp.exp(glast[i])

    state = {ch: gstate_ref[slot[ch]] for ch in chains}
    outs = {ch: [] for ch in chains}
    for n in range(n_chunks):
        sb = {ch: _bf(state[ch]) for ch in chains}
        ws = {ch: _dot(wq[ch + (n,)], sb[ch]) for ch in chains}
        vnb = {ch: _bf(uw[ch + (n,)][:, :HEAD_DIM] - ws[ch][:c]) for ch in chains}
        for ch in chains:
            outs[ch].append(ws[ch][c:] + _dot(attn[ch + (n,)], vnb[ch]))
        state = {ch: state[ch] * eg[ch + (n,)] + _dot_tn(kd[ch + (n,)], vnb[ch]) for ch in chains}

    gnorm = gnorm_ref[...]
    for ch in chains:
        s, h = ch
        gstate_ref[slot[ch]] = state[ch]
        o = jnp.concatenate(outs[ch], axis=0)
        o = o * lax.rsqrt(jnp.mean(o * o, axis=-1, keepdims=True) + EPS)
        gdn_ref[s, :, hsl[h]] = _bf(o * gnorm * _silu(gz_ref[s, :, hsl[h]].astype(F32)))


def _token_mixers(conv, proj, gab, gabt, tabs, prow, pcol, gnorm, batch, seq, blk, nb):
    conv3 = conv.reshape(batch, seq, CONV_CH)
    proj3 = proj.reshape(batch, seq, MAIN_COLS - CONV_CH)
    gab3 = gab.reshape(batch, seq, LANES)
    n_blk = seq // blk
    cos2, sin2, dmat, qdec, kdec, cdec, tril, triu = tabs
    w512 = lambda col: pl.BlockSpec((nb, blk, HEADS_W), lambda b, n, col=col: (b, n, col))
    full = lambda shape: pl.BlockSpec(shape, lambda b, n: (0,) * len(shape))
    gabt_specs = [pl.BlockSpec((SMALL_ROWS, blk), lambda b, n, s=s: (0, (b * nb + s) * n_blk + n)) for s in range(nb)]
    out = pl.pallas_call(
        functools.partial(_mixer_kernel, blk=blk, nb=nb),
        grid=(batch // nb, n_blk),
        in_specs=[
            pl.BlockSpec((nb, blk, CONV_CH), lambda b, n: (b, n, 0)),
            w512(4), w512(5), w512(6), w512(7), w512(8),
            pl.BlockSpec((nb, blk, LANES), lambda b, n: (b, n, 0)),
            *gabt_specs,
            pl.BlockSpec((blk, HEAD_DIM), lambda b, n: (n, 0)),
            pl.BlockSpec((blk, HEAD_DIM), lambda b, n: (n, 0)),
            full((RET_HEADS, blk, blk)),
            full((RET_HEADS, blk, HEAD_DIM)),
            full((RET_HEADS, blk, HEAD_DIM)),
            full((RET_HEADS, 1, HEAD_DIM)),
            full((8, LANES)),
            full((SMALL_ROWS, LANES)),
            full((1, HEAD_DIM)),
            full((blk, blk)),
            full((blk, 2 * blk)),
        ],
        out_specs=[
            pl.BlockSpec((nb, blk, HEADS_W), lambda b, n: (b, n, 0)),
            pl.BlockSpec((nb, blk, HEADS_W), lambda b, n: (b, n, 0)),
        ],
        out_shape=[
            jax.ShapeDtypeStruct((batch, seq, HEADS_W), BF16),
            jax.ShapeDtypeStruct((batch, seq, HEADS_W), BF16),
        ],
        scratch_shapes=[
            pltpu.VMEM((nb * RET_HEADS, HEAD_DIM, HEAD_DIM), F32),
            pltpu.VMEM((nb * GDN_HEADS, HEAD_DIM, HEAD_DIM), F32),
        ],
        compiler_params=_params("parallel", "arbitrary"),
        name="token_mixers",
    )(conv3, proj3, proj3, proj3, proj3, proj3, gab3, *([gabt] * nb),
      cos2, sin2, dmat, qdec, kdec, cdec, prow, pcol, gnorm, tril, triu)
    ret, gdn = out
    return ret.reshape(batch * seq, HEADS_W), gdn.reshape(batch * seq, HEADS_W)


def _mixer_tables(seq, blk):
    inv_freq = ROPE_BASE ** (-jnp.arange(0, HEAD_DIM, 2, dtype=F32) / HEAD_DIM)
    ang = jnp.arange(seq, dtype=F32)[:, None] * inv_freq[None, :]
    cos, sin = jnp.cos(ang), jnp.sin(ang)
    cos2 = jnp.concatenate([cos, cos], axis=-1)
    sin2 = jnp.concatenate([-sin, sin], axis=-1)
    log_gamma = jnp.log(1.0 - jnp.exp2(-5.0 - jnp.arange(RET_HEADS, dtype=F32)))
    pos = jnp.arange(blk, dtype=F32)
    diff = pos[:, None] - pos[None, :]
    dmat = jnp.where(diff >= 0, jnp.exp(jnp.maximum(diff, 0.0)[None] * log_gamma[:, None, None]), 0.0)
    ones = jnp.ones((1, 1, HEAD_DIM), F32)
    qdec = jnp.exp((pos + 1.0)[None, :] * log_gamma[:, None])[:, :, None] * ones
    kdec = jnp.exp((blk - 1 - pos)[None, :] * log_gamma[:, None])[:, :, None] * ones
    cdec = jnp.exp(blk * log_gamma)[:, None, None] * ones
    idx = np.arange(blk)
    same = (idx[:, None] // GDN_CHUNK) == (idx[None, :] // GDN_CHUNK)
    tril = jnp.asarray((same & (idx[:, None] >= idx[None, :])).astype(np.float32))
    col = np.arange(2 * blk)
    col_chunk, col_pos = col // (2 * GDN_CHUNK), col % GDN_CHUNK
    triu = jnp.asarray(((idx[:, None] // GDN_CHUNK == col_chunk[None, :])
                        & (idx[:, None] % GDN_CHUNK <= col_pos[None, :])).astype(np.float32))
    return cos2, sin2, dmat, qdec, kdec, cdec, tril, triu


def _swiglu_rows(x, wg_ref, wu_ref, wd_ref, acc, tf):
    for f in range(D_FF // tf):
        cols = slice(f * tf, (f + 1) * tf)
        act = _silu(_dot(x, wg_ref[0, :, cols])) * _dot(x, wu_ref[0, :, cols])
        acc = acc + _dot(_bf(act), wd_ref[0, cols, :])
    return acc


def _merged_rows(ret_ref, gdn_ref, ga_ref, gb_ref, h_ref, wa_ref, wb_ref, wo_ref, nw_ref):
    ya = _dot(ret_ref[...], wa_ref[...])
    yb = _dot(gdn_ref[...], wb_ref[...])
    merged = _sigmoid(ga_ref[...].astype(F32)) * ya + _sigmoid(gb_ref[...].astype(F32)) * yb
    h_new = h_ref[...] + _dot(_bf(merged), wo_ref[...])
    return h_new, _bf(_rms_rows(h_new, nw_ref[...]))


def _merge_ffn_kernel(*refs, tf):
    wg_ref, wu_ref, wd_ref, o_ref = refs[9:]
    h_new, hn = _merged_rows(*refs[:9])
    o_ref[...] = _swiglu_rows(hn, wg_ref, wu_ref, wd_ref, h_new, tf)


def _merge_router_kernel(*refs):
    wr_ref, tri_ref, ho_ref, meta_ref, meta_t_ref, count_ref, run_ref = refs[9:]

    @pl.when(pl.program_id(0) == 0)
    def _():
        run_ref[...] = jnp.zeros_like(run_ref)

    h_new, hn = _merged_rows(*refs[:9])
    ho_ref[...] = h_new
    meta, run_new = _route_rows(_dot(hn, wr_ref[...]), tri_ref, run_ref[0:1, :])
    run_ref[0:1, :] = run_new
    count_ref[...] = jnp.broadcast_to(run_new, count_ref.shape)
    meta_ref[...] = meta
    meta_t_ref[...] = meta.T[:8, :]


def _resident(shape):
    return pl.BlockSpec(shape, lambda i: (0,) * len(shape), pipeline_mode=pl.Buffered(1))


def _merge_specs(tm):
    rows = lambda width, col: pl.BlockSpec((tm, width), lambda i: (i, col))
    return [rows(HEADS_W, 0), rows(HEADS_W, 0), rows(D_MODEL, 0), rows(D_MODEL, 1), rows(D_MODEL, 0),
            _resident((HEADS_W, D_MODEL)), _resident((HEADS_W, D_MODEL)), _resident((D_MODEL, D_MODEL)),
            _resident((1, D_MODEL))]


def _merge_dense_ffn(ret, gdn, proj, h, wa, wb, wo, norm_w, wg, wu, wd, tm, tf):
    t = h.shape[0]
    return pl.pallas_call(
        functools.partial(_merge_ffn_kernel, tf=tf),
        grid=(t // tm,),
        in_specs=_merge_specs(tm) + [_resident((1, D_MODEL, D_FF)), _resident((1, D_MODEL, D_FF)),
                                     _resident((1, D_FF, D_MODEL))],
        out_specs=pl.BlockSpec((tm, D_MODEL), lambda i: (i, 0)),
        out_shape=jax.ShapeDtypeStruct((t, D_MODEL), F32),
        compiler_params=_params("parallel"),
        name="merge_dense_ffn",
    )(ret, gdn, proj, proj, h, wa, wb, wo, norm_w, wg, wu, wd)


def _merge_router(ret, gdn, proj, h, wa, wb, wo, norm_w, router_w, tm):
    t = h.shape[0]
    idx = np.arange(tm)
    tri = jnp.asarray((idx[:, None] > idx[None, :]).astype(np.float32), dtype=BF16)
    return pl.pallas_call(
        _merge_router_kernel,
        grid=(t // tm,),
        in_specs=_merge_specs(tm) + [_resident((D_MODEL, LANES)), _resident((tm, tm))],
        out_specs=[pl.BlockSpec((tm, D_MODEL), lambda i: (i, 0)),
                   pl.BlockSpec((tm, LANES), lambda i: (i, 0)),
                   pl.BlockSpec((8, tm), lambda i: (0, i)),
                   pl.BlockSpec((8, LANES), lambda i: (0, 0))],
        out_shape=[jax.ShapeDtypeStruct((t, D_MODEL), F32),
                   jax.ShapeDtypeStruct((t, LANES), F32),
                   jax.ShapeDtypeStruct((8, t), F32),
                   jax.ShapeDtypeStruct((8, LANES), F32)],
        scratch_shapes=[pltpu.VMEM((8, LANES), F32)],
        compiler_params=_params("arbitrary"),
        name="merge_router",
    )(ret, gdn, proj, proj, h, wa, wb, wo, norm_w, router_w, tri)


def _route_rows(logits, tri_ref, before_tile):
    tm = logits.shape[0]
    lane = lax.broadcasted_iota(jnp.int32, (tm, LANES), 1)
    neg = jnp.float32(-jnp.inf)
    l1 = jnp.where(lane < N_EXPERTS, logits, neg)
    m1 = jnp.max(l1, axis=-1, keepdims=True)
    i1 = jnp.min(jnp.where(l1 == m1, lane, LANES), axis=-1, keepdims=True)
    l2 = jnp.where(lane == i1, neg, l1)
    m2 = jnp.max(l2, axis=-1, keepdims=True)
    i2 = jnp.min(jnp.where(l2 == m2, lane, LANES), axis=-1, keepdims=True)
    e2 = jnp.exp(m2 - m1)
    g1 = 1.0 / (1.0 + e2)
    g2 = e2 / (1.0 + e2)
    sel1 = lane == i1
    sel2 = lane == i2
    onehot = jnp.where(sel1 | sel2, 1.0, 0.0).astype(F32)
    before = _dot(tri_ref[...], _bf(onehot)) + before_tile
    r1 = jnp.sum(jnp.where(sel1, before, 0.0), axis=-1, keepdims=True)
    r2 = jnp.sum(jnp.where(sel2, before, 0.0), axis=-1, keepdims=True)
    meta = jnp.where(lane == 0, i1.astype(F32), 0.0)
    meta = jnp.where(lane == 1, i2.astype(F32), meta)
    meta = jnp.where(lane == 2, r1, meta)
    meta = jnp.where(lane == 3, r2, meta)
    meta = jnp.where(lane == 4, g1, meta)
    meta = jnp.where(lane == 5, g2, meta)
    return meta, before_tile + jnp.sum(onehot, axis=0, keepdims=True)


def _row_wait(src_hbm, dst_vmem, sem, rows):
    pltpu.make_async_copy(src_hbm.at[pl.ds(0, rows)], dst_vmem, sem).wait()


def _dispatch_kernel(sched_ref, dest_ref, h_ref, nw_ref, xg_ref, hn_ref, zero_ref, sem, zsem, *, rows):
    tm = h_ref.shape[0]

    @pl.when(pl.program_id(0) == 0)
    def _():
        zero_ref[...] = jnp.zeros_like(zero_ref)
        for e in range(N_EXPERTS):
            @pl.when(sched_ref[N_EXPERTS + e] > 0)
            def _():
                start = pl.multiple_of(sched_ref[e] - rows, rows)
                clear = pltpu.make_async_copy(zero_ref, xg_ref.at[pl.ds(start, rows)], zsem)
                clear.start()
                clear.wait()

        def clear_unused(b, carry):
            clear = pltpu.make_async_copy(zero_ref, xg_ref.at[pl.ds(pl.multiple_of(b * rows, rows), rows)], zsem)
            clear.start()
            clear.wait()
            return carry

        lax.fori_loop(sched_ref[2 * N_EXPERTS], xg_ref.shape[0] // rows, clear_unused, 0)

    step = pl.program_id(0)
    slot = step % 2
    rows_ref = hn_ref.at[slot]
    rows_ref[...] = _rms_rows(h_ref[...], nw_ref[...])

    def body(g, carry):
        base = pl.multiple_of(g * ROW_GROUP, ROW_GROUP)
        for u in range(ROW_GROUP):
            for j in range(TOP_K):
                d = dest_ref[j * tm + base + u]
                pltpu.make_async_copy(rows_ref.at[pl.ds(base + u, 1)], xg_ref.at[pl.ds(d, 1)],
                                      sem.at[slot]).start(priority=j)
        return carry

    lax.fori_loop(0, tm // ROW_GROUP, body, 0)

    def drain(s):
        for _ in range(TOP_K):
            pltpu.make_async_copy(hn_ref.at[s], xg_ref.at[pl.ds(0, tm)], sem.at[s]).wait()

    @pl.when(step > 0)
    def _():
        drain(1 - slot)

    @pl.when(step == pl.num_programs(0) - 1)
    def _():
        drain(slot)


def _dispatch(sched, dest_flat, h, norm_w, n_rows, tm, rows):
    t = h.shape[0]
    grid_spec = pltpu.PrefetchScalarGridSpec(
        num_scalar_prefetch=1,
        grid=(t // tm,),
        in_specs=[
            pl.BlockSpec((TOP_K * tm,), lambda i, sc: (i,), memory_space=pltpu.SMEM),
            pl.BlockSpec((tm, D_MODEL), lambda i, sc: (i, 0)),
            pl.BlockSpec((1, D_MODEL), lambda i, sc: (0, 0)),
        ],
        out_specs=pl.BlockSpec(memory_space=pl.ANY),
        scratch_shapes=[pltpu.VMEM((2, tm, D_MODEL), F32), pltpu.VMEM((rows, D_MODEL), F32),
                        pltpu.SemaphoreType.DMA((2,)), pltpu.SemaphoreType.DMA(())],
    )
    return pl.pallas_call(
        functools.partial(_dispatch_kernel, rows=rows),
        grid_spec=grid_spec,
        out_shape=jax.ShapeDtypeStruct((n_rows, D_MODEL), F32),
        compiler_params=_params("arbitrary"),
        name="dispatch",
    )(sched, dest_flat, h, norm_w)


def _expert_kernel(be_ref, sched_ref, x_ref, wg_ref, wu_ref, wd_ref, y_ref, *, tf):
    del be_ref

    @pl.when(pl.program_id(0) < sched_ref[2 * N_EXPERTS])
    def _():
        y_ref[...] = _swiglu_rows(_bf(x_ref[...]), wg_ref, wu_ref, wd_ref, jnp.zeros(y_ref.shape, F32), tf)

    @pl.when(pl.program_id(0) >= sched_ref[2 * N_EXPERTS])
    def _():
        y_ref[...] = jnp.zeros_like(y_ref)


def _experts(block_e, sched, xg, wg, wu, wd, rows, tf):
    n_rows = xg.shape[0]
    used = lambda i, sc: jnp.minimum(i, sc[2 * N_EXPERTS] - 1)
    grid_spec = pltpu.PrefetchScalarGridSpec(
        num_scalar_prefetch=2,
        grid=(n_rows // rows,),
        in_specs=[
            pl.BlockSpec((rows, D_MODEL), lambda i, be, sc: (used(i, sc), 0)),
            pl.BlockSpec((1, D_MODEL, D_FF), lambda i, be, sc: (be[i], 0, 0)),
            pl.BlockSpec((1, D_MODEL, D_FF), lambda i, be, sc: (be[i], 0, 0)),
            pl.BlockSpec((1, D_FF, D_MODEL), lambda i, be, sc: (be[i], 0, 0)),
        ],
        out_specs=pl.BlockSpec((rows, D_MODEL), lambda i, be, sc: (i, 0)),
    )
    return pl.pallas_call(
        functools.partial(_expert_kernel, tf=tf),
        grid_spec=grid_spec,
        out_shape=jax.ShapeDtypeStruct((n_rows, D_MODEL), F32),
        compiler_params=_params("arbitrary"),
        name="experts",
    )(block_e, sched, xg, wg, wu, wd)


def _combine_kernel(dest_ref, next_dest_ref, meta_ref, h_ref, nw_ref, y_ref, o_ref, buf_ref, sem):
    tm = h_ref.shape[0]
    step = pl.program_id(0)
    slot = step % 2

    def fetch(idx_ref, s):
        def body(g, carry):
            base = pl.multiple_of(g * ROW_GROUP, ROW_GROUP)
            for u in range(ROW_GROUP):
                for j in range(TOP_K):
                    d = idx_ref[j * tm + base + u]
                    pltpu.make_async_copy(y_ref.at[pl.ds(d, 1)], buf_ref.at[s, j, pl.ds(base + u, 1)],
                                          sem.at[s]).start(priority=j)
            return carry

        lax.fori_loop(0, tm // ROW_GROUP, body, 0)

    @pl.when(step == 0)
    def _():
        fetch(dest_ref, 0)

    @pl.when(step + 1 < pl.num_programs(0))
    def _():
        fetch(next_dest_ref, 1 - slot)

    for j in range(TOP_K):
        _row_wait(y_ref, buf_ref.at[slot, j], sem.at[slot], tm)
    meta = meta_ref[...]
    moe = meta[:, 4:5] * buf_ref[slot, 0] + meta[:, 5:6] * buf_ref[slot, 1]
    o_ref[...] = _rms_rows(h_ref[...] + moe, nw_ref[...])


def _combine(dest_flat, meta, h, norm_w, y, tm):
    t = h.shape[0]
    last = t // tm - 1
    return pl.pallas_call(
        _combine_kernel,
        grid=(t // tm,),
        in_specs=[
            pl.BlockSpec((TOP_K * tm,), lambda i: (i,), memory_space=pltpu.SMEM),
            pl.BlockSpec((TOP_K * tm,), lambda i: (jnp.minimum(i + 1, last),), memory_space=pltpu.SMEM),
            pl.BlockSpec((tm, LANES), lambda i: (i, 0)),
            pl.BlockSpec((tm, D_MODEL), lambda i: (i, 0)),
            pl.BlockSpec((1, D_MODEL), lambda i: (0, 0)),
            pl.BlockSpec(memory_space=pl.ANY),
        ],
        out_specs=pl.BlockSpec((tm, D_MODEL), lambda i: (i, 0)),
        out_shape=jax.ShapeDtypeStruct((t, D_MODEL), F32),
        scratch_shapes=[pltpu.VMEM((2, TOP_K, tm, D_MODEL), F32), pltpu.SemaphoreType.DMA((2,))],
        compiler_params=_params("arbitrary"),
        name="combine",
    )(dest_flat, dest_flat, meta, h, norm_w, y)


def _moe_layer(meta, meta_t, counts, h, ffn_norm_w, final_norm_w, wg, wu, wd, rows, tm_rows, tf):
    t = h.shape[0]
    counts = counts[0, :N_EXPERTS].astype(jnp.int32)
    padded = ((counts + rows - 1) // rows) * rows
    pend = jnp.cumsum(padded)
    pstart = pend - padded
    n_rows = t * TOP_K + N_EXPERTS * rows
    expert = meta_t[0:TOP_K].astype(jnp.int32)
    rank = meta_t[TOP_K:2 * TOP_K].astype(jnp.int32)
    dest = rank
    for e in range(N_EXPERTS):
        dest = dest + jnp.where(expert == e, pstart[e], 0)
    dest = dest.reshape(TOP_K, t // tm_rows, tm_rows).transpose(1, 0, 2).reshape(t * TOP_K)
    n_used = pend[N_EXPERTS - 1] // rows
    block_start = jnp.minimum(jnp.arange(n_rows // rows, dtype=jnp.int32), n_used - 1) * rows
    block_e = jnp.sum(block_start[:, None] >= pend[None, :], axis=1).astype(jnp.int32)
    sched = jnp.concatenate([pend, padded, n_used[None]]).astype(jnp.int32)
    xg = _dispatch(sched, dest, h, ffn_norm_w, n_rows, tm_rows, rows)
    y = _experts(block_e, sched, xg, wg, wu, wd, rows, tf)
    return _combine(dest, meta, h, final_norm_w, y, tm_rows)


def _pack_in_weights(w_in):
    o = np.cumsum((0, HEADS_W, HEADS_W, HEADS_W, HEADS_W, CONV_CH, HEADS_W, GDN_HEADS, GDN_HEADS, D_MODEL, D_MODEL))
    rq, rk, rv, rg, gqkv, gz, ga, gb, ma, mb = (w_in[:, o[i]:o[i + 1]] for i in range(10))
    slabs = [p[:, c:c + HEADS_W] for p in (gqkv, ma, mb, rq, rk, rv, rg, gz) for c in range(0, p.shape[1], HEADS_W)]
    main = _bf(jnp.stack(slabs, axis=0))
    small = jnp.concatenate([ga, gb], axis=1)
    small_cols = _bf(jnp.pad(small, ((0, 0), (0, LANES - 2 * GDN_HEADS))))
    small_rows = _bf(jnp.pad(small.T, ((0, SMALL_ROWS - 2 * GDN_HEADS), (0, 0))))
    return main, small_cols, small_rows


def _pick(n, prefs):
    for p in prefs:
        if n % p == 0:
            return p
    raise ValueError(f"no tile in {prefs} divides {n}")


def kernel(x, norm_mix, w_in, conv_w, a_log, dt_bias, gdn_norm, w_branch, w_out, norm_ffn,
           dense_w_gate, dense_w_up, dense_w_down, router, moe_w_gate, moe_w_up, moe_w_down, final_norm):
    batch, seq, d = x.shape
    depth = norm_mix.shape[0]
    assert d == D_MODEL and seq % GDN_CHUNK == 0
    assert depth % 2 == 0, "the final RMSNorm is fused into the last (routed) layer's combine"
    t = batch * seq
    blk = _pick(seq, (128, 64))
    nb = _pick(batch, (4, 2, 1))
    tm = _pick(seq, (1024, 512, 256))
    tm_rows = _pick(t, (512, 256))
    tm_moe = _pick(t, (1024, 512, 256))
    tf_dense = 256
    rows = 512
    tabs = _mixer_tables(seq, blk)

    h = x.reshape(t, d)
    out = None
    for layer in range(depth):
        w_main, w_small, w_small_t = _pack_in_weights(w_in[layer])
        convw = jnp.pad(conv_w[layer], ((0, 8 - CONV_K), (0, 0)))
        conv, proj, gab, gabt = _in_projection(h, norm_mix[layer][None, :], w_main, w_small, w_small_t, convw, tm, CONV_CH, seq)
        prow = jnp.zeros((8, LANES), F32)
        prow = prow.at[0, :GDN_HEADS].set(a_log[layer]).at[1, :GDN_HEADS].set(dt_bias[layer])
        pcol = jnp.zeros((SMALL_ROWS, LANES), F32)
        pcol = pcol.at[:GDN_HEADS, 0].set(a_log[layer]).at[:GDN_HEADS, 1].set(dt_bias[layer])
        ret, gdn = _token_mixers(conv, proj, gab, gabt, tabs, prow, pcol, gdn_norm[layer][None, :],
                                 batch, seq, blk, nb)
        i = layer // 2
        is_moe = layer % 2 == 1
        wa, wb, wo = _bf(w_branch[layer, 0]), _bf(w_branch[layer, 1]), _bf(w_out[layer])
        if not is_moe:
            h = _merge_dense_ffn(ret, gdn, proj, h, wa, wb, wo, norm_ffn[layer][None, :], _bf(dense_w_gate[i])[None],
                                 _bf(dense_w_up[i])[None], _bf(dense_w_down[i])[None], tm_rows, tf_dense)
        else:
            router_w = _bf(jnp.pad(router[i], ((0, 0), (0, LANES - N_EXPERTS))))
            h, meta, meta_t, counts = _merge_router(ret, gdn, proj, h, wa, wb, wo, norm_ffn[layer][None, :],
                                                    router_w, tm_rows)
            out = _moe_layer(meta, meta_t, counts, h, norm_ffn[layer][None, :], final_norm[None, :],
                             _bf(moe_w_gate[i]), _bf(moe_w_up[i]), _bf(moe_w_down[i]), rows, tm_moe, 256)
    return out.reshape(batch, seq, d)
```

```python
import functools
import math

import jax
import jax.numpy as jnp
import numpy as np
from jax import lax
from jax.experimental import pallas as pl
from jax.experimental.pallas import tpu as pltpu

F32 = jnp.float32
BF16 = jnp.bfloat16

D_MODEL = 1024
RET_HEADS = 4
GDN_HEADS = 4
HEAD_DIM = 128
HEADS_W = 512
CONV_K = 4
CONV_CH = 3 * HEADS_W
GDN_CHUNK = 64
ROPE_BASE = 10000.0
D_FF = 2816
N_EXPERTS = 8
TOP_K = 2
EPS = 1e-6
LANES = 128
SMALL_ROWS = 16
ROW_GROUP = 8
CONV_ROWS = 256
DOT_ROWS = 512
MIXER_GROUP = 16

MAIN_COLS = CONV_CH + 5 * HEADS_W + 2 * D_MODEL
VMEM_LIMIT = 56 * 1024 * 1024

HIGHEST = lax.Precision.HIGHEST


def _bf(x):
    return x.astype(BF16)


def _dot(a, b):
    return jnp.dot(a, b, preferred_element_type=F32)


def _dot_nt(a, b):
    return lax.dot_general(a, b, (((1,), (1,)), ((), ())), preferred_element_type=F32)


def _dot_tn(a, b):
    return lax.dot_general(a, b, (((0,), (0,)), ((), ())), preferred_element_type=F32)


def _dot_f32(a, b):
    return jnp.dot(a, b, preferred_element_type=F32, precision=HIGHEST)


def _sigmoid(x):
    return 1.0 / (1.0 + jnp.exp(-x))


def _silu(x):
    return x * _sigmoid(x)


def _softplus(x):
    return jnp.maximum(x, 0.0) + jnp.log(1.0 + jnp.exp(-jnp.abs(x)))


def _rms_rows(x, w):
    ms = jnp.mean(x * x, axis=-1, keepdims=True)
    return x * lax.rsqrt(ms + EPS) * w


def _params(*sem):
    return pltpu.CompilerParams(dimension_semantics=sem, vmem_limit_bytes=VMEM_LIMIT)


def _inproj_kernel(x_ref, nw_ref, w_ref, ws_ref, wst_ref, cw_ref, oc_ref, o_ref, os_ref, ost_ref,
                   hn_ref, xs_ref, *, tiles_per_seq, n_col):
    tm = x_ref.shape[0]
    tile, step = pl.program_id(0), pl.program_id(1)

    @pl.when(step == 0)
    def _():
        @pl.when(tile % tiles_per_seq == 0)
        def _():
            xs_ref[0:8, :] = jnp.zeros((8, CONV_CH), F32)

        hn = None
        for r0 in range(0, tm + CONV_ROWS, CONV_ROWS):
            prev, prev_rows = hn, slice(r0 - CONV_ROWS, r0)
            if r0 < tm:
                rows = slice(r0, r0 + CONV_ROWS)
                hn = _bf(_rms_rows(x_ref[rows, :], nw_ref[...]))
                hn_ref[rows, :] = hn
            if prev is not None:
                for c in range(w_ref.shape[0]):
                    xs_ref[8 + r0 - CONV_ROWS:8 + r0, c * HEADS_W:(c + 1) * HEADS_W] = _dot(prev, w_ref[c])
                os_ref[prev_rows, :] = _dot(prev, ws_ref[...])
                ost_ref[:, prev_rows] = _dot_nt(wst_ref[...], prev)

    def conv_piece(r0, c0):
        cols = slice(c0, c0 + LANES)
        acc = cw_ref[CONV_K - 1:CONV_K, cols] * xs_ref[pl.ds(r0 + 8, CONV_ROWS), cols]
        for d in range(1, CONV_K):
            acc = acc + cw_ref[CONV_K - 1 - d:CONV_K - d, cols] * xs_ref[pl.ds(r0 + 8 - d, CONV_ROWS), cols]
        oc_ref[r0:r0 + CONV_ROWS, cols] = _bf(_silu(acc))

    conv_pieces = [(r0, c0) for r0 in range(0, tm, CONV_ROWS) for c0 in range(0, CONV_CH, LANES)]
    dot_pieces = [(r0, c0) for c0 in range(0, o_ref.shape[1], HEADS_W) for r0 in range(0, tm, DOT_ROWS)]
    share = -(-len(conv_pieces) // (n_col - 1))
    for k in range(1, n_col):
        @pl.when(step == k)
        def _(k=k):
            mine = conv_pieces[(k - 1) * share:k * share]
            per_dot = -(-len(mine) // len(dot_pieces))
            for j, (r0, c0) in enumerate(dot_pieces):
                rows, cols = slice(r0, r0 + DOT_ROWS), slice(c0, c0 + HEADS_W)
                o_ref[rows, cols] = _bf(_dot(hn_ref[rows, :], w_ref[c0 // HEADS_W]))
                for piece in mine[j * per_dot:(j + 1) * per_dot]:
                    conv_piece(*piece)
            if k == n_col - 1:
                xs_ref[0:8, :] = xs_ref[tm:tm + 8, :]


def _in_projection(h, norm_w, w_main, w_small, w_small_t, conv_w, tm, tn, seq):
    t = h.shape[0]
    assert tn == CONV_CH and seq % tm == 0 and tm % CONV_ROWS == 0
    n_col = MAIN_COLS // tn
    return pl.pallas_call(
        functools.partial(_inproj_kernel, tiles_per_seq=seq // tm, n_col=n_col),
        grid=(t // tm, n_col),
        in_specs=[
            pl.BlockSpec((tm, D_MODEL), lambda i, j: (i, 0)),
            pl.BlockSpec((1, D_MODEL), lambda i, j: (0, 0)),
            pl.BlockSpec((tn // HEADS_W, D_MODEL, HEADS_W), lambda i, j: (j, 0, 0)),
            pl.BlockSpec((D_MODEL, LANES), lambda i, j: (0, 0)),
            pl.BlockSpec((SMALL_ROWS, D_MODEL), lambda i, j: (0, 0)),
            pl.BlockSpec((8, CONV_CH), lambda i, j: (0, 0)),
        ],
        out_specs=[
            pl.BlockSpec((tm, CONV_CH), lambda i, j: (i, 0)),
            pl.BlockSpec((tm, tn), lambda i, j: (i, jnp.maximum(j - 1, 0))),
            pl.BlockSpec((tm, LANES), lambda i, j: (i, 0)),
            pl.BlockSpec((SMALL_ROWS, tm), lambda i, j: (0, i)),
        ],
        out_shape=[
            jax.ShapeDtypeStruct((t, CONV_CH), BF16),
            jax.ShapeDtypeStruct((t, MAIN_COLS - CONV_CH), BF16),
            jax.ShapeDtypeStruct((t, LANES), F32),
            jax.ShapeDtypeStruct((SMALL_ROWS, t), F32),
        ],
        scratch_shapes=[pltpu.VMEM((tm, D_MODEL), BF16), pltpu.VMEM((tm + 8, CONV_CH), F32)],
        compiler_params=_params("arbitrary", "arbitrary"),
        name="in_projection",
    )(h, norm_w, w_main, w_small, w_small_t, conv_w)


def _hi_lo(x):
    hi = _bf(x)
    lo_f = x - hi.astype(F32)
    return hi, _bf(lo_f), lo_f


def _three_lhs(hi, lo):
    return jnp.concatenate([hi, lo, hi], axis=1)


def _three_rhs(hi, lo):
    return jnp.concatenate([hi, hi, lo], axis=0)


def _block_diag(m_pair, half_l, half_r):
    return jnp.concatenate([m_pair * half_l, m_pair * half_r], axis=0)


def _mixer_kernel(*refs, blk, nb):
    gqkv_ref, rq_ref, rk_ref, rv_ref, rg_ref, gz_ref, gab_ref = refs[:7]
    gabt_refs = refs[7:7 + nb]
    (cos_ref, sin_ref, dmat_ref, qdec_ref, kdec_ref, cdec_ref, prow_ref, pcol_ref, gnorm_ref, tril_ref, triu_ref,
     ret_ref, gdn_ref, rstate_ref, gstate_ref) = refs[7 + nb:]
    c = GDN_CHUNK
    n_chunks = blk // c
    seqs = range(nb)
    chains = [(s, h) for s in seqs for h in range(GDN_HEADS)]
    inst = [(s, h, n) for s, h in chains for n in range(n_chunks)]
    slot = {ch: ch[0] * GDN_HEADS + ch[1] for ch in chains}

    @pl.when(pl.program_id(1) == 0)
    def _():
        rstate_ref[...] = jnp.zeros_like(rstate_ref)
        gstate_ref[...] = jnp.zeros_like(gstate_ref)

    hsl = [slice(h * HEAD_DIM, (h + 1) * HEAD_DIM) for h in range(RET_HEADS)]

    cos2 = cos_ref[...]
    sin2 = sin_ref[...]
    half = HEAD_DIM // 2
    rq = {(s, h): rq_ref[s, :, hsl[h]].astype(F32) for s, h in chains}
    rk = {(s, h): rk_ref[s, :, hsl[h]].astype(F32) for s, h in chains}
    rv = {(s, h): rv_ref[s, :, hsl[h]] for s, h in chains}
    rq = {ch: q * cos2 + pltpu.roll(q, half, 1) * sin2 for ch, q in rq.items()}
    rk = {ch: (k * cos2 + pltpu.roll(k, half, 1) * sin2) * (HEAD_DIM ** -0.5) for ch, k in rk.items()}
    rstate = {ch: rstate_ref[slot[ch]] for ch in chains}
    scores = {ch: _dot_nt(_bf(rq[ch]), _bf(rk[ch])) * dmat_ref[ch[1]] for ch in chains}
    inter = {ch: _dot(_bf(rq[ch] * qdec_ref[ch[1]]), _bf(rstate[ch])) for ch in chains}
    kv = {ch: _dot_tn(_bf(rk[ch] * kdec_ref[ch[1]]), rv[ch]) for ch in chains}
    ro = {ch: _dot(_bf(scores[ch]), rv[ch]) + inter[ch] for ch in chains}
    for ch in chains:
        s, h = ch
        rstate_ref[slot[ch]] = rstate[ch] * cdec_ref[h] + kv[ch]
        o = ro[ch]
        o = o * lax.rsqrt(jnp.mean(o * o, axis=-1, keepdims=True) + EPS)
        ret_ref[s, :, hsl[h]] = _bf(_silu(rg_ref[s, :, hsl[h]].astype(F32)) * o)

    neg_a_row = -jnp.exp(prow_ref[0:1, :])
    neg_a_col = -jnp.exp(pcol_ref[:, 0:1])
    beta_cols, gc_cols, egc_cols, gc_rows = [], [], [], []
    for s in seqs:
        gab = gab_ref[s]
        beta_cols.append(_sigmoid(gab))
        gc_cols.append(_dot_f32(tril_ref[...], neg_a_row * _softplus(gab + prow_ref[1:2, :])))
        egc_cols.append(jnp.exp(gc_cols[s]))
        gc_rows.append(_dot_f32(neg_a_col * _softplus(gabt_refs[s][...] + pcol_ref[:, 1:2]), triu_ref[...]))

    ri = lax.broadcasted_iota(jnp.int32, (c, 2 * c), 0)
    lane = lax.broadcasted_iota(jnp.int32, (c, 2 * c), 1)
    ci = lane & (c - 1)
    left = lane < c
    half_l = jnp.where(left, 1.0, 0.0).astype(BF16)
    half_r = jnp.where(left, 0.0, 1.0).astype(BF16)
    ge = ri >= ci
    gt = ri > ci
    eye = jnp.where(ri == ci, 1.0, 0.0).astype(F32)
    level_masks = []
    for lg in range(int(math.log2(c))):
        b = 1 << lg
        same_block = (ri >> (lg + 1)) == (ci >> (lg + 1))
        level_masks.append(same_block & ((ri & (2 * b - 1)) >= b) & ((ci & (2 * b - 1)) < b))

    qh, kh, vh = {}, {}, {}
    for s, h in chains:
        q = gqkv_ref[s, :, hsl[h]].astype(F32)
        k = gqkv_ref[s, :, HEADS_W + h * HEAD_DIM:HEADS_W + (h + 1) * HEAD_DIM].astype(F32)
        vh[s, h] = gqkv_ref[s, :, 2 * HEADS_W + h * HEAD_DIM:2 * HEADS_W + (h + 1) * HEAD_DIM].astype(F32)
        qh[s, h] = q * lax.rsqrt(jnp.sum(q * q, axis=-1, keepdims=True) + EPS) * (HEAD_DIM ** -0.5)
        kh[s, h] = k * lax.rsqrt(jnp.sum(k * k, axis=-1, keepdims=True) + EPS)

    rows = [slice(n * c, (n + 1) * c) for n in range(n_chunks)]
    uw, wq, kd, eg, attn = {}, {}, {}, {}, {}
    for g0 in range(0, len(inst), MIXER_GROUP):
        grp = inst[g0:g0 + MIXER_GROUP]
        kc = {(s, h, n): kh[s, h][rows[n]] for s, h, n in grp}
        qc = {(s, h, n): qh[s, h][rows[n]] for s, h, n in grp}
        bcol = {(s, h, n): beta_cols[s][rows[n], 4 + h:5 + h] for s, h, n in grp}
        gcol = {(s, h, n): gc_cols[s][rows[n], h:h + 1] for s, h, n in grp}
        egcol = {(s, h, n): egc_cols[s][rows[n], h:h + 1] for s, h, n in grp}
        glast = {(s, h, n): gc_cols[s][n * c + c - 1:(n + 1) * c, h:h + 1] for s, h, n in grp}
        grow = {(s, h, n): gc_rows[s][h:h + 1, 2 * c * n:2 * c * (n + 1)] for s, h, n in grp}
        kb = {i: kc[i] * bcol[i] for i in grp}
        kcb = {i: _bf(kc[i]) for i in grp}
        kcb2 = {i: jnp.concatenate([kcb[i], kcb[i]], axis=0) for i in grp}
        decay = {i: jnp.where(ge, jnp.exp(jnp.where(ge, gcol[i] - grow[i], 0.0)), 0.0) for i in grp}
        a2 = {i: jnp.where(gt, _dot_nt(_bf(kb[i]), kcb2[i]) * decay[i], 0.0) for i in grp}
        for i in grp:
            attn[i] = _bf(jnp.where(ge, _dot_nt(_bf(qc[i]), kcb2[i]) * decay[i], 0.0)[:, :c])

        pairs = [(grp[j], grp[j + 1]) for j in range(0, len(grp), 2)]
        a_pair = {p: jnp.where(left, a2[p[0]], a2[p[1]]) for p in pairs}
        z = {p: eye - jnp.where(level_masks[0], a_pair[p], 0.0) for p in pairs}
        a_hl = {p: _hi_lo(a_pair[p]) for p in pairs}
        for m in level_masks[1:]:
            mb = jnp.where(m, 1.0, 0.0).astype(BF16)
            z_hl = {p: _hi_lo(z[p]) for p in pairs}
            t = {p: _dot(_three_lhs(z_hl[p][0], z_hl[p][1]),
                         _three_rhs(_block_diag(a_hl[p][0] * mb, half_l, half_r), _block_diag(a_hl[p][1] * mb, half_l, half_r)))
                 for p in pairs}
            t_hl = {p: _hi_lo(t[p]) for p in pairs}
            z = {p: z[p] - _dot(_three_lhs(t_hl[p][0], t_hl[p][1]),
                                _three_rhs(_block_diag(z_hl[p][0], half_l, half_r), _block_diag(z_hl[p][1], half_l, half_r)))
                 for p in pairs}

        for p in pairs:
            stacked = []
            for i in p:
                s, h, n = i
                stacked.append(jnp.concatenate([vh[s, h][rows[n]] * bcol[i], kb[i] * egcol[i]], axis=1))
            r_hi, r_lo, _ = _hi_lo(jnp.concatenate(stacked, axis=0))
            rr = _three_rhs(r_hi, r_lo)
            z_hi, z_lo, _ = _hi_lo(z[p])
            uw[p[0]] = _dot(_three_lhs(z_hi * half_l, z_lo * half_l), rr)
            uw[p[1]] = _dot(_three_lhs(z_hi * half_r, z_lo * half_r), rr)
        for i in grp:
            wq[i] = _bf(jnp.concatenate([uw[i][:, HEAD_DIM:], qc[i] * egcol[i]], axis=0))
            kd[i] = _bf(kc[i] * jnp.exp(glast[i] - gcol[i]))
            eg[i] = jnp.exp(glast[i])

    state = {ch: gstate_ref[slot[ch]] for ch in chains}
    outs = {ch: [] for ch in chains}
    for n in range(n_chunks):
        sb = {ch: _bf(state[ch]) for ch in chains}
        ws = {ch: _dot(wq[ch + (n,)], sb[ch]) for ch in chains}
        vnb = {ch: _bf(uw[ch + (n,)][:, :HEAD_DIM] - ws[ch][:c]) for ch in chains}
        for ch in chains:
            outs[ch].append(ws[ch][c:] + _dot(attn[ch + (n,)], vnb[ch]))
        state = {ch: state[ch] * eg[ch + (n,)] + _dot_tn(kd[ch + (n,)], vnb[ch]) for ch in chains}

    gnorm = gnorm_ref[...]
    for ch in chains:
        s, h = ch
        gstate_ref[slot[ch]] = state[ch]
        o = jnp.concatenate(outs[ch], axis=0)
        o = o * lax.rsqrt(jnp.mean(o * o, axis=-1, keepdims=True) + EPS)
        gdn_ref[s, :, hsl[h]] = _bf(o * gnorm * _silu(gz_ref[s, :, hsl[h]].astype(F32)))


def _token_mixers(conv, proj, gab, gabt, tabs, prow, pcol, gnorm, batch, seq, blk, nb):
    conv3 = conv.reshape(batch, seq, CONV_CH)
    proj3 = proj.reshape(batch, seq, MAIN_COLS - CONV_CH)
    gab3 = gab.reshape(batch, seq, LANES)
    n_blk = seq // blk
    cos2, sin2, dmat, qdec, kdec, cdec, tril, triu = tabs
    w512 = lambda col: pl.BlockSpec((nb, blk, HEADS_W), lambda b, n, col=col: (b, n, col))
    full = lambda shape: pl.BlockSpec(shape, lambda b, n: (0,) * len(shape))
    gabt_specs = [pl.BlockSpec((SMALL_ROWS, blk), lambda b, n, s=s: (0, (b * nb + s) * n_blk + n)) for s in range(nb)]
    out = pl.pallas_call(
        functools.partial(_mixer_kernel, blk=blk, nb=nb),
        grid=(batch // nb, n_blk),
        in_specs=[
            pl.BlockSpec((nb, blk, CONV_CH), lambda b, n: (b, n, 0)),
            w512(4), w512(5), w512(6), w512(7), w512(8),
            pl.BlockSpec((nb, blk, LANES), lambda b, n: (b, n, 0)),
            *gabt_specs,
            pl.BlockSpec((blk, HEAD_DIM), lambda b, n: (n, 0)),
            pl.BlockSpec((blk, HEAD_DIM), lambda b, n: (n, 0)),
            full((RET_HEADS, blk, blk)),
            full((RET_HEADS, blk, HEAD_DIM)),
            full((RET_HEADS, blk, HEAD_DIM)),
            full((RET_HEADS, 1, HEAD_DIM)),
            full((8, LANES)),
            full((SMALL_ROWS, LANES)),
            full((1, HEAD_DIM)),
            full((blk, blk)),
            full((blk, 2 * blk)),
        ],
        out_specs=[
            pl.BlockSpec((nb, blk, HEADS_W), lambda b, n: (b, n, 0)),
            pl.BlockSpec((nb, blk, HEADS_W), lambda b, n: (b, n, 0)),
        ],
        out_shape=[
            jax.ShapeDtypeStruct((batch, seq, HEADS_W), BF16),
            jax.ShapeDtypeStruct((batch, seq, HEADS_W), BF16),
        ],
        scratch_shapes=[
            pltpu.VMEM((nb * RET_HEADS, HEAD_DIM, HEAD_DIM), F32),
            pltpu.VMEM((nb * GDN_HEADS, HEAD_DIM, HEAD_DIM), F32),
        ],
        compiler_params=_params("parallel", "arbitrary"),
        name="token_mixers",
    )(conv3, proj3, proj3, proj3, proj3, proj3, gab3, *([gabt] * nb),
      cos2, sin2, dmat, qdec, kdec, cdec, prow, pcol, gnorm, tril, triu)
    ret, gdn = out
    return ret.reshape(batch * seq, HEADS_W), gdn.reshape(batch * seq, HEADS_W)


def _mixer_tables(seq, blk):
    inv_freq = ROPE_BASE ** (-jnp.arange(0, HEAD_DIM, 2, dtype=F32) / HEAD_DIM)
    ang = jnp.arange(seq, dtype=F32)[:, None] * inv_freq[None, :]
    cos, sin = jnp.cos(ang), jnp.sin(ang)
    cos2 = jnp.concatenate([cos, cos], axis=-1)
    sin2 = jnp.concatenate([-sin, sin], axis=-1)
    log_gamma = jnp.log(1.0 - jnp.exp2(-5.0 - jnp.arange(RET_HEADS, dtype=F32)))
    pos = jnp.arange(blk, dtype=F32)
    diff = pos[:, None] - pos[None, :]
    dmat = jnp.where(diff >= 0, jnp.exp(jnp.maximum(diff, 0.0)[None] * log_gamma[:, None, None]), 0.0)
    ones = jnp.ones((1, 1, HEAD_DIM), F32)
    qdec = jnp.exp((pos + 1.0)[None, :] * log_gamma[:, None])[:, :, None] * ones
    kdec = jnp.exp((blk - 1 - pos)[None, :] * log_gamma[:, None])[:, :, None] * ones
    cdec = jnp.exp(blk * log_gamma)[:, None, None] * ones
    idx = np.arange(blk)
    same = (idx[:, None] // GDN_CHUNK) == (idx[None, :] // GDN_CHUNK)
    tril = jnp.asarray((same & (idx[:, None] >= idx[None, :])).astype(np.float32))
    col = np.arange(2 * blk)
    col_chunk, col_pos = col // (2 * GDN_CHUNK), col % GDN_CHUNK
    triu = jnp.asarray(((idx[:, None] // GDN_CHUNK == col_chunk[None, :])
                        & (idx[:, None] % GDN_CHUNK <= col_pos[None, :])).astype(np.float32))
    return cos2, sin2, dmat, qdec, kdec, cdec, tril, triu


def _swiglu_rows(x, wg_ref, wu_ref, wd_ref, acc, tf):
    for f in range(D_FF // tf):
        cols = slice(f * tf, (f + 1) * tf)
        act = _silu(_dot(x, wg_ref[0, :, cols])) * _dot(x, wu_ref[0, :, cols])
        acc = acc + _dot(_bf(act), wd_ref[0, cols, :])
    return acc


def _merged_rows(ret_ref, gdn_ref, ga_ref, gb_ref, h_ref, wa_ref, wb_ref, wo_ref, nw_ref):
    ya = _dot(ret_ref[...], wa_ref[...])
    yb = _dot(gdn_ref[...], wb_ref[...])
    merged = _sigmoid(ga_ref[...].astype(F32)) * ya + _sigmoid(gb_ref[...].astype(F32)) * yb
    h_new = h_ref[...] + _dot(_bf(merged), wo_ref[...])
    return h_new, _bf(_rms_rows(h_new, nw_ref[...]))


def _merge_ffn_kernel(*refs, tf):
    wg_ref, wu_ref, wd_ref, o_ref = refs[9:]
    h_new, hn = _merged_rows(*refs[:9])
    o_ref[...] = _swiglu_rows(hn, wg_ref, wu_ref, wd_ref, h_new, tf)


def _merge_router_kernel(*refs):
    wr_ref, tri_ref, ho_ref, meta_ref, meta_t_ref, count_ref, run_ref = refs[9:]

    @pl.when(pl.program_id(0) == 0)
    def _():
        run_ref[...] = jnp.zeros_like(run_ref)

    h_new, hn = _merged_rows(*refs[:9])
    ho_ref[...] = h_new
    meta, run_new = _route_rows(_dot(hn, wr_ref[...]), tri_ref, run_ref[0:1, :])
    run_ref[0:1, :] = run_new
    count_ref[...] = jnp.broadcast_to(run_new, count_ref.shape)
    meta_ref[...] = meta
    meta_t_ref[...] = meta.T[:8, :]


def _resident(shape):
    return pl.BlockSpec(shape, lambda i: (0,) * len(shape), pipeline_mode=pl.Buffered(1))


def _merge_specs(tm):
    rows = lambda width, col: pl.BlockSpec((tm, width), lambda i: (i, col))
    return [rows(HEADS_W, 0), rows(HEADS_W, 0), rows(D_MODEL, 0), rows(D_MODEL, 1), rows(D_MODEL, 0),
            _resident((HEADS_W, D_MODEL)), _resident((HEADS_W, D_MODEL)), _resident((D_MODEL, D_MODEL)),
            _resident((1, D_MODEL))]


def _merge_dense_ffn(ret, gdn, proj, h, wa, wb, wo, norm_w, wg, wu, wd, tm, tf):
    t = h.shape[0]
    return pl.pallas_call(
        functools.partial(_merge_ffn_kernel, tf=tf),
        grid=(t // tm,),
        in_specs=_merge_specs(tm) + [_resident((1, D_MODEL, D_FF)), _resident((1, D_MODEL, D_FF)),
                                     _resident((1, D_FF, D_MODEL))],
        out_specs=pl.BlockSpec((tm, D_MODEL), lambda i: (i, 0)),
        out_shape=jax.ShapeDtypeStruct((t, D_MODEL), F32),
        compiler_params=_params("parallel"),
        name="merge_dense_ffn",
    )(ret, gdn, proj, proj, h, wa, wb, wo, norm_w, wg, wu, wd)


def _merge_router(ret, gdn, proj, h, wa, wb, wo, norm_w, router_w, tm):
    t = h.shape[0]
    idx = np.arange(tm)
    tri = jnp.asarray((idx[:, None] > idx[None, :]).astype(np.float32), dtype=BF16)
    return pl.pallas_call(
        _merge_router_kernel,
        grid=(t // tm,),
        in_specs=_merge_specs(tm) + [_resident((D_MODEL, LANES)), _resident((tm, tm))],
        out_specs=[pl.BlockSpec((tm, D_MODEL), lambda i: (i, 0)),
                   pl.BlockSpec((tm, LANES), lambda i: (i, 0)),
                   pl.BlockSpec((8, tm), lambda i: (0, i)),
                   pl.BlockSpec((8, LANES), lambda i: (0, 0))],
        out_shape=[jax.ShapeDtypeStruct((t, D_MODEL), F32),
                   jax.ShapeDtypeStruct((t, LANES), F32),
                   jax.ShapeDtypeStruct((8, t), F32),
                   jax.ShapeDtypeStruct((8, LANES), F32)],
        scratch_shapes=[pltpu.VMEM((8, LANES), F32)],
        compiler_params=_params("arbitrary"),
        name="merge_router",
    )(ret, gdn, proj, proj, h, wa, wb, wo, norm_w, router_w, tri)


def _route_rows(logits, tri_ref, before_tile):
    tm = logits.shape[0]
    lane = lax.broadcasted_iota(jnp.int32, (tm, LANES), 1)
    neg = jnp.float32(-jnp.inf)
    l1 = jnp.where(lane < N_EXPERTS, logits, neg)
    m1 = jnp.max(l1, axis=-1, keepdims=True)
    i1 = jnp.min(jnp.where(l1 == m1, lane, LANES), axis=-1, keepdims=True)
    l2 = jnp.where(lane == i1, neg, l1)
    m2 = jnp.max(l2, axis=-1, keepdims=True)
    i2 = jnp.min(jnp.where(l2 == m2, lane, LANES), axis=-1, keepdims=True)
    e2 = jnp.exp(m2 - m1)
    g1 = 1.0 / (1.0 + e2)
    g2 = e2 / (1.0 + e2)
    sel1 = lane == i1
    sel2 = lane == i2
    onehot = jnp.where(sel1 | sel2, 1.0, 0.0).astype(F32)
    before = _dot(tri_ref[...], _bf(onehot)) + before_tile
    r1 = jnp.sum(jnp.where(sel1, before, 0.0), axis=-1, keepdims=True)
    r2 = jnp.sum(jnp.where(sel2, before, 0.0), axis=-1, keepdims=True)
    meta = jnp.where(lane == 0, i1.astype(F32), 0.0)
    meta = jnp.where(lane == 1, i2.astype(F32), meta)
    meta = jnp.where(lane == 2, r1, meta)
    meta = jnp.where(lane == 3, r2, meta)
    meta = jnp.where(lane == 4, g1, meta)
    meta = jnp.where(lane == 5, g2, meta)
    return meta, before_tile + jnp.sum(onehot, axis=0, keepdims=True)


def _row_wait(src_hbm, dst_vmem, sem, rows):
    pltpu.make_async_copy(src_hbm.at[pl.ds(0, rows)], dst_vmem, sem).wait()


def _dispatch_kernel(sched_ref, dest_ref, h_ref, nw_ref, xg_ref, hn_ref, zero_ref, sem, zsem, *, rows):
    tm = h_ref.shape[0]

    @pl.when(pl.program_id(0) == 0)
    def _():
        zero_ref[...] = jnp.zeros_like(zero_ref)
        for e in range(N_EXPERTS):
            @pl.when(sched_ref[N_EXPERTS + e] > 0)
            def _():
                start = pl.multiple_of(sched_ref[e] - rows, rows)
                clear = pltpu.make_async_copy(zero_ref, xg_ref.at[pl.ds(start, rows)], zsem)
                clear.start()
                clear.wait()

        def clear_unused(b, carry):
            clear = pltpu.make_async_copy(zero_ref, xg_ref.at[pl.ds(pl.multiple_of(b * rows, rows), rows)], zsem)
            clear.start()
            clear.wait()
            return carry

        lax.fori_loop(sched_ref[2 * N_EXPERTS], xg_ref.shape[0] // rows, clear_unused, 0)

    step = pl.program_id(0)
    slot = step % 2
    rows_ref = hn_ref.at[slot]
    rows_ref[...] = _rms_rows(h_ref[...], nw_ref[...])

    def body(g, carry):
        base = pl.multiple_of(g * ROW_GROUP, ROW_GROUP)
        for u in range(ROW_GROUP):
            for j in range(TOP_K):
                d = dest_ref[j * tm + base + u]
                pltpu.make_async_copy(rows_ref.at[pl.ds(base + u, 1)], xg_ref.at[pl.ds(d, 1)],
                                      sem.at[slot]).start(priority=j)
        return carry

    lax.fori_loop(0, tm // ROW_GROUP, body, 0)

    def drain(s):
        for _ in range(TOP_K):
            pltpu.make_async_copy(hn_ref.at[s], xg_ref.at[pl.ds(0, tm)], sem.at[s]).wait()

    @pl.when(step > 0)
    def _():
        drain(1 - slot)

    @pl.when(step == pl.num_programs(0) - 1)
    def _():
        drain(slot)


def _dispatch(sched, dest_flat, h, norm_w, n_rows, tm, rows):
    t = h.shape[0]
    grid_spec = pltpu.PrefetchScalarGridSpec(
        num_scalar_prefetch=1,
        grid=(t // tm,),
        in_specs=[
            pl.BlockSpec((TOP_K * tm,), lambda i, sc: (i,), memory_space=pltpu.SMEM),
            pl.BlockSpec((tm, D_MODEL), lambda i, sc: (i, 0)),
            pl.BlockSpec((1, D_MODEL), lambda i, sc: (0, 0)),
        ],
        out_specs=pl.BlockSpec(memory_space=pl.ANY),
        scratch_shapes=[pltpu.VMEM((2, tm, D_MODEL), F32), pltpu.VMEM((rows, D_MODEL), F32),
                        pltpu.SemaphoreType.DMA((2,)), pltpu.SemaphoreType.DMA(())],
    )
    return pl.pallas_call(
        functools.partial(_dispatch_kernel, rows=rows),
        grid_spec=grid_spec,
        out_shape=jax.ShapeDtypeStruct((n_rows, D_MODEL), F32),
        compiler_params=_params("arbitrary"),
        name="dispatch",
    )(sched, dest_flat, h, norm_w)


def _expert_kernel(be_ref, sched_ref, x_ref, wg_ref, wu_ref, wd_ref, y_ref, *, tf):
    del be_ref

    @pl.when(pl.program_id(0) < sched_ref[2 * N_EXPERTS])
    def _():
        y_ref[...] = _swiglu_rows(_bf(x_ref[...]), wg_ref, wu_ref, wd_ref, jnp.zeros(y_ref.shape, F32), tf)

    @pl.when(pl.program_id(0) >= sched_ref[2 * N_EXPERTS])
    def _():
        y_ref[...] = jnp.zeros_like(y_ref)


def _experts(block_e, sched, xg, wg, wu, wd, rows, tf):
    n_rows = xg.shape[0]
    used = lambda i, sc: jnp.minimum(i, sc[2 * N_EXPERTS] - 1)
    grid_spec = pltpu.PrefetchScalarGridSpec(
        num_scalar_prefetch=2,
        grid=(n_rows // rows,),
        in_specs=[
            pl.BlockSpec((rows, D_MODEL), lambda i, be, sc: (used(i, sc), 0)),
            pl.BlockSpec((1, D_MODEL, D_FF), lambda i, be, sc: (be[i], 0, 0)),
            pl.BlockSpec((1, D_MODEL, D_FF), lambda i, be, sc: (be[i], 0, 0)),
            pl.BlockSpec((1, D_FF, D_MODEL), lambda i, be, sc: (be[i], 0, 0)),
        ],
        out_specs=pl.BlockSpec((rows, D_MODEL), lambda i, be, sc: (i, 0)),
    )
    return pl.pallas_call(
        functools.partial(_expert_kernel, tf=tf),
        grid_spec=grid_spec,
        out_shape=jax.ShapeDtypeStruct((n_rows, D_MODEL), F32),
        compiler_params=_params("arbitrary"),
        name="experts",
    )(block_e, sched, xg, wg, wu, wd)


def _combine_kernel(dest_ref, next_dest_ref, meta_ref, h_ref, nw_ref, y_ref, o_ref, buf_ref, sem):
    tm = h_ref.shape[0]
    step = pl.program_id(0)
    slot = step % 2

    def fetch(idx_ref, s):
        def body(g, carry):
            base = pl.multiple_of(g * ROW_GROUP, ROW_GROUP)
            for u in range(ROW_GROUP):
                for j in range(TOP_K):
                    d = idx_ref[j * tm + base + u]
                    pltpu.make_async_copy(y_ref.at[pl.ds(d, 1)], buf_ref.at[s, j, pl.ds(base + u, 1)],
                                          sem.at[s]).start(priority=j)
            return carry

        lax.fori_loop(0, tm // ROW_GROUP, body, 0)

    @pl.when(step == 0)
    def _():
        fetch(dest_ref, 0)

    @pl.when(step + 1 < pl.num_programs(0))
    def _():
        fetch(next_dest_ref, 1 - slot)

    for j in range(TOP_K):
        _row_wait(y_ref, buf_ref.at[slot, j], sem.at[slot], tm)
    meta = meta_ref[...]
    moe = meta[:, 4:5] * buf_ref[slot, 0] + meta[:, 5:6] * buf_ref[slot, 1]
    o_ref[...] = _rms_rows(h_ref[...] + moe, nw_ref[...])


def _combine(dest_flat, meta, h, norm_w, y, tm):
    t = h.shape[0]
    last = t // tm - 1
    return pl.pallas_call(
        _combine_kernel,
        grid=(t // tm,),
        in_specs=[
            pl.BlockSpec((TOP_K * tm,), lambda i: (i,), memory_space=pltpu.SMEM),
            pl.BlockSpec((TOP_K * tm,), lambda i: (jnp.minimum(i + 1, last),), memory_space=pltpu.SMEM),
            pl.BlockSpec((tm, LANES), lambda i: (i, 0)),
            pl.BlockSpec((tm, D_MODEL), lambda i: (i, 0)),
            pl.BlockSpec((1, D_MODEL), lambda i: (0, 0)),
            pl.BlockSpec(memory_space=pl.ANY),
        ],
        out_specs=pl.BlockSpec((tm, D_MODEL), lambda i: (i, 0)),
        out_shape=jax.ShapeDtypeStruct((t, D_MODEL), F32),
        scratch_shapes=[pltpu.VMEM((2, TOP_K, tm, D_MODEL), F32), pltpu.SemaphoreType.DMA((2,))],
        compiler_params=_params("arbitrary"),
        name="combine",
    )(dest_flat, dest_flat, meta, h, norm_w, y)


def _moe_layer(meta, meta_t, counts, h, ffn_norm_w, final_norm_w, wg, wu, wd, rows, tm_rows, tf):
    t = h.shape[0]
    counts = counts[0, :N_EXPERTS].astype(jnp.int32)
    padded = ((counts + rows - 1) // rows) * rows
    pend = jnp.cumsum(padded)
    pstart = pend - padded
    n_rows = t * TOP_K + N_EXPERTS * rows
    expert = meta_t[0:TOP_K].astype(jnp.int32)
    rank = meta_t[TOP_K:2 * TOP_K].astype(jnp.int32)
    dest = rank
    for e in range(N_EXPERTS):
        dest = dest + jnp.where(expert == e, pstart[e], 0)
    dest = dest.reshape(TOP_K, t // tm_rows, tm_rows).transpose(1, 0, 2).reshape(t * TOP_K)
    n_used = pend[N_EXPERTS - 1] // rows
    block_start = jnp.minimum(jnp.arange(n_rows // rows, dtype=jnp.int32), n_used - 1) * rows
    block_e = jnp.sum(block_start[:, None] >= pend[None, :], axis=1).astype(jnp.int32)
    sched = jnp.concatenate([pend, padded, n_used[None]]).astype(jnp.int32)
    xg = _dispatch(sched, dest, h, ffn_norm_w, n_rows, tm_rows, rows)
    y = _experts(block_e, sched, xg, wg, wu, wd, rows, tf)
    return _combine(dest, meta, h, final_norm_w, y, tm_rows)


def _pack_in_weights(w_in):
    o = np.cumsum((0, HEADS_W, HEADS_W, HEADS_W, HEADS_W, CONV_CH, HEADS_W, GDN_HEADS, GDN_HEADS, D_MODEL, D_MODEL))
    rq, rk, rv, rg, gqkv, gz, ga, gb, ma, mb = (w_in[:, o[i]:o[i + 1]] for i in range(10))
    slabs = [p[:, c:c + HEADS_W] for p in (gqkv, ma, mb, rq, rk, rv, rg, gz) for c in range(0, p.shape[1], HEADS_W)]
    main = _bf(jnp.stack(slabs, axis=0))
    small = jnp.concatenate([ga, gb], axis=1)
    small_cols = _bf(jnp.pad(small, ((0, 0), (0, LANES - 2 * GDN_HEADS))))
    small_rows = _bf(jnp.pad(small.T, ((0, SMALL_ROWS - 2 * GDN_HEADS), (0, 0))))
    return main, small_cols, small_rows


def _pick(n, prefs):
    for p in prefs:
        if n % p == 0:
            return p
    raise ValueError(f"no tile in {prefs} divides {n}")


def kernel(x, norm_mix, w_in, conv_w, a_log, dt_bias, gdn_norm, w_branch, w_out, norm_ffn,
           dense_w_gate, dense_w_up, dense_w_down, router, moe_w_gate, moe_w_up, moe_w_down, final_norm):
    batch, seq, d = x.shape
    depth = norm_mix.shape[0]
    assert d == D_MODEL and seq % GDN_CHUNK == 0
    assert depth % 2 == 0, "the final RMSNorm is fused into the last (routed) layer's combine"
    t = batch * seq
    blk = _pick(seq, (128, 64))
    nb = _pick(batch, (4, 2, 1))
    tm = _pick(seq, (1024, 512, 256))
    tm_rows = _pick(t, (512, 256))
    tm_moe = _pick(t, (1024, 512, 256))
    tf_dense = 256
    rows = 512
    tabs = _mixer_tables(seq, blk)

    h = x.reshape(t, d)
    out = None
    for layer in range(depth):
        w_main, w_small, w_small_t = _pack_in_weights(w_in[layer])
        convw = jnp.pad(conv_w[layer], ((0, 8 - CONV_K), (0, 0)))
        conv, proj, gab, gabt = _in_projection(h, norm_mix[layer][None, :], w_main, w_small, w_small_t, convw, tm, CONV_CH, seq)
        prow = jnp.zeros((8, LANES), F32)
        prow = prow.at[0, :GDN_HEADS].set(a_log[layer]).at[1, :GDN_HEADS].set(dt_bias[layer])
        pcol = jnp.zeros((SMALL_ROWS, LANES), F32)
        pcol = pcol.at[:GDN_HEADS, 0].set(a_log[layer]).at[:GDN_HEADS, 1].set(dt_bias[layer])
        ret, gdn = _token_mixers(conv, proj, gab, gabt, tabs, prow, pcol, gdn_norm[layer][None, :],
                                 batch, seq, blk, nb)
        i = layer // 2
        is_moe = layer % 2 == 1
        wa, wb, wo = _bf(w_branch[layer, 0]), _bf(w_branch[layer, 1]), _bf(w_out[layer])
        if not is_moe:
            h = _merge_dense_ffn(ret, gdn, proj, h, wa, wb, wo, norm_ffn[layer][None, :], _bf(dense_w_gate[i])[None],
                                 _bf(dense_w_up[i])[None], _bf(dense_w_down[i])[None], tm_rows, tf_dense)
        else:
            router_w = _bf(jnp.pad(router[i], ((0, 0), (0, LANES - N_EXPERTS))))
            h, meta, meta_t, counts = _merge_router(ret, gdn, proj, h, wa, wb, wo, norm_ffn[layer][None, :],
                                                    router_w, tm_rows)
            out = _moe_layer(meta, meta_t, counts, h, norm_ffn[layer][None, :], final_norm[None, :],
                             _bf(moe_w_gate[i]), _bf(moe_w_up[i]), _bf(moe_w_down[i]), rows, tm_moe, 256)
    return out.reshape(batch, seq, d)
```

```python
import functools
import math

import jax
import jax.numpy as jnp
import numpy as np
from jax import lax
from jax.experimental import pallas as pl
from jax.experimental.pallas import tpu as pltpu

F32 = jnp.float32
BF16 = jnp.bfloat16

D_MODEL = 1024
RET_HEADS = 4
GDN_HEADS = 4
HEAD_DIM = 128
HEADS_W = 512
CONV_K = 4
CONV_CH = 3 * HEADS_W
GDN_CHUNK = 64
ROPE_BASE = 10000.0
D_FF = 2816
N_EXPERTS = 8
TOP_K = 2
EPS = 1e-6
LANES = 128
SMALL_ROWS = 16
ROW_GROUP = 8
CONV_ROWS = 256
MIXER_GROUP = 16

MAIN_COLS = CONV_CH + 5 * HEADS_W + 2 * D_MODEL
VMEM_LIMIT = 56 * 1024 * 1024

HIGHEST = lax.Precision.HIGHEST


def _bf(x):
    return x.astype(BF16)


def _dot(a, b):
    return jnp.dot(a, b, preferred_element_type=F32)


def _dot_nt(a, b):
    return lax.dot_general(a, b, (((1,), (1,)), ((), ())), preferred_element_type=F32)


def _dot_tn(a, b):
    return lax.dot_general(a, b, (((0,), (0,)), ((), ())), preferred_element_type=F32)


def _dot_f32(a, b):
    return jnp.dot(a, b, preferred_element_type=F32, precision=HIGHEST)


def _sigmoid(x):
    return 1.0 / (1.0 + jnp.exp(-x))


def _silu(x):
    return x * _sigmoid(x)


def _softplus(x):
    return jnp.maximum(x, 0.0) + jnp.log(1.0 + jnp.exp(-jnp.abs(x)))


def _rms_rows(x, w):
    ms = jnp.mean(x * x, axis=-1, keepdims=True)
    return x * lax.rsqrt(ms + EPS) * w


def _params(*sem):
    return pltpu.CompilerParams(dimension_semantics=sem, vmem_limit_bytes=VMEM_LIMIT)


def _inproj_kernel(x_ref, nw_ref, w_ref, ws_ref, wst_ref, cw_ref, oc_ref, o_ref, os_ref, ost_ref,
                   hn_ref, xs_ref, *, tiles_per_seq, n_col):
    tm = x_ref.shape[0]
    tile, step = pl.program_id(0), pl.program_id(1)

    @pl.when(step == 0)
    def _():
        @pl.when(tile % tiles_per_seq == 0)
        def _():
            xs_ref[0:8, :] = jnp.zeros((8, CONV_CH), F32)

        hn = None
        for r0 in range(0, tm + CONV_ROWS, CONV_ROWS):
            prev, prev_rows = hn, slice(r0 - CONV_ROWS, r0)
            if r0 < tm:
                rows = slice(r0, r0 + CONV_ROWS)
                hn = _bf(_rms_rows(x_ref[rows, :], nw_ref[...]))
                hn_ref[rows, :] = hn
            if prev is not None:
                for c in range(w_ref.shape[0]):
                    xs_ref[8 + r0 - CONV_ROWS:8 + r0, c * HEADS_W:(c + 1) * HEADS_W] = _dot(prev, w_ref[c])
                os_ref[prev_rows, :] = _dot(prev, ws_ref[...])
                ost_ref[:, prev_rows] = _dot_nt(wst_ref[...], prev)

    def conv_piece(r0, c0):
        cols = slice(c0, c0 + LANES)
        acc = cw_ref[CONV_K - 1:CONV_K, cols] * xs_ref[pl.ds(r0 + 8, CONV_ROWS), cols]
        for d in range(1, CONV_K):
            acc = acc + cw_ref[CONV_K - 1 - d:CONV_K - d, cols] * xs_ref[pl.ds(r0 + 8 - d, CONV_ROWS), cols]
        oc_ref[r0:r0 + CONV_ROWS, cols] = _bf(_silu(acc))

    conv_pieces = [(r0, c0) for r0 in range(0, tm, CONV_ROWS) for c0 in range(0, CONV_CH, LANES)]
    dot_pieces = [(r0, c0) for r0 in range(0, tm, CONV_ROWS) for c0 in range(0, o_ref.shape[1], HEADS_W)]
    share = -(-len(conv_pieces) // (n_col - 1))
    for k in range(1, n_col):
        @pl.when(step == k)
        def _(k=k):
            mine = conv_pieces[(k - 1) * share:k * share]
            per_dot = -(-len(mine) // len(dot_pieces))
            for j, (r0, c0) in enumerate(dot_pieces):
                rows, cols = slice(r0, r0 + CONV_ROWS), slice(c0, c0 + HEADS_W)
                o_ref[rows, cols] = _bf(_dot(hn_ref[rows, :], w_ref[c0 // HEADS_W]))
                for piece in mine[j * per_dot:(j + 1) * per_dot]:
                    conv_piece(*piece)
            if k == n_col - 1:
                xs_ref[0:8, :] = xs_ref[tm:tm + 8, :]


def _in_projection(h, norm_w, w_main, w_small, w_small_t, conv_w, tm, tn, seq):
    t = h.shape[0]
    assert tn == CONV_CH and seq % tm == 0 and tm % CONV_ROWS == 0
    n_col = MAIN_COLS // tn
    return pl.pallas_call(
        functools.partial(_inproj_kernel, tiles_per_seq=seq // tm, n_col=n_col),
        grid=(t // tm, n_col),
        in_specs=[
            pl.BlockSpec((tm, D_MODEL), lambda i, j: (i, 0)),
            pl.BlockSpec((1, D_MODEL), lambda i, j: (0, 0)),
            pl.BlockSpec((tn // HEADS_W, D_MODEL, HEADS_W), lambda i, j: (j, 0, 0)),
            pl.BlockSpec((D_MODEL, LANES), lambda i, j: (0, 0)),
            pl.BlockSpec((SMALL_ROWS, D_MODEL), lambda i, j: (0, 0)),
            pl.BlockSpec((8, CONV_CH), lambda i, j: (0, 0)),
        ],
        out_specs=[
            pl.BlockSpec((tm, CONV_CH), lambda i, j: (i, 0)),
            pl.BlockSpec((None, tm, tn), lambda i, j: (jnp.maximum(j - 1, 0), i, 0)),
            pl.BlockSpec((tm, LANES), lambda i, j: (i, 0)),
            pl.BlockSpec((SMALL_ROWS, tm), lambda i, j: (0, i)),
        ],
        out_shape=[
            jax.ShapeDtypeStruct((t, CONV_CH), BF16),
            jax.ShapeDtypeStruct((n_col - 1, t, tn), BF16),
            jax.ShapeDtypeStruct((t, LANES), F32),
            jax.ShapeDtypeStruct((SMALL_ROWS, t), F32),
        ],
        scratch_shapes=[pltpu.VMEM((tm, D_MODEL), BF16), pltpu.VMEM((tm + 8, CONV_CH), F32)],
        compiler_params=_params("arbitrary", "arbitrary"),
        name="in_projection",
    )(h, norm_w, w_main, w_small, w_small_t, conv_w)


def _hi_lo(x):
    hi = _bf(x)
    lo_f = x - hi.astype(F32)
    return hi, _bf(lo_f), lo_f


def _three_lhs(hi, lo):
    return jnp.concatenate([hi, lo, hi], axis=1)


def _three_rhs(hi, lo):
    return jnp.concatenate([hi, hi, lo], axis=0)


def _block_diag(m_pair, half_l, half_r):
    return jnp.concatenate([m_pair * half_l, m_pair * half_r], axis=0)


def _mixer_kernel(*refs, blk, nb):
    gqkv_ref, rq_ref, rk_ref, rv_ref, rg_ref, gz_ref, gab_ref = refs[:7]
    gabt_refs = refs[7:7 + nb]
    (cos_ref, sin_ref, dmat_ref, qdec_ref, kdec_ref, cdec_ref, prow_ref, pcol_ref, gnorm_ref, tril_ref, triu_ref,
     ret_ref, gdn_ref, rstate_ref, gstate_ref) = refs[7 + nb:]
    c = GDN_CHUNK
    n_chunks = blk // c
    seqs = range(nb)
    chains = [(s, h) for s in seqs for h in range(GDN_HEADS)]
    inst = [(s, h, n) for s, h in chains for n in range(n_chunks)]
    slot = {ch: ch[0] * GDN_HEADS + ch[1] for ch in chains}

    @pl.when(pl.program_id(1) == 0)
    def _():
        rstate_ref[...] = jnp.zeros_like(rstate_ref)
        gstate_ref[...] = jnp.zeros_like(gstate_ref)

    hsl = [slice(h * HEAD_DIM, (h + 1) * HEAD_DIM) for h in range(RET_HEADS)]

    cos2 = cos_ref[...]
    sin2 = sin_ref[...]
    half = HEAD_DIM // 2
    rq = {(s, h): rq_ref[s, :, hsl[h]].astype(F32) for s, h in chains}
    rk = {(s, h): rk_ref[s, :, hsl[h]].astype(F32) for s, h in chains}
    rv = {(s, h): rv_ref[s, :, hsl[h]] for s, h in chains}
    rq = {ch: q * cos2 + pltpu.roll(q, half, 1) * sin2 for ch, q in rq.items()}
    rk = {ch: (k * cos2 + pltpu.roll(k, half, 1) * sin2) * (HEAD_DIM ** -0.5) for ch, k in rk.items()}
    rstate = {ch: rstate_ref[slot[ch]] for ch in chains}
    scores = {ch: _dot_nt(_bf(rq[ch]), _bf(rk[ch])) * dmat_ref[ch[1]] for ch in chains}
    inter = {ch: _dot(_bf(rq[ch] * qdec_ref[ch[1]]), _bf(rstate[ch])) for ch in chains}
    kv = {ch: _dot_tn(_bf(rk[ch] * kdec_ref[ch[1]]), rv[ch]) for ch in chains}
    ro = {ch: _dot(_bf(scores[ch]), rv[ch]) + inter[ch] for ch in chains}
    for ch in chains:
        s, h = ch
        rstate_ref[slot[ch]] = rstate[ch] * cdec_ref[h] + kv[ch]
        o = ro[ch]
        o = o * lax.rsqrt(jnp.mean(o * o, axis=-1, keepdims=True) + EPS)
        ret_ref[s, :, hsl[h]] = _bf(_silu(rg_ref[s, :, hsl[h]].astype(F32)) * o)

    neg_a_row = -jnp.exp(prow_ref[0:1, :])
    neg_a_col = -jnp.exp(pcol_ref[:, 0:1])
    beta_cols, gc_cols, egc_cols, gc_rows = [], [], [], []
    for s in seqs:
        gab = gab_ref[s]
        beta_cols.append(_sigmoid(gab))
        gc_cols.append(_dot_f32(tril_ref[...], neg_a_row * _softplus(gab + prow_ref[1:2, :])))
        egc_cols.append(jnp.exp(gc_cols[s]))
        gc_rows.append(_dot_f32(neg_a_col * _softplus(gabt_refs[s][...] + pcol_ref[:, 1:2]), triu_ref[...]))

    ri = lax.broadcasted_iota(jnp.int32, (c, 2 * c), 0)
    lane = lax.broadcasted_iota(jnp.int32, (c, 2 * c), 1)
    ci = lane & (c - 1)
    left = lane < c
    half_l = jnp.where(left, 1.0, 0.0).astype(BF16)
    half_r = jnp.where(left, 0.0, 1.0).astype(BF16)
    ge = ri >= ci
    gt = ri > ci
    eye = jnp.where(ri == ci, 1.0, 0.0).astype(F32)
    level_masks = []
    for lg in range(int(math.log2(c))):
        b = 1 << lg
        same_block = (ri >> (lg + 1)) == (ci >> (lg + 1))
        level_masks.append(same_block & ((ri & (2 * b - 1)) >= b) & ((ci & (2 * b - 1)) < b))

    qh, kh, vh = {}, {}, {}
    for s, h in chains:
        q = gqkv_ref[s, :, hsl[h]].astype(F32)
        k = gqkv_ref[s, :, HEADS_W + h * HEAD_DIM:HEADS_W + (h + 1) * HEAD_DIM].astype(F32)
        vh[s, h] = gqkv_ref[s, :, 2 * HEADS_W + h * HEAD_DIM:2 * HEADS_W + (h + 1) * HEAD_DIM].astype(F32)
        qh[s, h] = q * lax.rsqrt(jnp.sum(q * q, axis=-1, keepdims=True) + EPS) * (HEAD_DIM ** -0.5)
        kh[s, h] = k * lax.rsqrt(jnp.sum(k * k, axis=-1, keepdims=True) + EPS)

    rows = [slice(n * c, (n + 1) * c) for n in range(n_chunks)]
    uw, wq, kd, eg, attn = {}, {}, {}, {}, {}
    for g0 in range(0, len(inst), MIXER_GROUP):
        grp = inst[g0:g0 + MIXER_GROUP]
        kc = {(s, h, n): kh[s, h][rows[n]] for s, h, n in grp}
        qc = {(s, h, n): qh[s, h][rows[n]] for s, h, n in grp}
        bcol = {(s, h, n): beta_cols[s][rows[n], 4 + h:5 + h] for s, h, n in grp}
        gcol = {(s, h, n): gc_cols[s][rows[n], h:h + 1] for s, h, n in grp}
        egcol = {(s, h, n): egc_cols[s][rows[n], h:h + 1] for s, h, n in grp}
        glast = {(s, h, n): gc_cols[s][n * c + c - 1:(n + 1) * c, h:h + 1] for s, h, n in grp}
        grow = {(s, h, n): gc_rows[s][h:h + 1, 2 * c * n:2 * c * (n + 1)] for s, h, n in grp}
        kb = {i: kc[i] * bcol[i] for i in grp}
        kcb = {i: _bf(kc[i]) for i in grp}
        kcb2 = {i: jnp.concatenate([kcb[i], kcb[i]], axis=0) for i in grp}
        decay = {i: jnp.where(ge, jnp.exp(jnp.where(ge, gcol[i] - grow[i], 0.0)), 0.0) for i in grp}
        a2 = {i: jnp.where(gt, _dot_nt(_bf(kb[i]), kcb2[i]) * decay[i], 0.0) for i in grp}
        for i in grp:
            attn[i] = _bf(jnp.where(ge, _dot_nt(_bf(qc[i]), kcb2[i]) * decay[i], 0.0)[:, :c])

        pairs = [(grp[j], grp[j + 1]) for j in range(0, len(grp), 2)]
        a_pair = {p: jnp.where(left, a2[p[0]], a2[p[1]]) for p in pairs}
        z = {p: eye - jnp.where(level_masks[0], a_pair[p], 0.0) for p in pairs}
        a_hl = {p: _hi_lo(a_pair[p]) for p in pairs}
        for m in level_masks[1:]:
            mb = jnp.where(m, 1.0, 0.0).astype(BF16)
            z_hl = {p: _hi_lo(z[p]) for p in pairs}
            t = {p: _dot(_three_lhs(z_hl[p][0], z_hl[p][1]),
                         _three_rhs(_block_diag(a_hl[p][0] * mb, half_l, half_r), _block_diag(a_hl[p][1] * mb, half_l, half_r)))
                 for p in pairs}
            t_hl = {p: _hi_lo(t[p]) for p in pairs}
            z = {p: z[p] - _dot(_three_lhs(t_hl[p][0], t_hl[p][1]),
                                _three_rhs(_block_diag(z_hl[p][0], half_l, half_r), _block_diag(z_hl[p][1], half_l, half_r)))
                 for p in pairs}

        for p in pairs:
            stacked = []
            for i in p:
                s, h, n = i
                stacked.append(jnp.concatenate([vh[s, h][rows[n]] * bcol[i], kb[i] * egcol[i]], axis=1))
            r_hi, r_lo, _ = _hi_lo(jnp.concatenate(stacked, axis=0))
            rr = _three_rhs(r_hi, r_lo)
            z_hi, z_lo, _ = _hi_lo(z[p])
            uw[p[0]] = _dot(_three_lhs(z_hi * half_l, z_lo * half_l), rr)
            uw[p[1]] = _dot(_three_lhs(z_hi * half_r, z_lo * half_r), rr)
        for i in grp:
            wq[i] = _bf(jnp.concatenate([uw[i][:, HEAD_DIM:], qc[i] * egcol[i]], axis=0))
            kd[i] = _bf(kc[i] * jnp.exp(glast[i] - gcol[i]))
            eg[i] = jnp.exp(glast[i])

    state = {ch: gstate_ref[slot[ch]] for ch in chains}
    outs = {ch: [] for ch in chains}
    for n in range(n_chunks):
        sb = {ch: _bf(state[ch]) for ch in chains}
        ws = {ch: _dot(wq[ch + (n,)], sb[ch]) for ch in chains}
        vnb = {ch: _bf(uw[ch + (n,)][:, :HEAD_DIM] - ws[ch][:c]) for ch in chains}
        for ch in chains:
            outs[ch].append(ws[ch][c:] + _dot(attn[ch + (n,)], vnb[ch]))
        state = {ch: state[ch] * eg[ch + (n,)] + _dot_tn(kd[ch + (n,)], vnb[ch]) for ch in chains}

    gnorm = gnorm_ref[...]
    for ch in chains:
        s, h = ch
        gstate_ref[slot[ch]] = state[ch]
        o = jnp.concatenate(outs[ch], axis=0)
        o = o * lax.rsqrt(jnp.mean(o * o, axis=-1, keepdims=True) + EPS)
        gdn_ref[s, :, hsl[h]] = _bf(o * gnorm * _silu(gz_ref[s, :, hsl[h]].astype(F32)))


def _token_mixers(conv, proj, gab, gabt, tabs, prow, pcol, gnorm, batch, seq, blk, nb):
    conv3 = conv.reshape(batch, seq, CONV_CH)
    proj4 = proj.reshape(proj.shape[0], batch, seq, proj.shape[2])
    gab3 = gab.reshape(batch, seq, LANES)
    n_blk = seq // blk
    cos2, sin2, dmat, qdec, kdec, cdec, tril, triu = tabs
    w512 = lambda plane, col: pl.BlockSpec((None, nb, blk, HEADS_W), lambda b, n: (plane, b, n, col))
    full = lambda shape: pl.BlockSpec(shape, lambda b, n: (0,) * len(shape))
    gabt_specs = [pl.BlockSpec((SMALL_ROWS, blk), lambda b, n, s=s: (0, (b * nb + s) * n_blk + n)) for s in range(nb)]
    out = pl.pallas_call(
        functools.partial(_mixer_kernel, blk=blk, nb=nb),
        grid=(batch // nb, n_blk),
        in_specs=[
            pl.BlockSpec((nb, blk, CONV_CH), lambda b, n: (b, n, 0)),
            w512(0, 2), w512(1, 2), w512(2, 0), w512(2, 1), w512(2, 2),
            pl.BlockSpec((nb, blk, LANES), lambda b, n: (b, n, 0)),
            *gabt_specs,
            pl.BlockSpec((blk, HEAD_DIM), lambda b, n: (n, 0)),
            pl.BlockSpec((blk, HEAD_DIM), lambda b, n: (n, 0)),
            full((RET_HEADS, blk, blk)),
            full((RET_HEADS, blk, HEAD_DIM)),
            full((RET_HEADS, blk, HEAD_DIM)),
            full((RET_HEADS, 1, HEAD_DIM)),
            full((8, LANES)),
            full((SMALL_ROWS, LANES)),
            full((1, HEAD_DIM)),
            full((blk, blk)),
            full((blk, 2 * blk)),
        ],
        out_specs=[
            pl.BlockSpec((nb, blk, HEADS_W), lambda b, n: (b, n, 0)),
            pl.BlockSpec((nb, blk, HEADS_W), lambda b, n: (b, n, 0)),
        ],
        out_shape=[
            jax.ShapeDtypeStruct((batch, seq, HEADS_W), BF16),
            jax.ShapeDtypeStruct((batch, seq, HEADS_W), BF16),
        ],
        scratch_shapes=[
            pltpu.VMEM((nb * RET_HEADS, HEAD_DIM, HEAD_DIM), F32),
            pltpu.VMEM((nb * GDN_HEADS, HEAD_DIM, HEAD_DIM), F32),
        ],
        compiler_params=_params("parallel", "arbitrary"),
        name="token_mixers",
    )(conv3, proj4, proj4, proj4, proj4, proj4, gab3, *([gabt] * nb),
      cos2, sin2, dmat, qdec, kdec, cdec, prow, pcol, gnorm, tril, triu)
    ret, gdn = out
    return ret.reshape(batch * seq, HEADS_W), gdn.reshape(batch * seq, HEADS_W)


def _mixer_tables(seq, blk):
    inv_freq = ROPE_BASE ** (-jnp.arange(0, HEAD_DIM, 2, dtype=F32) / HEAD_DIM)
    ang = jnp.arange(seq, dtype=F32)[:, None] * inv_freq[None, :]
    cos, sin = jnp.cos(ang), jnp.sin(ang)
    cos2 = jnp.concatenate([cos, cos], axis=-1)
    sin2 = jnp.concatenate([-sin, sin], axis=-1)
    log_gamma = jnp.log(1.0 - jnp.exp2(-5.0 - jnp.arange(RET_HEADS, dtype=F32)))
    pos = jnp.arange(blk, dtype=F32)
    diff = pos[:, None] - pos[None, :]
    dmat = jnp.where(diff >= 0, jnp.exp(jnp.maximum(diff, 0.0)[None] * log_gamma[:, None, None]), 0.0)
    ones = jnp.ones((1, 1, HEAD_DIM), F32)
    qdec = jnp.exp((pos + 1.0)[None, :] * log_gamma[:, None])[:, :, None] * ones
    kdec = jnp.exp((blk - 1 - pos)[None, :] * log_gamma[:, None])[:, :, None] * ones
    cdec = jnp.exp(blk * log_gamma)[:, None, None] * ones
    idx = np.arange(blk)
    same = (idx[:, None] // GDN_CHUNK) == (idx[None, :] // GDN_CHUNK)
    tril = jnp.asarray((same & (idx[:, None] >= idx[None, :])).astype(np.float32))
    col = np.arange(2 * blk)
    col_chunk, col_pos = col // (2 * GDN_CHUNK), col % GDN_CHUNK
    triu = jnp.asarray(((idx[:, None] // GDN_CHUNK == col_chunk[None, :])
                        & (idx[:, None] % GDN_CHUNK <= col_pos[None, :])).astype(np.float32))
    return cos2, sin2, dmat, qdec, kdec, cdec, tril, triu


def _swiglu_rows(x, wg_ref, wu_ref, wd_ref, acc, tf):
    for f in range(D_FF // tf):
        cols = slice(f * tf, (f + 1) * tf)
        act = _silu(_dot(x, wg_ref[0, :, cols])) * _dot(x, wu_ref[0, :, cols])
        acc = acc + _dot(_bf(act), wd_ref[0, cols, :])
    return acc


def _merged_rows(ret_ref, gdn_ref, ga_ref, gb_ref, h_ref, wa_ref, wb_ref, wo_ref, nw_ref):
    ya = _dot(ret_ref[...], wa_ref[...])
    yb = _dot(gdn_ref[...], wb_ref[...])
    merged = _sigmoid(ga_ref[...].astype(F32)) * ya + _sigmoid(gb_ref[...].astype(F32)) * yb
    h_new = h_ref[...] + _dot(_bf(merged), wo_ref[...])
    return h_new, _bf(_rms_rows(h_new, nw_ref[...]))


def _merge_ffn_kernel(*refs, tf):
    wg_ref, wu_ref, wd_ref, o_ref = refs[9:]
    h_new, hn = _merged_rows(*refs[:9])
    o_ref[...] = _swiglu_rows(hn, wg_ref, wu_ref, wd_ref, h_new, tf)


def _merge_router_kernel(*refs):
    wr_ref, tri_ref, ho_ref, meta_ref, meta_t_ref, count_ref, run_ref = refs[9:]

    @pl.when(pl.program_id(0) == 0)
    def _():
        run_ref[...] = jnp.zeros_like(run_ref)

    h_new, hn = _merged_rows(*refs[:9])
    ho_ref[...] = h_new
    meta, run_new = _route_rows(_dot(hn, wr_ref[...]), tri_ref, run_ref[0:1, :])
    run_ref[0:1, :] = run_new
    count_ref[...] = jnp.broadcast_to(run_new, count_ref.shape)
    meta_ref[...] = meta
    meta_t_ref[...] = meta.T[:8, :]


def _resident(shape):
    return pl.BlockSpec(shape, lambda i: (0,) * len(shape), pipeline_mode=pl.Buffered(1))


def _merge_specs(tm):
    rows = lambda width, col: pl.BlockSpec((tm, width), lambda i: (i, col))
    gate = lambda plane: pl.BlockSpec((None, tm, D_MODEL), lambda i: (plane, i, 0))
    return [rows(HEADS_W, 0), rows(HEADS_W, 0), gate(0), gate(1), rows(D_MODEL, 0),
            _resident((HEADS_W, D_MODEL)), _resident((HEADS_W, D_MODEL)), _resident((D_MODEL, D_MODEL)),
            _resident((1, D_MODEL))]


def _merge_dense_ffn(ret, gdn, proj, h, wa, wb, wo, norm_w, wg, wu, wd, tm, tf):
    t = h.shape[0]
    return pl.pallas_call(
        functools.partial(_merge_ffn_kernel, tf=tf),
        grid=(t // tm,),
        in_specs=_merge_specs(tm) + [_resident((1, D_MODEL, D_FF)), _resident((1, D_MODEL, D_FF)),
                                     _resident((1, D_FF, D_MODEL))],
        out_specs=pl.BlockSpec((tm, D_MODEL), lambda i: (i, 0)),
        out_shape=jax.ShapeDtypeStruct((t, D_MODEL), F32),
        compiler_params=_params("parallel"),
        name="merge_dense_ffn",
    )(ret, gdn, proj, proj, h, wa, wb, wo, norm_w, wg, wu, wd)


def _merge_router(ret, gdn, proj, h, wa, wb, wo, norm_w, router_w, tm):
    t = h.shape[0]
    idx = np.arange(tm)
    tri = jnp.asarray((idx[:, None] > idx[None, :]).astype(np.float32), dtype=BF16)
    return pl.pallas_call(
        _merge_router_kernel,
        grid=(t // tm,),
        in_specs=_merge_specs(tm) + [_resident((D_MODEL, LANES)), _resident((tm, tm))],
        out_specs=[pl.BlockSpec((tm, D_MODEL), lambda i: (i, 0)),
                   pl.BlockSpec((tm, LANES), lambda i: (i, 0)),
                   pl.BlockSpec((8, tm), lambda i: (0, i)),
                   pl.BlockSpec((8, LANES), lambda i: (0, 0))],
        out_shape=[jax.ShapeDtypeStruct((t, D_MODEL), F32),
                   jax.ShapeDtypeStruct((t, LANES), F32),
                   jax.ShapeDtypeStruct((8, t), F32),
                   jax.ShapeDtypeStruct((8, LANES), F32)],
        scratch_shapes=[pltpu.VMEM((8, LANES), F32)],
        compiler_params=_params("arbitrary"),
        name="merge_router",
    )(ret, gdn, proj, proj, h, wa, wb, wo, norm_w, router_w, tri)


def _route_rows(logits, tri_ref, before_tile):
    tm = logits.shape[0]
    lane = lax.broadcasted_iota(jnp.int32, (tm, LANES), 1)
    neg = jnp.float32(-jnp.inf)
    l1 = jnp.where(lane < N_EXPERTS, logits, neg)
    m1 = jnp.max(l1, axis=-1, keepdims=True)
    i1 = jnp.min(jnp.where(l1 == m1, lane, LANES), axis=-1, keepdims=True)
    l2 = jnp.where(lane == i1, neg, l1)
    m2 = jnp.max(l2, axis=-1, keepdims=True)
    i2 = jnp.min(jnp.where(l2 == m2, lane, LANES), axis=-1, keepdims=True)
    e2 = jnp.exp(m2 - m1)
    g1 = 1.0 / (1.0 + e2)
    g2 = e2 / (1.0 + e2)
    sel1 = lane == i1
    sel2 = lane == i2
    onehot = jnp.where(sel1 | sel2, 1.0, 0.0).astype(F32)
    before = _dot(tri_ref[...], _bf(onehot)) + before_tile
    r1 = jnp.sum(jnp.where(sel1, before, 0.0), axis=-1, keepdims=True)
    r2 = jnp.sum(jnp.where(sel2, before, 0.0), axis=-1, keepdims=True)
    meta = jnp.where(lane == 0, i1.astype(F32), 0.0)
    meta = jnp.where(lane == 1, i2.astype(F32), meta)
    meta = jnp.where(lane == 2, r1, meta)
    meta = jnp.where(lane == 3, r2, meta)
    meta = jnp.where(lane == 4, g1, meta)
    meta = jnp.where(lane == 5, g2, meta)
    return meta, before_tile + jnp.sum(onehot, axis=0, keepdims=True)


def _row_wait(src_hbm, dst_vmem, sem, rows):
    pltpu.make_async_copy(src_hbm.at[pl.ds(0, rows)], dst_vmem, sem).wait()


def _dispatch_kernel(sched_ref, dest_ref, h_ref, nw_ref, xg_ref, hn_ref, zero_ref, sem, zsem, *, rows):
    tm = h_ref.shape[0]

    @pl.when(pl.program_id(0) == 0)
    def _():
        zero_ref[...] = jnp.zeros_like(zero_ref)
        for e in range(N_EXPERTS):
            @pl.when(sched_ref[N_EXPERTS + e] > 0)
            def _():
                start = pl.multiple_of(sched_ref[e] - rows, rows)
                clear = pltpu.make_async_copy(zero_ref, xg_ref.at[pl.ds(start, rows)], zsem)
                clear.start()
                clear.wait()

        def clear_unused(b, carry):
            clear = pltpu.make_async_copy(zero_ref, xg_ref.at[pl.ds(pl.multiple_of(b * rows, rows), rows)], zsem)
            clear.start()
            clear.wait()
            return carry

        lax.fori_loop(sched_ref[2 * N_EXPERTS], xg_ref.shape[0] // rows, clear_unused, 0)

    step = pl.program_id(0)
    slot = step % 2
    rows_ref = hn_ref.at[slot]
    rows_ref[...] = _rms_rows(h_ref[...], nw_ref[...])

    def body(g, carry):
        base = pl.multiple_of(g * ROW_GROUP, ROW_GROUP)
        for u in range(ROW_GROUP):
            for j in range(TOP_K):
                d = dest_ref[j * tm + base + u]
                pltpu.make_async_copy(rows_ref.at[pl.ds(base + u, 1)], xg_ref.at[pl.ds(d, 1)],
                                      sem.at[slot]).start(priority=j)
        return carry

    lax.fori_loop(0, tm // ROW_GROUP, body, 0)

    def drain(s):
        for _ in range(TOP_K):
            pltpu.make_async_copy(hn_ref.at[s], xg_ref.at[pl.ds(0, tm)], sem.at[s]).wait()

    @pl.when(step > 0)
    def _():
        drain(1 - slot)

    @pl.when(step == pl.num_programs(0) - 1)
    def _():
        drain(slot)


def _dispatch(sched, dest_flat, h, norm_w, n_rows, tm, rows):
    t = h.shape[0]
    grid_spec = pltpu.PrefetchScalarGridSpec(
        num_scalar_prefetch=1,
        grid=(t // tm,),
        in_specs=[
            pl.BlockSpec((TOP_K * tm,), lambda i, sc: (i,), memory_space=pltpu.SMEM),
            pl.BlockSpec((tm, D_MODEL), lambda i, sc: (i, 0)),
            pl.BlockSpec((1, D_MODEL), lambda i, sc: (0, 0)),
        ],
        out_specs=pl.BlockSpec(memory_space=pl.ANY),
        scratch_shapes=[pltpu.VMEM((2, tm, D_MODEL), F32), pltpu.VMEM((rows, D_MODEL), F32),
                        pltpu.SemaphoreType.DMA((2,)), pltpu.SemaphoreType.DMA(())],
    )
    return pl.pallas_call(
        functools.partial(_dispatch_kernel, rows=rows),
        grid_spec=grid_spec,
        out_shape=jax.ShapeDtypeStruct((n_rows, D_MODEL), F32),
        compiler_params=_params("arbitrary"),
        name="dispatch",
    )(sched, dest_flat, h, norm_w)


def _expert_kernel(be_ref, sched_ref, x_ref, wg_ref, wu_ref, wd_ref, y_ref, *, tf):
    del be_ref

    @pl.when(pl.program_id(0) < sched_ref[2 * N_EXPERTS])
    def _():
        y_ref[...] = _swiglu_rows(_bf(x_ref[...]), wg_ref, wu_ref, wd_ref, jnp.zeros(y_ref.shape, F32), tf)

    @pl.when(pl.program_id(0) >= sched_ref[2 * N_EXPERTS])
    def _():
        y_ref[...] = jnp.zeros_like(y_ref)


def _experts(block_e, sched, xg, wg, wu, wd, rows, tf):
    n_rows = xg.shape[0]
    used = lambda i, sc: jnp.minimum(i, sc[2 * N_EXPERTS] - 1)
    grid_spec = pltpu.PrefetchScalarGridSpec(
        num_scalar_prefetch=2,
        grid=(n_rows // rows,),
        in_specs=[
            pl.BlockSpec((rows, D_MODEL), lambda i, be, sc: (used(i, sc), 0)),
            pl.BlockSpec((1, D_MODEL, D_FF), lambda i, be, sc: (be[i], 0, 0)),
            pl.BlockSpec((1, D_MODEL, D_FF), lambda i, be, sc: (be[i], 0, 0)),
            pl.BlockSpec((1, D_FF, D_MODEL), lambda i, be, sc: (be[i], 0, 0)),
        ],
        out_specs=pl.BlockSpec((rows, D_MODEL), lambda i, be, sc: (i, 0)),
    )
    return pl.pallas_call(
        functools.partial(_expert_kernel, tf=tf),
        grid_spec=grid_spec,
        out_shape=jax.ShapeDtypeStruct((n_rows, D_MODEL), F32),
        compiler_params=_params("arbitrary"),
        name="experts",
    )(block_e, sched, xg, wg, wu, wd)


def _combine_kernel(dest_ref, next_dest_ref, meta_ref, h_ref, nw_ref, y_ref, o_ref, buf_ref, sem):
    tm = h_ref.shape[0]
    step = pl.program_id(0)
    slot = step % 2

    def fetch(idx_ref, s):
        def body(g, carry):
            base = pl.multiple_of(g * ROW_GROUP, ROW_GROUP)
            for u in range(ROW_GROUP):
                for j in range(TOP_K):
                    d = idx_ref[j * tm + base + u]
                    pltpu.make_async_copy(y_ref.at[pl.ds(d, 1)], buf_ref.at[s, j, pl.ds(base + u, 1)],
                                          sem.at[s]).start(priority=j)
            return carry

        lax.fori_loop(0, tm // ROW_GROUP, body, 0)

    @pl.when(step == 0)
    def _():
        fetch(dest_ref, 0)

    @pl.when(step + 1 < pl.num_programs(0))
    def _():
        fetch(next_dest_ref, 1 - slot)

    for j in range(TOP_K):
        _row_wait(y_ref, buf_ref.at[slot, j], sem.at[slot], tm)
    meta = meta_ref[...]
    moe = meta[:, 4:5] * buf_ref[slot, 0] + meta[:, 5:6] * buf_ref[slot, 1]
    o_ref[...] = _rms_rows(h_ref[...] + moe, nw_ref[...])


def _combine(dest_flat, meta, h, norm_w, y, tm):
    t = h.shape[0]
    last = t // tm - 1
    return pl.pallas_call(
        _combine_kernel,
        grid=(t // tm,),
        in_specs=[
            pl.BlockSpec((TOP_K * tm,), lambda i: (i,), memory_space=pltpu.SMEM),
            pl.BlockSpec((TOP_K * tm,), lambda i: (jnp.minimum(i + 1, last),), memory_space=pltpu.SMEM),
            pl.BlockSpec((tm, LANES), lambda i: (i, 0)),
            pl.BlockSpec((tm, D_MODEL), lambda i: (i, 0)),
            pl.BlockSpec((1, D_MODEL), lambda i: (0, 0)),
            pl.BlockSpec(memory_space=pl.ANY),
        ],
        out_specs=pl.BlockSpec((tm, D_MODEL), lambda i: (i, 0)),
        out_shape=jax.ShapeDtypeStruct((t, D_MODEL), F32),
        scratch_shapes=[pltpu.VMEM((2, TOP_K, tm, D_MODEL), F32), pltpu.SemaphoreType.DMA((2,))],
        compiler_params=_params("arbitrary"),
        name="combine",
    )(dest_flat, dest_flat, meta, h, norm_w, y)


def _moe_layer(meta, meta_t, counts, h, ffn_norm_w, final_norm_w, wg, wu, wd, rows, tm_rows, tf):
    t = h.shape[0]
    counts = counts[0, :N_EXPERTS].astype(jnp.int32)
    padded = ((counts + rows - 1) // rows) * rows
    pend = jnp.cumsum(padded)
    pstart = pend - padded
    n_rows = t * TOP_K + N_EXPERTS * rows
    expert = meta_t[0:TOP_K].astype(jnp.int32)
    rank = meta_t[TOP_K:2 * TOP_K].astype(jnp.int32)
    dest = rank
    for e in range(N_EXPERTS):
        dest = dest + jnp.where(expert == e, pstart[e], 0)
    dest = dest.reshape(TOP_K, t // tm_rows, tm_rows).transpose(1, 0, 2).reshape(t * TOP_K)
    n_used = pend[N_EXPERTS - 1] // rows
    block_start = jnp.minimum(jnp.arange(n_rows // rows, dtype=jnp.int32), n_used - 1) * rows
    block_e = jnp.sum(block_start[:, None] >= pend[None, :], axis=1).astype(jnp.int32)
    sched = jnp.concatenate([pend, padded, n_used[None]]).astype(jnp.int32)
    xg = _dispatch(sched, dest, h, ffn_norm_w, n_rows, tm_rows, rows)
    y = _experts(block_e, sched, xg, wg, wu, wd, rows, tf)
    return _combine(dest, meta, h, final_norm_w, y, tm_rows)


def _pack_in_weights(w_in):
    o = np.cumsum((0, HEADS_W, HEADS_W, HEADS_W, HEADS_W, CONV_CH, HEADS_W, GDN_HEADS, GDN_HEADS, D_MODEL, D_MODEL))
    rq, rk, rv, rg, gqkv, gz, ga, gb, ma, mb = (w_in[:, o[i]:o[i + 1]] for i in range(10))
    slabs = [p[:, c:c + HEADS_W] for p in (gqkv, ma, rq, mb, rk, rv, rg, gz) for c in range(0, p.shape[1], HEADS_W)]
    main = _bf(jnp.stack(slabs, axis=0))
    small = jnp.concatenate([ga, gb], axis=1)
    small_cols = _bf(jnp.pad(small, ((0, 0), (0, LANES - 2 * GDN_HEADS))))
    small_rows = _bf(jnp.pad(small.T, ((0, SMALL_ROWS - 2 * GDN_HEADS), (0, 0))))
    return main, small_cols, small_rows


def _pick(n, prefs):
    for p in prefs:
        if n % p == 0:
            return p
    raise ValueError(f"no tile in {prefs} divides {n}")


def kernel(x, norm_mix, w_in, conv_w, a_log, dt_bias, gdn_norm, w_branch, w_out, norm_ffn,
           dense_w_gate, dense_w_up, dense_w_down, router, moe_w_gate, moe_w_up, moe_w_down, final_norm):
    batch, seq, d = x.shape
    depth = norm_mix.shape[0]
    assert d == D_MODEL and seq % GDN_CHUNK == 0
    assert depth % 2 == 0, "the final RMSNorm is fused into the last (routed) layer's combine"
    t = batch * seq
    blk = _pick(seq, (128, 64))
    nb = _pick(batch, (4, 2, 1))
    tm = _pick(seq, (1024, 512, 256))
    tm_rows = _pick(t, (512, 256))
    tm_moe = _pick(t, (1024, 512, 256))
    tf_dense = 256
    rows = 512
    tabs = _mixer_tables(seq, blk)

    h = x.reshape(t, d)
    out = None
    for layer in range(depth):
        w_main, w_small, w_small_t = _pack_in_weights(w_in[layer])
        convw = jnp.pad(conv_w[layer], ((0, 8 - CONV_K), (0, 0)))
        conv, proj, gab, gabt = _in_projection(h, norm_mix[layer][None, :], w_main, w_small, w_small_t, convw, tm, CONV_CH, seq)
        prow = jnp.zeros((8, LANES), F32)
        prow = prow.at[0, :GDN_HEADS].set(a_log[layer]).at[1, :GDN_HEADS].set(dt_bias[layer])
        pcol = jnp.zeros((SMALL_ROWS, LANES), F32)
        pcol = pcol.at[:GDN_HEADS, 0].set(a_log[layer]).at[:GDN_HEADS, 1].set(dt_bias[layer])
        ret, gdn = _token_mixers(conv, proj, gab, gabt, tabs, prow, pcol, gdn_norm[layer][None, :],
                                 batch, seq, blk, nb)
        i = layer // 2
        is_moe = layer % 2 == 1
        wa, wb, wo = _bf(w_branch[layer, 0]), _bf(w_branch[layer, 1]), _bf(w_out[layer])
        if not is_moe:
            h = _merge_dense_ffn(ret, gdn, proj, h, wa, wb, wo, norm_ffn[layer][None, :], _bf(dense_w_gate[i])[None],
                                 _bf(dense_w_up[i])[None], _bf(dense_w_down[i])[None], tm_rows, tf_dense)
        else:
            router_w = _bf(jnp.pad(router[i], ((0, 0), (0, LANES - N_EXPERTS))))
            h, meta, meta_t, counts = _merge_router(ret, gdn, proj, h, wa, wb, wo, norm_ffn[layer][None, :],
                                                    router_w, tm_rows)
            out = _moe_layer(meta, meta_t, counts, h, norm_ffn[layer][None, :], final_norm[None, :],
                             _bf(moe_w_gate[i]), _bf(moe_w_up[i]), _bf(moe_w_down[i]), rows, tm_moe, 256)
    return out.reshape(batch, seq, d)
```

```python
import functools
import math

import jax
import jax.numpy as jnp
import numpy as np
from jax import lax
from jax.experimental import pallas as pl
from jax.experimental.pallas import tpu as pltpu

F32 = jnp.float32
BF16 = jnp.bfloat16

D_MODEL = 1024
RET_HEADS = 4
GDN_HEADS = 4
HEAD_DIM = 128
HEADS_W = 512
CONV_K = 4
CONV_CH = 3 * HEADS_W
GDN_CHUNK = 64
ROPE_BASE = 10000.0
D_FF = 2816
N_EXPERTS = 8
TOP_K = 2
EPS = 1e-6
LANES = 128
SMALL_ROWS = 16
ROW_GROUP = 8
CONV_ROWS = 256
MIXER_GROUP = 16

MAIN_COLS = CONV_CH + 5 * HEADS_W + 2 * D_MODEL
VMEM_LIMIT = 56 * 1024 * 1024

HIGHEST = lax.Precision.HIGHEST


def _bf(x):
    return x.astype(BF16)


def _dot(a, b):
    return jnp.dot(a, b, preferred_element_type=F32)


def _dot_nt(a, b):
    return lax.dot_general(a, b, (((1,), (1,)), ((), ())), preferred_element_type=F32)


def _dot_tn(a, b):
    return lax.dot_general(a, b, (((0,), (0,)), ((), ())), preferred_element_type=F32)


def _dot_f32(a, b):
    return jnp.dot(a, b, preferred_element_type=F32, precision=HIGHEST)


def _sigmoid(x):
    return 1.0 / (1.0 + jnp.exp(-x))


def _silu(x):
    return x * _sigmoid(x)


def _softplus(x):
    return jnp.maximum(x, 0.0) + jnp.log(1.0 + jnp.exp(-jnp.abs(x)))


def _rms_rows(x, w):
    ms = jnp.mean(x * x, axis=-1, keepdims=True)
    return x * lax.rsqrt(ms + EPS) * w


def _params(*sem):
    return pltpu.CompilerParams(dimension_semantics=sem, vmem_limit_bytes=VMEM_LIMIT)


def _inproj_kernel(x_ref, nw_ref, w_ref, ws_ref, wst_ref, cw_ref, oc_ref, o_ref, os_ref, ost_ref,
                   hn_ref, xs_ref, *, tiles_per_seq, n_col):
    tm = x_ref.shape[0]
    tile, step = pl.program_id(0), pl.program_id(1)

    @pl.when(step == 0)
    def _():
        @pl.when(tile % tiles_per_seq == 0)
        def _():
            xs_ref[0:8, :] = jnp.zeros((8, CONV_CH), F32)

        hn = None
        for r0 in range(0, tm + CONV_ROWS, CONV_ROWS):
            prev, prev_rows = hn, slice(r0 - CONV_ROWS, r0)
            if r0 < tm:
                rows = slice(r0, r0 + CONV_ROWS)
                hn = _bf(_rms_rows(x_ref[rows, :], nw_ref[...]))
                hn_ref[rows, :] = hn
            if prev is not None:
                for c in range(w_ref.shape[0]):
                    xs_ref[8 + r0 - CONV_ROWS:8 + r0, c * HEADS_W:(c + 1) * HEADS_W] = _dot(prev, w_ref[c])
                os_ref[prev_rows, :] = _dot(prev, ws_ref[...])
                ost_ref[:, prev_rows] = _dot_nt(wst_ref[...], prev)

    def conv_piece(r0, c0):
        cols = slice(c0, c0 + LANES)
        acc = cw_ref[CONV_K - 1:CONV_K, cols] * xs_ref[pl.ds(r0 + 8, CONV_ROWS), cols]
        for d in range(1, CONV_K):
            acc = acc + cw_ref[CONV_K - 1 - d:CONV_K - d, cols] * xs_ref[pl.ds(r0 + 8 - d, CONV_ROWS), cols]
        oc_ref[r0:r0 + CONV_ROWS, cols] = _bf(_silu(acc))

    conv_pieces = [(r0, c0) for r0 in range(0, tm, CONV_ROWS) for c0 in range(0, CONV_CH, LANES)]
    dot_pieces = [(r0, c0) for r0 in range(0, tm, CONV_ROWS) for c0 in range(0, o_ref.shape[1], HEADS_W)]
    share = -(-len(conv_pieces) // (n_col - 1))
    for k in range(1, n_col):
        @pl.when(step == k)
        def _(k=k):
            mine = conv_pieces[(k - 1) * share:k * share]
            per_dot = -(-len(mine) // len(dot_pieces))
            for j, (r0, c0) in enumerate(dot_pieces):
                rows, cols = slice(r0, r0 + CONV_ROWS), slice(c0, c0 + HEADS_W)
                o_ref[rows, cols] = _bf(_dot(hn_ref[rows, :], w_ref[c0 // HEADS_W]))
                for piece in mine[j * per_dot:(j + 1) * per_dot]:
                    conv_piece(*piece)
            if k == n_col - 1:
                xs_ref[0:8, :] = xs_ref[tm:tm + 8, :]


def _in_projection(h, norm_w, w_main, w_small, w_small_t, conv_w, tm, tn, seq):
    t = h.shape[0]
    assert tn == CONV_CH and seq % tm == 0 and tm % CONV_ROWS == 0
    n_col = MAIN_COLS // tn
    return pl.pallas_call(
        functools.partial(_inproj_kernel, tiles_per_seq=seq // tm, n_col=n_col),
        grid=(t // tm, n_col),
        in_specs=[
            pl.BlockSpec((tm, D_MODEL), lambda i, j: (i, 0)),
            pl.BlockSpec((1, D_MODEL), lambda i, j: (0, 0)),
            pl.BlockSpec((tn // HEADS_W, D_MODEL, HEADS_W), lambda i, j: (j, 0, 0)),
            pl.BlockSpec((D_MODEL, LANES), lambda i, j: (0, 0)),
            pl.BlockSpec((SMALL_ROWS, D_MODEL), lambda i, j: (0, 0)),
            pl.BlockSpec((8, CONV_CH), lambda i, j: (0, 0)),
        ],
        out_specs=[
            pl.BlockSpec((tm, CONV_CH), lambda i, j: (i, 0)),
            pl.BlockSpec((tm, tn), lambda i, j: (i, jnp.maximum(j - 1, 0))),
            pl.BlockSpec((tm, LANES), lambda i, j: (i, 0)),
            pl.BlockSpec((SMALL_ROWS, tm), lambda i, j: (0, i)),
        ],
        out_shape=[
            jax.ShapeDtypeStruct((t, CONV_CH), BF16),
            jax.ShapeDtypeStruct((t, MAIN_COLS - CONV_CH), BF16),
            jax.ShapeDtypeStruct((t, LANES), F32),
            jax.ShapeDtypeStruct((SMALL_ROWS, t), F32),
        ],
        scratch_shapes=[pltpu.VMEM((tm, D_MODEL), BF16), pltpu.VMEM((tm + 8, CONV_CH), F32)],
        compiler_params=_params("arbitrary", "arbitrary"),
        name="in_projection",
    )(h, norm_w, w_main, w_small, w_small_t, conv_w)


def _hi_lo(x):
    hi = _bf(x)
    lo_f = x - hi.astype(F32)
    return hi, _bf(lo_f), lo_f


def _three_lhs(hi, lo):
    return jnp.concatenate([hi, lo, hi], axis=1)


def _three_rhs(hi, lo):
    return jnp.concatenate([hi, hi, lo], axis=0)


def _block_diag(m_pair, half_l, half_r):
    return jnp.concatenate([m_pair * half_l, m_pair * half_r], axis=0)


def _mixer_kernel(*refs, blk, nb):
    gqkv_ref, rq_ref, rk_ref, rv_ref, rg_ref, gz_ref, gab_ref = refs[:7]
    gabt_refs = refs[7:7 + nb]
    (cos_ref, sin_ref, dmat_ref, qdec_ref, kdec_ref, cdec_ref, prow_ref, pcol_ref, gnorm_ref, tril_ref, triu_ref,
     ret_ref, gdn_ref, rstate_ref, gstate_ref) = refs[7 + nb:]
    c = GDN_CHUNK
    n_chunks = blk // c
    seqs = range(nb)
    chains = [(s, h) for s in seqs for h in range(GDN_HEADS)]
    inst = [(s, h, n) for s, h in chains for n in range(n_chunks)]
    slot = {ch: ch[0] * GDN_HEADS + ch[1] for ch in chains}

    @pl.when(pl.program_id(1) == 0)
    def _():
        rstate_ref[...] = jnp.zeros_like(rstate_ref)
        gstate_ref[...] = jnp.zeros_like(gstate_ref)

    hsl = [slice(h * HEAD_DIM, (h + 1) * HEAD_DIM) for h in range(RET_HEADS)]

    cos2 = cos_ref[...]
    sin2 = sin_ref[...]
    half = HEAD_DIM // 2
    rq = {(s, h): rq_ref[s, :, hsl[h]].astype(F32) for s, h in chains}
    rk = {(s, h): rk_ref[s, :, hsl[h]].astype(F32) for s, h in chains}
    rv = {(s, h): rv_ref[s, :, hsl[h]] for s, h in chains}
    rq = {ch: q * cos2 + pltpu.roll(q, half, 1) * sin2 for ch, q in rq.items()}
    rk = {ch: (k * cos2 + pltpu.roll(k, half, 1) * sin2) * (HEAD_DIM ** -0.5) for ch, k in rk.items()}
    rstate = {ch: rstate_ref[slot[ch]] for ch in chains}
    scores = {ch: _dot_nt(_bf(rq[ch]), _bf(rk[ch])) * dmat_ref[ch[1]] for ch in chains}
    inter = {ch: _dot(_bf(rq[ch] * qdec_ref[ch[1]]), _bf(rstate[ch])) for ch in chains}
    kv = {ch: _dot_tn(_bf(rk[ch] * kdec_ref[ch[1]]), rv[ch]) for ch in chains}
    ro = {ch: _dot(_bf(scores[ch]), rv[ch]) + inter[ch] for ch in chains}
    for ch in chains:
        s, h = ch
        rstate_ref[slot[ch]] = rstate[ch] * cdec_ref[h] + kv[ch]
        o = ro[ch]
        o = o * lax.rsqrt(jnp.mean(o * o, axis=-1, keepdims=True) + EPS)
        ret_ref[s, :, hsl[h]] = _bf(_silu(rg_ref[s, :, hsl[h]].astype(F32)) * o)

    neg_a_row = -jnp.exp(prow_ref[0:1, :])
    neg_a_col = -jnp.exp(pcol_ref[:, 0:1])
    beta_cols, gc_cols, egc_cols, gc_rows = [], [], [], []
    for s in seqs:
        gab = gab_ref[s]
        beta_cols.append(_sigmoid(gab))
        gc_cols.append(_dot_f32(tril_ref[...], neg_a_row * _softplus(gab + prow_ref[1:2, :])))
        egc_cols.append(jnp.exp(gc_cols[s]))
        gc_rows.append(_dot_f32(neg_a_col * _softplus(gabt_refs[s][...] + pcol_ref[:, 1:2]), triu_ref[...]))

    ri = lax.broadcasted_iota(jnp.int32, (c, 2 * c), 0)
    lane = lax.broadcasted_iota(jnp.int32, (c, 2 * c), 1)
    ci = lane & (c - 1)
    left = lane < c
    half_l = jnp.where(left, 1.0, 0.0).astype(BF16)
    half_r = jnp.where(left, 0.0, 1.0).astype(BF16)
    ge = ri >= ci
    gt = ri > ci
    eye = jnp.where(ri == ci, 1.0, 0.0).astype(F32)
    level_masks = []
    for lg in range(int(math.log2(c))):
        b = 1 << lg
        same_block = (ri >> (lg + 1)) == (ci >> (lg + 1))
        level_masks.append(same_block & ((ri & (2 * b - 1)) >= b) & ((ci & (2 * b - 1)) < b))

    qh, kh, vh = {}, {}, {}
    for s, h in chains:
        q = gqkv_ref[s, :, hsl[h]].astype(F32)
        k = gqkv_ref[s, :, HEADS_W + h * HEAD_DIM:HEADS_W + (h + 1) * HEAD_DIM].astype(F32)
        vh[s, h] = gqkv_ref[s, :, 2 * HEADS_W + h * HEAD_DIM:2 * HEADS_W + (h + 1) * HEAD_DIM].astype(F32)
        qh[s, h] = q * lax.rsqrt(jnp.sum(q * q, axis=-1, keepdims=True) + EPS) * (HEAD_DIM ** -0.5)
        kh[s, h] = k * lax.rsqrt(jnp.sum(k * k, axis=-1, keepdims=True) + EPS)

    rows = [slice(n * c, (n + 1) * c) for n in range(n_chunks)]
    uw, wq, kd, eg, attn = {}, {}, {}, {}, {}
    for g0 in range(0, len(inst), MIXER_GROUP):
        grp = inst[g0:g0 + MIXER_GROUP]
        kc = {(s, h, n): kh[s, h][rows[n]] for s, h, n in grp}
        qc = {(s, h, n): qh[s, h][rows[n]] for s, h, n in grp}
        bcol = {(s, h, n): beta_cols[s][rows[n], 4 + h:5 + h] for s, h, n in grp}
        gcol = {(s, h, n): gc_cols[s][rows[n], h:h + 1] for s, h, n in grp}
        egcol = {(s, h, n): egc_cols[s][rows[n], h:h + 1] for s, h, n in grp}
        glast = {(s, h, n): gc_cols[s][n * c + c - 1:(n + 1) * c, h:h + 1] for s, h, n in grp}
        grow = {(s, h, n): gc_rows[s][h:h + 1, 2 * c * n:2 * c * (n + 1)] for s, h, n in grp}
        kb = {i: kc[i] * bcol[i] for i in grp}
        kcb = {i: _bf(kc[i]) for i in grp}
        kcb2 = {i: jnp.concatenate([kcb[i], kcb[i]], axis=0) for i in grp}
        decay = {i: jnp.where(ge, jnp.exp(jnp.where(ge, gcol[i] - grow[i], 0.0)), 0.0) for i in grp}
        a2 = {i: jnp.where(gt, _dot_nt(_bf(kb[i]), kcb2[i]) * decay[i], 0.0) for i in grp}
        for i in grp:
            attn[i] = _bf(jnp.where(ge, _dot_nt(_bf(qc[i]), kcb2[i]) * decay[i], 0.0)[:, :c])

        pairs = [(grp[j], grp[j + 1]) for j in range(0, len(grp), 2)]
        a_pair = {p: jnp.where(left, a2[p[0]], a2[p[1]]) for p in pairs}
        z = {p: eye - jnp.where(level_masks[0], a_pair[p], 0.0) for p in pairs}
        a_hl = {p: _hi_lo(a_pair[p]) for p in pairs}
        for m in level_masks[1:]:
            mb = jnp.where(m, 1.0, 0.0).astype(BF16)
            z_hl = {p: _hi_lo(z[p]) for p in pairs}
            t = {p: _dot(_three_lhs(z_hl[p][0], z_hl[p][1]),
                         _three_rhs(_block_diag(a_hl[p][0] * mb, half_l, half_r), _block_diag(a_hl[p][1] * mb, half_l, half_r)))
                 for p in pairs}
            t_hl = {p: _hi_lo(t[p]) for p in pairs}
            z = {p: z[p] - _dot(_three_lhs(t_hl[p][0], t_hl[p][1]),
                                _three_rhs(_block_diag(z_hl[p][0], half_l, half_r), _block_diag(z_hl[p][1], half_l, half_r)))
                 for p in pairs}

        for p in pairs:
            stacked = []
            for i in p:
                s, h, n = i
                stacked.append(jnp.concatenate([vh[s, h][rows[n]] * bcol[i], kb[i] * egcol[i]], axis=1))
            r_hi, r_lo, _ = _hi_lo(jnp.concatenate(stacked, axis=0))
            rr = _three_rhs(r_hi, r_lo)
            z_hi, z_lo, _ = _hi_lo(z[p])
            uw[p[0]] = _dot(_three_lhs(z_hi * half_l, z_lo * half_l), rr)
            uw[p[1]] = _dot(_three_lhs(z_hi * half_r, z_lo * half_r), rr)
        for i in grp:
            wq[i] = _bf(jnp.concatenate([uw[i][:, HEAD_DIM:], qc[i] * egcol[i]], axis=0))
            kd[i] = _bf(kc[i] * jnp.exp(glast[i] - gcol[i]))
            eg[i] = jnp.exp(glast[i])

    state = {ch: gstate_ref[slot[ch]] for ch in chains}
    outs = {ch: [] for ch in chains}
    for n in range(n_chunks):
        sb = {ch: _bf(state[ch]) for ch in chains}
        ws = {ch: _dot(wq[ch + (n,)], sb[ch]) for ch in chains}
        vnb = {ch: _bf(uw[ch + (n,)][:, :HEAD_DIM] - ws[ch][:c]) for ch in chains}
        for ch in chains:
            outs[ch].append(ws[ch][c:] + _dot(attn[ch + (n,)], vnb[ch]))
        state = {ch: state[ch] * eg[ch + (n,)] + _dot_tn(kd[ch + (n,)], vnb[ch]) for ch in chains}

    gnorm = gnorm_ref[...]
    for ch in chains:
        s, h = ch
        gstate_ref[slot[ch]] = state[ch]
        o = jnp.concatenate(outs[ch], axis=0)
        o = o * lax.rsqrt(jnp.mean(o * o, axis=-1, keepdims=True) + EPS)
        gdn_ref[s, :, hsl[h]] = _bf(o * gnorm * _silu(gz_ref[s, :, hsl[h]].astype(F32)))


def _token_mixers(conv, proj, gab, gabt, tabs, prow, pcol, gnorm, batch, seq, blk, nb):
    conv3 = conv.reshape(batch, seq, CONV_CH)
    proj3 = proj.reshape(batch, seq, MAIN_COLS - CONV_CH)
    gab3 = gab.reshape(batch, seq, LANES)
    n_blk = seq // blk
    cos2, sin2, dmat, qdec, kdec, cdec, tril, triu = tabs
    w512 = lambda col: pl.BlockSpec((nb, blk, HEADS_W), lambda b, n, col=col: (b, n, col))
    full = lambda shape: pl.BlockSpec(shape, lambda b, n: (0,) * len(shape))
    gabt_specs = [pl.BlockSpec((SMALL_ROWS, blk), lambda b, n, s=s: (0, (b * nb + s) * n_blk + n)) for s in range(nb)]
    out = pl.pallas_call(
        functools.partial(_mixer_kernel, blk=blk, nb=nb),
        grid=(batch // nb, n_blk),
        in_specs=[
            pl.BlockSpec((nb, blk, CONV_CH), lambda b, n: (b, n, 0)),
            w512(4), w512(5), w512(6), w512(7), w512(8),
            pl.BlockSpec((nb, blk, LANES), lambda b, n: (b, n, 0)),
            *gabt_specs,
            pl.BlockSpec((blk, HEAD_DIM), lambda b, n: (n, 0)),
            pl.BlockSpec((blk, HEAD_DIM), lambda b, n: (n, 0)),
            full((RET_HEADS, blk, blk)),
            full((RET_HEADS, blk, HEAD_DIM)),
            full((RET_HEADS, blk, HEAD_DIM)),
            full((RET_HEADS, 1, HEAD_DIM)),
            full((8, LANES)),
            full((SMALL_ROWS, LANES)),
            full((1, HEAD_DIM)),
            full((blk, blk)),
            full((blk, 2 * blk)),
        ],
        out_specs=[
            pl.BlockSpec((nb, blk, HEADS_W), lambda b, n: (b, n, 0)),
            pl.BlockSpec((nb, blk, HEADS_W), lambda b, n: (b, n, 0)),
        ],
        out_shape=[
            jax.ShapeDtypeStruct((batch, seq, HEADS_W), BF16),
            jax.ShapeDtypeStruct((batch, seq, HEADS_W), BF16),
        ],
        scratch_shapes=[
            pltpu.VMEM((nb * RET_HEADS, HEAD_DIM, HEAD_DIM), F32),
            pltpu.VMEM((nb * GDN_HEADS, HEAD_DIM, HEAD_DIM), F32),
        ],
        compiler_params=_params("parallel", "arbitrary"),
        name="token_mixers",
    )(conv3, proj3, proj3, proj3, proj3, proj3, gab3, *([gabt] * nb),
      cos2, sin2, dmat, qdec, kdec, cdec, prow, pcol, gnorm, tril, triu)
    ret, gdn = out
    return ret.reshape(batch * seq, HEADS_W), gdn.reshape(batch * seq, HEADS_W)


def _mixer_tables(seq, blk):
    inv_freq = ROPE_BASE ** (-jnp.arange(0, HEAD_DIM, 2, dtype=F32) / HEAD_DIM)
    ang = jnp.arange(seq, dtype=F32)[:, None] * inv_freq[None, :]
    cos, sin = jnp.cos(ang), jnp.sin(ang)
    cos2 = jnp.concatenate([cos, cos], axis=-1)
    sin2 = jnp.concatenate([-sin, sin], axis=-1)
    log_gamma = jnp.log(1.0 - jnp.exp2(-5.0 - jnp.arange(RET_HEADS, dtype=F32)))
    pos = jnp.arange(blk, dtype=F32)
    diff = pos[:, None] - pos[None, :]
    dmat = jnp.where(diff >= 0, jnp.exp(jnp.maximum(diff, 0.0)[None] * log_gamma[:, None, None]), 0.0)
    ones = jnp.ones((1, 1, HEAD_DIM), F32)
    qdec = jnp.exp((pos + 1.0)[None, :] * log_gamma[:, None])[:, :, None] * ones
    kdec = jnp.exp((blk - 1 - pos)[None, :] * log_gamma[:, None])[:, :, None] * ones
    cdec = jnp.exp(blk * log_gamma)[:, None, None] * ones
    idx = np.arange(blk)
    same = (idx[:, None] // GDN_CHUNK) == (idx[None, :] // GDN_CHUNK)
    tril = jnp.asarray((same & (idx[:, None] >= idx[None, :])).astype(np.float32))
    col = np.arange(2 * blk)
    col_chunk, col_pos = col // (2 * GDN_CHUNK), col % GDN_CHUNK
    triu = jnp.asarray(((idx[:, None] // GDN_CHUNK == col_chunk[None, :])
                        & (idx[:, None] % GDN_CHUNK <= col_pos[None, :])).astype(np.float32))
    return cos2, sin2, dmat, qdec, kdec, cdec, tril, triu


def _swiglu_rows(x, wg_ref, wu_ref, wd_ref, acc, tf):
    for f in range(D_FF // tf):
        cols = slice(f * tf, (f + 1) * tf)
        act = _silu(_dot(x, wg_ref[0, :, cols])) * _dot(x, wu_ref[0, :, cols])
        acc = acc + _dot(_bf(act), wd_ref[0, cols, :])
    return acc


def _merged_rows(ret_ref, gdn_ref, ga_ref, gb_ref, h_ref, wa_ref, wb_ref, wo_ref, nw_ref):
    ya = _dot(ret_ref[...], wa_ref[...])
    yb = _dot(gdn_ref[...], wb_ref[...])
    merged = _sigmoid(ga_ref[...].astype(F32)) * ya + _sigmoid(gb_ref[...].astype(F32)) * yb
    h_new = h_ref[...] + _dot(_bf(merged), wo_ref[...])
    return h_new, _bf(_rms_rows(h_new, nw_ref[...]))


def _merge_ffn_kernel(*refs, tf):
    wg_ref, wu_ref, wd_ref, o_ref = refs[9:]
    h_new, hn = _merged_rows(*refs[:9])
    o_ref[...] = _swiglu_rows(hn, wg_ref, wu_ref, wd_ref, h_new, tf)


def _merge_router_kernel(*refs):
    wr_ref, tri_ref, ho_ref, meta_ref, meta_t_ref, count_ref, run_ref = refs[9:]

    @pl.when(pl.program_id(0) == 0)
    def _():
        run_ref[...] = jnp.zeros_like(run_ref)

    h_new, hn = _merged_rows(*refs[:9])
    ho_ref[...] = h_new
    meta, run_new = _route_rows(_dot(hn, wr_ref[...]), tri_ref, run_ref[0:1, :])
    run_ref[0:1, :] = run_new
    count_ref[...] = jnp.broadcast_to(run_new, count_ref.shape)
    meta_ref[...] = meta
    meta_t_ref[...] = meta.T[:8, :]


def _resident(shape):
    return pl.BlockSpec(shape, lambda i: (0,) * len(shape), pipeline_mode=pl.Buffered(1))


def _merge_specs(tm):
    rows = lambda width, col: pl.BlockSpec((tm, width), lambda i: (i, col))
    return [rows(HEADS_W, 0), rows(HEADS_W, 0), rows(D_MODEL, 0), rows(D_MODEL, 1), rows(D_MODEL, 0),
            _resident((HEADS_W, D_MODEL)), _resident((HEADS_W, D_MODEL)), _resident((D_MODEL, D_MODEL)),
            _resident((1, D_MODEL))]


def _merge_dense_ffn(ret, gdn, proj, h, wa, wb, wo, norm_w, wg, wu, wd, tm, tf):
    t = h.shape[0]
    return pl.pallas_call(
        functools.partial(_merge_ffn_kernel, tf=tf),
        grid=(t // tm,),
        in_specs=_merge_specs(tm) + [_resident((1, D_MODEL, D_FF)), _resident((1, D_MODEL, D_FF)),
                                     _resident((1, D_FF, D_MODEL))],
        out_specs=pl.BlockSpec((tm, D_MODEL), lambda i: (i, 0)),
        out_shape=jax.ShapeDtypeStruct((t, D_MODEL), F32),
        compiler_params=_params("parallel"),
        name="merge_dense_ffn",
    )(ret, gdn, proj, proj, h, wa, wb, wo, norm_w, wg, wu, wd)


def _merge_router(ret, gdn, proj, h, wa, wb, wo, norm_w, router_w, tm):
    t = h.shape[0]
    idx = np.arange(tm)
    tri = jnp.asarray((idx[:, None] > idx[None, :]).astype(np.float32), dtype=BF16)
    return pl.pallas_call(
        _merge_router_kernel,
        grid=(t // tm,),
        in_specs=_merge_specs(tm) + [_resident((D_MODEL, LANES)), _resident((tm, tm))],
        out_specs=[pl.BlockSpec((tm, D_MODEL), lambda i: (i, 0)),
                   pl.BlockSpec((tm, LANES), lambda i: (i, 0)),
                   pl.BlockSpec((8, tm), lambda i: (0, i)),
                   pl.BlockSpec((8, LANES), lambda i: (0, 0))],
        out_shape=[jax.ShapeDtypeStruct((t, D_MODEL), F32),
                   jax.ShapeDtypeStruct((t, LANES), F32),
                   jax.ShapeDtypeStruct((8, t), F32),
                   jax.ShapeDtypeStruct((8, LANES), F32)],
        scratch_shapes=[pltpu.VMEM((8, LANES), F32)],
        compiler_params=_params("arbitrary"),
        name="merge_router",
    )(ret, gdn, proj, proj, h, wa, wb, wo, norm_w, router_w, tri)


def _route_rows(logits, tri_ref, before_tile):
    tm = logits.shape[0]
    lane = lax.broadcasted_iota(jnp.int32, (tm, LANES), 1)
    neg = jnp.float32(-jnp.inf)
    l1 = jnp.where(lane < N_EXPERTS, logits, neg)
    m1 = jnp.max(l1, axis=-1, keepdims=True)
    i1 = jnp.min(jnp.where(l1 == m1, lane, LANES), axis=-1, keepdims=True)
    l2 = jnp.where(lane == i1, neg, l1)
    m2 = jnp.max(l2, axis=-1, keepdims=True)
    i2 = jnp.min(jnp.where(l2 == m2, lane, LANES), axis=-1, keepdims=True)
    e2 = jnp.exp(m2 - m1)
    g1 = 1.0 / (1.0 + e2)
    g2 = e2 / (1.0 + e2)
    sel1 = lane == i1
    sel2 = lane == i2
    onehot = jnp.where(sel1 | sel2, 1.0, 0.0).astype(F32)
    before = _dot(tri_ref[...], _bf(onehot)) + before_tile
    r1 = jnp.sum(jnp.where(sel1, before, 0.0), axis=-1, keepdims=True)
    r2 = jnp.sum(jnp.where(sel2, before, 0.0), axis=-1, keepdims=True)
    meta = jnp.where(lane == 0, i1.astype(F32), 0.0)
    meta = jnp.where(lane == 1, i2.astype(F32), meta)
    meta = jnp.where(lane == 2, r1, meta)
    meta = jnp.where(lane == 3, r2, meta)
    meta = jnp.where(lane == 4, g1, meta)
    meta = jnp.where(lane == 5, g2, meta)
    return meta, before_tile + jnp.sum(onehot, axis=0, keepdims=True)


def _row_wait(src_hbm, dst_vmem, sem, rows):
    pltpu.make_async_copy(src_hbm.at[pl.ds(0, rows)], dst_vmem, sem).wait()


def _dispatch_kernel(sched_ref, dest_ref, h_ref, nw_ref, xg_ref, hn_ref, zero_ref, sem, zsem, *, rows):
    tm = h_ref.shape[0]

    @pl.when(pl.program_id(0) == 0)
    def _():
        zero_ref[...] = jnp.zeros_like(zero_ref)
        for e in range(N_EXPERTS):
            @pl.when(sched_ref[N_EXPERTS + e] > 0)
            def _():
                start = pl.multiple_of(sched_ref[e] - rows, rows)
                clear = pltpu.make_async_copy(zero_ref, xg_ref.at[pl.ds(start, rows)], zsem)
                clear.start()
                clear.wait()

        def clear_unused(b, carry):
            clear = pltpu.make_async_copy(zero_ref, xg_ref.at[pl.ds(pl.multiple_of(b * rows, rows), rows)], zsem)
            clear.start()
            clear.wait()
            return carry

        lax.fori_loop(sched_ref[2 * N_EXPERTS], xg_ref.shape[0] // rows, clear_unused, 0)

    step = pl.program_id(0)
    slot = step % 2
    rows_ref = hn_ref.at[slot]
    rows_ref[...] = _rms_rows(h_ref[...], nw_ref[...])

    def body(g, carry):
        base = pl.multiple_of(g * ROW_GROUP, ROW_GROUP)
        for u in range(ROW_GROUP):
            for j in range(TOP_K):
                d = dest_ref[j * tm + base + u]
                pltpu.make_async_copy(rows_ref.at[pl.ds(base + u, 1)], xg_ref.at[pl.ds(d, 1)],
                                      sem.at[slot]).start(priority=j)
        return carry

    lax.fori_loop(0, tm // ROW_GROUP, body, 0)

    def drain(s):
        for _ in range(TOP_K):
            pltpu.make_async_copy(hn_ref.at[s], xg_ref.at[pl.ds(0, tm)], sem.at[s]).wait()

    @pl.when(step > 0)
    def _():
        drain(1 - slot)

    @pl.when(step == pl.num_programs(0) - 1)
    def _():
        drain(slot)


def _dispatch(sched, dest_flat, h, norm_w, n_rows, tm, rows):
    t = h.shape[0]
    grid_spec = pltpu.PrefetchScalarGridSpec(
        num_scalar_prefetch=1,
        grid=(t // tm,),
        in_specs=[
            pl.BlockSpec((TOP_K * tm,), lambda i, sc: (i,), memory_space=pltpu.SMEM),
            pl.BlockSpec((tm, D_MODEL), lambda i, sc: (i, 0)),
            pl.BlockSpec((1, D_MODEL), lambda i, sc: (0, 0)),
        ],
        out_specs=pl.BlockSpec(memory_space=pl.ANY),
        scratch_shapes=[pltpu.VMEM((2, tm, D_MODEL), F32), pltpu.VMEM((rows, D_MODEL), F32),
                        pltpu.SemaphoreType.DMA((2,)), pltpu.SemaphoreType.DMA(())],
    )
    return pl.pallas_call(
        functools.partial(_dispatch_kernel, rows=rows),
        grid_spec=grid_spec,
        out_shape=jax.ShapeDtypeStruct((n_rows, D_MODEL), F32),
        compiler_params=_params("arbitrary"),
        name="dispatch",
    )(sched, dest_flat, h, norm_w)


def _expert_kernel(be_ref, sched_ref, x_ref, wg_ref, wu_ref, wd_ref, y_ref, *, tf):
    del be_ref

    @pl.when(pl.program_id(0) < sched_ref[2 * N_EXPERTS])
    def _():
        y_ref[...] = _swiglu_rows(_bf(x_ref[...]), wg_ref, wu_ref, wd_ref, jnp.zeros(y_ref.shape, F32), tf)

    @pl.when(pl.program_id(0) >= sched_ref[2 * N_EXPERTS])
    def _():
        y_ref[...] = jnp.zeros_like(y_ref)


def _experts(block_e, sched, xg, wg, wu, wd, rows, tf):
    n_rows = xg.shape[0]
    used = lambda i, sc: jnp.minimum(i, sc[2 * N_EXPERTS] - 1)
    grid_spec = pltpu.PrefetchScalarGridSpec(
        num_scalar_prefetch=2,
        grid=(n_rows // rows,),
        in_specs=[
            pl.BlockSpec((rows, D_MODEL), lambda i, be, sc: (used(i, sc), 0)),
            pl.BlockSpec((1, D_MODEL, D_FF), lambda i, be, sc: (be[i], 0, 0)),
            pl.BlockSpec((1, D_MODEL, D_FF), lambda i, be, sc: (be[i], 0, 0)),
            pl.BlockSpec((1, D_FF, D_MODEL), lambda i, be, sc: (be[i], 0, 0)),
        ],
        out_specs=pl.BlockSpec((rows, D_MODEL), lambda i, be, sc: (i, 0)),
    )
    return pl.pallas_call(
        functools.partial(_expert_kernel, tf=tf),
        grid_spec=grid_spec,
        out_shape=jax.ShapeDtypeStruct((n_rows, D_MODEL), F32),
        compiler_params=_params("arbitrary"),
        name="experts",
    )(block_e, sched, xg, wg, wu, wd)


def _combine_kernel(dest_ref, next_dest_ref, meta_ref, h_ref, nw_ref, y_ref, o_ref, buf_ref, sem):
    tm = h_ref.shape[0]
    step = pl.program_id(0)
    slot = step % 2

    def fetch(idx_ref, s):
        def body(g, carry):
            base = pl.multiple_of(g * ROW_GROUP, ROW_GROUP)
            for u in range(ROW_GROUP):
                for j in range(TOP_K):
                    d = idx_ref[j * tm + base + u]
                    pltpu.make_async_copy(y_ref.at[pl.ds(d, 1)], buf_ref.at[s, j, pl.ds(base + u, 1)],
                                          sem.at[s]).start(priority=j)
            return carry

        lax.fori_loop(0, tm // ROW_GROUP, body, 0)

    @pl.when(step == 0)
    def _():
        fetch(dest_ref, 0)

    @pl.when(step + 1 < pl.num_programs(0))
    def _():
        fetch(next_dest_ref, 1 - slot)

    for j in range(TOP_K):
        _row_wait(y_ref, buf_ref.at[slot, j], sem.at[slot], tm)
    meta = meta_ref[...]
    moe = meta[:, 4:5] * buf_ref[slot, 0] + meta[:, 5:6] * buf_ref[slot, 1]
    o_ref[...] = _rms_rows(h_ref[...] + moe, nw_ref[...])


def _combine(dest_flat, meta, h, norm_w, y, tm):
    t = h.shape[0]
    last = t // tm - 1
    return pl.pallas_call(
        _combine_kernel,
        grid=(t // tm,),
        in_specs=[
            pl.BlockSpec((TOP_K * tm,), lambda i: (i,), memory_space=pltpu.SMEM),
            pl.BlockSpec((TOP_K * tm,), lambda i: (jnp.minimum(i + 1, last),), memory_space=pltpu.SMEM),
            pl.BlockSpec((tm, LANES), lambda i: (i, 0)),
            pl.BlockSpec((tm, D_MODEL), lambda i: (i, 0)),
            pl.BlockSpec((1, D_MODEL), lambda i: (0, 0)),
            pl.BlockSpec(memory_space=pl.ANY),
        ],
        out_specs=pl.BlockSpec((tm, D_MODEL), lambda i: (i, 0)),
        out_shape=jax.ShapeDtypeStruct((t, D_MODEL), F32),
        scratch_shapes=[pltpu.VMEM((2, TOP_K, tm, D_MODEL), F32), pltpu.SemaphoreType.DMA((2,))],
        compiler_params=_params("arbitrary"),
        name="combine",
    )(dest_flat, dest_flat, meta, h, norm_w, y)


def _moe_layer(meta, meta_t, counts, h, ffn_norm_w, final_norm_w, wg, wu, wd, rows, tm_rows, tf):
    t = h.shape[0]
    counts = counts[0, :N_EXPERTS].astype(jnp.int32)
    padded = ((counts + rows - 1) // rows) * rows
    pend = jnp.cumsum(padded)
    pstart = pend - padded
    n_rows = t * TOP_K + N_EXPERTS * rows
    expert = meta_t[0:TOP_K].astype(jnp.int32)
    rank = meta_t[TOP_K:2 * TOP_K].astype(jnp.int32)
    dest = rank
    for e in range(N_EXPERTS):
        dest = dest + jnp.where(expert == e, pstart[e], 0)
    dest = dest.reshape(TOP_K, t // tm_rows, tm_rows).transpose(1, 0, 2).reshape(t * TOP_K)
    n_used = pend[N_EXPERTS - 1] // rows
    block_start = jnp.minimum(jnp.arange(n_rows // rows, dtype=jnp.int32), n_used - 1) * rows
    block_e = jnp.sum(block_start[:, None] >= pend[None, :], axis=1).astype(jnp.int32)
    sched = jnp.concatenate([pend, padded, n_used[None]]).astype(jnp.int32)
    xg = _dispatch(sched, dest, h, ffn_norm_w, n_rows, tm_rows, rows)
    y = _experts(block_e, sched, xg, wg, wu, wd, rows, tf)
    return _combine(dest, meta, h, final_norm_w, y, tm_rows)


def _pack_in_weights(w_in):
    o = np.cumsum((0, HEADS_W, HEADS_W, HEADS_W, HEADS_W, CONV_CH, HEADS_W, GDN_HEADS, GDN_HEADS, D_MODEL, D_MODEL))
    rq, rk, rv, rg, gqkv, gz, ga, gb, ma, mb = (w_in[:, o[i]:o[i + 1]] for i in range(10))
    slabs = [p[:, c:c + HEADS_W] for p in (gqkv, ma, mb, rq, rk, rv, rg, gz) for c in range(0, p.shape[1], HEADS_W)]
    main = _bf(jnp.stack(slabs, axis=0))
    small = jnp.concatenate([ga, gb], axis=1)
    small_cols = _bf(jnp.pad(small, ((0, 0), (0, LANES - 2 * GDN_HEADS))))
    small_rows = _bf(jnp.pad(small.T, ((0, SMALL_ROWS - 2 * GDN_HEADS), (0, 0))))
    return main, small_cols, small_rows


def _pick(n, prefs):
    for p in prefs:
        if n % p == 0:
            return p
    raise ValueError(f"no tile in {prefs} divides {n}")


def kernel(x, norm_mix, w_in, conv_w, a_log, dt_bias, gdn_norm, w_branch, w_out, norm_ffn,
           dense_w_gate, dense_w_up, dense_w_down, router, moe_w_gate, moe_w_up, moe_w_down, final_norm):
    batch, seq, d = x.shape
    depth = norm_mix.shape[0]
    assert d == D_MODEL and seq % GDN_CHUNK == 0
    assert depth % 2 == 0, "the final RMSNorm is fused into the last (routed) layer's combine"
    t = batch * seq
    blk = _pick(seq, (128, 64))
    nb = _pick(batch, (8, 4, 2, 1))
    tm = _pick(seq, (1024, 512, 256))
    tm_rows = _pick(t, (512, 256))
    tm_moe = _pick(t, (1024, 512, 256))
    tf_dense = 256
    rows = 512
    tabs = _mixer_tables(seq, blk)

    h = x.reshape(t, d)
    out = None
    for layer in range(depth):
        w_main, w_small, w_small_t = _pack_in_weights(w_in[layer])
        convw = jnp.pad(conv_w[layer], ((0, 8 - CONV_K), (0, 0)))
        conv, proj, gab, gabt = _in_projection(h, norm_mix[layer][None, :], w_main, w_small, w_small_t, convw, tm, CONV_CH, seq)
        prow = jnp.zeros((8, LANES), F32)
        prow = prow.at[0, :GDN_HEADS].set(a_log[layer]).at[1, :GDN_HEADS].set(dt_bias[layer])
        pcol = jnp.zeros((SMALL_ROWS, LANES), F32)
        pcol = pcol.at[:GDN_HEADS, 0].set(a_log[layer]).at[:GDN_HEADS, 1].set(dt_bias[layer])
        ret, gdn = _token_mixers(conv, proj, gab, gabt, tabs, prow, pcol, gdn_norm[layer][None, :],
                                 batch, seq, blk, nb)
        i = layer // 2
        is_moe = layer % 2 == 1
        wa, wb, wo = _bf(w_branch[layer, 0]), _bf(w_branch[layer, 1]), _bf(w_out[layer])
        if not is_moe:
            h = _merge_dense_ffn(ret, gdn, proj, h, wa, wb, wo, norm_ffn[layer][None, :], _bf(dense_w_gate[i])[None],
                                 _bf(dense_w_up[i])[None], _bf(dense_w_down[i])[None], tm_rows, tf_dense)
        else:
            router_w = _bf(jnp.pad(router[i], ((0, 0), (0, LANES - N_EXPERTS))))
            h, meta, meta_t, counts = _merge_router(ret, gdn, proj, h, wa, wb, wo, norm_ffn[layer][None, :],
                                                    router_w, tm_rows)
            out = _moe_layer(meta, meta_t, counts, h, norm_ffn[layer][None, :], final_norm[None, :],
                             _bf(moe_w_gate[i]), _bf(moe_w_up[i]), _bf(moe_w_down[i]), rows, tm_moe, 256)
    return out.reshape(batch, seq, d)
```
